```python
import math
import jax, jax.numpy as jnp
from jax import lax
import numpy as np

D_MODEL = 1024
BATCH = 16
SEQ = 2048
DEPTH = 4

GRID_W = 64
CTX_LEN = 256
HEAD_DIM = 64
ROPE_BASE = 10000.0
EPS = 1e-6
Q_BLOCK = 128
N_MOD = 6
D_FF = 4 * D_MODEL
N_EVEN = (DEPTH + 1) // 2
N_ODD = DEPTH // 2

GQA_Q_HEADS = 12
GQA_KV_HEADS = 4
GQA_GROUP = GQA_Q_HEADS // GQA_KV_HEADS
GQA_Q_W = GQA_Q_HEADS * HEAD_DIM
GQA_KV_W = GQA_KV_HEADS * HEAD_DIM
SSM_WIDTH = 256
SSM_GROUP = 16
SSM_GROUPS = SSM_WIDTH // SSM_GROUP
SSM_STATE = 64
SSM_DT_MIN = 0.001
SSM_DT_MAX = 0.1
EVEN_IN_W = GQA_Q_W + 2 * GQA_KV_W + SSM_WIDTH
EVEN_MIX_W = GQA_Q_W + SSM_WIDTH

MLA_HEADS = 8
MLA_Q_RANK = 512
MLA_KV_RANK = 256
MLA_NOPE = 64
MLA_ROPE = 32
MLA_QK = MLA_NOPE + MLA_ROPE
MLA_V = 64
NA_HEADS = 8
NA_W = NA_HEADS * HEAD_DIM
NA_WIN_R = 8
NA_WIN_C = 16
ODD_IN_W = MLA_Q_RANK + MLA_KV_RANK + MLA_ROPE + 3 * NA_W
ODD_MIX_W = MLA_HEADS * MLA_V + NA_W

kernel_name = 'hybrid_diffusion_gqa_s5_mla_natten'


def rms_norm(x, g):
    xf = x.astype(jnp.float32)
    y = xf * lax.rsqrt(jnp.mean(jnp.square(xf), axis=-1, keepdims=True) + EPS)
    return (y * g.astype(jnp.float32)).astype(x.dtype)


def modulate(x, g, shift, scale):
    return rms_norm(x, g) * (1 + scale) + shift


def axial_rope_tables(n_tokens, rot_dim):
    t = jnp.arange(n_tokens)
    rows = (t // GRID_W).astype(jnp.float32)
    cols = (t % GRID_W).astype(jnp.float32)
    axis_dim = rot_dim // 2
    freqs = ROPE_BASE ** (-jnp.arange(0, axis_dim, 2, dtype=jnp.float32) / axis_dim)
    ang_r = rows[:, None] * freqs
    ang_c = cols[:, None] * freqs
    ang = jnp.concatenate([ang_r, ang_r, ang_c, ang_c], axis=-1)
    return jnp.cos(ang), jnp.sin(ang)


def apply_axial_rope(x, cos, sin):
    xf = x.astype(jnp.float32)
    x1, x2, x3, x4 = jnp.split(xf, 4, axis=-1)
    rot = jnp.concatenate([-x2, x1, -x4, x3], axis=-1)
    return (xf * cos[:, None, :] + rot * sin[:, None, :]).astype(x.dtype)


def blocked_attention(q, k, v, scale):
    b, n = q.shape[:2]
    nb = n // Q_BLOCK
    qb = jnp.swapaxes(q.reshape((b, nb, Q_BLOCK) + q.shape[2:]), 0, 1)

    def one_block(qblk):
        s = jnp.einsum('bqkgd,bskd->bkgqs', qblk, k, preferred_element_type=jnp.float32) * scale
        p = jax.nn.softmax(s, axis=-1).astype(v.dtype)
        return jnp.einsum('bkgqs,bskd->bqkgd', p, v)

    out = lax.map(one_block, qb)
    return jnp.swapaxes(out, 0, 1).reshape((b, n) + out.shape[3:])


def s5_discretize(lam_re, lam_im, log_dt, b_re, b_im):
    f32 = jnp.float32
    lam_re, lam_im = lam_re.astype(f32), lam_im.astype(f32)
    dt = jnp.exp(log_dt.astype(f32))[:, None]
    mag = jnp.exp(lam_re * dt)
    a_re = mag * jnp.cos(lam_im * dt)
    a_im = mag * jnp.sin(lam_im * dt)
    den = jnp.square(lam_re) + jnp.square(lam_im)
    f_re = ((a_re - 1.0) * lam_re + a_im * lam_im) / den
    f_im = (a_im * lam_re - (a_re - 1.0) * lam_im) / den
    b_re, b_im = b_re.astype(f32), b_im.astype(f32)
    bb_re = f_re[..., None] * b_re - f_im[..., None] * b_im
    bb_im = f_re[..., None] * b_im + f_im[..., None] * b_re
    return a_re, a_im, bb_re, bb_im


def _complex_affine_combine(e1, e2):
    a1r, a1i, b1r, b1i = e1
    a2r, a2i, b2r, b2i = e2
    return (a2r * a1r - a2i * a1i, a2r * a1i + a2i * a1r,
            a2r * b1r - a2i * b1i + b2r, a2r * b1i + a2i * b1r + b2i)


def s5_scan(u, a_re, a_im, bb_re, bb_im, h0=None):
    n = u.shape[1]
    bu_re = jnp.einsum('bngp,gsp->bngs', u, bb_re)
    bu_im = jnp.einsum('bngp,gsp->bngs', u, bb_im)
    shape = (1, n) + a_re.shape
    ar = jnp.broadcast_to(a_re, shape)
    ai = jnp.broadcast_to(a_im, shape)
    p_re, p_im, h_re, h_im = lax.associative_scan(_complex_affine_combine, (ar, ai, bu_re, bu_im), axis=1)
    if h0 is not None:
        h0_re, h0_im = h0[0][:, None], h0[1][:, None]
        h_re = h_re + p_re * h0_re - p_im * h0_im
        h_im = h_im + p_re * h0_im + p_im * h0_re
    return h_re, h_im


def s5_readout(h_re, h_im, c_re, c_im):
    return jnp.einsum('bngs,gps->bngp', h_re, c_re) - jnp.einsum('bngs,gps->bngp', h_im, c_im)


def maybe_flip(t, d):
    return t[:, ::-1] if d == 1 else t


def s5_mixer(u_lat, u_ctx, lam_re, lam_im, log_dt, b_re, b_im, c_re, c_im, d_skip, w_glu, b_glu, need_ctx):
    f32 = jnp.float32
    out_dtype = u_lat.dtype

    def grouped(u):
        return u.astype(f32).reshape(u.shape[:2] + (SSM_GROUPS, SSM_GROUP))

    ul, uc = grouped(u_lat), grouped(u_ctx)
    d_g = d_skip.astype(f32).reshape(SSM_GROUPS, SSM_GROUP)
    y_lat = d_g * ul
    y_ctx = d_g * uc if need_ctx else None
    for d in range(2):
        a_re, a_im, bb_re, bb_im = s5_discretize(lam_re[d], lam_im[d], log_dt[d], b_re[d], b_im[d])
        cr, ci = c_re[d].astype(f32), c_im[d].astype(f32)
        hc_re, hc_im = s5_scan(maybe_flip(uc, d), a_re, a_im, bb_re, bb_im)
        hl_re, hl_im = s5_scan(maybe_flip(ul, d), a_re, a_im, bb_re, bb_im, h0=(hc_re[:, -1], hc_im[:, -1]))
        y_lat = y_lat + maybe_flip(s5_readout(hl_re, hl_im, cr, ci), d)
        if need_ctx:
            y_ctx = y_ctx + maybe_flip(s5_readout(hc_re, hc_im, cr, ci), d)
    wg, bg = w_glu.astype(f32), b_glu.astype(f32)

    def glu(y):
        y = jax.nn.gelu(y.reshape(y.shape[:2] + (SSM_WIDTH,)))
        return (y * jax.nn.sigmoid(y @ wg + bg)).astype(out_dtype)

    return glu(y_lat), (glu(y_ctx) if need_ctx else None)


def even_mixer(h_lat, h_ctx, w_in, w_out, g_q, g_k, lam_re, lam_im, log_dt, b_re, b_im, c_re, c_im,
               d_skip, w_glu, b_glu, need_ctx):
    b, n, _ = h_lat.shape
    n_ctx = h_ctx.shape[1]
    cos, sin = axial_rope_tables(n, HEAD_DIM)
    cuts = [GQA_Q_W, GQA_Q_W + GQA_KV_W, GQA_Q_W + 2 * GQA_KV_W]

    def project(h):
        t = h.shape[:2]
        q, k, v, u = jnp.split(h @ w_in, cuts, axis=-1)
        q = rms_norm(q.reshape(t + (GQA_Q_HEADS, HEAD_DIM)), g_q)
        k = rms_norm(k.reshape(t + (GQA_KV_HEADS, HEAD_DIM)), g_k)
        v = v.reshape(t + (GQA_KV_HEADS, HEAD_DIM))
        return q, k, v, u

    def grouped_q(q):
        return q.reshape(q.shape[:2] + (GQA_KV_HEADS, GQA_GROUP, HEAD_DIM))

    q_l, k_l, v_l, u_l = project(h_lat)
    q_c, k_c, v_c, u_c = project(h_ctx)
    q_l = apply_axial_rope(q_l, cos, sin)
    k_l = apply_axial_rope(k_l, cos, sin)
    scale = HEAD_DIM ** -0.5
    att_l = blocked_attention(grouped_q(q_l), jnp.concatenate([k_c, k_l], axis=1),
                              jnp.concatenate([v_c, v_l], axis=1), scale).reshape(b, n, GQA_Q_W)
    ssm_l, ssm_c = s5_mixer(u_l, u_c, lam_re, lam_im, log_dt, b_re, b_im, c_re, c_im, d_skip, w_glu, b_glu, need_ctx)
    out_l = jnp.concatenate([att_l, ssm_l], axis=-1) @ w_out
    out_c = None
    if need_ctx:
        att_c = blocked_attention(grouped_q(q_c), k_c, v_c, scale).reshape(b, n_ctx, GQA_Q_W)
        out_c = jnp.concatenate([att_c, ssm_c], axis=-1) @ w_out
    return out_l, out_c


def neighbourhood_attention(q, k, v, k_ctx, v_ctx, rpb, rows):
    b, n, h, dh = q.shape
    wr = min(NA_WIN_R, rows)
    n_loc = wr * NA_WIN_C
    scale = dh ** -0.5
    kg = k.reshape(b, rows, GRID_W, h, dh)
    vg = v.reshape(b, rows, GRID_W, h, dh)
    qg = jnp.swapaxes(q.reshape(b, rows, GRID_W, h, dh), 0, 1)
    row_start = jnp.clip(jnp.arange(rows) - wr // 2, 0, rows - wr)
    q_col = jnp.arange(GRID_W)
    col_idx = jnp.clip(q_col - NA_WIN_C // 2, 0, GRID_W - NA_WIN_C)[:, None] + jnp.arange(NA_WIN_C)
    col_bias = rpb[:, :, col_idx - q_col[:, None] + (NA_WIN_C - 1)]

    def one_row(args):
        r, q_row = args
        rs = row_start[r]
        kw = lax.dynamic_slice_in_dim(kg, rs, wr, axis=1)[:, :, col_idx]
        vw = lax.dynamic_slice_in_dim(vg, rs, wr, axis=1)[:, :, col_idx]
        bias = jnp.transpose(col_bias[:, rs + jnp.arange(wr) - r + (NA_WIN_R - 1)], (0, 2, 1, 3))
        s_loc = jnp.einsum('bqhd,bjqchd->bhqjc', q_row, kw, preferred_element_type=jnp.float32) * scale + bias
        s_ctx = jnp.einsum('bqhd,bshd->bhqs', q_row, k_ctx, preferred_element_type=jnp.float32) * scale
        s = jnp.concatenate([s_loc.reshape(b, h, GRID_W, n_loc), s_ctx], axis=-1)
        p = jax.nn.softmax(s, axis=-1).astype(v.dtype)
        p_loc = p[..., :n_loc].reshape(b, h, GRID_W, wr, NA_WIN_C)
        return (jnp.einsum('bhqjc,bjqchd->bqhd', p_loc, vw)
                + jnp.einsum('bhqs,bshd->bqhd', p[..., n_loc:], v_ctx))

    out = lax.map(one_row, (jnp.arange(rows), qg))
    return jnp.swapaxes(out, 0, 1).reshape(b, n, h * dh)


def odd_mixer(h_lat, h_ctx, w_in, w_out, g_cq, g_ckv, w_uq, w_ukv, g_mq, g_mk, g_nq, g_nk, rpb, need_ctx):
    b, n, _ = h_lat.shape
    n_ctx = h_ctx.shape[1]
    rows = n // GRID_W
    cos, sin = axial_rope_tables(n, MLA_ROPE)
    c1 = MLA_Q_RANK
    c2 = c1 + MLA_KV_RANK
    c3 = c2 + MLA_ROPE
    cuts = [c1, c2, c3, c3 + NA_W, c3 + 2 * NA_W]

    def project(h):
        t = h.shape[:2]
        cq, ckv, kr, nq, nk, nv = jnp.split(h @ w_in, cuts, axis=-1)
        q = (rms_norm(cq, g_cq) @ w_uq).reshape(t + (MLA_HEADS, MLA_QK))
        kv = (rms_norm(ckv, g_ckv) @ w_ukv).reshape(t + (MLA_HEADS, MLA_NOPE + MLA_V))
        k = jnp.concatenate([kv[..., :MLA_NOPE],
                             jnp.broadcast_to(kr[:, :, None, :], t + (MLA_HEADS, MLA_ROPE))], axis=-1)
        mla = (rms_norm(q, g_mq), rms_norm(k, g_mk), kv[..., MLA_NOPE:])
        na = (rms_norm(nq.reshape(t + (NA_HEADS, HEAD_DIM)), g_nq),
              rms_norm(nk.reshape(t + (NA_HEADS, HEAD_DIM)), g_nk),
              nv.reshape(t + (NA_HEADS, HEAD_DIM)))
        return mla, na

    def rope_tail(t):
        return jnp.concatenate([t[..., :MLA_NOPE], apply_axial_rope(t[..., MLA_NOPE:], cos, sin)], axis=-1)

    (mq_l, mk_l, mv_l), (nq_l, nk_l, nv_l) = project(h_lat)
    (mq_c, mk_c, mv_c), (nq_c, nk_c, nv_c) = project(h_ctx)
    mq_l, mk_l = rope_tail(mq_l), rope_tail(mk_l)
    mla_scale = MLA_QK ** -0.5
    mla_l = blocked_attention(mq_l[:, :, :, None], jnp.concatenate([mk_c, mk_l], axis=1),
                              jnp.concatenate([mv_c, mv_l], axis=1), mla_scale).reshape(b, n, MLA_HEADS * MLA_V)
    na_l = neighbourhood_attention(nq_l, nk_l, nv_l, nk_c, nv_c, rpb, rows)
    out_l = jnp.concatenate([mla_l, na_l], axis=-1) @ w_out
    out_c = None
    if need_ctx:
        mla_c = blocked_attention(mq_c[:, :, :, None], mk_c, mv_c, mla_scale).reshape(b, n_ctx, MLA_HEADS * MLA_V)
        na_c = blocked_attention(nq_c[:, :, :, None], nk_c, nv_c, HEAD_DIM ** -0.5).reshape(b, n_ctx, NA_W)
        out_c = jnp.concatenate([mla_c, na_c], axis=-1) @ w_out
    return out_l, out_c


def sq_relu_mlp(h, w1, w2):
    return jnp.square(jax.nn.relu(h @ w1)) @ w2


def _fwd_setup_inputs(seed: int = 0) -> dict:
    key = jax.random.key(seed)
    keys = iter(jax.random.split(key, 40))
    f32 = jnp.float32

    def normal(shape, scale):
        return scale * jax.random.normal(next(keys), shape, f32)

    def gain(shape):
        return 1.0 + 0.01 * jax.random.normal(next(keys), shape, f32)

    ne, no = N_EVEN, N_ODD
    G, N, P = SSM_GROUPS, SSM_STATE, SSM_GROUP
    return {
        'x': normal((BATCH, SEQ, D_MODEL), 1.0),
        'c': normal((BATCH, D_MODEL), 1.0),
        'ctx': normal((BATCH, CTX_LEN, D_MODEL), 1.0),
        'c_ctx': normal((D_MODEL,), 1.0),
        'w_mod': normal((DEPTH, D_MODEL, N_MOD * D_MODEL), D_MODEL ** -0.5),
        'b_mod': normal((DEPTH, N_MOD * D_MODEL), 0.01),
        'g_norm1': gain((DEPTH, D_MODEL)),
        'g_norm2': gain((DEPTH, D_MODEL)),
        'w_ff1': normal((DEPTH, D_MODEL, D_FF), D_MODEL ** -0.5),
        'w_ff2': normal((DEPTH, D_FF, D_MODEL), D_FF ** -0.5),
        'e_w_in': normal((ne, D_MODEL, EVEN_IN_W), D_MODEL ** -0.5),
        'e_w_out': normal((ne, EVEN_MIX_W, D_MODEL), EVEN_MIX_W ** -0.5),
        'e_g_q': gain((ne, HEAD_DIM)),
        'e_g_k': gain((ne, HEAD_DIM)),
        'ssm_lam_re': -0.5 + normal((ne, 2, G, N), 0.01),
        'ssm_lam_im': jnp.pi * jnp.arange(N, dtype=f32) + normal((ne, 2, G, N), 0.01),
        'ssm_log_dt': jax.random.uniform(next(keys), (ne, 2, G), f32, math.log(SSM_DT_MIN), math.log(SSM_DT_MAX)),
        'ssm_b_re': normal((ne, 2, G, N, P), (2 * P) ** -0.5),
        'ssm_b_im': normal((ne, 2, G, N, P), (2 * P) ** -0.5),
        'ssm_c_re': normal((ne, 2, G, P, N), 0.5),
        'ssm_c_im': normal((ne, 2, G, P, N), 0.5),
        'ssm_d': normal((ne, SSM_WIDTH), 0.5),
        'ssm_w_glu': normal((ne, SSM_WIDTH, SSM_WIDTH), SSM_WIDTH ** -0.5),
        'ssm_b_glu': normal((ne, SSM_WIDTH), 0.01),
        'o_w_in': normal((no, D_MODEL, ODD_IN_W), D_MODEL ** -0.5),
        'o_w_out': normal((no, ODD_MIX_W, D_MODEL), ODD_MIX_W ** -0.5),
        'mla_g_cq': gain((no, MLA_Q_RANK)),
        'mla_g_ckv': gain((no, MLA_KV_RANK)),
        'mla_w_uq': normal((no, MLA_Q_RANK, MLA_HEADS * MLA_QK), MLA_Q_RANK ** -0.5),
        'mla_w_ukv': normal((no, MLA_KV_RANK, MLA_HEADS * (MLA_NOPE + MLA_V)), MLA_KV_RANK ** -0.5),
        'mla_g_q': gain((no, MLA_QK)),
        'mla_g_k': gain((no, MLA_QK)),
        'na_g_q': gain((no, HEAD_DIM)),
        'na_g_k': gain((no, HEAD_DIM)),
        'na_rpb': normal((no, NA_HEADS, 2 * NA_WIN_R - 1, 2 * NA_WIN_C - 1), 0.1),
    }


def _fwd_reference(x, c, ctx, c_ctx, w_mod, b_mod, g_norm1, g_norm2, w_ff1, w_ff2,
              e_w_in, e_w_out, e_g_q, e_g_k, ssm_lam_re, ssm_lam_im, ssm_log_dt, ssm_b_re, ssm_b_im,
              ssm_c_re, ssm_c_im, ssm_d, ssm_w_glu, ssm_b_glu,
              o_w_in, o_w_out, mla_g_cq, mla_g_ckv, mla_w_uq, mla_w_ukv, mla_g_q, mla_g_k,
              na_g_q, na_g_k, na_rpb):
    cond_lat = jax.nn.silu(c)[:, None, :]
    cond_ctx = jax.nn.silu(c_ctx)[None, None, :]
    xc = ctx
    for i in range(DEPTH):
        need_ctx = i < DEPTH - 1
        j = i // 2
        m_lat = jnp.split(cond_lat @ w_mod[i] + b_mod[i], N_MOD, axis=-1)
        m_ctx = jnp.split(cond_ctx @ w_mod[i] + b_mod[i], N_MOD, axis=-1)
        a_lat = modulate(x, g_norm1[i], m_lat[0], m_lat[1])
        a_ctx = modulate(xc, g_norm1[i], m_ctx[0], m_ctx[1])
        if i % 2 == 0:
            o_lat, o_ctx = even_mixer(a_lat, a_ctx, e_w_in[j], e_w_out[j], e_g_q[j], e_g_k[j],
                                      ssm_lam_re[j], ssm_lam_im[j], ssm_log_dt[j], ssm_b_re[j], ssm_b_im[j],
                                      ssm_c_re[j], ssm_c_im[j], ssm_d[j], ssm_w_glu[j], ssm_b_glu[j], need_ctx)
        else:
            o_lat, o_ctx = odd_mixer(a_lat, a_ctx, o_w_in[j], o_w_out[j], mla_g_cq[j], mla_g_ckv[j],
                                     mla_w_uq[j], mla_w_ukv[j], mla_g_q[j], mla_g_k[j],
                                     na_g_q[j], na_g_k[j], na_rpb[j], need_ctx)
        x = x + m_lat[2] * o_lat
        x = x + m_lat[5] * sq_relu_mlp(modulate(x, g_norm2[i], m_lat[3], m_lat[4]), w_ff1[i], w_ff2[i])
        if need_ctx:
            xc = xc + m_ctx[2] * o_ctx
            xc = xc + m_ctx[5] * sq_relu_mlp(modulate(xc, g_norm2[i], m_ctx[3], m_ctx[4]), w_ff1[i], w_ff2[i])
    return x


import jax as _jax
import jax.numpy as _jnp

TWIN_FORMAT = 'train_step'
FWD_PARAMS = ['x', 'c', 'ctx', 'c_ctx', 'w_mod', 'b_mod', 'g_norm1', 'g_norm2', 'w_ff1', 'w_ff2', 'e_w_in', 'e_w_out', 'e_g_q', 'e_g_k', 'ssm_lam_re', 'ssm_lam_im', 'ssm_log_dt', 'ssm_b_re', 'ssm_b_im', 'ssm_c_re', 'ssm_c_im', 'ssm_d', 'ssm_w_glu', 'ssm_b_glu', 'o_w_in', 'o_w_out', 'mla_g_cq', 'mla_g_ckv', 'mla_w_uq', 'mla_w_ukv', 'mla_g_q', 'mla_g_k', 'na_g_q', 'na_g_k', 'na_rpb']
TWIN_WEIGHTS = ['c_ctx', 'w_mod', 'b_mod', 'g_norm1', 'g_norm2', 'w_ff1', 'w_ff2', 'e_w_in', 'e_w_out', 'e_g_q', 'e_g_k', 'ssm_lam_re', 'ssm_lam_im', 'ssm_log_dt', 'ssm_b_re', 'ssm_b_im', 'ssm_c_re', 'ssm_c_im', 'ssm_d', 'ssm_w_glu', 'ssm_b_glu', 'o_w_in', 'o_w_out', 'mla_g_cq', 'mla_g_ckv', 'mla_w_uq', 'mla_w_ukv', 'mla_g_q', 'mla_g_k', 'na_g_q', 'na_g_k', 'na_rpb']
TWIN_DIFF_INPUT = 'x'
TWIN_INPUTS = ['x', 'c', 'ctx', 'c_ctx', 'w_mod', 'b_mod', 'g_norm1', 'g_norm2', 'w_ff1', 'w_ff2', 'e_w_in', 'e_w_out', 'e_g_q', 'e_g_k', 'ssm_lam_re', 'ssm_lam_im', 'ssm_log_dt', 'ssm_b_re', 'ssm_b_im', 'ssm_c_re', 'ssm_c_im', 'ssm_d', 'ssm_w_glu', 'ssm_b_glu', 'o_w_in', 'o_w_out', 'mla_g_cq', 'mla_g_ckv', 'mla_w_uq', 'mla_w_ukv', 'mla_g_q', 'mla_g_k', 'na_g_q', 'na_g_k', 'na_rpb', 'loss_target', 'm_c_ctx', 'm_w_mod', 'm_b_mod', 'm_g_norm1', 'm_g_norm2', 'm_w_ff1', 'm_w_ff2', 'm_e_w_in', 'm_e_w_out', 'm_e_g_q', 'm_e_g_k', 'm_ssm_lam_re', 'm_ssm_lam_im', 'm_ssm_log_dt', 'm_ssm_b_re', 'm_ssm_b_im', 'm_ssm_c_re', 'm_ssm_c_im', 'm_ssm_d', 'm_ssm_w_glu', 'm_ssm_b_glu', 'm_o_w_in', 'm_o_w_out', 'm_mla_g_cq', 'm_mla_g_ckv', 'm_mla_w_uq', 'm_mla_w_ukv', 'm_mla_g_q', 'm_mla_g_k', 'm_na_g_q', 'm_na_g_k', 'm_na_rpb', 'v_c_ctx', 'v_w_mod', 'v_b_mod', 'v_g_norm1', 'v_g_norm2', 'v_w_ff1', 'v_w_ff2', 'v_e_w_in', 'v_e_w_out', 'v_e_g_q', 'v_e_g_k', 'v_ssm_lam_re', 'v_ssm_lam_im', 'v_ssm_log_dt', 'v_ssm_b_re', 'v_ssm_b_im', 'v_ssm_c_re', 'v_ssm_c_im', 'v_ssm_d', 'v_ssm_w_glu', 'v_ssm_b_glu', 'v_o_w_in', 'v_o_w_out', 'v_mla_g_cq', 'v_mla_g_ckv', 'v_mla_w_uq', 'v_mla_w_ukv', 'v_mla_g_q', 'v_mla_g_k', 'v_na_g_q', 'v_na_g_k', 'v_na_rpb']
TWIN_OUTPUTS = ['loss', 'grad_x', 'grad_c_ctx', 'grad_w_mod', 'grad_b_mod', 'grad_g_norm1', 'grad_g_norm2', 'grad_w_ff1', 'grad_w_ff2', 'grad_e_w_in', 'grad_e_w_out', 'grad_e_g_q', 'grad_e_g_k', 'grad_ssm_lam_re', 'grad_ssm_lam_im', 'grad_ssm_log_dt', 'grad_ssm_b_re', 'grad_ssm_b_im', 'grad_ssm_c_re', 'grad_ssm_c_im', 'grad_ssm_d', 'grad_ssm_w_glu', 'grad_ssm_b_glu', 'grad_o_w_in', 'grad_o_w_out', 'grad_mla_g_cq', 'grad_mla_g_ckv', 'grad_mla_w_uq', 'grad_mla_w_ukv', 'grad_mla_g_q', 'grad_mla_g_k', 'grad_na_g_q', 'grad_na_g_k', 'grad_na_rpb', 'delta_c_ctx', 'delta_w_mod', 'delta_b_mod', 'delta_g_norm1', 'delta_g_norm2', 'delta_w_ff1', 'delta_w_ff2', 'delta_e_w_in', 'delta_e_w_out', 'delta_e_g_q', 'delta_e_g_k', 'delta_ssm_lam_re', 'delta_ssm_lam_im', 'delta_ssm_log_dt', 'delta_ssm_b_re', 'delta_ssm_b_im', 'delta_ssm_c_re', 'delta_ssm_c_im', 'delta_ssm_d', 'delta_ssm_w_glu', 'delta_ssm_b_glu', 'delta_o_w_in', 'delta_o_w_out', 'delta_mla_g_cq', 'delta_mla_g_ckv', 'delta_mla_w_uq', 'delta_mla_w_ukv', 'delta_mla_g_q', 'delta_mla_g_k', 'delta_na_g_q', 'delta_na_g_k', 'delta_na_rpb', 'new_m_c_ctx', 'new_m_w_mod', 'new_m_b_mod', 'new_m_g_norm1', 'new_m_g_norm2', 'new_m_w_ff1', 'new_m_w_ff2', 'new_m_e_w_in', 'new_m_e_w_out', 'new_m_e_g_q', 'new_m_e_g_k', 'new_m_ssm_lam_re', 'new_m_ssm_lam_im', 'new_m_ssm_log_dt', 'new_m_ssm_b_re', 'new_m_ssm_b_im', 'new_m_ssm_c_re', 'new_m_ssm_c_im', 'new_m_ssm_d', 'new_m_ssm_w_glu', 'new_m_ssm_b_glu', 'new_m_o_w_in', 'new_m_o_w_out', 'new_m_mla_g_cq', 'new_m_mla_g_ckv', 'new_m_mla_w_uq', 'new_m_mla_w_ukv', 'new_m_mla_g_q', 'new_m_mla_g_k', 'new_m_na_g_q', 'new_m_na_g_k', 'new_m_na_rpb', 'new_v_c_ctx', 'new_v_w_mod', 'new_v_b_mod', 'new_v_g_norm1', 'new_v_g_norm2', 'new_v_w_ff1', 'new_v_w_ff2', 'new_v_e_w_in', 'new_v_e_w_out', 'new_v_e_g_q', 'new_v_e_g_k', 'new_v_ssm_lam_re', 'new_v_ssm_lam_im', 'new_v_ssm_log_dt', 'new_v_ssm_b_re', 'new_v_ssm_b_im', 'new_v_ssm_c_re', 'new_v_ssm_c_im', 'new_v_ssm_d', 'new_v_ssm_w_glu', 'new_v_ssm_b_glu', 'new_v_o_w_in', 'new_v_o_w_out', 'new_v_mla_g_cq', 'new_v_mla_g_ckv', 'new_v_mla_w_uq', 'new_v_mla_w_ukv', 'new_v_mla_g_q', 'new_v_mla_g_k', 'new_v_na_g_q', 'new_v_na_g_k', 'new_v_na_rpb']
TWIN_LEAF_KINDS = {'loss': 'loss', 'grad_x': 'grad_x', 'grad_c_ctx': 'grad_w', 'grad_w_mod': 'grad_w', 'grad_b_mod': 'grad_w', 'grad_g_norm1': 'grad_w', 'grad_g_norm2': 'grad_w', 'grad_w_ff1': 'grad_w', 'grad_w_ff2': 'grad_w', 'grad_e_w_in': 'grad_w', 'grad_e_w_out': 'grad_w', 'grad_e_g_q': 'grad_w', 'grad_e_g_k': 'grad_w', 'grad_ssm_lam_re': 'grad_w', 'grad_ssm_lam_im': 'grad_w', 'grad_ssm_log_dt': 'grad_w', 'grad_ssm_b_re': 'grad_w', 'grad_ssm_b_im': 'grad_w', 'grad_ssm_c_re': 'grad_w', 'grad_ssm_c_im': 'grad_w', 'grad_ssm_d': 'grad_w', 'grad_ssm_w_glu': 'grad_w', 'grad_ssm_b_glu': 'grad_w', 'grad_o_w_in': 'grad_w', 'grad_o_w_out': 'grad_w', 'grad_mla_g_cq': 'grad_w', 'grad_mla_g_ckv': 'grad_w', 'grad_mla_w_uq': 'grad_w', 'grad_mla_w_ukv': 'grad_w', 'grad_mla_g_q': 'grad_w', 'grad_mla_g_k': 'grad_w', 'grad_na_g_q': 'grad_w', 'grad_na_g_k': 'grad_w', 'grad_na_rpb': 'grad_w', 'delta_c_ctx': 'delta_w', 'delta_w_mod': 'delta_w', 'delta_b_mod': 'delta_w', 'delta_g_norm1': 'delta_w', 'delta_g_norm2': 'delta_w', 'delta_w_ff1': 'delta_w', 'delta_w_ff2': 'delta_w', 'delta_e_w_in': 'delta_w', 'delta_e_w_out': 'delta_w', 'delta_e_g_q': 'delta_w', 'delta_e_g_k': 'delta_w', 'delta_ssm_lam_re': 'delta_w', 'delta_ssm_lam_im': 'delta_w', 'delta_ssm_log_dt': 'delta_w', 'delta_ssm_b_re': 'delta_w', 'delta_ssm_b_im': 'delta_w', 'delta_ssm_c_re': 'delta_w', 'delta_ssm_c_im': 'delta_w', 'delta_ssm_d': 'delta_w', 'delta_ssm_w_glu': 'delta_w', 'delta_ssm_b_glu': 'delta_w', 'delta_o_w_in': 'delta_w', 'delta_o_w_out': 'delta_w', 'delta_mla_g_cq': 'delta_w', 'delta_mla_g_ckv': 'delta_w', 'delta_mla_w_uq': 'delta_w', 'delta_mla_w_ukv': 'delta_w', 'delta_mla_g_q': 'delta_w', 'delta_mla_g_k': 'delta_w', 'delta_na_g_q': 'delta_w', 'delta_na_g_k': 'delta_w', 'delta_na_rpb': 'delta_w', 'new_m_c_ctx': 'new_m', 'new_m_w_mod': 'new_m', 'new_m_b_mod': 'new_m', 'new_m_g_norm1': 'new_m', 'new_m_g_norm2': 'new_m', 'new_m_w_ff1': 'new_m', 'new_m_w_ff2': 'new_m', 'new_m_e_w_in': 'new_m', 'new_m_e_w_out': 'new_m', 'new_m_e_g_q': 'new_m', 'new_m_e_g_k': 'new_m', 'new_m_ssm_lam_re': 'new_m', 'new_m_ssm_lam_im': 'new_m', 'new_m_ssm_log_dt': 'new_m', 'new_m_ssm_b_re': 'new_m', 'new_m_ssm_b_im': 'new_m', 'new_m_ssm_c_re': 'new_m', 'new_m_ssm_c_im': 'new_m', 'new_m_ssm_d': 'new_m', 'new_m_ssm_w_glu': 'new_m', 'new_m_ssm_b_glu': 'new_m', 'new_m_o_w_in': 'new_m', 'new_m_o_w_out': 'new_m', 'new_m_mla_g_cq': 'new_m', 'new_m_mla_g_ckv': 'new_m', 'new_m_mla_w_uq': 'new_m', 'new_m_mla_w_ukv': 'new_m', 'new_m_mla_g_q': 'new_m', 'new_m_mla_g_k': 'new_m', 'new_m_na_g_q': 'new_m', 'new_m_na_g_k': 'new_m', 'new_m_na_rpb': 'new_m', 'new_v_c_ctx': 'new_v', 'new_v_w_mod': 'new_v', 'new_v_b_mod': 'new_v', 'new_v_g_norm1': 'new_v', 'new_v_g_norm2': 'new_v', 'new_v_w_ff1': 'new_v', 'new_v_w_ff2': 'new_v', 'new_v_e_w_in': 'new_v', 'new_v_e_w_out': 'new_v', 'new_v_e_g_q': 'new_v', 'new_v_e_g_k': 'new_v', 'new_v_ssm_lam_re': 'new_v', 'new_v_ssm_lam_im': 'new_v', 'new_v_ssm_log_dt': 'new_v', 'new_v_ssm_b_re': 'new_v', 'new_v_ssm_b_im': 'new_v', 'new_v_ssm_c_re': 'new_v', 'new_v_ssm_c_im': 'new_v', 'new_v_ssm_d': 'new_v', 'new_v_ssm_w_glu': 'new_v', 'new_v_ssm_b_glu': 'new_v', 'new_v_o_w_in': 'new_v', 'new_v_o_w_out': 'new_v', 'new_v_mla_g_cq': 'new_v', 'new_v_mla_g_ckv': 'new_v', 'new_v_mla_w_uq': 'new_v', 'new_v_mla_w_ukv': 'new_v', 'new_v_mla_g_q': 'new_v', 'new_v_mla_g_k': 'new_v', 'new_v_na_g_q': 'new_v', 'new_v_na_g_k': 'new_v', 'new_v_na_rpb': 'new_v'}


def _forward(args):
    return _fwd_reference(*[args[k] for k in FWD_PARAMS])


def _output_shape():
    out = _jax.eval_shape(lambda: _forward(_fwd_setup_inputs(0)))
    return out.shape, out.dtype

N_MICROBATCH = 1
ADAM_LR = 0.001
ADAM_B1 = 0.9
ADAM_B2 = 0.999
ADAM_EPS = 1e-08
ADAM_WD = 0.01
ADAM_STEP = 10
PER_EXAMPLE_BATCH_AXIS = {'x': 0, 'c': 0, 'ctx': 0, 'loss_target': 0}
SHARED_INPUTS = []
_WEIGHT_DTYPES = {'c_ctx': _jnp.float32, 'w_mod': _jnp.float32, 'b_mod': _jnp.float32, 'g_norm1': _jnp.float32, 'g_norm2': _jnp.float32, 'w_ff1': _jnp.float32, 'w_ff2': _jnp.float32, 'e_w_in': _jnp.float32, 'e_w_out': _jnp.float32, 'e_g_q': _jnp.float32, 'e_g_k': _jnp.float32, 'ssm_lam_re': _jnp.float32, 'ssm_lam_im': _jnp.float32, 'ssm_log_dt': _jnp.float32, 'ssm_b_re': _jnp.float32, 'ssm_b_im': _jnp.float32, 'ssm_c_re': _jnp.float32, 'ssm_c_im': _jnp.float32, 'ssm_d': _jnp.float32, 'ssm_w_glu': _jnp.float32, 'ssm_b_glu': _jnp.float32, 'o_w_in': _jnp.float32, 'o_w_out': _jnp.float32, 'mla_g_cq': _jnp.float32, 'mla_g_ckv': _jnp.float32, 'mla_w_uq': _jnp.float32, 'mla_w_ukv': _jnp.float32, 'mla_g_q': _jnp.float32, 'mla_g_k': _jnp.float32, 'na_g_q': _jnp.float32, 'na_g_k': _jnp.float32, 'na_rpb': _jnp.float32}
MOMENT_SCALE = {'c_ctx': 1.414371e+01, 'w_mod': 3.124323e+01, 'b_mod': 5.835544e+01, 'g_norm1': 1.268426e+01, 'g_norm2': 1.103945e+02, 'w_ff1': 2.006566e+01, 'w_ff2': 4.282019e+01, 'e_w_in': 1.988164e+01, 'e_w_out': 2.245283e+01, 'e_g_q': 4.381644e+00, 'e_g_k': 4.543328e+00, 'ssm_lam_re': 8.256228e+00, 'ssm_lam_im': 5.851678e+00, 'ssm_log_dt': 6.749905e+01, 'ssm_b_re': 3.338147e+00, 'ssm_b_im': 3.804732e+00, 'ssm_c_re': 1.305256e+00, 'ssm_c_im': 1.498311e+00, 'ssm_d': 1.211512e+01, 'ssm_w_glu': 4.006399e+00, 'ssm_b_glu': 6.489610e+00, 'o_w_in': 1.224737e+01, 'o_w_out': 1.838467e+01, 'mla_g_cq': 1.640806e+00, 'mla_g_ckv': 2.440757e+01, 'mla_w_uq': 1.337600e+00, 'mla_w_ukv': 1.101339e+01, 'mla_g_q': 3.197201e+00, 'mla_g_k': 3.149509e+00, 'na_g_q': 7.859380e+00, 'na_g_k': 7.872866e+00, 'na_rpb': 1.598115e-01}


def _to_microbatches(a, axis):
    t = _jnp.moveaxis(a, axis, 0)
    t = t.reshape((N_MICROBATCH, t.shape[0] // N_MICROBATCH) + t.shape[1:])
    return _jnp.moveaxis(t, 1, axis + 1)


def setup_inputs(seed: int = 0) -> dict:
    inp = _fwd_setup_inputs(seed)
    key = _jax.random.fold_in(_jax.random.key(seed), 7919)
    shape, _ = _output_shape()
    out = dict(inp)
    out["loss_target"] = _jax.random.normal(_jax.random.fold_in(key, 0), shape, _jnp.float32)
    for i, name in enumerate(TWIN_WEIGHTS):
        w = inp[name].astype(_jnp.float32)
        if MOMENT_SCALE is None:
            s = _jnp.sqrt(_jnp.mean(_jnp.square(w)) + 1e-30)
        else:
            s = MOMENT_SCALE[name]
        km, kv = _jax.random.split(_jax.random.fold_in(key, i + 1))
        out[name] = w
        out["m_" + name] = s * _jax.random.normal(km, w.shape, _jnp.float32)
        out["v_" + name] = (s * s) * _jax.random.uniform(kv, w.shape, _jnp.float32, 0.5, 1.5)
    if N_MICROBATCH > 1:
        for name, axis in PER_EXAMPLE_BATCH_AXIS.items():
            out[name] = _to_microbatches(out[name], axis)
    return {'x': out['x'], 'c': out['c'], 'ctx': out['ctx'], 'c_ctx': out['c_ctx'], 'w_mod': out['w_mod'], 'b_mod': out['b_mod'], 'g_norm1': out['g_norm1'], 'g_norm2': out['g_norm2'], 'w_ff1': out['w_ff1'], 'w_ff2': out['w_ff2'], 'e_w_in': out['e_w_in'], 'e_w_out': out['e_w_out'], 'e_g_q': out['e_g_q'], 'e_g_k': out['e_g_k'], 'ssm_lam_re': out['ssm_lam_re'], 'ssm_lam_im': out['ssm_lam_im'], 'ssm_log_dt': out['ssm_log_dt'], 'ssm_b_re': out['ssm_b_re'], 'ssm_b_im': out['ssm_b_im'], 'ssm_c_re': out['ssm_c_re'], 'ssm_c_im': out['ssm_c_im'], 'ssm_d': out['ssm_d'], 'ssm_w_glu': out['ssm_w_glu'], 'ssm_b_glu': out['ssm_b_glu'], 'o_w_in': out['o_w_in'], 'o_w_out': out['o_w_out'], 'mla_g_cq': out['mla_g_cq'], 'mla_g_ckv': out['mla_g_ckv'], 'mla_w_uq': out['mla_w_uq'], 'mla_w_ukv': out['mla_w_ukv'], 'mla_g_q': out['mla_g_q'], 'mla_g_k': out['mla_g_k'], 'na_g_q': out['na_g_q'], 'na_g_k': out['na_g_k'], 'na_rpb': out['na_rpb'], 'loss_target': out['loss_target'], 'm_c_ctx': out['m_c_ctx'], 'm_w_mod': out['m_w_mod'], 'm_b_mod': out['m_b_mod'], 'm_g_norm1': out['m_g_norm1'], 'm_g_norm2': out['m_g_norm2'], 'm_w_ff1': out['m_w_ff1'], 'm_w_ff2': out['m_w_ff2'], 'm_e_w_in': out['m_e_w_in'], 'm_e_w_out': out['m_e_w_out'], 'm_e_g_q': out['m_e_g_q'], 'm_e_g_k': out['m_e_g_k'], 'm_ssm_lam_re': out['m_ssm_lam_re'], 'm_ssm_lam_im': out['m_ssm_lam_im'], 'm_ssm_log_dt': out['m_ssm_log_dt'], 'm_ssm_b_re': out['m_ssm_b_re'], 'm_ssm_b_im': out['m_ssm_b_im'], 'm_ssm_c_re': out['m_ssm_c_re'], 'm_ssm_c_im': out['m_ssm_c_im'], 'm_ssm_d': out['m_ssm_d'], 'm_ssm_w_glu': out['m_ssm_w_glu'], 'm_ssm_b_glu': out['m_ssm_b_glu'], 'm_o_w_in': out['m_o_w_in'], 'm_o_w_out': out['m_o_w_out'], 'm_mla_g_cq': out['m_mla_g_cq'], 'm_mla_g_ckv': out['m_mla_g_ckv'], 'm_mla_w_uq': out['m_mla_w_uq'], 'm_mla_w_ukv': out['m_mla_w_ukv'], 'm_mla_g_q': out['m_mla_g_q'], 'm_mla_g_k': out['m_mla_g_k'], 'm_na_g_q': out['m_na_g_q'], 'm_na_g_k': out['m_na_g_k'], 'm_na_rpb': out['m_na_rpb'], 'v_c_ctx': out['v_c_ctx'], 'v_w_mod': out['v_w_mod'], 'v_b_mod': out['v_b_mod'], 'v_g_norm1': out['v_g_norm1'], 'v_g_norm2': out['v_g_norm2'], 'v_w_ff1': out['v_w_ff1'], 'v_w_ff2': out['v_w_ff2'], 'v_e_w_in': out['v_e_w_in'], 'v_e_w_out': out['v_e_w_out'], 'v_e_g_q': out['v_e_g_q'], 'v_e_g_k': out['v_e_g_k'], 'v_ssm_lam_re': out['v_ssm_lam_re'], 'v_ssm_lam_im': out['v_ssm_lam_im'], 'v_ssm_log_dt': out['v_ssm_log_dt'], 'v_ssm_b_re': out['v_ssm_b_re'], 'v_ssm_b_im': out['v_ssm_b_im'], 'v_ssm_c_re': out['v_ssm_c_re'], 'v_ssm_c_im': out['v_ssm_c_im'], 'v_ssm_d': out['v_ssm_d'], 'v_ssm_w_glu': out['v_ssm_w_glu'], 'v_ssm_b_glu': out['v_ssm_b_glu'], 'v_o_w_in': out['v_o_w_in'], 'v_o_w_out': out['v_o_w_out'], 'v_mla_g_cq': out['v_mla_g_cq'], 'v_mla_g_ckv': out['v_mla_g_ckv'], 'v_mla_w_uq': out['v_mla_w_uq'], 'v_mla_w_ukv': out['v_mla_w_ukv'], 'v_mla_g_q': out['v_mla_g_q'], 'v_mla_g_k': out['v_mla_g_k'], 'v_na_g_q': out['v_na_g_q'], 'v_na_g_k': out['v_na_g_k'], 'v_na_rpb': out['v_na_rpb']}


def _loss(weights, diff, rest, loss_target):
    with _jax.named_scope("forward"):
        args = {**rest, TWIN_DIFF_INPUT: diff, **{k: w.astype(_WEIGHT_DTYPES[k]) for k, w in weights.items()}}
        y = _forward(args)
    with _jax.named_scope("loss_head"):
        err = _jnp.square(y.astype(_jnp.float32) - loss_target)
        return 0.5 * _jnp.sum(_jnp.mean(err, axis=-1)) if err.ndim else 0.5 * err


def _adamw(w, g, m, v):
    m = ADAM_B1 * m + (1.0 - ADAM_B1) * g
    v = ADAM_B2 * v + (1.0 - ADAM_B2) * _jnp.square(g)
    m_hat = m / (1.0 - ADAM_B1 ** ADAM_STEP)
    v_hat = v / (1.0 - ADAM_B2 ** ADAM_STEP)
    delta = -ADAM_LR * (m_hat / (_jnp.sqrt(v_hat) + ADAM_EPS) + ADAM_WD * w)
    return delta, m, v


def reference(x, c, ctx, c_ctx, w_mod, b_mod, g_norm1, g_norm2, w_ff1, w_ff2, e_w_in, e_w_out, e_g_q, e_g_k, ssm_lam_re, ssm_lam_im, ssm_log_dt, ssm_b_re, ssm_b_im, ssm_c_re, ssm_c_im, ssm_d, ssm_w_glu, ssm_b_glu, o_w_in, o_w_out, mla_g_cq, mla_g_ckv, mla_w_uq, mla_w_ukv, mla_g_q, mla_g_k, na_g_q, na_g_k, na_rpb, loss_target, m_c_ctx, m_w_mod, m_b_mod, m_g_norm1, m_g_norm2, m_w_ff1, m_w_ff2, m_e_w_in, m_e_w_out, m_e_g_q, m_e_g_k, m_ssm_lam_re, m_ssm_lam_im, m_ssm_log_dt, m_ssm_b_re, m_ssm_b_im, m_ssm_c_re, m_ssm_c_im, m_ssm_d, m_ssm_w_glu, m_ssm_b_glu, m_o_w_in, m_o_w_out, m_mla_g_cq, m_mla_g_ckv, m_mla_w_uq, m_mla_w_ukv, m_mla_g_q, m_mla_g_k, m_na_g_q, m_na_g_k, m_na_rpb, v_c_ctx, v_w_mod, v_b_mod, v_g_norm1, v_g_norm2, v_w_ff1, v_w_ff2, v_e_w_in, v_e_w_out, v_e_g_q, v_e_g_k, v_ssm_lam_re, v_ssm_lam_im, v_ssm_log_dt, v_ssm_b_re, v_ssm_b_im, v_ssm_c_re, v_ssm_c_im, v_ssm_d, v_ssm_w_glu, v_ssm_b_glu, v_o_w_in, v_o_w_out, v_mla_g_cq, v_mla_g_ckv, v_mla_w_uq, v_mla_w_ukv, v_mla_g_q, v_mla_g_k, v_na_g_q, v_na_g_k, v_na_rpb):
    given = dict(x=x, c=c, ctx=ctx, c_ctx=c_ctx, w_mod=w_mod, b_mod=b_mod, g_norm1=g_norm1, g_norm2=g_norm2, w_ff1=w_ff1, w_ff2=w_ff2, e_w_in=e_w_in, e_w_out=e_w_out, e_g_q=e_g_q, e_g_k=e_g_k, ssm_lam_re=ssm_lam_re, ssm_lam_im=ssm_lam_im, ssm_log_dt=ssm_log_dt, ssm_b_re=ssm_b_re, ssm_b_im=ssm_b_im, ssm_c_re=ssm_c_re, ssm_c_im=ssm_c_im, ssm_d=ssm_d, ssm_w_glu=ssm_w_glu, ssm_b_glu=ssm_b_glu, o_w_in=o_w_in, o_w_out=o_w_out, mla_g_cq=mla_g_cq, mla_g_ckv=mla_g_ckv, mla_w_uq=mla_w_uq, mla_w_ukv=mla_w_ukv, mla_g_q=mla_g_q, mla_g_k=mla_g_k, na_g_q=na_g_q, na_g_k=na_g_k, na_rpb=na_rpb, loss_target=loss_target, m_c_ctx=m_c_ctx, m_w_mod=m_w_mod, m_b_mod=m_b_mod, m_g_norm1=m_g_norm1, m_g_norm2=m_g_norm2, m_w_ff1=m_w_ff1, m_w_ff2=m_w_ff2, m_e_w_in=m_e_w_in, m_e_w_out=m_e_w_out, m_e_g_q=m_e_g_q, m_e_g_k=m_e_g_k, m_ssm_lam_re=m_ssm_lam_re, m_ssm_lam_im=m_ssm_lam_im, m_ssm_log_dt=m_ssm_log_dt, m_ssm_b_re=m_ssm_b_re, m_ssm_b_im=m_ssm_b_im, m_ssm_c_re=m_ssm_c_re, m_ssm_c_im=m_ssm_c_im, m_ssm_d=m_ssm_d, m_ssm_w_glu=m_ssm_w_glu, m_ssm_b_glu=m_ssm_b_glu, m_o_w_in=m_o_w_in, m_o_w_out=m_o_w_out, m_mla_g_cq=m_mla_g_cq, m_mla_g_ckv=m_mla_g_ckv, m_mla_w_uq=m_mla_w_uq, m_mla_w_ukv=m_mla_w_ukv, m_mla_g_q=m_mla_g_q, m_mla_g_k=m_mla_g_k, m_na_g_q=m_na_g_q, m_na_g_k=m_na_g_k, m_na_rpb=m_na_rpb, v_c_ctx=v_c_ctx, v_w_mod=v_w_mod, v_b_mod=v_b_mod, v_g_norm1=v_g_norm1, v_g_norm2=v_g_norm2, v_w_ff1=v_w_ff1, v_w_ff2=v_w_ff2, v_e_w_in=v_e_w_in, v_e_w_out=v_e_w_out, v_e_g_q=v_e_g_q, v_e_g_k=v_e_g_k, v_ssm_lam_re=v_ssm_lam_re, v_ssm_lam_im=v_ssm_lam_im, v_ssm_log_dt=v_ssm_log_dt, v_ssm_b_re=v_ssm_b_re, v_ssm_b_im=v_ssm_b_im, v_ssm_c_re=v_ssm_c_re, v_ssm_c_im=v_ssm_c_im, v_ssm_d=v_ssm_d, v_ssm_w_glu=v_ssm_w_glu, v_ssm_b_glu=v_ssm_b_glu, v_o_w_in=v_o_w_in, v_o_w_out=v_o_w_out, v_mla_g_cq=v_mla_g_cq, v_mla_g_ckv=v_mla_g_ckv, v_mla_w_uq=v_mla_w_uq, v_mla_w_ukv=v_mla_w_ukv, v_mla_g_q=v_mla_g_q, v_mla_g_k=v_mla_g_k, v_na_g_q=v_na_g_q, v_na_g_k=v_na_g_k, v_na_rpb=v_na_rpb)
    weights = {n: given[n] for n in TWIN_WEIGHTS}
    shared = {n: given[n] for n in SHARED_INPUTS}
    per_example = {n: given[n] for n in ['x', 'c', 'ctx']}
    grad_fn = _jax.value_and_grad(_loss, argnums=(0, 1))

    def one_microbatch(ex, loss_target):
        ex = dict(ex)
        diff = ex.pop(TWIN_DIFF_INPUT)
        return grad_fn(weights, diff, {**shared, **ex}, loss_target)

    if N_MICROBATCH == 1:
        loss, (grad_w, grad_x) = one_microbatch(per_example, given["loss_target"])
    else:
        def body(carry, xs):
            loss_sum, grad_sum = carry
            l_k, (gw_k, gx_k) = one_microbatch(xs[0], xs[1])
            with _jax.named_scope("update"):
                return (loss_sum + l_k, _jax.tree.map(_jnp.add, grad_sum, gw_k)), gx_k

        init = (_jnp.zeros((), _jnp.float32), _jax.tree.map(_jnp.zeros_like, weights))
        (loss, grad_w), grad_x = _jax.lax.scan(body, init, (per_example, given["loss_target"]))
    with _jax.named_scope("update"):
        delta_w, new_m, new_v = {}, {}, {}
        for n in TWIN_WEIGHTS:
            delta_w[n], new_m[n], new_v[n] = _adamw(weights[n], grad_w[n], given["m_" + n], given["v_" + n])
    return (loss, grad_x, *[grad_w[n] for n in TWIN_WEIGHTS], *[delta_w[n] for n in TWIN_WEIGHTS],
            *[new_m[n] for n in TWIN_WEIGHTS], *[new_v[n] for n in TWIN_WEIGHTS])
```

```python
import functools
import math

import jax
import jax.numpy as jnp
from jax import lax
from jax.experimental import pallas as pl
from jax.experimental.pallas import tpu as pltpu

F32, BF16 = jnp.float32, jnp.bfloat16
MESH = pl.DeviceIdType.MESH
N_DEV = 8
VMEM_LIMIT_BYTES = 56 * 1024 * 1024
LANES = 128
PACK_BIG = (1024, 32)
PACK_SMALL = (128, 8)

GRID_W = 64
HEAD_DIM = 64
ROPE_BASE = 10000.0
EPS = 1e-6
N_MOD = 6
GQA_Q_HEADS, GQA_KV_HEADS = 12, 4
GQA_Q_W, GQA_KV_W = GQA_Q_HEADS * HEAD_DIM, GQA_KV_HEADS * HEAD_DIM
SSM_WIDTH, SSM_GROUP, SSM_GROUPS, SSM_STATE = 256, 16, 16, 64
MLA_HEADS, MLA_Q_RANK, MLA_KV_RANK, MLA_NOPE, MLA_ROPE, MLA_V = 8, 512, 256, 64, 32, 64
MLA_QK = MLA_NOPE + MLA_ROPE
NA_HEADS, NA_WIN_R, NA_WIN_C = 8, 8, 16
NA_W = NA_HEADS * HEAD_DIM
ODD_IN_W = MLA_Q_RANK + MLA_KV_RANK + MLA_ROPE + 3 * NA_W
ODD_NA_AT = 1024
ODD_IN_PAD = ODD_NA_AT + 3 * NA_W
NEG = -1e30

ADAM_LR, ADAM_B1, ADAM_B2, ADAM_EPS, ADAM_WD, ADAM_STEP = 0.001, 0.9, 0.999, 1e-08, 0.01, 10

FWD_PARAMS = ['x', 'c', 'ctx', 'c_ctx', 'w_mod', 'b_mod', 'g_norm1', 'g_norm2', 'w_ff1', 'w_ff2', 'e_w_in', 'e_w_out',
              'e_g_q', 'e_g_k', 'ssm_lam_re', 'ssm_lam_im', 'ssm_log_dt', 'ssm_b_re', 'ssm_b_im', 'ssm_c_re', 'ssm_c_im',
              'ssm_d', 'ssm_w_glu', 'ssm_b_glu', 'o_w_in', 'o_w_out', 'mla_g_cq', 'mla_g_ckv', 'mla_w_uq', 'mla_w_ukv',
              'mla_g_q', 'mla_g_k', 'na_g_q', 'na_g_k', 'na_rpb']
WEIGHTS = FWD_PARAMS[3:]
BIG = {'w_ff1': 2, 'w_ff2': 1, 'e_w_in': 2, 'e_w_out': 1, 'o_w_in': 2, 'o_w_out': 1, 'mla_w_uq': 2, 'mla_w_ukv': 2,
       'ssm_w_glu': 1}
SHARDED_SMALL = {'mla_g_cq': 1, 'mla_g_ckv': 1}
REPLICATED = [n for n in WEIGHTS if n not in BIG and n not in SHARDED_SMALL and n not in ('w_mod', 'c_ctx', 'b_mod')]


def _tile(dim, prefs):
    for p in prefs:
        if dim >= p and dim % p == 0:
            return p
    return dim


def _params(*sem):
    return pltpu.CompilerParams(dimension_semantics=sem, vmem_limit_bytes=VMEM_LIMIT_BYTES)


def _dot_nt(a, b):
    return lax.dot_general(a, b, (((1,), (1,)), ((), ())), preferred_element_type=F32)


def _dot_tn(a, b):
    return lax.dot_general(a, b, (((0,), (0,)), ((), ())), preferred_element_type=F32)


def _dot(a, b):
    return jnp.dot(a, b, preferred_element_type=F32)


def _matmul(a, b, kind, out_dtype):
    a, b = a.astype(BF16), b.astype(BF16)
    if kind == 'nn':
        (m, kd), n = a.shape, b.shape[1]
    elif kind == 'nt':
        (m, kd), n = a.shape, b.shape[0]
    else:
        (kd, m), n = a.shape, b.shape[1]
    tm = _tile(m, (768, 512, 256, 128))
    tn = _tile(n, (1024, 768, 512, 256, 128))
    tk = _tile(kd, (1024, 512, 256, 128))
    nk = kd // tk
    dn = {'nn': (((1,), (0,)), ((), ())), 'nt': (((1,), (1,)), ((), ())), 'tn': (((0,), (0,)), ((), ()))}[kind]

    def body(a_ref, b_ref, o_ref, acc_ref):
        k = pl.program_id(2)

        @pl.when(k == 0)
        def _():
            acc_ref[...] = jnp.zeros_like(acc_ref)

        acc_ref[...] += lax.dot_general(a_ref[...], b_ref[...], dn, preferred_element_type=F32)

        @pl.when(k == nk - 1)
        def _():
            o_ref[...] = acc_ref[...].astype(o_ref.dtype)

    a_spec = pl.BlockSpec((tk, tm), lambda i, j, k: (k, i)) if kind == 'tn' else pl.BlockSpec((tm, tk), lambda i, j, k: (i, k))
    b_spec = pl.BlockSpec((tn, tk), lambda i, j, k: (j, k)) if kind == 'nt' else pl.BlockSpec((tk, tn), lambda i, j, k: (k, j))
    return pl.pallas_call(
        body, name='mm_' + kind, grid=(m // tm, n // tn, nk),
        out_shape=jax.ShapeDtypeStruct((m, n), out_dtype),
        in_specs=[a_spec, b_spec], out_specs=pl.BlockSpec((tm, tn), lambda i, j, k: (i, j)),
        scratch_shapes=[pltpu.VMEM((tm, tn), F32)],
        compiler_params=_params('parallel', 'parallel', 'arbitrary'),
    )(a, b)


@jax.custom_vjp
def linear(a, w):
    return _matmul(a, w, 'nn', F32)


def _linear_fwd(a, w):
    ab = a.astype(BF16)
    return _matmul(ab, w, 'nn', F32), (ab, w)


def _linear_bwd(res, g):
    ab, w = res
    gb = g.astype(BF16)
    return _matmul(gb, w, 'nt', F32), _matmul(ab, gb, 'tn', w.dtype)


linear.defvjp(_linear_fwd, _linear_bwd)


def _softmax_rows(s):
    m = jnp.max(s, axis=-1, keepdims=True)
    e = jnp.exp(s - m)
    return e * (1.0 / jnp.sum(e, axis=-1, keepdims=True))


def _attn_specs(q, k, v, bq):
    _, h, nq, dq = q.shape
    _, hk, nk, dv = v.shape
    g = h // hk
    q_spec = pl.BlockSpec((None, None, bq, dq), lambda b, j, gi, i: (b, j * g + gi, i, 0))
    k_spec = pl.BlockSpec((None, None, nk, dq), lambda b, j, gi, i: (b, j, 0, 0))
    v_spec = pl.BlockSpec((None, None, nk, dv), lambda b, j, gi, i: (b, j, 0, 0))
    o_spec = pl.BlockSpec((None, None, bq, dv), lambda b, j, gi, i: (b, j * g + gi, i, 0))
    return (q.shape[0], hk, g, nq // bq), q_spec, k_spec, v_spec, o_spec


def _attn_fwd_call(q, k, v, scale):
    bq = _tile(q.shape[2], (256, 128))
    grid, q_spec, k_spec, v_spec, o_spec = _attn_specs(q, k, v, bq)

    def body(q_ref, k_ref, v_ref, o_ref):
        p = _softmax_rows(_dot_nt(q_ref[...], k_ref[...]) * scale)
        o_ref[...] = _dot(p.astype(BF16), v_ref[...])

    return pl.pallas_call(
        body, name='attn_fwd', grid=grid, out_shape=jax.ShapeDtypeStruct(q.shape[:3] + (v.shape[3],), F32),
        in_specs=[q_spec, k_spec, v_spec], out_specs=o_spec,
        compiler_params=_params('parallel', 'parallel', 'arbitrary', 'arbitrary'),
    )(q, k, v)


def _attn_bwd_call(q, k, v, do, scale):
    bq = _tile(q.shape[2], (256, 128))
    grid, q_spec, k_spec, v_spec, o_spec = _attn_specs(q, k, v, bq)

    def body(q_ref, k_ref, v_ref, do_ref, dq_ref, dk_ref, dv_ref):
        @pl.when((pl.program_id(2) == 0) & (pl.program_id(3) == 0))
        def _():
            dk_ref[...] = jnp.zeros_like(dk_ref)
            dv_ref[...] = jnp.zeros_like(dv_ref)

        qb, kb, vb, dob = q_ref[...], k_ref[...], v_ref[...], do_ref[...]
        p = _softmax_rows(_dot_nt(qb, kb) * scale)
        dp = _dot_nt(dob, vb)
        ds = p * (dp - jnp.sum(p * dp, axis=-1, keepdims=True))
        dsb = (ds * scale).astype(BF16)
        dq_ref[...] = _dot(dsb, kb)
        dk_ref[...] += _dot_tn(dsb, qb)
        dv_ref[...] += _dot_tn(p.astype(BF16), dob)

    return pl.pallas_call(
        body, name='attn_bwd', grid=grid,
        out_shape=(jax.ShapeDtypeStruct(q.shape, F32), jax.ShapeDtypeStruct(k.shape, F32), jax.ShapeDtypeStruct(v.shape, F32)),
        in_specs=[q_spec, k_spec, v_spec, o_spec], out_specs=(q_spec, k_spec, v_spec),
        compiler_params=_params('parallel', 'parallel', 'arbitrary', 'arbitrary'),
    )(q, k, v, do)


@functools.partial(jax.custom_vjp, nondiff_argnums=(3,))
def attention(q, k, v, scale):
    return _attn_fwd_call(q.astype(BF16), k.astype(BF16), v.astype(BF16), scale)


def _attention_fwd(q, k, v, scale):
    qb, kb, vb = q.astype(BF16), k.astype(BF16), v.astype(BF16)
    return _attn_fwd_call(qb, kb, vb, scale), (qb, kb, vb)


def _attention_bwd(scale, res, g):
    return _attn_bwd_call(*res, g.astype(BF16), scale)


attention.defvjp(_attention_fwd, _attention_bwd)


def _na_window(r, rows):
    start = jnp.clip(r - NA_WIN_R // 2, 0, rows - NA_WIN_R)
    return start, r - start


def _na_scores(q, kw, kc, bias, scale):
    s1 = _dot_nt(q, kw) * scale + bias
    s2 = _dot_nt(q, kc) * scale
    m = jnp.maximum(jnp.max(s1, axis=-1, keepdims=True), jnp.max(s2, axis=-1, keepdims=True))
    e1, e2 = jnp.exp(s1 - m), jnp.exp(s2 - m)
    inv = 1.0 / (jnp.sum(e1, axis=-1, keepdims=True) + jnp.sum(e2, axis=-1, keepdims=True))
    return e1 * inv, e2 * inv


def _na_specs(q, kc):
    _, _, n, d = q.shape
    c = kc.shape[2]
    win = NA_WIN_R * GRID_W
    tok = pl.BlockSpec((None, None, n, d), lambda b, h: (b, h, 0, 0))
    ctx = pl.BlockSpec((None, None, c, d), lambda b, h: (b, h, 0, 0))
    bias = pl.BlockSpec((None, NA_WIN_R, GRID_W, win), lambda b, h: (h, 0, 0, 0))
    dbias = pl.BlockSpec((None, None, NA_WIN_R, GRID_W, win), lambda b, h: (b, h, 0, 0, 0))
    return tok, ctx, bias, dbias


def _na_fwd_call(q, k, v, kc, vc, bias, scale):
    b, h, n, d = q.shape
    rows, win = n // GRID_W, NA_WIN_R * GRID_W
    tok, ctx, bias_spec, _ = _na_specs(q, kc)

    def body(q_ref, k_ref, v_ref, kc_ref, vc_ref, b_ref, o_ref):
        def row(r, carry):
            start, off = _na_window(r, rows)
            at = pl.ds(pl.multiple_of(r * GRID_W, GRID_W), GRID_W)
            wat = pl.ds(pl.multiple_of(start * GRID_W, GRID_W), win)
            p1, p2 = _na_scores(q_ref[at, :], k_ref[wat, :], kc_ref[...], b_ref[off], scale)
            o_ref[at, :] = _dot(p1.astype(BF16), v_ref[wat, :]) + _dot(p2.astype(BF16), vc_ref[...])
            return carry

        lax.fori_loop(0, rows, row, 0)

    return pl.pallas_call(
        body, name='na_fwd', grid=(b, h), out_shape=jax.ShapeDtypeStruct(q.shape, F32),
        in_specs=[tok, tok, tok, ctx, ctx, bias_spec], out_specs=tok,
        compiler_params=_params('parallel', 'parallel'),
    )(q, k, v, kc, vc, bias)


def _na_bwd_call(q, k, v, kc, vc, bias, do, scale):
    b, h, n, d = q.shape
    rows, win = n // GRID_W, NA_WIN_R * GRID_W
    tok, ctx, bias_spec, dbias_spec = _na_specs(q, kc)

    def body(q_ref, k_ref, v_ref, kc_ref, vc_ref, b_ref, do_ref, dq_ref, dk_ref, dv_ref, dkc_ref, dvc_ref, db_ref):
        for ref in (dk_ref, dv_ref, dkc_ref, dvc_ref, db_ref):
            ref[...] = jnp.zeros_like(ref)

        def row(r, carry):
            start, off = _na_window(r, rows)
            at = pl.ds(pl.multiple_of(r * GRID_W, GRID_W), GRID_W)
            wat = pl.ds(pl.multiple_of(start * GRID_W, GRID_W), win)
            qb, kw, vw, dob = q_ref[at, :], k_ref[wat, :], v_ref[wat, :], do_ref[at, :]
            kcb, vcb = kc_ref[...], vc_ref[...]
            p1, p2 = _na_scores(qb, kw, kcb, b_ref[off], scale)
            dp1, dp2 = _dot_nt(dob, vw), _dot_nt(dob, vcb)
            delta = jnp.sum(p1 * dp1, axis=-1, keepdims=True) + jnp.sum(p2 * dp2, axis=-1, keepdims=True)
            ds1, ds2 = p1 * (dp1 - delta), p2 * (dp2 - delta)
            db_ref[off] += ds1
            ds1b, ds2b = (ds1 * scale).astype(BF16), (ds2 * scale).astype(BF16)
            dq_ref[at, :] = _dot(ds1b, kw) + _dot(ds2b, kcb)
            dk_ref[wat, :] += _dot_tn(ds1b, qb)
            dv_ref[wat, :] += _dot_tn(p1.astype(BF16), dob)
            dkc_ref[...] += _dot_tn(ds2b, qb)
            dvc_ref[...] += _dot_tn(p2.astype(BF16), dob)
            return carry

        lax.fori_loop(0, rows, row, 0)

    f = lambda a: jax.ShapeDtypeStruct(a.shape, F32)
    return pl.pallas_call(
        body, name='na_bwd', grid=(b, h),
        out_shape=(f(q), f(k), f(v), f(kc), f(vc), jax.ShapeDtypeStruct((b,) + bias.shape, F32)),
        in_specs=[tok, tok, tok, ctx, ctx, bias_spec, tok], out_specs=(tok, tok, tok, ctx, ctx, dbias_spec),
        compiler_params=_params('parallel', 'parallel'),
    )(q, k, v, kc, vc, bias, do)


@functools.partial(jax.custom_vjp, nondiff_argnums=(6,))
def na_attention(q, k, v, kc, vc, bias, scale):
    return _na_fwd_call(q.astype(BF16), k.astype(BF16), v.astype(BF16), kc.astype(BF16), vc.astype(BF16), bias, scale)


def _na_attention_fwd(q, k, v, kc, vc, bias, scale):
    res = (q.astype(BF16), k.astype(BF16), v.astype(BF16), kc.astype(BF16), vc.astype(BF16), bias)
    return _na_fwd_call(*res, scale), res


def _na_attention_bwd(scale, res, g):
    dq, dk, dv, dkc, dvc, db = _na_bwd_call(*res, g.astype(BF16), scale)
    return dq, dk, dv, dkc, dvc, jnp.sum(db, axis=0)


na_attention.defvjp(_na_attention_fwd, _na_attention_bwd)


def na_bias_table(rpb):
    qcol = jnp.arange(GRID_W)
    kcol = jnp.arange(GRID_W)
    cstart = jnp.clip(qcol - NA_WIN_C // 2, 0, GRID_W - NA_WIN_C)
    inside = (kcol[None, :] >= cstart[:, None]) & (kcol[None, :] < cstart[:, None] + NA_WIN_C)
    cidx = jnp.clip(kcol[None, :] - qcol[:, None] + (NA_WIN_C - 1), 0, 2 * NA_WIN_C - 2)
    ridx = jnp.arange(NA_WIN_R)[None, :] - jnp.arange(NA_WIN_R)[:, None] + (NA_WIN_R - 1)
    t = rpb[:, ridx][:, :, :, cidx]
    t = jnp.where(inside[None, None, None], t, NEG)
    return jnp.transpose(t, (0, 1, 3, 2, 4)).reshape(rpb.shape[0], NA_WIN_R, GRID_W, NA_WIN_R * GRID_W)


S5_HALF = SSM_WIDTH // 2
S5_LANES = (SSM_GROUPS // 2) * SSM_STATE
S5_Q = S5_LANES // LANES


def _s5_tiles(a):
    r = a.shape[0]
    return jnp.transpose(a.reshape(r, S5_Q, LANES), (1, 0, 2)).reshape(S5_Q * r, LANES)


def _s5_untiles(a):
    r = a.shape[0] // S5_Q
    return jnp.transpose(a.reshape(S5_Q, r, LANES), (1, 0, 2)).reshape(r, S5_LANES)


def _s5_put(ref, r, rr, tc, val):
    for q in range(S5_Q):
        ref[pl.ds((q * rr + r) * tc, tc), :] = val[:, q * LANES:(q + 1) * LANES]


def _s5_get(ref, r, rr, tc):
    return jnp.concatenate([ref[pl.ds((q * rr + r) * tc, tc), :] for q in range(S5_Q)], axis=1)


def _s5_fwd_call(u, a_re, a_im, b_re, b_im, c_re, c_im):
    rr, t_len, _ = u.shape
    sets = b_re.shape[0]
    per = rr // sets
    tc = _tile(t_len, (256, 128))
    nt = t_len // tc
    qr = S5_Q * rr

    def body(u_ref, ar_ref, ai_ref, br_ref, bi_ref, cr_ref, ci_ref, y_ref, hr_ref, hi_ref, sr_ref, si_ref):
        @pl.when(pl.program_id(0) == 0)
        def _():
            sr_ref[...] = jnp.zeros_like(sr_ref)
            si_ref[...] = jnp.zeros_like(si_ref)

        for r in range(rr):
            ub = u_ref[r]
            _s5_put(hr_ref, r, rr, tc, _dot(ub, br_ref[r // per]))
            _s5_put(hi_ref, r, rr, tc, _dot(ub, bi_ref[r // per]))
        ar, ai = ar_ref[...], ai_ref[...]

        def step(t, carry):
            hr, hi = carry
            at = pl.ds(t, qr, stride=tc)
            nr = ar * hr - ai * hi + hr_ref[at, :]
            ni = ar * hi + ai * hr + hi_ref[at, :]
            hr_ref[at, :] = nr
            hi_ref[at, :] = ni
            return nr, ni

        hr, hi = lax.fori_loop(0, tc, step, (sr_ref[...], si_ref[...]))
        sr_ref[...] = hr
        si_ref[...] = hi
        for r in range(rr):
            y_ref[r] = (_dot(_s5_get(hr_ref, r, rr, tc).astype(BF16), cr_ref[r // per])
                        - _dot(_s5_get(hi_ref, r, rr, tc).astype(BF16), ci_ref[r // per]))

    full = lambda a: pl.BlockSpec(a.shape, lambda i: (0,) * a.ndim)
    h_spec = pl.BlockSpec((None, qr * tc, LANES), lambda i: (i, 0, 0))
    h_shape = jax.ShapeDtypeStruct((nt, qr * tc, LANES), F32)
    a_re, a_im = _s5_tiles(a_re), _s5_tiles(a_im)
    return pl.pallas_call(
        body, name='s5_fwd', grid=(nt,),
        out_shape=(jax.ShapeDtypeStruct((rr, t_len, S5_HALF), F32), h_shape, h_shape),
        in_specs=[pl.BlockSpec((rr, tc, S5_HALF), lambda i: (0, i, 0)), full(a_re), full(a_im), full(b_re), full(b_im),
                  full(c_re), full(c_im)],
        out_specs=(pl.BlockSpec((rr, tc, S5_HALF), lambda i: (0, i, 0)), h_spec, h_spec),
        scratch_shapes=[pltpu.VMEM((qr, LANES), F32), pltpu.VMEM((qr, LANES), F32)],
        compiler_params=_params('arbitrary'),
    )(u, a_re, a_im, b_re, b_im, c_re, c_im)


def _s5_bwd_call(u, a_re, a_im, b_re, b_im, c_re, c_im, h_re, h_im, dy):
    rr, t_len, _ = u.shape
    sets = b_re.shape[0]
    per = rr // sets
    nt, rows, _ = h_re.shape
    qr = S5_Q * rr
    tc = rows // qr

    def body(u_ref, dy_ref, ar_ref, ai_ref, br_ref, bi_ref, cr_ref, ci_ref, hr_ref, hi_ref,
             du_ref, dar_ref, dai_ref, dbr_ref, dbi_ref, dcr_ref, dci_ref, gr_ref, gi_ref, sr_ref, si_ref):
        i = pl.program_id(0)

        @pl.when(i == 0)
        def _():
            for ref in (dar_ref, dai_ref, dbr_ref, dbi_ref, dcr_ref, dci_ref, sr_ref, si_ref):
                ref[...] = jnp.zeros_like(ref)

        for r in range(rr):
            dyb = dy_ref[r]
            _s5_put(gr_ref, r, rr, tc, _dot_nt(dyb, cr_ref[r // per]))
            _s5_put(gi_ref, r, rr, tc, -_dot_nt(dyb, ci_ref[r // per]))
        ar, ai = ar_ref[...], ai_ref[...]
        g_r, g_i = sr_ref[...], si_ref[...]
        last = pl.ds(tc - 1, qr, stride=tc)
        d_r = g_r * hr_ref[last, :] + g_i * hi_ref[last, :]
        d_i = g_i * hr_ref[last, :] - g_r * hi_ref[last, :]

        def advance(t, g_r, g_i):
            at = pl.ds(t, qr, stride=tc)
            n_r = ar * g_r + ai * g_i + gr_ref[at, :]
            n_i = ar * g_i - ai * g_r + gi_ref[at, :]
            gr_ref[at, :] = n_r
            gi_ref[at, :] = n_i
            return n_r, n_i

        def step(k, carry):
            g_r, g_i, d_r, d_i = carry
            t = tc - 1 - k
            g_r, g_i = advance(t, g_r, g_i)
            before = pl.ds(t - 1, qr, stride=tc)
            p_r, p_i = hr_ref[before, :], hi_ref[before, :]
            return g_r, g_i, d_r + g_r * p_r + g_i * p_i, d_i + g_i * p_r - g_r * p_i

        g_r, g_i, d_r, d_i = lax.fori_loop(0, tc - 1, step, (g_r, g_i, d_r, d_i))
        g_r, g_i = advance(0, g_r, g_i)
        sr_ref[...] = g_r
        si_ref[...] = g_i
        dar_ref[...] += d_r
        dai_ref[...] += d_i
        for r in range(rr):
            s = r // per
            ub, dyb = u_ref[r], dy_ref[r]
            grb, gib = _s5_get(gr_ref, r, rr, tc).astype(BF16), _s5_get(gi_ref, r, rr, tc).astype(BF16)
            du_ref[r] = _dot_nt(grb, br_ref[s]) + _dot_nt(gib, bi_ref[s])
            dbr_ref[s] += _dot_tn(ub, grb)
            dbi_ref[s] += _dot_tn(ub, gib)
            dcr_ref[s] += _dot_tn(_s5_get(hr_ref, r, rr, tc).astype(BF16), dyb)
            dci_ref[s] -= _dot_tn(_s5_get(hi_ref, r, rr, tc).astype(BF16), dyb)

    full = lambda a: pl.BlockSpec(a.shape, lambda i: (0,) * a.ndim)
    back = lambda i: nt - 1 - i
    tok = pl.BlockSpec((rr, tc, S5_HALF), lambda i: (0, back(i), 0))
    h_spec = pl.BlockSpec((None, qr * tc, LANES), lambda i: (back(i), 0, 0))
    f = lambda a: jax.ShapeDtypeStruct(a.shape, F32)
    a_re, a_im = _s5_tiles(a_re), _s5_tiles(a_im)
    du, da_re, da_im, db_re, db_im, dc_re, dc_im = pl.pallas_call(
        body, name='s5_bwd', grid=(nt,),
        out_shape=(jax.ShapeDtypeStruct(u.shape, F32), f(a_re), f(a_im), f(b_re), f(b_im), f(c_re), f(c_im)),
        in_specs=[tok, tok, full(a_re), full(a_im), full(b_re), full(b_im), full(c_re), full(c_im), h_spec, h_spec],
        out_specs=(tok, full(a_re), full(a_im), full(b_re), full(b_im), full(c_re), full(c_im)),
        scratch_shapes=[pltpu.VMEM((qr * tc, LANES), F32), pltpu.VMEM((qr * tc, LANES), F32),
                        pltpu.VMEM((qr, LANES), F32), pltpu.VMEM((qr, LANES), F32)],
        compiler_params=_params('arbitrary'),
    )(u, dy, a_re, a_im, b_re, b_im, c_re, c_im, h_re, h_im)
    return du, _s5_untiles(da_re), _s5_untiles(da_im), db_re, db_im, dc_re, dc_im


@jax.custom_vjp
def s5_core(u, a_re, a_im, b_re, b_im, c_re, c_im):
    return _s5_fwd_call(u.astype(BF16), a_re, a_im, b_re.astype(BF16), b_im.astype(BF16), c_re.astype(BF16),
                        c_im.astype(BF16))[0]


def _s5_core_fwd(u, a_re, a_im, b_re, b_im, c_re, c_im):
    args = (u.astype(BF16), a_re, a_im, b_re.astype(BF16), b_im.astype(BF16), c_re.astype(BF16), c_im.astype(BF16))
    y, h_re, h_im = _s5_fwd_call(*args)
    return y, args + (h_re, h_im)


def _s5_core_bwd(res, g):
    return _s5_bwd_call(*res, g.astype(BF16))


s5_core.defvjp(_s5_core_fwd, _s5_core_bwd)


def _exchange(x, gather, hbm, name):
    block = x.shape if gather else x.shape[1:]

    def body(x_ref, out_ref, send_sems, recv_sems, local_sem):
        ix, iy, ic = lax.axis_index('x'), lax.axis_index('y'), lax.axis_index('c')
        me = 4 * ix + 2 * iy + ic

        def flipped(k):
            px = 1 - ix if k & 4 else ix
            py = 1 - iy if k & 2 else iy
            pc = 1 - ic if k & 1 else ic
            return (px, py, pc), 4 * px + 2 * py + pc

        def copy(k, src, dst):
            return pltpu.make_async_remote_copy(src_ref=src, dst_ref=dst, send_sem=send_sems.at[k - 1],
                                                 recv_sem=recv_sems.at[k - 1], device_id=flipped(k)[0], device_id_type=MESH)

        own = pltpu.make_async_copy(x_ref if gather else x_ref.at[me], out_ref.at[me], local_sem)
        own.start()
        sent = []
        for k in range(1, N_DEV):
            src = x_ref if gather else x_ref.at[flipped(k)[1]]
            sent.append(copy(k, src, out_ref.at[me]))
            sent[-1].start()
        for k in range(1, N_DEV):
            src = x_ref if gather else x_ref.at[flipped(k)[1]]
            copy(k, src, out_ref.at[flipped(k)[1]]).wait_recv()
        for cp in sent:
            cp.wait_send()
        own.wait()

    space = pltpu.HBM if hbm else pltpu.VMEM
    return pl.pallas_call(
        body, name=name, out_shape=jax.ShapeDtypeStruct((N_DEV,) + tuple(block), x.dtype),
        in_specs=[pl.BlockSpec(memory_space=space)], out_specs=pl.BlockSpec(memory_space=space),
        scratch_shapes=[pltpu.SemaphoreType.DMA((N_DEV - 1,)), pltpu.SemaphoreType.DMA((N_DEV - 1,)), pltpu.SemaphoreType.DMA],
        compiler_params=pltpu.CompilerParams(vmem_limit_bytes=VMEM_LIMIT_BYTES),
    )(x)


def _sum_slots(x):
    _, r, w = x.shape
    tr = _tile(r, (256, 128, 64, 32))

    def body(x_ref, o_ref):
        acc = x_ref[0].astype(F32)
        for d in range(1, N_DEV):
            acc = acc + x_ref[d].astype(F32)
        o_ref[...] = acc

    return pl.pallas_call(
        body, name='sum_slots', grid=(r // tr,), out_shape=jax.ShapeDtypeStruct((r, w), F32),
        in_specs=[pl.BlockSpec((N_DEV, tr, w), lambda i: (0, i, 0))], out_specs=pl.BlockSpec((tr, w), lambda i: (i, 0)),
        compiler_params=_params('parallel'),
    )(x)


def _all_reduce_small(x):
    g = _exchange(x, True, False, 'gather_small_grads')

    def body(g_ref, o_ref):
        acc = g_ref[0]
        for d in range(1, N_DEV):
            acc = acc + g_ref[d]
        o_ref[...] = acc

    return pl.pallas_call(body, name='sum_small', out_shape=jax.ShapeDtypeStruct(x.shape, F32))(g)


def _adamw_math(w, g, m, v):
    m = ADAM_B1 * m + (1.0 - ADAM_B1) * g
    v = ADAM_B2 * v + (1.0 - ADAM_B2) * (g * g)
    m_hat = m / (1.0 - ADAM_B1 ** ADAM_STEP)
    v_hat = v / (1.0 - ADAM_B2 ** ADAM_STEP)
    return -ADAM_LR * (m_hat / (jnp.sqrt(v_hat) + ADAM_EPS) + ADAM_WD * w), m, v


def _as_rows(a):
    return a.reshape(1, -1) if a.ndim < 2 else a.reshape(-1, a.shape[-1])


def _as_lanes(a):
    return a.reshape(-1, LANES) if a.size % LANES == 0 else a.reshape(1, -1)


def _adamw_big(w, g, m, v):
    shape = w.shape
    w, g, m, v = (_as_rows(a) for a in (w, g, m, v))
    rows, cols = w.shape
    tr = _tile(rows, (512, 256, 128, 64, 32, 16, 8))

    def body(w_ref, g_ref, m_ref, v_ref, d_ref, nm_ref, nv_ref):
        d_ref[...], nm_ref[...], nv_ref[...] = _adamw_math(w_ref[...], g_ref[...], m_ref[...], v_ref[...])

    spec = pl.BlockSpec((tr, cols), lambda i: (i, 0))
    out = pl.pallas_call(
        body, name='adamw', grid=(rows // tr,), out_shape=(jax.ShapeDtypeStruct(w.shape, F32),) * 3,
        in_specs=[spec] * 4, out_specs=(spec,) * 3, compiler_params=_params('parallel'),
    )(w, g, m, v)
    return tuple(o.reshape(shape) for o in out)


def _adamw_small(ws, gs, ms, vs):
    n = len(ws)
    shapes = [w.shape for w in ws]
    flat = [_as_lanes(a) for group in (ws, gs, ms, vs) for a in group]

    def body(*refs):
        ins, outs = refs[:4 * n], refs[4 * n:]
        for i in range(n):
            d, m, v = _adamw_math(ins[i][...], ins[n + i][...], ins[2 * n + i][...], ins[3 * n + i][...])
            outs[i][...], outs[n + i][...], outs[2 * n + i][...] = d, m, v

    out = pl.pallas_call(
        body, name='adamw_small', out_shape=tuple(jax.ShapeDtypeStruct(flat[i].shape, F32) for _ in range(3) for i in range(n)),
    )(*flat)
    return [tuple(out[j * n + i].reshape(shapes[i]) for j in range(3)) for i in range(n)]


def rms_norm(x, g):
    return x * lax.rsqrt(jnp.mean(jnp.square(x), axis=-1, keepdims=True) + EPS) * g


def modulate(x, g, shift, scale):
    return rms_norm(x, g) * (1 + scale) + shift


def rope_tables(n_tokens, rot_dim):
    t = jnp.arange(n_tokens)
    rows = (t // GRID_W).astype(F32)
    cols = (t % GRID_W).astype(F32)
    axis_dim = rot_dim // 2
    freqs = ROPE_BASE ** (-jnp.arange(0, axis_dim, 2, dtype=F32) / axis_dim)
    ang_r, ang_c = rows[:, None] * freqs, cols[:, None] * freqs
    ang = jnp.concatenate([ang_r, ang_r, ang_c, ang_c], axis=-1)
    return jnp.cos(ang), jnp.sin(ang)


def rope(x, cos, sin):
    x1, x2, x3, x4 = jnp.split(x, 4, axis=-1)
    rot = jnp.concatenate([-x2, x1, -x4, x3], axis=-1)
    return x * cos[:, None, :] + rot * sin[:, None, :]


def heads_first(t):
    return jnp.swapaxes(t, 1, 2)


def tokens_matmul(t, w):
    b, n, k = t.shape
    return linear(t.reshape(b * n, k), w).reshape(b, n, w.shape[1])


def s5_discretize(lam_re, lam_im, log_dt, b_re, b_im):
    dt = jnp.exp(log_dt)[:, None]
    mag = jnp.exp(lam_re * dt)
    a_re = mag * jnp.cos(lam_im * dt)
    a_im = mag * jnp.sin(lam_im * dt)
    den = jnp.square(lam_re) + jnp.square(lam_im)
    f_re = ((a_re - 1.0) * lam_re + a_im * lam_im) / den
    f_im = (a_im * lam_re - (a_re - 1.0) * lam_im) / den
    bb_re = f_re[..., None] * b_re - f_im[..., None] * b_im
    bb_im = f_re[..., None] * b_im + f_im[..., None] * b_re
    return a_re, a_im, bb_re, bb_im


def s5_mixer(u_lat, u_ctx, p, j, need_ctx):
    b, n, _ = u_lat.shape
    c = u_ctx.shape[1]
    half_groups = SSM_GROUPS // 2
    eye = jnp.eye(half_groups, dtype=F32)
    a_res, a_ims, b_res, b_ims, c_res, c_ims, seqs = [], [], [], [], [], [], []
    for d in range(2):
        a_re, a_im, bb_re, bb_im = s5_discretize(p['ssm_lam_re'][j, d], p['ssm_lam_im'][j, d], p['ssm_log_dt'][j, d],
                                                 p['ssm_b_re'][j, d], p['ssm_b_im'][j, d])
        for half in range(2):
            grp = slice(half * half_groups, (half + 1) * half_groups)
            a_res.append(a_re[grp].reshape(S5_LANES))
            a_ims.append(a_im[grp].reshape(S5_LANES))
            b_res.append(jnp.einsum('gsp,gh->gphs', bb_re[grp], eye).reshape(S5_HALF, S5_LANES))
            b_ims.append(jnp.einsum('gsp,gh->gphs', bb_im[grp], eye).reshape(S5_HALF, S5_LANES))
            c_res.append(jnp.einsum('gps,gh->gshp', p['ssm_c_re'][j, d][grp], eye).reshape(S5_LANES, S5_HALF))
            c_ims.append(jnp.einsum('gps,gh->gshp', p['ssm_c_im'][j, d][grp], eye).reshape(S5_LANES, S5_HALF))
        flip = (lambda t: t[:, ::-1]) if d == 1 else (lambda t: t)
        seq = jnp.concatenate([flip(u_ctx), flip(u_lat)], axis=1)
        seqs.append(jnp.transpose(seq.reshape(b, c + n, 2, S5_HALF), (2, 0, 1, 3)))
    u = jnp.stack(seqs).reshape(4 * b, c + n, S5_HALF)
    rep = lambda parts: jnp.repeat(jnp.stack(parts), b, axis=0)
    y = s5_core(u, rep(a_res), rep(a_ims), jnp.stack(b_res), jnp.stack(b_ims), jnp.stack(c_res), jnp.stack(c_ims))
    y = jnp.transpose(y.reshape(2, 2, b, c + n, S5_HALF), (0, 2, 3, 1, 4)).reshape(2, b, c + n, SSM_WIDTH)
    d_skip = p['ssm_d'][j]
    y_lat = d_skip * u_lat + y[0, :, c:] + y[1, :, c:][:, ::-1]
    wg, bg = p['ssm_w_glu'][j], p['ssm_b_glu'][j]

    def glu(t):
        t = jax.nn.gelu(t)
        return t * jax.nn.sigmoid(tokens_matmul(t, wg) + bg)

    if not need_ctx:
        return glu(y_lat), None
    y_ctx = d_skip * u_ctx + y[0, :, :c] + y[1, :, :c][:, ::-1]
    return glu(y_lat), glu(y_ctx)


def even_mixer(a_lat, a_ctx, p, j, need_ctx):
    b, n, _ = a_lat.shape
    c = a_ctx.shape[1]
    cos, sin = rope_tables(n, HEAD_DIM)
    proj = tokens_matmul(jnp.concatenate([a_ctx, a_lat], axis=1), p['e_w_in'][j])
    q, k, v, u = jnp.split(proj, [GQA_Q_W, GQA_Q_W + GQA_KV_W, GQA_Q_W + 2 * GQA_KV_W], axis=-1)
    q = rms_norm(q.reshape(b, c + n, GQA_Q_HEADS, HEAD_DIM), p['e_g_q'][j])
    k = rms_norm(k.reshape(b, c + n, GQA_KV_HEADS, HEAD_DIM), p['e_g_k'][j])
    v = v.reshape(b, c + n, GQA_KV_HEADS, HEAD_DIM)
    q_l = rope(q[:, c:], cos, sin)
    k = jnp.concatenate([k[:, :c], rope(k[:, c:], cos, sin)], axis=1)
    scale = HEAD_DIM ** -0.5
    kh, vh = heads_first(k), heads_first(v)
    att_l = heads_first(attention(heads_first(q_l), kh, vh, scale)).reshape(b, n, GQA_Q_W)
    ssm_l, ssm_c = s5_mixer(u[:, c:], u[:, :c], p, j, need_ctx)
    mix_l = jnp.concatenate([att_l, ssm_l], axis=-1)
    if not need_ctx:
        return tokens_matmul(mix_l, p['e_w_out'][j]), None
    att_c = heads_first(attention(heads_first(q[:, :c]), kh[:, :, :c], vh[:, :, :c], scale)).reshape(b, c, GQA_Q_W)
    mix = jnp.concatenate([jnp.concatenate([att_c, ssm_c], axis=-1), mix_l], axis=1)
    out = tokens_matmul(mix, p['e_w_out'][j])
    return out[:, c:], out[:, :c]


def odd_mixer(a_lat, a_ctx, p, j, need_ctx):
    b, n, _ = a_lat.shape
    c = a_ctx.shape[1]
    t = c + n
    cos, sin = rope_tables(n, MLA_ROPE)
    proj = tokens_matmul(jnp.concatenate([a_ctx, a_lat], axis=1), p['o_w_in'][j])
    c1, c2, c3 = MLA_Q_RANK, MLA_Q_RANK + MLA_KV_RANK, MLA_Q_RANK + MLA_KV_RANK + MLA_ROPE
    cq, ckv, kr = proj[..., :c1], proj[..., c1:c2], proj[..., c2:c3]
    nq, nk, nv = jnp.split(proj[..., ODD_NA_AT:], 3, axis=-1)
    q = tokens_matmul(rms_norm(cq, p['mla_g_cq'][j]), p['mla_w_uq'][j]).reshape(b, t, MLA_HEADS, MLA_QK)
    kv = tokens_matmul(rms_norm(ckv, p['mla_g_ckv'][j]), p['mla_w_ukv'][j]).reshape(b, t, MLA_HEADS, MLA_NOPE + MLA_V)
    k = jnp.concatenate([kv[..., :MLA_NOPE], jnp.broadcast_to(kr[:, :, None, :], (b, t, MLA_HEADS, MLA_ROPE))], axis=-1)
    q, k, mv = rms_norm(q, p['mla_g_q'][j]), rms_norm(k, p['mla_g_k'][j]), kv[..., MLA_NOPE:]

    def rope_tail(x):
        tail = jnp.concatenate([x[:, :c, :, MLA_NOPE:], rope(x[:, c:, :, MLA_NOPE:], cos, sin)], axis=1)
        return jnp.concatenate([x[..., :MLA_NOPE], tail], axis=-1)

    q, k = rope_tail(q), rope_tail(k)
    qh, kh, vh = heads_first(q), heads_first(k), heads_first(mv)
    mla_scale = MLA_QK ** -0.5
    mla_l = heads_first(attention(qh[:, :, c:], kh, vh, mla_scale)).reshape(b, n, MLA_HEADS * MLA_V)
    nq = heads_first(rms_norm(nq.reshape(b, t, NA_HEADS, HEAD_DIM), p['na_g_q'][j]))
    nk = heads_first(rms_norm(nk.reshape(b, t, NA_HEADS, HEAD_DIM), p['na_g_k'][j]))
    nv = heads_first(nv.reshape(b, t, NA_HEADS, HEAD_DIM))
    na_scale = HEAD_DIM ** -0.5
    na_l = na_attention(nq[:, :, c:], nk[:, :, c:], nv[:, :, c:], nk[:, :, :c], nv[:, :, :c], na_bias_table(p['na_rpb'][j]),
                        na_scale)
    na_l = heads_first(na_l).reshape(b, n, NA_W)
    mix_l = jnp.concatenate([mla_l, na_l], axis=-1)
    if not need_ctx:
        return tokens_matmul(mix_l, p['o_w_out'][j]), None
    mla_c = heads_first(attention(qh[:, :, :c], kh[:, :, :c], vh[:, :, :c], mla_scale)).reshape(b, c, MLA_HEADS * MLA_V)
    na_c = heads_first(attention(nq[:, :, :c], nk[:, :, :c], nv[:, :, :c], na_scale)).reshape(b, c, NA_W)
    mix = jnp.concatenate([jnp.concatenate([mla_c, na_c], axis=-1), mix_l], axis=1)
    out = tokens_matmul(mix, p['o_w_out'][j])
    return out[:, c:], out[:, :c]


def mlp(h, w1, w2):
    return tokens_matmul(jnp.square(jax.nn.relu(tokens_matmul(h, w1))), w2)


def local_loss(x, p, m_lat, m_ctx, ctx, target):
    depth = m_lat.shape[0]
    c = ctx.shape[1]
    xc = ctx
    for i in range(depth):
        need_ctx = i < depth - 1
        j = i // 2
        ml = [m_lat[i, :, s][:, None, :] for s in range(N_MOD)]
        mc = [m_ctx[i, s][None, None, :] for s in range(N_MOD)]
        a_lat = modulate(x, p['g_norm1'][i], ml[0], ml[1])
        a_ctx = modulate(xc, p['g_norm1'][i], mc[0], mc[1])
        mixer = even_mixer if i % 2 == 0 else odd_mixer
        o_lat, o_ctx = mixer(a_lat, a_ctx, p, j, need_ctx)
        x = x + ml[2] * o_lat
        h_lat = modulate(x, p['g_norm2'][i], ml[3], ml[4])
        if need_ctx:
            xc = xc + mc[2] * o_ctx
            h_ctx = modulate(xc, p['g_norm2'][i], mc[3], mc[4])
            ff = mlp(jnp.concatenate([h_ctx, h_lat], axis=1), p['w_ff1'][i], p['w_ff2'][i])
            x = x + ml[5] * ff[:, c:]
            xc = xc + mc[5] * ff[:, :c]
        else:
            x = x + ml[5] * mlp(h_lat, p['w_ff1'][i], p['w_ff2'][i])
    return 0.5 * jnp.sum(jnp.mean(jnp.square(x - target), axis=-1))


def _packed_rows(size, layout):
    width, group = layout
    return -(-size // (width * group)) * group


def _pack_rows(flat, layout):
    width = layout[0]
    rows = _packed_rows(flat.shape[-1], layout)
    flat = jnp.pad(flat, [(0, 0)] * (flat.ndim - 1) + [(0, rows * width - flat.shape[-1])])
    return flat.reshape(flat.shape[:-1] + (rows, width))


def _unpack_rows(rows, shape):
    lead = rows.shape[:-2]
    return rows.reshape(lead + (-1,))[..., :math.prod(shape)].reshape(lead + tuple(shape))


def _unpack_all(packed, shapes, layout):
    out, at = [], 0
    for shape in shapes:
        rows = _packed_rows(math.prod(shape), layout)
        out.append(_unpack_rows(packed[..., at:at + rows, :], shape))
        at += rows
    return out


def _join_shards(g, axis):
    g = jnp.moveaxis(g, 0, axis)
    return g.reshape(g.shape[:axis] + (N_DEV * g.shape[axis + 1],) + g.shape[axis + 2:])


def _split_shards(full, axis):
    s = full.shape
    return jnp.moveaxis(full.reshape(s[:axis] + (N_DEV, s[axis] // N_DEV) + s[axis + 1:]), axis, 0)


def _gather_packed(parts, dtype, layout, hbm, name):
    packed = jnp.concatenate([_pack_rows(a.astype(dtype).reshape(-1), layout) for a in parts], axis=0)
    return _unpack_all(_exchange(packed, True, hbm, name), [a.shape for a in parts], layout)


def kernel(x, c, ctx, c_ctx, w_mod, b_mod, g_norm1, g_norm2, w_ff1, w_ff2, e_w_in, e_w_out, e_g_q, e_g_k, ssm_lam_re, ssm_lam_im, ssm_log_dt, ssm_b_re, ssm_b_im, ssm_c_re, ssm_c_im, ssm_d, ssm_w_glu, ssm_b_glu, o_w_in, o_w_out, mla_g_cq, mla_g_ckv, mla_w_uq, mla_w_ukv, mla_g_q, mla_g_k, na_g_q, na_g_k, na_rpb, loss_target, m_c_ctx, m_w_mod, m_b_mod, m_g_norm1, m_g_norm2, m_w_ff1, m_w_ff2, m_e_w_in, m_e_w_out, m_e_g_q, m_e_g_k, m_ssm_lam_re, m_ssm_lam_im, m_ssm_log_dt, m_ssm_b_re, m_ssm_b_im, m_ssm_c_re, m_ssm_c_im, m_ssm_d, m_ssm_w_glu, m_ssm_b_glu, m_o_w_in, m_o_w_out, m_mla_g_cq, m_mla_g_ckv, m_mla_w_uq, m_mla_w_ukv, m_mla_g_q, m_mla_g_k, m_na_g_q, m_na_g_k, m_na_rpb, v_c_ctx, v_w_mod, v_b_mod, v_g_norm1, v_g_norm2, v_w_ff1, v_w_ff2, v_e_w_in, v_e_w_out, v_e_g_q, v_e_g_k, v_ssm_lam_re, v_ssm_lam_im, v_ssm_log_dt, v_ssm_b_re, v_ssm_b_im, v_ssm_c_re, v_ssm_c_im, v_ssm_d, v_ssm_w_glu, v_ssm_b_glu, v_o_w_in, v_o_w_out, v_mla_g_cq, v_mla_g_ckv, v_mla_w_uq, v_mla_w_ukv, v_mla_g_q, v_mla_g_k, v_na_g_q, v_na_g_k, v_na_rpb):
    given = dict(locals())
    x, c, ctx, target = given['x'], given['c'], given['ctx'], given['loss_target']
    b_loc, _, d_model = x.shape
    depth = given['w_mod'].shape[0]
    ix, iy, ic = lax.axis_index('x'), lax.axis_index('y'), lax.axis_index('c')
    me = 4 * ix + 2 * iy + ic
    n_batch = N_DEV * b_loc
    mod_w = given['w_mod'].shape[2]

    c_rows = jnp.concatenate([c, jnp.zeros((8 - b_loc, d_model), F32)], axis=0)
    small = _gather_packed([c_rows] + [given[n] for n in SHARDED_SMALL], F32, PACK_SMALL, False, 'gather_small')
    c_all = small[0][:, :b_loc].reshape(n_batch, d_model)
    full = {n: _join_shards(g, SHARDED_SMALL[n]) for n, g in zip(SHARDED_SMALL, small[1:])}
    big = _gather_packed([given[n] for n in BIG], BF16, PACK_BIG, True, 'gather_weights')
    full.update({n: _join_shards(g, BIG[n]) for n, g in zip(BIG, big)})
    w_in = full['o_w_in']
    c3 = MLA_Q_RANK + MLA_KV_RANK + MLA_ROPE
    full['o_w_in'] = jnp.concatenate([w_in[..., :c3], jnp.zeros(w_in.shape[:2] + (ODD_NA_AT - c3,), BF16), w_in[..., c3:]],
                                     axis=-1)
    for n in REPLICATED:
        full[n] = given[n]

    rows17 = 16 * (-(-(n_batch + 1) // 16))
    cond = jnp.concatenate([jax.nn.silu(c_all), jax.nn.silu(given['c_ctx'])[None],
                            jnp.zeros((rows17 - n_batch - 1, d_model), F32)], axis=0)
    mod_mine = jnp.stack([_matmul(cond, given['w_mod'][i], 'nn', F32) for i in range(depth)])
    b_mine = lax.dynamic_slice_in_dim(given['b_mod'], me * mod_w, mod_w, axis=1)
    mod_mine = mod_mine + b_mine[:, None, :]
    mod_all = _gather_packed([mod_mine], F32, PACK_SMALL, False, 'gather_mod')[0]
    mod_all = jnp.moveaxis(mod_all, 0, 2).reshape(depth, rows17, N_MOD, d_model)
    m_lat = lax.dynamic_slice_in_dim(mod_all, me * b_loc, b_loc, axis=1)
    m_ctx = mod_all[:, n_batch]

    diff = {n: full[n] for n in list(BIG) + list(SHARDED_SMALL) + REPLICATED}
    loss, (g_x, g_p, g_ml, g_mc) = jax.value_and_grad(local_loss, argnums=(0, 1, 2, 3))(x, diff, m_lat, m_ctx, ctx, target)
    loss = lax.psum(loss, ('x', 'y', 'c'))
    w_in_g = g_p['o_w_in']
    g_p['o_w_in'] = jnp.concatenate([w_in_g[..., :c3], w_in_g[..., ODD_NA_AT:]], axis=-1)

    g_rows = jnp.concatenate([g_ml.reshape(depth, b_loc, N_MOD * d_model), g_mc.reshape(depth, 1, N_MOD * d_model),
                              jnp.zeros((depth, 8 - b_loc - 1, N_MOD * d_model), F32)], axis=1)
    g_mod_all = _gather_packed([g_rows], F32, PACK_SMALL, False, 'gather_mod_grads')[0]
    g_lat_all = jnp.moveaxis(g_mod_all[:, :, :b_loc], 0, 1).reshape(depth, n_batch, N_MOD * d_model)
    g_ctx_all = g_mod_all[0, :, b_loc]
    for dev in range(1, N_DEV):
        g_ctx_all = g_ctx_all + g_mod_all[dev, :, b_loc]
    g_mod17 = jnp.concatenate([g_lat_all, g_ctx_all[:, None], jnp.zeros((depth, rows17 - n_batch - 1, N_MOD * d_model), F32)],
                              axis=1)
    grad_b_mod = jnp.sum(g_mod17, axis=1)
    g_mod_mine = lax.dynamic_slice_in_dim(g_mod17, me * mod_w, mod_w, axis=2)
    grad_w_mod = jnp.stack([_matmul(cond, g_mod_mine[i], 'tn', F32) for i in range(depth)])
    d_cond = _matmul(g_mod_mine[0], given['w_mod'][0], 'nt', F32)
    for i in range(1, depth):
        d_cond = d_cond + _matmul(g_mod_mine[i], given['w_mod'][i], 'nt', F32)
    d_cond_ctx = d_cond[n_batch]

    small_names = REPLICATED + list(SHARDED_SMALL)
    parts = [d_cond_ctx] + [g_p[n] for n in small_names]
    packed = jnp.concatenate([_pack_rows(a.reshape(-1), PACK_SMALL) for a in parts], axis=0)
    summed = _unpack_all(_all_reduce_small(packed), [a.shape for a in parts], PACK_SMALL)
    grads = dict(zip(['c_ctx'] + small_names, summed))
    c_ctx = given['c_ctx']
    sig = jax.nn.sigmoid(c_ctx)
    grads['c_ctx'] = grads['c_ctx'] * (sig * (1 + c_ctx * (1 - sig)))
    for n, axis in SHARDED_SMALL.items():
        width = given[n].shape[axis]
        grads[n] = lax.dynamic_slice_in_dim(grads[n], me * width, width, axis=axis)
    grads['w_mod'], grads['b_mod'] = grad_w_mod, grad_b_mod

    packed = jnp.concatenate([_pack_rows(_split_shards(g_p[n], BIG[n]).reshape(N_DEV, -1), PACK_BIG) for n in BIG], axis=1)
    landed = _exchange(packed, False, True, 'scatter_weight_grads')
    grads.update(zip(BIG, _unpack_all(_sum_slots(landed), [given[n].shape for n in BIG], PACK_BIG)))

    big_names = list(BIG) + ['w_mod']
    upd = {n: _adamw_big(given[n], grads[n], given['m_' + n], given['v_' + n]) for n in big_names}
    rest = [n for n in WEIGHTS if n not in big_names]
    out = _adamw_small([given[n] for n in rest], [grads[n] for n in rest], [given['m_' + n] for n in rest],
                       [given['v_' + n] for n in rest])
    upd.update(dict(zip(rest, out)))
    return (loss, g_x, *[grads[n] for n in WEIGHTS], *[upd[n][0] for n in WEIGHTS], *[upd[n][1] for n in WEIGHTS],
            *[upd[n][2] for n in WEIGHTS])
```

```python
import functools
import math

import jax
import jax.numpy as jnp
from jax import lax
from jax.experimental import pallas as pl
from jax.experimental.pallas import tpu as pltpu

F32, BF16 = jnp.float32, jnp.bfloat16
MESH = pl.DeviceIdType.MESH
N_DEV = 8
VMEM_LIMIT_BYTES = 56 * 1024 * 1024
LANES = 128
PACK_SMALL = (128, 8)

GRID_W = 64
HEAD_DIM = 64
ROPE_BASE = 10000.0
EPS = 1e-6
N_MOD = 6
GQA_Q_HEADS, GQA_KV_HEADS = 12, 4
GQA_Q_W, GQA_KV_W = GQA_Q_HEADS * HEAD_DIM, GQA_KV_HEADS * HEAD_DIM
SSM_WIDTH, SSM_GROUP, SSM_GROUPS, SSM_STATE = 256, 16, 16, 64
MLA_HEADS, MLA_Q_RANK, MLA_KV_RANK, MLA_NOPE, MLA_ROPE, MLA_V = 8, 512, 256, 64, 32, 64
MLA_QK = MLA_NOPE + MLA_ROPE
NA_HEADS, NA_WIN_R, NA_WIN_C = 8, 8, 16
NA_W = NA_HEADS * HEAD_DIM
ODD_IN_W = MLA_Q_RANK + MLA_KV_RANK + MLA_ROPE + 3 * NA_W
ODD_NA_AT = 1024
ODD_IN_PAD = ODD_NA_AT + 3 * NA_W
NEG = -1e30

ADAM_LR, ADAM_B1, ADAM_B2, ADAM_EPS, ADAM_WD, ADAM_STEP = 0.001, 0.9, 0.999, 1e-08, 0.01, 10

FWD_PARAMS = ['x', 'c', 'ctx', 'c_ctx', 'w_mod', 'b_mod', 'g_norm1', 'g_norm2', 'w_ff1', 'w_ff2', 'e_w_in', 'e_w_out',
              'e_g_q', 'e_g_k', 'ssm_lam_re', 'ssm_lam_im', 'ssm_log_dt', 'ssm_b_re', 'ssm_b_im', 'ssm_c_re', 'ssm_c_im',
              'ssm_d', 'ssm_w_glu', 'ssm_b_glu', 'o_w_in', 'o_w_out', 'mla_g_cq', 'mla_g_ckv', 'mla_w_uq', 'mla_w_ukv',
              'mla_g_q', 'mla_g_k', 'na_g_q', 'na_g_k', 'na_rpb']
WEIGHTS = FWD_PARAMS[3:]
BIG = {'w_ff1': 2, 'w_ff2': 1, 'e_w_in': 2, 'e_w_out': 1, 'o_w_in': 2, 'o_w_out': 1, 'mla_w_uq': 2, 'mla_w_ukv': 2,
       'ssm_w_glu': 1}
SHARDED_SMALL = {'mla_g_cq': 1, 'mla_g_ckv': 1}
REPLICATED = [n for n in WEIGHTS if n not in BIG and n not in SHARDED_SMALL and n not in ('w_mod', 'c_ctx', 'b_mod')]


def _tile(dim, prefs):
    for p in prefs:
        if dim >= p and dim % p == 0:
            return p
    return dim


def _params(*sem):
    return pltpu.CompilerParams(dimension_semantics=sem, vmem_limit_bytes=VMEM_LIMIT_BYTES)


def _dot_nt(a, b):
    return lax.dot_general(a, b, (((1,), (1,)), ((), ())), preferred_element_type=F32)


def _dot_tn(a, b):
    return lax.dot_general(a, b, (((0,), (0,)), ((), ())), preferred_element_type=F32)


def _dot(a, b):
    return jnp.dot(a, b, preferred_element_type=F32)


def _matmul(a, b, kind, out_dtype):
    a, b = a.astype(BF16), b.astype(BF16)
    if kind == 'nn':
        (m, kd), n = a.shape, b.shape[1]
    elif kind == 'nt':
        (m, kd), n = a.shape, b.shape[0]
    else:
        (kd, m), n = a.shape, b.shape[1]
    tm = _tile(m, (768, 512, 256, 128))
    tn = _tile(n, (1024, 768, 512, 256, 128))
    tk = _tile(kd, (2048, 1536, 1024, 512, 256, 128))
    nk = kd // tk
    dn = {'nn': (((1,), (0,)), ((), ())), 'nt': (((1,), (1,)), ((), ())), 'tn': (((0,), (0,)), ((), ()))}[kind]

    def body(a_ref, b_ref, o_ref, *acc):
        part = lax.dot_general(a_ref[...], b_ref[...], dn, preferred_element_type=F32)
        if nk == 1:
            o_ref[...] = part.astype(o_ref.dtype)
            return
        acc_ref, k = acc[0], pl.program_id(2)

        @pl.when(k == 0)
        def _():
            acc_ref[...] = part

        @pl.when((k > 0) & (k < nk - 1))
        def _():
            acc_ref[...] += part

        @pl.when(k == nk - 1)
        def _():
            o_ref[...] = (acc_ref[...] + part).astype(o_ref.dtype)

    a_spec = pl.BlockSpec((tk, tm), lambda i, j, k: (k, i)) if kind == 'tn' else pl.BlockSpec((tm, tk), lambda i, j, k: (i, k))
    b_spec = pl.BlockSpec((tn, tk), lambda i, j, k: (j, k)) if kind == 'nt' else pl.BlockSpec((tk, tn), lambda i, j, k: (k, j))
    return pl.pallas_call(
        body, name='mm_' + kind, grid=(m // tm, n // tn, nk),
        out_shape=jax.ShapeDtypeStruct((m, n), out_dtype),
        in_specs=[a_spec, b_spec], out_specs=pl.BlockSpec((tm, tn), lambda i, j, k: (i, j)),
        scratch_shapes=[pltpu.VMEM((tm, tn), F32)] if nk > 1 else [],
        compiler_params=_params('parallel', 'parallel', 'arbitrary'),
    )(a, b)


@jax.custom_vjp
def linear(a, w):
    return _matmul(a, w, 'nn', F32)


def _linear_fwd(a, w):
    ab = a.astype(BF16)
    return _matmul(ab, w, 'nn', F32), (ab, w)


def _linear_bwd(res, g):
    ab, w = res
    gb = g.astype(BF16)
    return _matmul(gb, w, 'nt', F32), _matmul(ab, gb, 'tn', w.dtype)


linear.defvjp(_linear_fwd, _linear_bwd)


LOG2E = math.log2(math.e)


def _softmax_rows(t):
    m = jnp.max(t, axis=-1, keepdims=True)
    e = jnp.exp2(t - m)
    return e * (1.0 / jnp.sum(e, axis=-1, keepdims=True))


def _attn_specs(q, k, v, bq):
    _, h, nq, dq = q.shape
    _, hk, nk, dv = v.shape
    g = h // hk
    q_spec = pl.BlockSpec((None, None, bq, dq), lambda b, j, gi, i: (b, j * g + gi, i, 0))
    k_spec = pl.BlockSpec((None, None, nk, dq), lambda b, j, gi, i: (b, j, 0, 0))
    v_spec = pl.BlockSpec((None, None, nk, dv), lambda b, j, gi, i: (b, j, 0, 0))
    o_spec = pl.BlockSpec((None, None, bq, dv), lambda b, j, gi, i: (b, j * g + gi, i, 0))
    return (q.shape[0], hk, g, nq // bq), q_spec, k_spec, v_spec, o_spec


def _attn_fwd_call(q, k, v, scale):
    bq = _tile(q.shape[2], (256, 128))
    grid, q_spec, k_spec, v_spec, o_spec = _attn_specs(q, k, v, bq)

    def body(q_ref, k_ref, v_ref, o_ref):
        p = _softmax_rows(_dot_nt(q_ref[...], k_ref[...]) * (scale * LOG2E))
        o_ref[...] = _dot(p.astype(BF16), v_ref[...])

    return pl.pallas_call(
        body, name='attn_fwd', grid=grid, out_shape=jax.ShapeDtypeStruct(q.shape[:3] + (v.shape[3],), F32),
        in_specs=[q_spec, k_spec, v_spec], out_specs=o_spec,
        compiler_params=_params('parallel', 'parallel', 'arbitrary', 'arbitrary'),
    )(q, k, v)


def _attn_bwd_call(q, k, v, do, scale):
    bq = _tile(q.shape[2], (256, 128))
    grid, q_spec, k_spec, v_spec, o_spec = _attn_specs(q, k, v, bq)

    def body(q_ref, k_ref, v_ref, do_ref, dq_ref, dk_ref, dv_ref):
        @pl.when((pl.program_id(2) == 0) & (pl.program_id(3) == 0))
        def _():
            dk_ref[...] = jnp.zeros_like(dk_ref)
            dv_ref[...] = jnp.zeros_like(dv_ref)

        qb, kb, vb, dob = q_ref[...], k_ref[...], v_ref[...], do_ref[...]
        p = _softmax_rows(_dot_nt(qb, kb) * (scale * LOG2E))
        dp = _dot_nt(dob, vb)
        ds = p * (dp - jnp.sum(p * dp, axis=-1, keepdims=True))
        dsb = (ds * scale).astype(BF16)
        dq_ref[...] = _dot(dsb, kb)
        dk_ref[...] += _dot_tn(dsb, qb)
        dv_ref[...] += _dot_tn(p.astype(BF16), dob)

    return pl.pallas_call(
        body, name='attn_bwd', grid=grid,
        out_shape=(jax.ShapeDtypeStruct(q.shape, F32), jax.ShapeDtypeStruct(k.shape, F32), jax.ShapeDtypeStruct(v.shape, F32)),
        in_specs=[q_spec, k_spec, v_spec, o_spec], out_specs=(q_spec, k_spec, v_spec),
        compiler_params=_params('parallel', 'parallel', 'arbitrary', 'arbitrary'),
    )(q, k, v, do)


@functools.partial(jax.custom_vjp, nondiff_argnums=(3,))
def attention(q, k, v, scale):
    return _attn_fwd_call(q.astype(BF16), k.astype(BF16), v.astype(BF16), scale)


def _attention_fwd(q, k, v, scale):
    qb, kb, vb = q.astype(BF16), k.astype(BF16), v.astype(BF16)
    return _attn_fwd_call(qb, kb, vb, scale), (qb, kb, vb)


def _attention_bwd(scale, res, g):
    return _attn_bwd_call(*res, g.astype(BF16), scale)


attention.defvjp(_attention_fwd, _attention_bwd)


def _na_window(r, rows):
    start = jnp.clip(r - NA_WIN_R // 2, 0, rows - NA_WIN_R)
    return start, r - start


def _na_scores(q, kw, kc, bias, scale):
    s1 = _dot_nt(q, kw) * scale + bias
    s2 = _dot_nt(q, kc) * scale
    m = jnp.maximum(jnp.max(s1, axis=-1, keepdims=True), jnp.max(s2, axis=-1, keepdims=True))
    e1, e2 = jnp.exp(s1 - m), jnp.exp(s2 - m)
    inv = 1.0 / (jnp.sum(e1, axis=-1, keepdims=True) + jnp.sum(e2, axis=-1, keepdims=True))
    return e1 * inv, e2 * inv


def _na_specs(q, kc):
    _, _, n, d = q.shape
    c = kc.shape[2]
    win = NA_WIN_R * GRID_W
    tok = pl.BlockSpec((None, None, n, d), lambda b, h: (b, h, 0, 0))
    ctx = pl.BlockSpec((None, None, c, d), lambda b, h: (b, h, 0, 0))
    bias = pl.BlockSpec((None, NA_WIN_R, GRID_W, win), lambda b, h: (h, 0, 0, 0))
    dbias = pl.BlockSpec((None, None, NA_WIN_R, GRID_W, win), lambda b, h: (b, h, 0, 0, 0))
    return tok, ctx, bias, dbias


def _na_fwd_call(q, k, v, kc, vc, bias, scale):
    b, h, n, d = q.shape
    rows, win = n // GRID_W, NA_WIN_R * GRID_W
    tok, ctx, bias_spec, _ = _na_specs(q, kc)

    def body(q_ref, k_ref, v_ref, kc_ref, vc_ref, b_ref, o_ref):
        def row(r, carry):
            start, off = _na_window(r, rows)
            at = pl.ds(pl.multiple_of(r * GRID_W, GRID_W), GRID_W)
            wat = pl.ds(pl.multiple_of(start * GRID_W, GRID_W), win)
            p1, p2 = _na_scores(q_ref[at, :], k_ref[wat, :], kc_ref[...], b_ref[off], scale)
            o_ref[at, :] = _dot(p1.astype(BF16), v_ref[wat, :]) + _dot(p2.astype(BF16), vc_ref[...])
            return carry

        lax.fori_loop(0, rows, row, 0)

    return pl.pallas_call(
        body, name='na_fwd', grid=(b, h), out_shape=jax.ShapeDtypeStruct(q.shape, F32),
        in_specs=[tok, tok, tok, ctx, ctx, bias_spec], out_specs=tok,
        compiler_params=_params('parallel', 'parallel'),
    )(q, k, v, kc, vc, bias)


def _na_bwd_call(q, k, v, kc, vc, bias, do, scale):
    b, h, n, d = q.shape
    rows, win = n // GRID_W, NA_WIN_R * GRID_W
    tok, ctx, bias_spec, dbias_spec = _na_specs(q, kc)

    def body(q_ref, k_ref, v_ref, kc_ref, vc_ref, b_ref, do_ref, dq_ref, dk_ref, dv_ref, dkc_ref, dvc_ref, db_ref):
        for ref in (dk_ref, dv_ref, dkc_ref, dvc_ref, db_ref):
            ref[...] = jnp.zeros_like(ref)

        def row(r, carry):
            start, off = _na_window(r, rows)
            at = pl.ds(pl.multiple_of(r * GRID_W, GRID_W), GRID_W)
            wat = pl.ds(pl.multiple_of(start * GRID_W, GRID_W), win)
            qb, kw, vw, dob = q_ref[at, :], k_ref[wat, :], v_ref[wat, :], do_ref[at, :]
            kcb, vcb = kc_ref[...], vc_ref[...]
            p1, p2 = _na_scores(qb, kw, kcb, b_ref[off], scale)
            dp1, dp2 = _dot_nt(dob, vw), _dot_nt(dob, vcb)
            delta = jnp.sum(p1 * dp1, axis=-1, keepdims=True) + jnp.sum(p2 * dp2, axis=-1, keepdims=True)
            ds1, ds2 = p1 * (dp1 - delta), p2 * (dp2 - delta)
            db_ref[off] += ds1
            ds1b, ds2b = (ds1 * scale).astype(BF16), (ds2 * scale).astype(BF16)
            dq_ref[at, :] = _dot(ds1b, kw) + _dot(ds2b, kcb)
            dk_ref[wat, :] += _dot_tn(ds1b, qb)
            dv_ref[wat, :] += _dot_tn(p1.astype(BF16), dob)
            dkc_ref[...] += _dot_tn(ds2b, qb)
            dvc_ref[...] += _dot_tn(p2.astype(BF16), dob)
            return carry

        lax.fori_loop(0, rows, row, 0)

    f = lambda a: jax.ShapeDtypeStruct(a.shape, F32)
    return pl.pallas_call(
        body, name='na_bwd', grid=(b, h),
        out_shape=(f(q), f(k), f(v), f(kc), f(vc), jax.ShapeDtypeStruct((b,) + bias.shape, F32)),
        in_specs=[tok, tok, tok, ctx, ctx, bias_spec, tok], out_specs=(tok, tok, tok, ctx, ctx, dbias_spec),
        compiler_params=_params('parallel', 'parallel'),
    )(q, k, v, kc, vc, bias, do)


@functools.partial(jax.custom_vjp, nondiff_argnums=(6,))
def na_attention(q, k, v, kc, vc, bias, scale):
    return _na_fwd_call(q.astype(BF16), k.astype(BF16), v.astype(BF16), kc.astype(BF16), vc.astype(BF16), bias, scale)


def _na_attention_fwd(q, k, v, kc, vc, bias, scale):
    res = (q.astype(BF16), k.astype(BF16), v.astype(BF16), kc.astype(BF16), vc.astype(BF16), bias)
    return _na_fwd_call(*res, scale), res


def _na_attention_bwd(scale, res, g):
    dq, dk, dv, dkc, dvc, db = _na_bwd_call(*res, g.astype(BF16), scale)
    return dq, dk, dv, dkc, dvc, jnp.sum(db, axis=0)


na_attention.defvjp(_na_attention_fwd, _na_attention_bwd)


def _na_table_index():
    qcol = jnp.arange(GRID_W)
    kcol = jnp.arange(GRID_W)
    cstart = jnp.clip(qcol - NA_WIN_C // 2, 0, GRID_W - NA_WIN_C)
    inside = (kcol[None, :] >= cstart[:, None]) & (kcol[None, :] < cstart[:, None] + NA_WIN_C)
    cidx = jnp.clip(kcol[None, :] - qcol[:, None] + (NA_WIN_C - 1), 0, 2 * NA_WIN_C - 2)
    ridx = jnp.arange(NA_WIN_R)[None, :] - jnp.arange(NA_WIN_R)[:, None] + (NA_WIN_R - 1)
    return inside, cidx, ridx


@jax.custom_vjp
def na_bias_table(rpb):
    inside, cidx, ridx = _na_table_index()
    t = rpb[:, ridx][:, :, :, cidx]
    t = jnp.where(inside[None, None, None], t, NEG)
    return jnp.transpose(t, (0, 1, 3, 2, 4)).reshape(rpb.shape[0], NA_WIN_R, GRID_W, NA_WIN_R * GRID_W)


def _na_bias_table_bwd(_, dt):
    inside, cidx, ridx = _na_table_index()
    pick_c = ((cidx[..., None] == jnp.arange(2 * NA_WIN_C - 1)) & inside[..., None]).astype(F32)
    pick_r = (ridx[..., None] == jnp.arange(2 * NA_WIN_R - 1)).astype(F32)
    d5 = dt.reshape(dt.shape[0], NA_WIN_R, GRID_W, NA_WIN_R, GRID_W)
    part = jnp.einsum('hoqjk,qkb->hojb', d5, pick_c, precision=lax.Precision.HIGHEST)
    return (jnp.einsum('hojb,oja->hab', part, pick_r, precision=lax.Precision.HIGHEST),)


na_bias_table.defvjp(lambda rpb: (na_bias_table(rpb), None), _na_bias_table_bwd)


S5_HALF = SSM_WIDTH // 2
S5_LANES = (SSM_GROUPS // 2) * SSM_STATE
S5_Q = S5_LANES // LANES


def _s5_tiles(a):
    r = a.shape[0]
    return jnp.transpose(a.reshape(r, S5_Q, LANES), (1, 0, 2)).reshape(S5_Q * r, LANES)


def _s5_untiles(a):
    r = a.shape[0] // S5_Q
    return jnp.transpose(a.reshape(S5_Q, r, LANES), (1, 0, 2)).reshape(r, S5_LANES)


def _s5_put(ref, r, rr, tc, val):
    for q in range(S5_Q):
        ref[pl.ds((q * rr + r) * tc, tc), :] = val[:, q * LANES:(q + 1) * LANES]


def _s5_get(ref, r, rr, tc):
    return jnp.concatenate([ref[pl.ds((q * rr + r) * tc, tc), :] for q in range(S5_Q)], axis=1)


def _s5_fwd_call(u, a_re, a_im, b_re, b_im, c_re, c_im):
    rr, t_len, _ = u.shape
    sets = b_re.shape[0]
    per = rr // sets
    tc = _tile(t_len, (256, 128))
    nt = t_len // tc
    qr = S5_Q * rr

    def body(u_ref, ar_ref, ai_ref, br_ref, bi_ref, cr_ref, ci_ref, y_ref, hr_ref, hi_ref, sr_ref, si_ref):
        @pl.when(pl.program_id(0) == 0)
        def _():
            sr_ref[...] = jnp.zeros_like(sr_ref)
            si_ref[...] = jnp.zeros_like(si_ref)

        for r in range(rr):
            ub = u_ref[r]
            _s5_put(hr_ref, r, rr, tc, _dot(ub, br_ref[r // per]))
            _s5_put(hi_ref, r, rr, tc, _dot(ub, bi_ref[r // per]))
        ar, ai = ar_ref[...], ai_ref[...]

        def step(t, carry):
            hr, hi = carry
            at = pl.ds(t, qr, stride=tc)
            nr = ar * hr - ai * hi + hr_ref[at, :]
            ni = ar * hi + ai * hr + hi_ref[at, :]
            hr_ref[at, :] = nr
            hi_ref[at, :] = ni
            return nr, ni

        hr, hi = lax.fori_loop(0, tc, step, (sr_ref[...], si_ref[...]))
        sr_ref[...] = hr
        si_ref[...] = hi
        for r in range(rr):
            y_ref[r] = (_dot(_s5_get(hr_ref, r, rr, tc).astype(BF16), cr_ref[r // per])
                        - _dot(_s5_get(hi_ref, r, rr, tc).astype(BF16), ci_ref[r // per]))

    full = lambda a: pl.BlockSpec(a.shape, lambda i: (0,) * a.ndim)
    h_spec = pl.BlockSpec((None, qr * tc, LANES), lambda i: (i, 0, 0))
    h_shape = jax.ShapeDtypeStruct((nt, qr * tc, LANES), F32)
    a_re, a_im = _s5_tiles(a_re), _s5_tiles(a_im)
    return pl.pallas_call(
        body, name='s5_fwd', grid=(nt,),
        out_shape=(jax.ShapeDtypeStruct((rr, t_len, S5_HALF), F32), h_shape, h_shape),
        in_specs=[pl.BlockSpec((rr, tc, S5_HALF), lambda i: (0, i, 0)), full(a_re), full(a_im), full(b_re), full(b_im),
                  full(c_re), full(c_im)],
        out_specs=(pl.BlockSpec((rr, tc, S5_HALF), lambda i: (0, i, 0)), h_spec, h_spec),
        scratch_shapes=[pltpu.VMEM((qr, LANES), F32), pltpu.VMEM((qr, LANES), F32)],
        compiler_params=_params('arbitrary'),
    )(u, a_re, a_im, b_re, b_im, c_re, c_im)


def _s5_bwd_call(u, a_re, a_im, b_re, b_im, c_re, c_im, h_re, h_im, dy):
    rr, t_len, _ = u.shape
    sets = b_re.shape[0]
    per = rr // sets
    nt, rows, _ = h_re.shape
    qr = S5_Q * rr
    tc = rows // qr

    def body(u_ref, dy_ref, ar_ref, ai_ref, br_ref, bi_ref, cr_ref, ci_ref, hr_ref, hi_ref,
             du_ref, dar_ref, dai_ref, dbr_ref, dbi_ref, dcr_ref, dci_ref, gr_ref, gi_ref, sr_ref, si_ref):
        i = pl.program_id(0)

        @pl.when(i == 0)
        def _():
            for ref in (dar_ref, dai_ref, dbr_ref, dbi_ref, dcr_ref, dci_ref, sr_ref, si_ref):
                ref[...] = jnp.zeros_like(ref)

        for r in range(rr):
            dyb = dy_ref[r]
            _s5_put(gr_ref, r, rr, tc, _dot_nt(dyb, cr_ref[r // per]))
            _s5_put(gi_ref, r, rr, tc, -_dot_nt(dyb, ci_ref[r // per]))
        ar, ai = ar_ref[...], ai_ref[...]
        g_r, g_i = sr_ref[...], si_ref[...]
        last = pl.ds(tc - 1, qr, stride=tc)
        d_r = g_r * hr_ref[last, :] + g_i * hi_ref[last, :]
        d_i = g_i * hr_ref[last, :] - g_r * hi_ref[last, :]

        def advance(t, g_r, g_i):
            at = pl.ds(t, qr, stride=tc)
            n_r = ar * g_r + ai * g_i + gr_ref[at, :]
            n_i = ar * g_i - ai * g_r + gi_ref[at, :]
            gr_ref[at, :] = n_r
            gi_ref[at, :] = n_i
            return n_r, n_i

        def step(k, carry):
            g_r, g_i, d_r, d_i = carry
            t = tc - 1 - k
            g_r, g_i = advance(t, g_r, g_i)
            before = pl.ds(t - 1, qr, stride=tc)
            p_r, p_i = hr_ref[before, :], hi_ref[before, :]
            return g_r, g_i, d_r + g_r * p_r + g_i * p_i, d_i + g_i * p_r - g_r * p_i

        g_r, g_i, d_r, d_i = lax.fori_loop(0, tc - 1, step, (g_r, g_i, d_r, d_i))
        g_r, g_i = advance(0, g_r, g_i)
        sr_ref[...] = g_r
        si_ref[...] = g_i
        dar_ref[...] += d_r
        dai_ref[...] += d_i
        for r in range(rr):
            s = r // per
            ub, dyb = u_ref[r], dy_ref[r]
            grb, gib = _s5_get(gr_ref, r, rr, tc).astype(BF16), _s5_get(gi_ref, r, rr, tc).astype(BF16)
            du_ref[r] = _dot_nt(grb, br_ref[s]) + _dot_nt(gib, bi_ref[s])
            dbr_ref[s] += _dot_tn(ub, grb)
            dbi_ref[s] += _dot_tn(ub, gib)
            dcr_ref[s] += _dot_tn(_s5_get(hr_ref, r, rr, tc).astype(BF16), dyb)
            dci_ref[s] -= _dot_tn(_s5_get(hi_ref, r, rr, tc).astype(BF16), dyb)

    full = lambda a: pl.BlockSpec(a.shape, lambda i: (0,) * a.ndim)
    back = lambda i: nt - 1 - i
    tok = pl.BlockSpec((rr, tc, S5_HALF), lambda i: (0, back(i), 0))
    h_spec = pl.BlockSpec((None, qr * tc, LANES), lambda i: (back(i), 0, 0))
    f = lambda a: jax.ShapeDtypeStruct(a.shape, F32)
    a_re, a_im = _s5_tiles(a_re), _s5_tiles(a_im)
    du, da_re, da_im, db_re, db_im, dc_re, dc_im = pl.pallas_call(
        body, name='s5_bwd', grid=(nt,),
        out_shape=(jax.ShapeDtypeStruct(u.shape, F32), f(a_re), f(a_im), f(b_re), f(b_im), f(c_re), f(c_im)),
        in_specs=[tok, tok, full(a_re), full(a_im), full(b_re), full(b_im), full(c_re), full(c_im), h_spec, h_spec],
        out_specs=(tok, full(a_re), full(a_im), full(b_re), full(b_im), full(c_re), full(c_im)),
        scratch_shapes=[pltpu.VMEM((qr * tc, LANES), F32), pltpu.VMEM((qr * tc, LANES), F32),
                        pltpu.VMEM((qr, LANES), F32), pltpu.VMEM((qr, LANES), F32)],
        compiler_params=_params('arbitrary'),
    )(u, dy, a_re, a_im, b_re, b_im, c_re, c_im, h_re, h_im)
    return du, _s5_untiles(da_re), _s5_untiles(da_im), db_re, db_im, dc_re, dc_im


@jax.custom_vjp
def s5_core(u, a_re, a_im, b_re, b_im, c_re, c_im):
    return _s5_fwd_call(u.astype(BF16), a_re, a_im, b_re.astype(BF16), b_im.astype(BF16), c_re.astype(BF16),
                        c_im.astype(BF16))[0]


def _s5_core_fwd(u, a_re, a_im, b_re, b_im, c_re, c_im):
    args = (u.astype(BF16), a_re, a_im, b_re.astype(BF16), b_im.astype(BF16), c_re.astype(BF16), c_im.astype(BF16))
    y, h_re, h_im = _s5_fwd_call(*args)
    return y, args + (h_re, h_im)


def _s5_core_bwd(res, g):
    return _s5_bwd_call(*res, g.astype(BF16))


s5_core.defvjp(_s5_core_fwd, _s5_core_bwd)


def _exchange(x, gather, hbm, name):
    block = x.shape if gather else x.shape[1:]

    def body(x_ref, out_ref, send_sems, recv_sems, local_sem):
        ix, iy, ic = lax.axis_index('x'), lax.axis_index('y'), lax.axis_index('c')
        me = 4 * ix + 2 * iy + ic

        def flipped(k):
            px = 1 - ix if k & 4 else ix
            py = 1 - iy if k & 2 else iy
            pc = 1 - ic if k & 1 else ic
            return (px, py, pc), 4 * px + 2 * py + pc

        def copy(k, src, dst):
            return pltpu.make_async_remote_copy(src_ref=src, dst_ref=dst, send_sem=send_sems.at[k - 1],
                                                 recv_sem=recv_sems.at[k - 1], device_id=flipped(k)[0], device_id_type=MESH)

        own = pltpu.make_async_copy(x_ref if gather else x_ref.at[me], out_ref.at[me], local_sem)
        own.start()
        sent = []
        for k in range(1, N_DEV):
            src = x_ref if gather else x_ref.at[flipped(k)[1]]
            sent.append(copy(k, src, out_ref.at[me]))
            sent[-1].start()
        for k in range(1, N_DEV):
            src = x_ref if gather else x_ref.at[flipped(k)[1]]
            copy(k, src, out_ref.at[flipped(k)[1]]).wait_recv()
        for cp in sent:
            cp.wait_send()
        own.wait()

    space = pltpu.HBM if hbm else pltpu.VMEM
    return pl.pallas_call(
        body, name=name, out_shape=jax.ShapeDtypeStruct((N_DEV,) + tuple(block), x.dtype),
        in_specs=[pl.BlockSpec(memory_space=space)], out_specs=pl.BlockSpec(memory_space=space),
        scratch_shapes=[pltpu.SemaphoreType.DMA((N_DEV - 1,)), pltpu.SemaphoreType.DMA((N_DEV - 1,)), pltpu.SemaphoreType.DMA],
        compiler_params=pltpu.CompilerParams(vmem_limit_bytes=VMEM_LIMIT_BYTES),
    )(x)


def _shard_view(ref, axis, index, width):
    return ref.at[(slice(None),) * axis + (pl.ds(pl.multiple_of(index * width, width), width),)]


def _exchange_many(xs, cuts, gather, name):
    n = len(xs)
    if gather:
        shards = [x.shape for x in xs]
    else:
        shards = [x.shape[1:] if cut is None else x.shape[:cut] + (x.shape[cut] // N_DEV,) + x.shape[cut + 1:]
                  for x, cut in zip(xs, cuts)]

    def full_shape(shard, cut):
        return shard[:cut] + (N_DEV * shard[cut],) + shard[cut + 1:]

    out_shapes = [jax.ShapeDtypeStruct((N_DEV,) + tuple(s) if (cut is None or not gather) else full_shape(tuple(s), cut), x.dtype)
                  for x, s, cut in zip(xs, shards, cuts)]

    def body(*refs):
        x_refs, out_refs = refs[:n], refs[n:2 * n]
        send_sems, recv_sems, local_sems = refs[2 * n:]
        ix, iy, ic = lax.axis_index('x'), lax.axis_index('y'), lax.axis_index('c')
        me = 4 * ix + 2 * iy + ic

        def flipped(k):
            px = 1 - ix if k & 4 else ix
            py = 1 - iy if k & 2 else iy
            pc = 1 - ic if k & 1 else ic
            return (px, py, pc), 4 * px + 2 * py + pc

        def block(ref, cut, shard, who):
            return ref.at[who] if cut is None else _shard_view(ref, cut, who, shard[cut])

        def ends(i, sender, receiver):
            if gather:
                return x_refs[i], block(out_refs[i], cuts[i], shards[i], sender)
            return block(x_refs[i], cuts[i], shards[i], receiver), out_refs[i].at[sender]

        def copy(i, k, sender, receiver):
            src, dst = ends(i, sender, receiver)
            return pltpu.make_async_remote_copy(src_ref=src, dst_ref=dst, send_sem=send_sems.at[i * (N_DEV - 1) + k - 1],
                                                 recv_sem=recv_sems.at[i * (N_DEV - 1) + k - 1], device_id=flipped(k)[0],
                                                 device_id_type=MESH)

        own = [pltpu.make_async_copy(*ends(i, me, me), local_sems.at[i]) for i in range(n)]
        for cp in own:
            cp.start()
        sent = [copy(i, k, me, flipped(k)[1]) for k in range(1, N_DEV) for i in range(n)]
        for cp in sent:
            cp.start()
        for k in range(1, N_DEV):
            for i in range(n):
                copy(i, k, flipped(k)[1], me).wait_recv()
        for cp in sent:
            cp.wait_send()
        for cp in own:
            cp.wait()

    hbm = pl.BlockSpec(memory_space=pltpu.HBM)
    pairs = n * (N_DEV - 1)
    return pl.pallas_call(
        body, name=name, out_shape=out_shapes, in_specs=[hbm] * n, out_specs=[hbm] * n,
        scratch_shapes=[pltpu.SemaphoreType.DMA((pairs,)), pltpu.SemaphoreType.DMA((pairs,)), pltpu.SemaphoreType.DMA((n,))],
    )(*xs)


def _adamw_landed(landed, w, m, v):
    shape = w.shape
    w, m, v = (_as_rows(a) for a in (w, m, v))
    rows, cols = w.shape
    landed = landed.reshape(N_DEV, rows, cols)
    tr = _tile(rows, (256, 128, 64, 32, 16))

    def body(l_ref, w_ref, m_ref, v_ref, g_ref, d_ref, nm_ref, nv_ref):
        g = l_ref[0].astype(F32)
        for d in range(1, N_DEV):
            g = g + l_ref[d].astype(F32)
        g_ref[...] = g
        d_ref[...], nm_ref[...], nv_ref[...] = _adamw_math(w_ref[...], g, m_ref[...], v_ref[...])

    spec = pl.BlockSpec((tr, cols), lambda i: (i, 0))
    out = pl.pallas_call(
        body, name='adamw_landed', grid=(rows // tr,), out_shape=(jax.ShapeDtypeStruct(w.shape, F32),) * 4,
        in_specs=[pl.BlockSpec((N_DEV, tr, cols), lambda i: (0, i, 0))] + [spec] * 3, out_specs=(spec,) * 4,
        compiler_params=_params('parallel'),
    )(landed, w, m, v)
    return tuple(o.reshape(shape) for o in out)


def _all_reduce_small(x):
    g = _exchange(x, True, False, 'gather_small_grads')

    def body(g_ref, o_ref):
        acc = g_ref[0]
        for d in range(1, N_DEV):
            acc = acc + g_ref[d]
        o_ref[...] = acc

    return pl.pallas_call(body, name='sum_small', out_shape=jax.ShapeDtypeStruct(x.shape, F32))(g)


def _adamw_math(w, g, m, v):
    m = ADAM_B1 * m + (1.0 - ADAM_B1) * g
    v = ADAM_B2 * v + (1.0 - ADAM_B2) * (g * g)
    m_hat = m / (1.0 - ADAM_B1 ** ADAM_STEP)
    v_hat = v / (1.0 - ADAM_B2 ** ADAM_STEP)
    return -ADAM_LR * (m_hat / (jnp.sqrt(v_hat) + ADAM_EPS) + ADAM_WD * w), m, v


def _as_rows(a):
    return a.reshape(1, -1) if a.ndim < 2 else a.reshape(-1, a.shape[-1])


def _as_lanes(a):
    return a.reshape(-1, LANES) if a.size % LANES == 0 else a.reshape(1, -1)


def _adamw_big(w, g, m, v):
    shape = w.shape
    w, g, m, v = (_as_rows(a) for a in (w, g, m, v))
    rows, cols = w.shape
    tr = _tile(rows, (512, 256, 128, 64, 32, 16, 8))

    def body(w_ref, g_ref, m_ref, v_ref, d_ref, nm_ref, nv_ref):
        d_ref[...], nm_ref[...], nv_ref[...] = _adamw_math(w_ref[...], g_ref[...], m_ref[...], v_ref[...])

    spec = pl.BlockSpec((tr, cols), lambda i: (i, 0))
    out = pl.pallas_call(
        body, name='adamw', grid=(rows // tr,), out_shape=(jax.ShapeDtypeStruct(w.shape, F32),) * 3,
        in_specs=[spec] * 4, out_specs=(spec,) * 3, compiler_params=_params('parallel'),
    )(w, g, m, v)
    return tuple(o.reshape(shape) for o in out)


def _adamw_small(ws, gs, ms, vs):
    n = len(ws)
    shapes = [w.shape for w in ws]
    flat = [_as_lanes(a) for group in (ws, gs, ms, vs) for a in group]

    def body(*refs):
        ins, outs = refs[:4 * n], refs[4 * n:]
        for i in range(n):
            d, m, v = _adamw_math(ins[i][...], ins[n + i][...], ins[2 * n + i][...], ins[3 * n + i][...])
            outs[i][...], outs[n + i][...], outs[2 * n + i][...] = d, m, v

    out = pl.pallas_call(
        body, name='adamw_small', out_shape=tuple(jax.ShapeDtypeStruct(flat[i].shape, F32) for _ in range(3) for i in range(n)),
    )(*flat)
    return [tuple(out[j * n + i].reshape(shapes[i]) for j in range(3)) for i in range(n)]


def rms_norm(x, g):
    return x * lax.rsqrt(jnp.mean(jnp.square(x), axis=-1, keepdims=True) + EPS) * g


def modulate(x, g, shift, scale):
    return rms_norm(x, g) * (1 + scale) + shift


def rope_tables(n_tokens, rot_dim):
    t = jnp.arange(n_tokens)
    rows = (t // GRID_W).astype(F32)
    cols = (t % GRID_W).astype(F32)
    axis_dim = rot_dim // 2
    freqs = ROPE_BASE ** (-jnp.arange(0, axis_dim, 2, dtype=F32) / axis_dim)
    ang_r, ang_c = rows[:, None] * freqs, cols[:, None] * freqs
    ang = jnp.concatenate([ang_r, ang_r, ang_c, ang_c], axis=-1)
    return jnp.cos(ang), jnp.sin(ang)


def rope(x, cos, sin):
    x1, x2, x3, x4 = jnp.split(x, 4, axis=-1)
    rot = jnp.concatenate([-x2, x1, -x4, x3], axis=-1)
    return x * cos[:, None, :] + rot * sin[:, None, :]


def heads_first(t):
    return jnp.swapaxes(t, 1, 2)


def tokens_matmul(t, w):
    b, n, k = t.shape
    return linear(t.reshape(b * n, k), w).reshape(b, n, w.shape[1])


def s5_discretize(lam_re, lam_im, log_dt, b_re, b_im):
    dt = jnp.exp(log_dt)[:, None]
    mag = jnp.exp(lam_re * dt)
    a_re = mag * jnp.cos(lam_im * dt)
    a_im = mag * jnp.sin(lam_im * dt)
    den = jnp.square(lam_re) + jnp.square(lam_im)
    f_re = ((a_re - 1.0) * lam_re + a_im * lam_im) / den
    f_im = (a_im * lam_re - (a_re - 1.0) * lam_im) / den
    bb_re = f_re[..., None] * b_re - f_im[..., None] * b_im
    bb_im = f_re[..., None] * b_im + f_im[..., None] * b_re
    return a_re, a_im, bb_re, bb_im


def s5_mixer(u_lat, u_ctx, p, j, need_ctx):
    b, n, _ = u_lat.shape
    c = u_ctx.shape[1]
    half_groups = SSM_GROUPS // 2
    eye = jnp.eye(half_groups, dtype=F32)
    a_res, a_ims, b_res, b_ims, c_res, c_ims, seqs = [], [], [], [], [], [], []
    for d in range(2):
        a_re, a_im, bb_re, bb_im = s5_discretize(p['ssm_lam_re'][j, d], p['ssm_lam_im'][j, d], p['ssm_log_dt'][j, d],
                                                 p['ssm_b_re'][j, d], p['ssm_b_im'][j, d])
        for half in range(2):
            grp = slice(half * half_groups, (half + 1) * half_groups)
            a_res.append(a_re[grp].reshape(S5_LANES))
            a_ims.append(a_im[grp].reshape(S5_LANES))
            b_res.append(jnp.einsum('gsp,gh->gphs', bb_re[grp], eye).reshape(S5_HALF, S5_LANES))
            b_ims.append(jnp.einsum('gsp,gh->gphs', bb_im[grp], eye).reshape(S5_HALF, S5_LANES))
            c_res.append(jnp.einsum('gps,gh->gshp', p['ssm_c_re'][j, d][grp], eye).reshape(S5_LANES, S5_HALF))
            c_ims.append(jnp.einsum('gps,gh->gshp', p['ssm_c_im'][j, d][grp], eye).reshape(S5_LANES, S5_HALF))
        flip = (lambda t: t[:, ::-1]) if d == 1 else (lambda t: t)
        seq = jnp.concatenate([flip(u_ctx), flip(u_lat)], axis=1)
        seqs.append(jnp.transpose(seq.reshape(b, c + n, 2, S5_HALF), (2, 0, 1, 3)))
    u = jnp.stack(seqs).reshape(4 * b, c + n, S5_HALF)
    rep = lambda parts: jnp.repeat(jnp.stack(parts), b, axis=0)
    y = s5_core(u, rep(a_res), rep(a_ims), jnp.stack(b_res), jnp.stack(b_ims), jnp.stack(c_res), jnp.stack(c_ims))
    y = jnp.transpose(y.reshape(2, 2, b, c + n, S5_HALF), (0, 2, 3, 1, 4)).reshape(2, b, c + n, SSM_WIDTH)
    d_skip = p['ssm_d'][j]
    y_lat = d_skip * u_lat + y[0, :, c:] + y[1, :, c:][:, ::-1]
    wg, bg = p['ssm_w_glu'][j], p['ssm_b_glu'][j]

    def glu(t):
        t = jax.nn.gelu(t)
        return t * jax.nn.sigmoid(tokens_matmul(t, wg) + bg)

    if not need_ctx:
        return glu(y_lat), None
    y_ctx = d_skip * u_ctx + y[0, :, :c] + y[1, :, :c][:, ::-1]
    return glu(y_lat), glu(y_ctx)


def even_mixer(a_lat, a_ctx, p, j, need_ctx):
    b, n, _ = a_lat.shape
    c = a_ctx.shape[1]
    cos, sin = rope_tables(n, HEAD_DIM)
    proj = tokens_matmul(jnp.concatenate([a_ctx, a_lat], axis=1), p['e_w_in'][j])
    q, k, v, u = jnp.split(proj, [GQA_Q_W, GQA_Q_W + GQA_KV_W, GQA_Q_W + 2 * GQA_KV_W], axis=-1)
    q = rms_norm(q.reshape(b, c + n, GQA_Q_HEADS, HEAD_DIM), p['e_g_q'][j])
    k = rms_norm(k.reshape(b, c + n, GQA_KV_HEADS, HEAD_DIM), p['e_g_k'][j])
    v = v.reshape(b, c + n, GQA_KV_HEADS, HEAD_DIM)
    q_l = rope(q[:, c:], cos, sin)
    k = jnp.concatenate([k[:, :c], rope(k[:, c:], cos, sin)], axis=1)
    scale = HEAD_DIM ** -0.5
    kh, vh = heads_first(k), heads_first(v)
    att_l = heads_first(attention(heads_first(q_l), kh, vh, scale)).reshape(b, n, GQA_Q_W)
    ssm_l, ssm_c = s5_mixer(u[:, c:], u[:, :c], p, j, need_ctx)
    mix_l = jnp.concatenate([att_l, ssm_l], axis=-1)
    if not need_ctx:
        return tokens_matmul(mix_l, p['e_w_out'][j]), None
    att_c = heads_first(attention(heads_first(q[:, :c]), kh[:, :, :c], vh[:, :, :c], scale)).reshape(b, c, GQA_Q_W)
    mix = jnp.concatenate([jnp.concatenate([att_c, ssm_c], axis=-1), mix_l], axis=1)
    out = tokens_matmul(mix, p['e_w_out'][j])
    return out[:, c:], out[:, :c]


def odd_mixer(a_lat, a_ctx, p, j, need_ctx):
    b, n, _ = a_lat.shape
    c = a_ctx.shape[1]
    t = c + n
    cos, sin = rope_tables(n, MLA_ROPE)
    proj = tokens_matmul(jnp.concatenate([a_ctx, a_lat], axis=1), p['o_w_in'][j])
    c1, c2, c3 = MLA_Q_RANK, MLA_Q_RANK + MLA_KV_RANK, MLA_Q_RANK + MLA_KV_RANK + MLA_ROPE
    cq, ckv, kr = proj[..., :c1], proj[..., c1:c2], proj[..., c2:c3]
    nq, nk, nv = jnp.split(proj[..., ODD_NA_AT:], 3, axis=-1)
    q = tokens_matmul(rms_norm(cq, p['mla_g_cq'][j]), p['mla_w_uq'][j]).reshape(b, t, MLA_HEADS, MLA_QK)
    kv = tokens_matmul(rms_norm(ckv, p['mla_g_ckv'][j]), p['mla_w_ukv'][j]).reshape(b, t, MLA_HEADS, MLA_NOPE + MLA_V)
    k = jnp.concatenate([kv[..., :MLA_NOPE], jnp.broadcast_to(kr[:, :, None, :], (b, t, MLA_HEADS, MLA_ROPE))], axis=-1)
    q, k, mv = rms_norm(q, p['mla_g_q'][j]), rms_norm(k, p['mla_g_k'][j]), kv[..., MLA_NOPE:]

    def rope_tail(x):
        tail = jnp.concatenate([x[:, :c, :, MLA_NOPE:], rope(x[:, c:, :, MLA_NOPE:], cos, sin)], axis=1)
        return jnp.concatenate([x[..., :MLA_NOPE], tail], axis=-1)

    q, k = rope_tail(q), rope_tail(k)
    qh, kh, vh = heads_first(q), heads_first(k), heads_first(mv)
    mla_scale = MLA_QK ** -0.5
    mla_l = heads_first(attention(qh[:, :, c:], kh, vh, mla_scale)).reshape(b, n, MLA_HEADS * MLA_V)
    nq = heads_first(rms_norm(nq.reshape(b, t, NA_HEADS, HEAD_DIM), p['na_g_q'][j]))
    nk = heads_first(rms_norm(nk.reshape(b, t, NA_HEADS, HEAD_DIM), p['na_g_k'][j]))
    nv = heads_first(nv.reshape(b, t, NA_HEADS, HEAD_DIM))
    na_scale = HEAD_DIM ** -0.5
    na_l = na_attention(nq[:, :, c:], nk[:, :, c:], nv[:, :, c:], nk[:, :, :c], nv[:, :, :c], na_bias_table(p['na_rpb'][j]),
                        na_scale)
    na_l = heads_first(na_l).reshape(b, n, NA_W)
    mix_l = jnp.concatenate([mla_l, na_l], axis=-1)
    if not need_ctx:
        return tokens_matmul(mix_l, p['o_w_out'][j]), None
    mla_c = heads_first(attention(qh[:, :, :c], kh[:, :, :c], vh[:, :, :c], mla_scale)).reshape(b, c, MLA_HEADS * MLA_V)
    na_c = heads_first(attention(nq[:, :, :c], nk[:, :, :c], nv[:, :, :c], na_scale)).reshape(b, c, NA_W)
    mix = jnp.concatenate([jnp.concatenate([mla_c, na_c], axis=-1), mix_l], axis=1)
    out = tokens_matmul(mix, p['o_w_out'][j])
    return out[:, c:], out[:, :c]


def mlp(h, w1, w2):
    return tokens_matmul(jnp.square(jax.nn.relu(tokens_matmul(h, w1))), w2)


def local_loss(x, p, m_lat, m_ctx, ctx, target):
    depth = m_lat.shape[0]
    c = ctx.shape[1]
    xc = ctx
    for i in range(depth):
        need_ctx = i < depth - 1
        j = i // 2
        ml = [m_lat[i, :, s][:, None, :] for s in range(N_MOD)]
        mc = [m_ctx[i, s][None, None, :] for s in range(N_MOD)]
        a_lat = modulate(x, p['g_norm1'][i], ml[0], ml[1])
        a_ctx = modulate(xc, p['g_norm1'][i], mc[0], mc[1])
        mixer = even_mixer if i % 2 == 0 else odd_mixer
        o_lat, o_ctx = mixer(a_lat, a_ctx, p, j, need_ctx)
        x = x + ml[2] * o_lat
        h_lat = modulate(x, p['g_norm2'][i], ml[3], ml[4])
        if need_ctx:
            xc = xc + mc[2] * o_ctx
            h_ctx = modulate(xc, p['g_norm2'][i], mc[3], mc[4])
            ff = mlp(jnp.concatenate([h_ctx, h_lat], axis=1), p['w_ff1'][i], p['w_ff2'][i])
            x = x + ml[5] * ff[:, c:]
            xc = xc + mc[5] * ff[:, :c]
        else:
            x = x + ml[5] * mlp(h_lat, p['w_ff1'][i], p['w_ff2'][i])
    return 0.5 * jnp.sum(jnp.mean(jnp.square(x - target), axis=-1))


def _packed_rows(size, layout):
    width, group = layout
    return -(-size // (width * group)) * group


def _pack_rows(flat, layout):
    width = layout[0]
    rows = _packed_rows(flat.shape[-1], layout)
    flat = jnp.pad(flat, [(0, 0)] * (flat.ndim - 1) + [(0, rows * width - flat.shape[-1])])
    return flat.reshape(flat.shape[:-1] + (rows, width))


def _unpack_rows(rows, shape):
    lead = rows.shape[:-2]
    return rows.reshape(lead + (-1,))[..., :math.prod(shape)].reshape(lead + tuple(shape))


def _unpack_all(packed, shapes, layout):
    out, at = [], 0
    for shape in shapes:
        rows = _packed_rows(math.prod(shape), layout)
        out.append(_unpack_rows(packed[..., at:at + rows, :], shape))
        at += rows
    return out


def _join_shards(g, axis):
    g = jnp.moveaxis(g, 0, axis)
    return g.reshape(g.shape[:axis] + (N_DEV * g.shape[axis + 1],) + g.shape[axis + 2:])


def _split_shards(full, axis):
    s = full.shape
    return jnp.moveaxis(full.reshape(s[:axis] + (N_DEV, s[axis] // N_DEV) + s[axis + 1:]), axis, 0)


def _gather_packed(parts, dtype, layout, hbm, name):
    packed = jnp.concatenate([_pack_rows(a.astype(dtype).reshape(-1), layout) for a in parts], axis=0)
    return _unpack_all(_exchange(packed, True, hbm, name), [a.shape for a in parts], layout)


def kernel(x, c, ctx, c_ctx, w_mod, b_mod, g_norm1, g_norm2, w_ff1, w_ff2, e_w_in, e_w_out, e_g_q, e_g_k, ssm_lam_re, ssm_lam_im, ssm_log_dt, ssm_b_re, ssm_b_im, ssm_c_re, ssm_c_im, ssm_d, ssm_w_glu, ssm_b_glu, o_w_in, o_w_out, mla_g_cq, mla_g_ckv, mla_w_uq, mla_w_ukv, mla_g_q, mla_g_k, na_g_q, na_g_k, na_rpb, loss_target, m_c_ctx, m_w_mod, m_b_mod, m_g_norm1, m_g_norm2, m_w_ff1, m_w_ff2, m_e_w_in, m_e_w_out, m_e_g_q, m_e_g_k, m_ssm_lam_re, m_ssm_lam_im, m_ssm_log_dt, m_ssm_b_re, m_ssm_b_im, m_ssm_c_re, m_ssm_c_im, m_ssm_d, m_ssm_w_glu, m_ssm_b_glu, m_o_w_in, m_o_w_out, m_mla_g_cq, m_mla_g_ckv, m_mla_w_uq, m_mla_w_ukv, m_mla_g_q, m_mla_g_k, m_na_g_q, m_na_g_k, m_na_rpb, v_c_ctx, v_w_mod, v_b_mod, v_g_norm1, v_g_norm2, v_w_ff1, v_w_ff2, v_e_w_in, v_e_w_out, v_e_g_q, v_e_g_k, v_ssm_lam_re, v_ssm_lam_im, v_ssm_log_dt, v_ssm_b_re, v_ssm_b_im, v_ssm_c_re, v_ssm_c_im, v_ssm_d, v_ssm_w_glu, v_ssm_b_glu, v_o_w_in, v_o_w_out, v_mla_g_cq, v_mla_g_ckv, v_mla_w_uq, v_mla_w_ukv, v_mla_g_q, v_mla_g_k, v_na_g_q, v_na_g_k, v_na_rpb):
    given = dict(locals())
    x, c, ctx, target = given['x'], given['c'], given['ctx'], given['loss_target']
    b_loc, _, d_model = x.shape
    depth = given['w_mod'].shape[0]
    ix, iy, ic = lax.axis_index('x'), lax.axis_index('y'), lax.axis_index('c')
    me = 4 * ix + 2 * iy + ic
    n_batch = N_DEV * b_loc
    mod_w = given['w_mod'].shape[2]

    c_rows = jnp.concatenate([c, jnp.zeros((8 - b_loc, d_model), F32)], axis=0)
    small = _gather_packed([c_rows] + [given[n] for n in SHARDED_SMALL], F32, PACK_SMALL, False, 'gather_small')
    c_all = small[0][:, :b_loc].reshape(n_batch, d_model)
    full = {n: _join_shards(g, SHARDED_SMALL[n]) for n, g in zip(SHARDED_SMALL, small[1:])}
    cuts = {n: (a if given[n].shape[a] % (16 if a == 1 else LANES) == 0 else None) for n, a in BIG.items()}
    big = _exchange_many([given[n].astype(BF16) for n in BIG], [cuts[n] for n in BIG], True, 'gather_weights')
    for n, g in zip(BIG, big):
        g = g if cuts[n] is not None else _join_shards(g, BIG[n])
        full[n] = [g[i] for i in range(g.shape[0])]
    c3 = MLA_Q_RANK + MLA_KV_RANK + MLA_ROPE
    full['o_w_in'] = [jnp.concatenate([w[:, :c3], jnp.zeros((w.shape[0], ODD_NA_AT - c3), BF16), w[:, c3:]], axis=-1)
                      for w in full['o_w_in']]
    for n in REPLICATED:
        full[n] = given[n]

    rows17 = 16 * (-(-(n_batch + 1) // 16))
    cond = jnp.concatenate([jax.nn.silu(c_all), jax.nn.silu(given['c_ctx'])[None],
                            jnp.zeros((rows17 - n_batch - 1, d_model), F32)], axis=0)
    mod_mine = jnp.stack([_matmul(cond, given['w_mod'][i], 'nn', F32) for i in range(depth)])
    b_mine = lax.dynamic_slice_in_dim(given['b_mod'], me * mod_w, mod_w, axis=1)
    mod_mine = mod_mine + b_mine[:, None, :]
    mod_all = _gather_packed([mod_mine], F32, PACK_SMALL, False, 'gather_mod')[0]
    mod_all = jnp.moveaxis(mod_all, 0, 2).reshape(depth, rows17, N_MOD, d_model)
    m_lat = lax.dynamic_slice_in_dim(mod_all, me * b_loc, b_loc, axis=1)
    m_ctx = mod_all[:, n_batch]

    diff = {n: full[n] for n in list(BIG) + list(SHARDED_SMALL) + REPLICATED}
    loss, (g_x, g_p, g_ml, g_mc) = jax.value_and_grad(local_loss, argnums=(0, 1, 2, 3))(x, diff, m_lat, m_ctx, ctx, target)
    loss = lax.psum(loss, ('x', 'y', 'c'))
    g_p['o_w_in'] = [jnp.concatenate([g[:, :c3], g[:, ODD_NA_AT:]], axis=-1) for g in g_p['o_w_in']]

    g_rows = jnp.concatenate([g_ml.reshape(depth, b_loc, N_MOD * d_model), g_mc.reshape(depth, 1, N_MOD * d_model),
                              jnp.zeros((depth, 8 - b_loc - 1, N_MOD * d_model), F32)], axis=1)
    g_mod_all = _gather_packed([g_rows], F32, PACK_SMALL, False, 'gather_mod_grads')[0]
    g_lat_all = jnp.moveaxis(g_mod_all[:, :, :b_loc], 0, 1).reshape(depth, n_batch, N_MOD * d_model)
    g_ctx_all = g_mod_all[0, :, b_loc]
    for dev in range(1, N_DEV):
        g_ctx_all = g_ctx_all + g_mod_all[dev, :, b_loc]
    g_mod17 = jnp.concatenate([g_lat_all, g_ctx_all[:, None], jnp.zeros((depth, rows17 - n_batch - 1, N_MOD * d_model), F32)],
                              axis=1)
    grad_b_mod = jnp.sum(g_mod17, axis=1)
    g_mod_mine = lax.dynamic_slice_in_dim(g_mod17, me * mod_w, mod_w, axis=2)
    grad_w_mod = jnp.stack([_matmul(cond, g_mod_mine[i], 'tn', F32) for i in range(depth)])
    d_cond = _matmul(g_mod_mine[0], given['w_mod'][0], 'nt', F32)
    for i in range(1, depth):
        d_cond = d_cond + _matmul(g_mod_mine[i], given['w_mod'][i], 'nt', F32)
    d_cond_ctx = d_cond[n_batch]

    small_names = REPLICATED + list(SHARDED_SMALL)
    parts = [d_cond_ctx] + [g_p[n] for n in small_names]
    packed = jnp.concatenate([_pack_rows(a.reshape(-1), PACK_SMALL) for a in parts], axis=0)
    summed = _unpack_all(_all_reduce_small(packed), [a.shape for a in parts], PACK_SMALL)
    grads = dict(zip(['c_ctx'] + small_names, summed))
    c_ctx = given['c_ctx']
    sig = jax.nn.sigmoid(c_ctx)
    grads['c_ctx'] = grads['c_ctx'] * (sig * (1 + c_ctx * (1 - sig)))
    for n, axis in SHARDED_SMALL.items():
        width = given[n].shape[axis]
        grads[n] = lax.dynamic_slice_in_dim(grads[n], me * width, width, axis=axis)
    grads['w_mod'], grads['b_mod'] = grad_w_mod, grad_b_mod

    stacked = [jnp.stack(g_p[n]) for n in BIG]
    stacked = [g if cuts[n] is not None else _split_shards(g, BIG[n]) for n, g in zip(BIG, stacked)]
    landed = _exchange_many(stacked, [cuts[n] for n in BIG], False, 'scatter_weight_grads')

    upd = {}
    for n, slots in zip(BIG, landed):
        grads[n], *upd[n] = _adamw_landed(slots, given[n], given['m_' + n], given['v_' + n])
    upd['w_mod'] = _adamw_big(given['w_mod'], grads['w_mod'], given['m_w_mod'], given['v_w_mod'])
    rest = [n for n in WEIGHTS if n not in upd]
    out = _adamw_small([given[n] for n in rest], [grads[n] for n in rest], [given['m_' + n] for n in rest],
                       [given['v_' + n] for n in rest])
    upd.update(dict(zip(rest, out)))
    return (loss, g_x, *[grads[n] for n in WEIGHTS], *[upd[n][0] for n in WEIGHTS], *[upd[n][1] for n in WEIGHTS],
            *[upd[n][2] for n in WEIGHTS])
```

```python
import functools
import math

import jax
import jax.numpy as jnp
from jax import lax
from jax.experimental import pallas as pl
from jax.experimental.pallas import tpu as pltpu

F32, BF16 = jnp.float32, jnp.bfloat16
MESH = pl.DeviceIdType.MESH
N_DEV = 8
VMEM_LIMIT_BYTES = 56 * 1024 * 1024
LANES = 128
PACK_SMALL = (128, 8)

GRID_W = 64
HEAD_DIM = 64
ROPE_BASE = 10000.0
EPS = 1e-6
N_MOD = 6
GQA_Q_HEADS, GQA_KV_HEADS = 12, 4
GQA_Q_W, GQA_KV_W = GQA_Q_HEADS * HEAD_DIM, GQA_KV_HEADS * HEAD_DIM
SSM_WIDTH, SSM_GROUP, SSM_GROUPS, SSM_STATE = 256, 16, 16, 64
MLA_HEADS, MLA_Q_RANK, MLA_KV_RANK, MLA_NOPE, MLA_ROPE, MLA_V = 8, 512, 256, 64, 32, 64
MLA_QK = MLA_NOPE + MLA_ROPE
NA_HEADS, NA_WIN_R, NA_WIN_C = 8, 8, 16
NA_W = NA_HEADS * HEAD_DIM
ODD_IN_W = MLA_Q_RANK + MLA_KV_RANK + MLA_ROPE + 3 * NA_W
ODD_NA_AT = 1024
ODD_IN_PAD = ODD_NA_AT + 3 * NA_W
NEG = -1e30

ADAM_LR, ADAM_B1, ADAM_B2, ADAM_EPS, ADAM_WD, ADAM_STEP = 0.001, 0.9, 0.999, 1e-08, 0.01, 10

FWD_PARAMS = ['x', 'c', 'ctx', 'c_ctx', 'w_mod', 'b_mod', 'g_norm1', 'g_norm2', 'w_ff1', 'w_ff2', 'e_w_in', 'e_w_out',
              'e_g_q', 'e_g_k', 'ssm_lam_re', 'ssm_lam_im', 'ssm_log_dt', 'ssm_b_re', 'ssm_b_im', 'ssm_c_re', 'ssm_c_im',
              'ssm_d', 'ssm_w_glu', 'ssm_b_glu', 'o_w_in', 'o_w_out', 'mla_g_cq', 'mla_g_ckv', 'mla_w_uq', 'mla_w_ukv',
              'mla_g_q', 'mla_g_k', 'na_g_q', 'na_g_k', 'na_rpb']
WEIGHTS = FWD_PARAMS[3:]
BIG = {'w_ff1': 2, 'w_ff2': 1, 'e_w_in': 2, 'e_w_out': 1, 'o_w_in': 2, 'o_w_out': 1, 'mla_w_uq': 2, 'mla_w_ukv': 2,
       'ssm_w_glu': 1}
SHARDED_SMALL = {'mla_g_cq': 1, 'mla_g_ckv': 1}
REPLICATED = [n for n in WEIGHTS if n not in BIG and n not in SHARDED_SMALL and n not in ('w_mod', 'c_ctx', 'b_mod')]


def _tile(dim, prefs):
    for p in prefs:
        if dim >= p and dim % p == 0:
            return p
    return dim


def _params(*sem):
    return pltpu.CompilerParams(dimension_semantics=sem, vmem_limit_bytes=VMEM_LIMIT_BYTES)


def _dot_nt(a, b):
    return lax.dot_general(a, b, (((1,), (1,)), ((), ())), preferred_element_type=F32)


def _dot_tn(a, b):
    return lax.dot_general(a, b, (((0,), (0,)), ((), ())), preferred_element_type=F32)


def _dot(a, b):
    return jnp.dot(a, b, preferred_element_type=F32)


def _matmul(a, b, kind, out_dtype, finish=None, extra=None, n_out=1):
    a, b = a.astype(BF16), b.astype(BF16)
    if kind == 'nn':
        (m, kd), n = a.shape, b.shape[1]
    elif kind == 'nt':
        (m, kd), n = a.shape, b.shape[0]
    else:
        (kd, m), n = a.shape, b.shape[1]
    tm = _tile(m, (768, 512, 256, 128))
    tn = _tile(n, (1024, 768, 512, 256, 128))
    tk = _tile(kd, (2048, 1536, 1024, 512, 256, 128))
    nk = kd // tk
    dn = {'nn': (((1,), (0,)), ((), ())), 'nt': (((1,), (1,)), ((), ())), 'tn': (((0,), (0,)), ((), ()))}[kind]

    n_in = 2 if extra is None else 3

    def body(*refs):
        a_ref, b_ref = refs[:2]
        o_refs = refs[n_in:n_in + n_out]

        def store(total):
            outs = (total,) if finish is None else finish(total, refs[2][...] if extra is not None else None)
            for o_ref, val in zip(o_refs, outs):
                o_ref[...] = val.astype(o_ref.dtype)

        part = lax.dot_general(a_ref[...], b_ref[...], dn, preferred_element_type=F32)
        if nk == 1:
            store(part)
            return
        acc_ref, k = refs[n_in + n_out], pl.program_id(2)

        @pl.when(k == 0)
        def _():
            acc_ref[...] = part

        @pl.when((k > 0) & (k < nk - 1))
        def _():
            acc_ref[...] += part

        @pl.when(k == nk - 1)
        def _():
            store(acc_ref[...] + part)

    a_spec = pl.BlockSpec((tk, tm), lambda i, j, k: (k, i)) if kind == 'tn' else pl.BlockSpec((tm, tk), lambda i, j, k: (i, k))
    b_spec = pl.BlockSpec((tn, tk), lambda i, j, k: (j, k)) if kind == 'nt' else pl.BlockSpec((tk, tn), lambda i, j, k: (k, j))
    o_spec = pl.BlockSpec((tm, tn), lambda i, j, k: (i, j))
    out = pl.pallas_call(
        body, name='mm_' + kind, grid=(m // tm, n // tn, nk),
        out_shape=[jax.ShapeDtypeStruct((m, n), out_dtype)] * n_out,
        in_specs=[a_spec, b_spec] + ([o_spec] if extra is not None else []), out_specs=[o_spec] * n_out,
        scratch_shapes=[pltpu.VMEM((tm, tn), F32)] if nk > 1 else [],
        compiler_params=_params('parallel', 'parallel', 'arbitrary'),
    )(*((a, b) if extra is None else (a, b, extra)))
    return out[0] if n_out == 1 else out


@jax.custom_vjp
def linear(a, w):
    return _matmul(a, w, 'nn', F32)


def _linear_fwd(a, w):
    ab = a.astype(BF16)
    return _matmul(ab, w, 'nn', F32), (ab, w)


def _linear_bwd(res, g):
    ab, w = res
    gb = g.astype(BF16)
    return _matmul(gb, w, 'nt', F32), _matmul(ab, gb, 'tn', w.dtype)


linear.defvjp(_linear_fwd, _linear_bwd)


def _relu2(z, _):
    r = jnp.maximum(z, 0.0)
    return r, r * r


def _relu2_grad(d_act, r):
    return (d_act * (2.0 * r.astype(F32)),)


@jax.custom_vjp
def mlp_rows(h, w1, w2):
    return _mlp_rows_fwd(h, w1, w2)[0]


def _mlp_rows_fwd(h, w1, w2):
    hb = h.astype(BF16)
    r, act = _matmul(hb, w1, 'nn', BF16, finish=_relu2, n_out=2)
    return _matmul(act, w2, 'nn', F32), (hb, w1, w2, r, act)


def _mlp_rows_bwd(res, g):
    hb, w1, w2, r, act = res
    gb = g.astype(BF16)
    dz = _matmul(gb, w2, 'nt', BF16, finish=_relu2_grad, extra=r)
    return _matmul(dz, w1, 'nt', F32), _matmul(hb, dz, 'tn', w1.dtype), _matmul(act, gb, 'tn', w2.dtype)


mlp_rows.defvjp(_mlp_rows_fwd, _mlp_rows_bwd)


LOG2E = math.log2(math.e)


def _softmax_cols(t):
    m = jnp.max(t, axis=0, keepdims=True)
    e = jnp.exp2(t - m)
    return e * (1.0 / jnp.sum(e, axis=0, keepdims=True))


def _attn_specs(q, k, v, bq):
    _, h, nq, dq = q.shape
    _, hk, nk, dv = v.shape
    g = h // hk
    at_q = lambda b, j, gi, i: (b, j * g + gi, i, 0)
    at_qt = lambda b, j, gi, i: (b, j * g + gi, 0, i)
    at_k = lambda b, j, gi, i: (b, j, 0, 0)
    specs = dict(q=pl.BlockSpec((None, None, bq, dq), at_q), o=pl.BlockSpec((None, None, bq, dv), at_q),
                 qt=pl.BlockSpec((None, None, dq, bq), at_qt), ot=pl.BlockSpec((None, None, dv, bq), at_qt),
                 k=pl.BlockSpec((None, None, nk, dq), at_k), v=pl.BlockSpec((None, None, nk, dv), at_k),
                 kt=pl.BlockSpec((None, None, dq, nk), at_k), vt=pl.BlockSpec((None, None, dv, nk), at_k))
    return (q.shape[0], hk, g, nq // bq), specs


def _attn_fwd_call(q, k, vt, scale):
    b, h, nq, _ = q.shape
    dv = vt.shape[2]
    bq = _tile(nq, (256, 128))
    grid, s = _attn_specs(q, k, jnp.swapaxes(vt, 2, 3), bq)

    def body(q_ref, k_ref, vt_ref, ot_ref):
        p = _softmax_cols(_dot_nt(k_ref[...], q_ref[...]) * (scale * LOG2E))
        ot_ref[...] = _dot(vt_ref[...], p.astype(BF16))

    return pl.pallas_call(
        body, name='attn_fwd', grid=grid, out_shape=jax.ShapeDtypeStruct((b, h, dv, nq), F32),
        in_specs=[s['q'], s['k'], s['vt']], out_specs=s['ot'],
        compiler_params=_params('parallel', 'parallel', 'arbitrary', 'arbitrary'),
    )(q, k, vt)


def _attn_bwd_call(q, k, kt, v, do, scale):
    b, h, nq, dq = q.shape
    bq = _tile(nq, (256, 128))
    grid, s = _attn_specs(q, k, v, bq)

    def body(q_ref, k_ref, kt_ref, v_ref, do_ref, dqt_ref, dk_ref, dv_ref):
        @pl.when((pl.program_id(2) == 0) & (pl.program_id(3) == 0))
        def _():
            dk_ref[...] = jnp.zeros_like(dk_ref)
            dv_ref[...] = jnp.zeros_like(dv_ref)

        qb, kb, dob = q_ref[...], k_ref[...], do_ref[...]
        p = _softmax_cols(_dot_nt(kb, qb) * (scale * LOG2E))
        dp = _dot_nt(v_ref[...], dob)
        ds = p * (dp - jnp.sum(p * dp, axis=0, keepdims=True))
        dsb = (ds * scale).astype(BF16)
        dqt_ref[...] = _dot(kt_ref[...], dsb)
        dk_ref[...] += _dot(dsb, qb)
        dv_ref[...] += _dot(p.astype(BF16), dob)

    return pl.pallas_call(
        body, name='attn_bwd', grid=grid,
        out_shape=(jax.ShapeDtypeStruct((b, h, dq, nq), F32), jax.ShapeDtypeStruct(k.shape, F32), jax.ShapeDtypeStruct(v.shape, F32)),
        in_specs=[s['q'], s['k'], s['kt'], s['v'], s['o']], out_specs=(s['qt'], s['k'], s['v']),
        compiler_params=_params('parallel', 'parallel', 'arbitrary', 'arbitrary'),
    )(q, k, kt, v, do)


@functools.partial(jax.custom_vjp, nondiff_argnums=(3,))
def attention(q, k, v, scale):
    return _attention_fwd(q, k, v, scale)[0]


def _attention_fwd(q, k, v, scale):
    qb, kb, vb = q.astype(BF16), k.astype(BF16), v.astype(BF16)
    return jnp.swapaxes(_attn_fwd_call(qb, kb, jnp.swapaxes(vb, 2, 3), scale), 2, 3), (qb, kb, vb)


def _attention_bwd(scale, res, g):
    qb, kb, vb = res
    dqt, dk, dv = _attn_bwd_call(qb, kb, jnp.swapaxes(kb, 2, 3), vb, g.astype(BF16), scale)
    return jnp.swapaxes(dqt, 2, 3), dk, dv


attention.defvjp(_attention_fwd, _attention_bwd)


def _na_window(r, rows):
    start = jnp.clip(r - NA_WIN_R // 2, 0, rows - NA_WIN_R)
    return start, r - start


def _na_scores(q, kw, kc, bias, scale):
    s1 = _dot_nt(q, kw) * scale + bias
    s2 = _dot_nt(q, kc) * scale
    m = jnp.maximum(jnp.max(s1, axis=-1, keepdims=True), jnp.max(s2, axis=-1, keepdims=True))
    e1, e2 = jnp.exp(s1 - m), jnp.exp(s2 - m)
    inv = 1.0 / (jnp.sum(e1, axis=-1, keepdims=True) + jnp.sum(e2, axis=-1, keepdims=True))
    return e1 * inv, e2 * inv


def _na_specs(q, kc):
    _, _, n, d = q.shape
    c = kc.shape[2]
    win = NA_WIN_R * GRID_W
    tok = pl.BlockSpec((None, None, n, d), lambda b, h: (b, h, 0, 0))
    ctx = pl.BlockSpec((None, None, c, d), lambda b, h: (b, h, 0, 0))
    bias = pl.BlockSpec((None, NA_WIN_R, GRID_W, win), lambda b, h: (h, 0, 0, 0))
    dbias = pl.BlockSpec((None, None, NA_WIN_R, GRID_W, win), lambda b, h: (b, h, 0, 0, 0))
    return tok, ctx, bias, dbias


def _na_fwd_call(q, k, v, kc, vc, bias, scale):
    b, h, n, d = q.shape
    rows, win = n // GRID_W, NA_WIN_R * GRID_W
    tok, ctx, bias_spec, _ = _na_specs(q, kc)

    def body(q_ref, k_ref, v_ref, kc_ref, vc_ref, b_ref, o_ref):
        def row(r, carry):
            start, off = _na_window(r, rows)
            at = pl.ds(pl.multiple_of(r * GRID_W, GRID_W), GRID_W)
            wat = pl.ds(pl.multiple_of(start * GRID_W, GRID_W), win)
            p1, p2 = _na_scores(q_ref[at, :], k_ref[wat, :], kc_ref[...], b_ref[off], scale)
            o_ref[at, :] = _dot(p1.astype(BF16), v_ref[wat, :]) + _dot(p2.astype(BF16), vc_ref[...])
            return carry

        lax.fori_loop(0, rows, row, 0)

    return pl.pallas_call(
        body, name='na_fwd', grid=(b, h), out_shape=jax.ShapeDtypeStruct(q.shape, F32),
        in_specs=[tok, tok, tok, ctx, ctx, bias_spec], out_specs=tok,
        compiler_params=_params('parallel', 'parallel'),
    )(q, k, v, kc, vc, bias)


def _na_bwd_call(q, k, v, kc, vc, bias, do, scale):
    b, h, n, d = q.shape
    rows, win = n // GRID_W, NA_WIN_R * GRID_W
    tok, ctx, bias_spec, dbias_spec = _na_specs(q, kc)

    def body(q_ref, k_ref, v_ref, kc_ref, vc_ref, b_ref, do_ref, dq_ref, dk_ref, dv_ref, dkc_ref, dvc_ref, db_ref):
        for ref in (dk_ref, dv_ref, dkc_ref, dvc_ref, db_ref):
            ref[...] = jnp.zeros_like(ref)

        def row(r, carry):
            start, off = _na_window(r, rows)
            at = pl.ds(pl.multiple_of(r * GRID_W, GRID_W), GRID_W)
            wat = pl.ds(pl.multiple_of(start * GRID_W, GRID_W), win)
            qb, kw, vw, dob = q_ref[at, :], k_ref[wat, :], v_ref[wat, :], do_ref[at, :]
            kcb, vcb = kc_ref[...], vc_ref[...]
            p1, p2 = _na_scores(qb, kw, kcb, b_ref[off], scale)
            dp1, dp2 = _dot_nt(dob, vw), _dot_nt(dob, vcb)
            delta = jnp.sum(p1 * dp1, axis=-1, keepdims=True) + jnp.sum(p2 * dp2, axis=-1, keepdims=True)
            ds1, ds2 = p1 * (dp1 - delta), p2 * (dp2 - delta)
            db_ref[off] += ds1
            ds1b, ds2b = (ds1 * scale).astype(BF16), (ds2 * scale).astype(BF16)
            dq_ref[at, :] = _dot(ds1b, kw) + _dot(ds2b, kcb)
            dk_ref[wat, :] += _dot_tn(ds1b, qb)
            dv_ref[wat, :] += _dot_tn(p1.astype(BF16), dob)
            dkc_ref[...] += _dot_tn(ds2b, qb)
            dvc_ref[...] += _dot_tn(p2.astype(BF16), dob)
            return carry

        lax.fori_loop(0, rows, row, 0)

    f = lambda a: jax.ShapeDtypeStruct(a.shape, F32)
    return pl.pallas_call(
        body, name='na_bwd', grid=(b, h),
        out_shape=(f(q), f(k), f(v), f(kc), f(vc), jax.ShapeDtypeStruct((b,) + bias.shape, F32)),
        in_specs=[tok, tok, tok, ctx, ctx, bias_spec, tok], out_specs=(tok, tok, tok, ctx, ctx, dbias_spec),
        compiler_params=_params('parallel', 'parallel'),
    )(q, k, v, kc, vc, bias, do)


@functools.partial(jax.custom_vjp, nondiff_argnums=(6,))
def na_attention(q, k, v, kc, vc, bias, scale):
    return _na_fwd_call(q.astype(BF16), k.astype(BF16), v.astype(BF16), kc.astype(BF16), vc.astype(BF16), bias, scale)


def _na_attention_fwd(q, k, v, kc, vc, bias, scale):
    res = (q.astype(BF16), k.astype(BF16), v.astype(BF16), kc.astype(BF16), vc.astype(BF16), bias)
    return _na_fwd_call(*res, scale), res


def _na_attention_bwd(scale, res, g):
    dq, dk, dv, dkc, dvc, db = _na_bwd_call(*res, g.astype(BF16), scale)
    return dq, dk, dv, dkc, dvc, jnp.sum(db, axis=0)


na_attention.defvjp(_na_attention_fwd, _na_attention_bwd)


def _na_table_index():
    qcol = jnp.arange(GRID_W)
    kcol = jnp.arange(GRID_W)
    cstart = jnp.clip(qcol - NA_WIN_C // 2, 0, GRID_W - NA_WIN_C)
    inside = (kcol[None, :] >= cstart[:, None]) & (kcol[None, :] < cstart[:, None] + NA_WIN_C)
    cidx = jnp.clip(kcol[None, :] - qcol[:, None] + (NA_WIN_C - 1), 0, 2 * NA_WIN_C - 2)
    ridx = jnp.arange(NA_WIN_R)[None, :] - jnp.arange(NA_WIN_R)[:, None] + (NA_WIN_R - 1)
    return inside, cidx, ridx


@jax.custom_vjp
def na_bias_table(rpb):
    inside, cidx, ridx = _na_table_index()
    t = rpb[:, ridx][:, :, :, cidx]
    t = jnp.where(inside[None, None, None], t, NEG)
    return jnp.transpose(t, (0, 1, 3, 2, 4)).reshape(rpb.shape[0], NA_WIN_R, GRID_W, NA_WIN_R * GRID_W)


def _na_bias_table_bwd(_, dt):
    inside, cidx, ridx = _na_table_index()
    pick_c = ((cidx[..., None] == jnp.arange(2 * NA_WIN_C - 1)) & inside[..., None]).astype(F32)
    pick_r = (ridx[..., None] == jnp.arange(2 * NA_WIN_R - 1)).astype(F32)
    d5 = dt.reshape(dt.shape[0], NA_WIN_R, GRID_W, NA_WIN_R, GRID_W)
    part = jnp.einsum('hoqjk,qkb->hojb', d5, pick_c, precision=lax.Precision.HIGHEST)
    return (jnp.einsum('hojb,oja->hab', part, pick_r, precision=lax.Precision.HIGHEST),)


na_bias_table.defvjp(lambda rpb: (na_bias_table(rpb), None), _na_bias_table_bwd)


S5_HALF = SSM_WIDTH // 2
S5_LANES = (SSM_GROUPS // 2) * SSM_STATE
S5_Q = S5_LANES // LANES


def _s5_tiles(a):
    r = a.shape[0]
    return jnp.transpose(a.reshape(r, S5_Q, LANES), (1, 0, 2)).reshape(S5_Q * r, LANES)


def _s5_untiles(a):
    r = a.shape[0] // S5_Q
    return jnp.transpose(a.reshape(S5_Q, r, LANES), (1, 0, 2)).reshape(r, S5_LANES)


def _s5_put(ref, r, rr, tc, val):
    for q in range(S5_Q):
        ref[pl.ds((q * rr + r) * tc, tc), :] = val[:, q * LANES:(q + 1) * LANES]


def _s5_get(ref, r, rr, tc):
    return jnp.concatenate([ref[pl.ds((q * rr + r) * tc, tc), :] for q in range(S5_Q)], axis=1)


def _s5_fwd_call(u, a_re, a_im, b_re, b_im, c_re, c_im):
    rr, t_len, _ = u.shape
    sets = b_re.shape[0]
    per = rr // sets
    tc = _tile(t_len, (256, 128))
    nt = t_len // tc
    qr = S5_Q * rr

    def body(u_ref, ar_ref, ai_ref, br_ref, bi_ref, cr_ref, ci_ref, y_ref, hr_ref, hi_ref, sr_ref, si_ref):
        @pl.when(pl.program_id(0) == 0)
        def _():
            sr_ref[...] = jnp.zeros_like(sr_ref)
            si_ref[...] = jnp.zeros_like(si_ref)

        for r in range(rr):
            ub = u_ref[r]
            _s5_put(hr_ref, r, rr, tc, _dot(ub, br_ref[r // per]))
            _s5_put(hi_ref, r, rr, tc, _dot(ub, bi_ref[r // per]))
        ar, ai = ar_ref[...], ai_ref[...]

        def step(t, carry):
            hr, hi = carry
            at = pl.ds(t, qr, stride=tc)
            nr = ar * hr - ai * hi + hr_ref[at, :]
            ni = ar * hi + ai * hr + hi_ref[at, :]
            hr_ref[at, :] = nr
            hi_ref[at, :] = ni
            return nr, ni

        hr, hi = lax.fori_loop(0, tc, step, (sr_ref[...], si_ref[...]))
        sr_ref[...] = hr
        si_ref[...] = hi
        for r in range(rr):
            y_ref[r] = (_dot(_s5_get(hr_ref, r, rr, tc).astype(BF16), cr_ref[r // per])
                        - _dot(_s5_get(hi_ref, r, rr, tc).astype(BF16), ci_ref[r // per]))

    full = lambda a: pl.BlockSpec(a.shape, lambda i: (0,) * a.ndim)
    h_spec = pl.BlockSpec((None, qr * tc, LANES), lambda i: (i, 0, 0))
    h_shape = jax.ShapeDtypeStruct((nt, qr * tc, LANES), F32)
    a_re, a_im = _s5_tiles(a_re), _s5_tiles(a_im)
    return pl.pallas_call(
        body, name='s5_fwd', grid=(nt,),
        out_shape=(jax.ShapeDtypeStruct((rr, t_len, S5_HALF), F32), h_shape, h_shape),
        in_specs=[pl.BlockSpec((rr, tc, S5_HALF), lambda i: (0, i, 0)), full(a_re), full(a_im), full(b_re), full(b_im),
                  full(c_re), full(c_im)],
        out_specs=(pl.BlockSpec((rr, tc, S5_HALF), lambda i: (0, i, 0)), h_spec, h_spec),
        scratch_shapes=[pltpu.VMEM((qr, LANES), F32), pltpu.VMEM((qr, LANES), F32)],
        compiler_params=_params('arbitrary'),
    )(u, a_re, a_im, b_re, b_im, c_re, c_im)


def _s5_bwd_call(u, a_re, a_im, b_re, b_im, c_re, c_im, h_re, h_im, dy):
    rr, t_len, _ = u.shape
    sets = b_re.shape[0]
    per = rr // sets
    nt, rows, _ = h_re.shape
    qr = S5_Q * rr
    tc = rows // qr

    def body(u_ref, dy_ref, ar_ref, ai_ref, br_ref, bi_ref, cr_ref, ci_ref, hr_ref, hi_ref,
             du_ref, dar_ref, dai_ref, dbr_ref, dbi_ref, dcr_ref, dci_ref, gr_ref, gi_ref, sr_ref, si_ref):
        i = pl.program_id(0)

        @pl.when(i == 0)
        def _():
            for ref in (dar_ref, dai_ref, dbr_ref, dbi_ref, dcr_ref, dci_ref, sr_ref, si_ref):
                ref[...] = jnp.zeros_like(ref)

        for r in range(rr):
            dyb = dy_ref[r]
            _s5_put(gr_ref, r, rr, tc, _dot_nt(dyb, cr_ref[r // per]))
            _s5_put(gi_ref, r, rr, tc, -_dot_nt(dyb, ci_ref[r // per]))
        ar, ai = ar_ref[...], ai_ref[...]
        g_r, g_i = sr_ref[...], si_ref[...]
        last = pl.ds(tc - 1, qr, stride=tc)
        d_r = g_r * hr_ref[last, :] + g_i * hi_ref[last, :]
        d_i = g_i * hr_ref[last, :] - g_r * hi_ref[last, :]

        def advance(t, g_r, g_i):
            at = pl.ds(t, qr, stride=tc)
            n_r = ar * g_r + ai * g_i + gr_ref[at, :]
            n_i = ar * g_i - ai * g_r + gi_ref[at, :]
            gr_ref[at, :] = n_r
            gi_ref[at, :] = n_i
            return n_r, n_i

        def step(k, carry):
            g_r, g_i, d_r, d_i = carry
            t = tc - 1 - k
            g_r, g_i = advance(t, g_r, g_i)
            before = pl.ds(t - 1, qr, stride=tc)
            p_r, p_i = hr_ref[before, :], hi_ref[before, :]
            return g_r, g_i, d_r + g_r * p_r + g_i * p_i, d_i + g_i * p_r - g_r * p_i

        g_r, g_i, d_r, d_i = lax.fori_loop(0, tc - 1, step, (g_r, g_i, d_r, d_i))
        g_r, g_i = advance(0, g_r, g_i)
        sr_ref[...] = g_r
        si_ref[...] = g_i
        dar_ref[...] += d_r
        dai_ref[...] += d_i
        for r in range(rr):
            s = r // per
            ub, dyb = u_ref[r], dy_ref[r]
            grb, gib = _s5_get(gr_ref, r, rr, tc).astype(BF16), _s5_get(gi_ref, r, rr, tc).astype(BF16)
            du_ref[r] = _dot_nt(grb, br_ref[s]) + _dot_nt(gib, bi_ref[s])
            dbr_ref[s] += _dot_tn(ub, grb)
            dbi_ref[s] += _dot_tn(ub, gib)
            dcr_ref[s] += _dot_tn(_s5_get(hr_ref, r, rr, tc).astype(BF16), dyb)
            dci_ref[s] -= _dot_tn(_s5_get(hi_ref, r, rr, tc).astype(BF16), dyb)

    full = lambda a: pl.BlockSpec(a.shape, lambda i: (0,) * a.ndim)
    back = lambda i: nt - 1 - i
    tok = pl.BlockSpec((rr, tc, S5_HALF), lambda i: (0, back(i), 0))
    h_spec = pl.BlockSpec((None, qr * tc, LANES), lambda i: (back(i), 0, 0))
    f = lambda a: jax.ShapeDtypeStruct(a.shape, F32)
    a_re, a_im = _s5_tiles(a_re), _s5_tiles(a_im)
    du, da_re, da_im, db_re, db_im, dc_re, dc_im = pl.pallas_call(
        body, name='s5_bwd', grid=(nt,),
        out_shape=(jax.ShapeDtypeStruct(u.shape, F32), f(a_re), f(a_im), f(b_re), f(b_im), f(c_re), f(c_im)),
        in_specs=[tok, tok, full(a_re), full(a_im), full(b_re), full(b_im), full(c_re), full(c_im), h_spec, h_spec],
        out_specs=(tok, full(a_re), full(a_im), full(b_re), full(b_im), full(c_re), full(c_im)),
        scratch_shapes=[pltpu.VMEM((qr * tc, LANES), F32), pltpu.VMEM((qr * tc, LANES), F32),
                        pltpu.VMEM((qr, LANES), F32), pltpu.VMEM((qr, LANES), F32)],
        compiler_params=_params('arbitrary'),
    )(u, dy, a_re, a_im, b_re, b_im, c_re, c_im, h_re, h_im)
    return du, _s5_untiles(da_re), _s5_untiles(da_im), db_re, db_im, dc_re, dc_im


@jax.custom_vjp
def s5_core(u, a_re, a_im, b_re, b_im, c_re, c_im):
    return _s5_fwd_call(u.astype(BF16), a_re, a_im, b_re.astype(BF16), b_im.astype(BF16), c_re.astype(BF16),
                        c_im.astype(BF16))[0]


def _s5_core_fwd(u, a_re, a_im, b_re, b_im, c_re, c_im):
    args = (u.astype(BF16), a_re, a_im, b_re.astype(BF16), b_im.astype(BF16), c_re.astype(BF16), c_im.astype(BF16))
    y, h_re, h_im = _s5_fwd_call(*args)
    return y, args + (h_re, h_im)


def _s5_core_bwd(res, g):
    return _s5_bwd_call(*res, g.astype(BF16))


s5_core.defvjp(_s5_core_fwd, _s5_core_bwd)


def _exchange(x, gather, hbm, name):
    block = x.shape if gather else x.shape[1:]

    def body(x_ref, out_ref, send_sems, recv_sems, local_sem):
        ix, iy, ic = lax.axis_index('x'), lax.axis_index('y'), lax.axis_index('c')
        me = 4 * ix + 2 * iy + ic

        def flipped(k):
            px = 1 - ix if k & 4 else ix
            py = 1 - iy if k & 2 else iy
            pc = 1 - ic if k & 1 else ic
            return (px, py, pc), 4 * px + 2 * py + pc

        def copy(k, src, dst):
            return pltpu.make_async_remote_copy(src_ref=src, dst_ref=dst, send_sem=send_sems.at[k - 1],
                                                 recv_sem=recv_sems.at[k - 1], device_id=flipped(k)[0], device_id_type=MESH)

        own = pltpu.make_async_copy(x_ref if gather else x_ref.at[me], out_ref.at[me], local_sem)
        own.start()
        sent = []
        for k in range(1, N_DEV):
            src = x_ref if gather else x_ref.at[flipped(k)[1]]
            sent.append(copy(k, src, out_ref.at[me]))
            sent[-1].start()
        for k in range(1, N_DEV):
            src = x_ref if gather else x_ref.at[flipped(k)[1]]
            copy(k, src, out_ref.at[flipped(k)[1]]).wait_recv()
        for cp in sent:
            cp.wait_send()
        own.wait()

    space = pltpu.HBM if hbm else pltpu.VMEM
    return pl.pallas_call(
        body, name=name, out_shape=jax.ShapeDtypeStruct((N_DEV,) + tuple(block), x.dtype),
        in_specs=[pl.BlockSpec(memory_space=space)], out_specs=pl.BlockSpec(memory_space=space),
        scratch_shapes=[pltpu.SemaphoreType.DMA((N_DEV - 1,)), pltpu.SemaphoreType.DMA((N_DEV - 1,)), pltpu.SemaphoreType.DMA],
        compiler_params=pltpu.CompilerParams(vmem_limit_bytes=VMEM_LIMIT_BYTES),
    )(x)


def _shard_view(ref, axis, index, width):
    return ref.at[(slice(None),) * axis + (pl.ds(pl.multiple_of(index * width, width), width),)]


def _exchange_many(xs, cuts, gather, name):
    n = len(xs)
    if gather:
        shards = [x.shape for x in xs]
    else:
        shards = [x.shape[1:] if cut is None else x.shape[:cut] + (x.shape[cut] // N_DEV,) + x.shape[cut + 1:]
                  for x, cut in zip(xs, cuts)]

    def full_shape(shard, cut):
        return shard[:cut] + (N_DEV * shard[cut],) + shard[cut + 1:]

    out_shapes = [jax.ShapeDtypeStruct((N_DEV,) + tuple(s) if (cut is None or not gather) else full_shape(tuple(s), cut), x.dtype)
                  for x, s, cut in zip(xs, shards, cuts)]

    def body(*refs):
        x_refs, out_refs = refs[:n], refs[n:2 * n]
        send_sems, recv_sems, local_sems = refs[2 * n:]
        ix, iy, ic = lax.axis_index('x'), lax.axis_index('y'), lax.axis_index('c')
        me = 4 * ix + 2 * iy + ic

        def flipped(k):
            px = 1 - ix if k & 4 else ix
            py = 1 - iy if k & 2 else iy
            pc = 1 - ic if k & 1 else ic
            return (px, py, pc), 4 * px + 2 * py + pc

        def block(ref, cut, shard, who):
            return ref.at[who] if cut is None else _shard_view(ref, cut, who, shard[cut])

        def ends(i, sender, receiver):
            if gather:
                return x_refs[i], block(out_refs[i], cuts[i], shards[i], sender)
            return block(x_refs[i], cuts[i], shards[i], receiver), out_refs[i].at[sender]

        def copy(i, k, sender, receiver):
            src, dst = ends(i, sender, receiver)
            return pltpu.make_async_remote_copy(src_ref=src, dst_ref=dst, send_sem=send_sems.at[i * (N_DEV - 1) + k - 1],
                                                 recv_sem=recv_sems.at[i * (N_DEV - 1) + k - 1], device_id=flipped(k)[0],
                                                 device_id_type=MESH)

        own = [pltpu.make_async_copy(*ends(i, me, me), local_sems.at[i]) for i in range(n)]
        for cp in own:
            cp.start()
        if gather:
            chips = (2, 4, 6)
            sent = [copy(i, k, me, flipped(k)[1]) for k in (1,) + chips for i in range(n)]
            for cp in sent:
                cp.start()
            for k in chips:
                for i in range(n):
                    copy(i, k, flipped(k)[1], me).wait_recv()
                    src, dst = ends(i, flipped(k)[1], me)
                    sent.append(pltpu.make_async_remote_copy(
                        src_ref=dst, dst_ref=dst, send_sem=send_sems.at[i * (N_DEV - 1) + k], recv_sem=recv_sems.at[i * (N_DEV - 1) + k],
                        device_id=flipped(1)[0], device_id_type=MESH))
                    sent[-1].start()
            for k in (1, 3, 5, 7):
                for i in range(n):
                    src, dst = ends(i, flipped(k)[1], me)
                    pltpu.make_async_remote_copy(
                        src_ref=dst, dst_ref=dst, send_sem=send_sems.at[i * (N_DEV - 1) + k - 1], recv_sem=recv_sems.at[i * (N_DEV - 1) + k - 1],
                        device_id=flipped(1)[0], device_id_type=MESH).wait_recv()
            for cp in sent:
                cp.wait_send()
            for cp in own:
                cp.wait()
            return
        sent = [copy(i, k, me, flipped(k)[1]) for k in range(1, N_DEV) for i in range(n)]
        for cp in sent:
            cp.start()
        for k in range(1, N_DEV):
            for i in range(n):
                copy(i, k, flipped(k)[1], me).wait_recv()
        for cp in sent:
            cp.wait_send()
        for cp in own:
            cp.wait()

    hbm = pl.BlockSpec(memory_space=pltpu.HBM)
    pairs = n * (N_DEV - 1)
    return pl.pallas_call(
        body, name=name, out_shape=out_shapes, in_specs=[hbm] * n, out_specs=[hbm] * n,
        scratch_shapes=[pltpu.SemaphoreType.DMA((pairs,)), pltpu.SemaphoreType.DMA((pairs,)), pltpu.SemaphoreType.DMA((n,))],
    )(*xs)


def _adamw_landed(landed, w, m, v):
    shape = w.shape
    w, m, v = (_as_rows(a) for a in (w, m, v))
    rows, cols = w.shape
    landed = landed.reshape(N_DEV, rows, cols)
    tr = _tile(rows, (256, 128, 64, 32, 16))

    def body(l_ref, w_ref, m_ref, v_ref, g_ref, d_ref, nm_ref, nv_ref):
        g = l_ref[0].astype(F32)
        for d in range(1, N_DEV):
            g = g + l_ref[d].astype(F32)
        g_ref[...] = g
        d_ref[...], nm_ref[...], nv_ref[...] = _adamw_math(w_ref[...], g, m_ref[...], v_ref[...])

    spec = pl.BlockSpec((tr, cols), lambda i: (i, 0))
    out = pl.pallas_call(
        body, name='adamw_landed', grid=(rows // tr,), out_shape=(jax.ShapeDtypeStruct(w.shape, F32),) * 4,
        in_specs=[pl.BlockSpec((N_DEV, tr, cols), lambda i: (0, i, 0))] + [spec] * 3, out_specs=(spec,) * 4,
        compiler_params=_params('parallel'),
    )(landed, w, m, v)
    return tuple(o.reshape(shape) for o in out)


def _all_reduce_small(x):
    g = _exchange(x, True, False, 'gather_small_grads')

    def body(g_ref, o_ref):
        acc = g_ref[0]
        for d in range(1, N_DEV):
            acc = acc + g_ref[d]
        o_ref[...] = acc

    return pl.pallas_call(body, name='sum_small', out_shape=jax.ShapeDtypeStruct(x.shape, F32))(g)


def _adamw_math(w, g, m, v):
    m = ADAM_B1 * m + (1.0 - ADAM_B1) * g
    v = ADAM_B2 * v + (1.0 - ADAM_B2) * (g * g)
    m_hat = m / (1.0 - ADAM_B1 ** ADAM_STEP)
    v_hat = v / (1.0 - ADAM_B2 ** ADAM_STEP)
    return -ADAM_LR * (m_hat / (jnp.sqrt(v_hat) + ADAM_EPS) + ADAM_WD * w), m, v


def _as_rows(a):
    return a.reshape(1, -1) if a.ndim < 2 else a.reshape(-1, a.shape[-1])


def _as_lanes(a):
    return a.reshape(-1, LANES) if a.size % LANES == 0 else a.reshape(1, -1)


def _adamw_big(w, g, m, v):
    shape = w.shape
    w, g, m, v = (_as_rows(a) for a in (w, g, m, v))
    rows, cols = w.shape
    tr = _tile(rows, (512, 256, 128, 64, 32, 16, 8))

    def body(w_ref, g_ref, m_ref, v_ref, d_ref, nm_ref, nv_ref):
        d_ref[...], nm_ref[...], nv_ref[...] = _adamw_math(w_ref[...], g_ref[...], m_ref[...], v_ref[...])

    spec = pl.BlockSpec((tr, cols), lambda i: (i, 0))
    out = pl.pallas_call(
        body, name='adamw', grid=(rows // tr,), out_shape=(jax.ShapeDtypeStruct(w.shape, F32),) * 3,
        in_specs=[spec] * 4, out_specs=(spec,) * 3, compiler_params=_params('parallel'),
    )(w, g, m, v)
    return tuple(o.reshape(shape) for o in out)


def _adamw_small(ws, gs, ms, vs):
    n = len(ws)
    shapes = [w.shape for w in ws]
    flat = [_as_lanes(a) for group in (ws, gs, ms, vs) for a in group]

    def body(*refs):
        ins, outs = refs[:4 * n], refs[4 * n:]
        for i in range(n):
            d, m, v = _adamw_math(ins[i][...], ins[n + i][...], ins[2 * n + i][...], ins[3 * n + i][...])
            outs[i][...], outs[n + i][...], outs[2 * n + i][...] = d, m, v

    out = pl.pallas_call(
        body, name='adamw_small', out_shape=tuple(jax.ShapeDtypeStruct(flat[i].shape, F32) for _ in range(3) for i in range(n)),
    )(*flat)
    return [tuple(out[j * n + i].reshape(shapes[i]) for j in range(3)) for i in range(n)]


def rms_norm(x, g):
    return x * lax.rsqrt(jnp.mean(jnp.square(x), axis=-1, keepdims=True) + EPS) * g


def modulate(x, g, shift, scale):
    return rms_norm(x, g) * (1 + scale) + shift


def rope_tables(n_tokens, rot_dim):
    t = jnp.arange(n_tokens)
    rows = (t // GRID_W).astype(F32)
    cols = (t % GRID_W).astype(F32)
    axis_dim = rot_dim // 2
    freqs = ROPE_BASE ** (-jnp.arange(0, axis_dim, 2, dtype=F32) / axis_dim)
    ang_r, ang_c = rows[:, None] * freqs, cols[:, None] * freqs
    ang = jnp.concatenate([ang_r, ang_r, ang_c, ang_c], axis=-1)
    return jnp.cos(ang), jnp.sin(ang)


def rope(x, cos, sin):
    x1, x2, x3, x4 = jnp.split(x, 4, axis=-1)
    rot = jnp.concatenate([-x2, x1, -x4, x3], axis=-1)
    return x * cos[:, None, :] + rot * sin[:, None, :]


def heads_first(t):
    return jnp.swapaxes(t, 1, 2)


def tokens_matmul(t, w):
    b, n, k = t.shape
    return linear(t.reshape(b * n, k), w).reshape(b, n, w.shape[1])


def s5_discretize(lam_re, lam_im, log_dt, b_re, b_im):
    dt = jnp.exp(log_dt)[:, None]
    mag = jnp.exp(lam_re * dt)
    a_re = mag * jnp.cos(lam_im * dt)
    a_im = mag * jnp.sin(lam_im * dt)
    den = jnp.square(lam_re) + jnp.square(lam_im)
    f_re = ((a_re - 1.0) * lam_re + a_im * lam_im) / den
    f_im = (a_im * lam_re - (a_re - 1.0) * lam_im) / den
    bb_re = f_re[..., None] * b_re - f_im[..., None] * b_im
    bb_im = f_re[..., None] * b_im + f_im[..., None] * b_re
    return a_re, a_im, bb_re, bb_im


def s5_mixer(u_lat, u_ctx, p, j, need_ctx):
    b, n, _ = u_lat.shape
    c = u_ctx.shape[1]
    half_groups = SSM_GROUPS // 2
    eye = jnp.eye(half_groups, dtype=F32)
    a_res, a_ims, b_res, b_ims, c_res, c_ims, seqs = [], [], [], [], [], [], []
    for d in range(2):
        a_re, a_im, bb_re, bb_im = s5_discretize(p['ssm_lam_re'][j, d], p['ssm_lam_im'][j, d], p['ssm_log_dt'][j, d],
                                                 p['ssm_b_re'][j, d], p['ssm_b_im'][j, d])
        for half in range(2):
            grp = slice(half * half_groups, (half + 1) * half_groups)
            a_res.append(a_re[grp].reshape(S5_LANES))
            a_ims.append(a_im[grp].reshape(S5_LANES))
            b_res.append(jnp.einsum('gsp,gh->gphs', bb_re[grp], eye).reshape(S5_HALF, S5_LANES))
            b_ims.append(jnp.einsum('gsp,gh->gphs', bb_im[grp], eye).reshape(S5_HALF, S5_LANES))
            c_res.append(jnp.einsum('gps,gh->gshp', p['ssm_c_re'][j, d][grp], eye).reshape(S5_LANES, S5_HALF))
            c_ims.append(jnp.einsum('gps,gh->gshp', p['ssm_c_im'][j, d][grp], eye).reshape(S5_LANES, S5_HALF))
        flip = (lambda t: t[:, ::-1]) if d == 1 else (lambda t: t)
        seq = jnp.concatenate([flip(u_ctx), flip(u_lat)], axis=1)
        seqs.append(jnp.transpose(seq.reshape(b, c + n, 2, S5_HALF), (2, 0, 1, 3)))
    u = jnp.stack(seqs).reshape(4 * b, c + n, S5_HALF)
    rep = lambda parts: jnp.repeat(jnp.stack(parts), b, axis=0)
    y = s5_core(u, rep(a_res), rep(a_ims), jnp.stack(b_res), jnp.stack(b_ims), jnp.stack(c_res), jnp.stack(c_ims))
    y = jnp.transpose(y.reshape(2, 2, b, c + n, S5_HALF), (0, 2, 3, 1, 4)).reshape(2, b, c + n, SSM_WIDTH)
    d_skip = p['ssm_d'][j]
    y_lat = d_skip * u_lat + y[0, :, c:] + y[1, :, c:][:, ::-1]
    wg, bg = p['ssm_w_glu'][j], p['ssm_b_glu'][j]

    def glu(t):
        t = jax.nn.gelu(t)
        return t * jax.nn.sigmoid(tokens_matmul(t, wg) + bg)

    if not need_ctx:
        return glu(y_lat), None
    y_ctx = d_skip * u_ctx + y[0, :, :c] + y[1, :, :c][:, ::-1]
    return glu(y_lat), glu(y_ctx)


def even_mixer(a_lat, a_ctx, p, j, need_ctx):
    b, n, _ = a_lat.shape
    c = a_ctx.shape[1]
    cos, sin = rope_tables(n, HEAD_DIM)
    proj = tokens_matmul(jnp.concatenate([a_ctx, a_lat], axis=1), p['e_w_in'][j])
    q, k, v, u = jnp.split(proj, [GQA_Q_W, GQA_Q_W + GQA_KV_W, GQA_Q_W + 2 * GQA_KV_W], axis=-1)
    q = rms_norm(q.reshape(b, c + n, GQA_Q_HEADS, HEAD_DIM), p['e_g_q'][j])
    k = rms_norm(k.reshape(b, c + n, GQA_KV_HEADS, HEAD_DIM), p['e_g_k'][j])
    v = v.reshape(b, c + n, GQA_KV_HEADS, HEAD_DIM)
    q_l = rope(q[:, c:], cos, sin)
    k = jnp.concatenate([k[:, :c], rope(k[:, c:], cos, sin)], axis=1)
    scale = HEAD_DIM ** -0.5
    kh, vh = heads_first(k), heads_first(v)
    att_l = heads_first(attention(heads_first(q_l), kh, vh, scale)).reshape(b, n, GQA_Q_W)
    ssm_l, ssm_c = s5_mixer(u[:, c:], u[:, :c], p, j, need_ctx)
    mix_l = jnp.concatenate([att_l, ssm_l], axis=-1)
    if not need_ctx:
        return tokens_matmul(mix_l, p['e_w_out'][j]), None
    att_c = heads_first(attention(heads_first(q[:, :c]), kh[:, :, :c], vh[:, :, :c], scale)).reshape(b, c, GQA_Q_W)
    mix = jnp.concatenate([jnp.concatenate([att_c, ssm_c], axis=-1), mix_l], axis=1)
    out = tokens_matmul(mix, p['e_w_out'][j])
    return out[:, c:], out[:, :c]


def odd_mixer(a_lat, a_ctx, p, j, need_ctx):
    b, n, _ = a_lat.shape
    c = a_ctx.shape[1]
    t = c + n
    cos, sin = rope_tables(n, MLA_ROPE)
    proj = tokens_matmul(jnp.concatenate([a_ctx, a_lat], axis=1), p['o_w_in'][j])
    c1, c2, c3 = MLA_Q_RANK, MLA_Q_RANK + MLA_KV_RANK, MLA_Q_RANK + MLA_KV_RANK + MLA_ROPE
    cq, ckv, kr = proj[..., :c1], proj[..., c1:c2], proj[..., c2:c3]
    nq, nk, nv = jnp.split(proj[..., ODD_NA_AT:], 3, axis=-1)
    q = tokens_matmul(rms_norm(cq, p['mla_g_cq'][j]), p['mla_w_uq'][j]).reshape(b, t, MLA_HEADS, MLA_QK)
    kv = tokens_matmul(rms_norm(ckv, p['mla_g_ckv'][j]), p['mla_w_ukv'][j]).reshape(b, t, MLA_HEADS, MLA_NOPE + MLA_V)
    k = jnp.concatenate([kv[..., :MLA_NOPE], jnp.broadcast_to(kr[:, :, None, :], (b, t, MLA_HEADS, MLA_ROPE))], axis=-1)
    q, k, mv = rms_norm(q, p['mla_g_q'][j]), rms_norm(k, p['mla_g_k'][j]), kv[..., MLA_NOPE:]

    def rope_tail(x):
        tail = jnp.concatenate([x[:, :c, :, MLA_NOPE:], rope(x[:, c:, :, MLA_NOPE:], cos, sin)], axis=1)
        return jnp.concatenate([x[..., :MLA_NOPE], tail], axis=-1)

    q, k = rope_tail(q), rope_tail(k)
    qh, kh, vh = heads_first(q), heads_first(k), heads_first(mv)
    mla_scale = MLA_QK ** -0.5
    mla_l = heads_first(attention(qh[:, :, c:], kh, vh, mla_scale)).reshape(b, n, MLA_HEADS * MLA_V)
    nq = heads_first(rms_norm(nq.reshape(b, t, NA_HEADS, HEAD_DIM), p['na_g_q'][j]))
    nk = heads_first(rms_norm(nk.reshape(b, t, NA_HEADS, HEAD_DIM), p['na_g_k'][j]))
    nv = heads_first(nv.reshape(b, t, NA_HEADS, HEAD_DIM))
    na_scale = HEAD_DIM ** -0.5
    na_l = na_attention(nq[:, :, c:], nk[:, :, c:], nv[:, :, c:], nk[:, :, :c], nv[:, :, :c], na_bias_table(p['na_rpb'][j]),
                        na_scale)
    na_l = heads_first(na_l).reshape(b, n, NA_W)
    mix_l = jnp.concatenate([mla_l, na_l], axis=-1)
    if not need_ctx:
        return tokens_matmul(mix_l, p['o_w_out'][j]), None
    mla_c = heads_first(attention(qh[:, :, :c], kh[:, :, :c], vh[:, :, :c], mla_scale)).reshape(b, c, MLA_HEADS * MLA_V)
    na_c = heads_first(attention(nq[:, :, :c], nk[:, :, :c], nv[:, :, :c], na_scale)).reshape(b, c, NA_W)
    mix = jnp.concatenate([jnp.concatenate([mla_c, na_c], axis=-1), mix_l], axis=1)
    out = tokens_matmul(mix, p['o_w_out'][j])
    return out[:, c:], out[:, :c]


def mlp(h, w1, w2):
    b, n, k = h.shape
    return mlp_rows(h.reshape(b * n, k), w1, w2).reshape(b, n, w2.shape[1])


def local_loss(x, p, m_lat, m_ctx, ctx, target):
    depth = m_lat.shape[0]
    c = ctx.shape[1]
    xc = ctx
    for i in range(depth):
        need_ctx = i < depth - 1
        j = i // 2
        ml = [m_lat[i, :, s][:, None, :] for s in range(N_MOD)]
        mc = [m_ctx[i, s][None, None, :] for s in range(N_MOD)]
        a_lat = modulate(x, p['g_norm1'][i], ml[0], ml[1])
        a_ctx = modulate(xc, p['g_norm1'][i], mc[0], mc[1])
        mixer = even_mixer if i % 2 == 0 else odd_mixer
        o_lat, o_ctx = mixer(a_lat, a_ctx, p, j, need_ctx)
        x = x + ml[2] * o_lat
        h_lat = modulate(x, p['g_norm2'][i], ml[3], ml[4])
        if need_ctx:
            xc = xc + mc[2] * o_ctx
            h_ctx = modulate(xc, p['g_norm2'][i], mc[3], mc[4])
            ff = mlp(jnp.concatenate([h_ctx, h_lat], axis=1), p['w_ff1'][i], p['w_ff2'][i])
            x = x + ml[5] * ff[:, c:]
            xc = xc + mc[5] * ff[:, :c]
        else:
            x = x + ml[5] * mlp(h_lat, p['w_ff1'][i], p['w_ff2'][i])
    return 0.5 * jnp.sum(jnp.mean(jnp.square(x - target), axis=-1))


def _packed_rows(size, layout):
    width, group = layout
    return -(-size // (width * group)) * group


def _pack_rows(flat, layout):
    width = layout[0]
    rows = _packed_rows(flat.shape[-1], layout)
    flat = jnp.pad(flat, [(0, 0)] * (flat.ndim - 1) + [(0, rows * width - flat.shape[-1])])
    return flat.reshape(flat.shape[:-1] + (rows, width))


def _unpack_rows(rows, shape):
    lead = rows.shape[:-2]
    return rows.reshape(lead + (-1,))[..., :math.prod(shape)].reshape(lead + tuple(shape))


def _unpack_all(packed, shapes, layout):
    out, at = [], 0
    for shape in shapes:
        rows = _packed_rows(math.prod(shape), layout)
        out.append(_unpack_rows(packed[..., at:at + rows, :], shape))
        at += rows
    return out


def _join_shards(g, axis):
    g = jnp.moveaxis(g, 0, axis)
    return g.reshape(g.shape[:axis] + (N_DEV * g.shape[axis + 1],) + g.shape[axis + 2:])


def _split_shards(full, axis):
    s = full.shape
    return jnp.moveaxis(full.reshape(s[:axis] + (N_DEV, s[axis] // N_DEV) + s[axis + 1:]), axis, 0)


def _gather_packed(parts, dtype, layout, hbm, name):
    packed = jnp.concatenate([_pack_rows(a.astype(dtype).reshape(-1), layout) for a in parts], axis=0)
    return _unpack_all(_exchange(packed, True, hbm, name), [a.shape for a in parts], layout)


def kernel(x, c, ctx, c_ctx, w_mod, b_mod, g_norm1, g_norm2, w_ff1, w_ff2, e_w_in, e_w_out, e_g_q, e_g_k, ssm_lam_re, ssm_lam_im, ssm_log_dt, ssm_b_re, ssm_b_im, ssm_c_re, ssm_c_im, ssm_d, ssm_w_glu, ssm_b_glu, o_w_in, o_w_out, mla_g_cq, mla_g_ckv, mla_w_uq, mla_w_ukv, mla_g_q, mla_g_k, na_g_q, na_g_k, na_rpb, loss_target, m_c_ctx, m_w_mod, m_b_mod, m_g_norm1, m_g_norm2, m_w_ff1, m_w_ff2, m_e_w_in, m_e_w_out, m_e_g_q, m_e_g_k, m_ssm_lam_re, m_ssm_lam_im, m_ssm_log_dt, m_ssm_b_re, m_ssm_b_im, m_ssm_c_re, m_ssm_c_im, m_ssm_d, m_ssm_w_glu, m_ssm_b_glu, m_o_w_in, m_o_w_out, m_mla_g_cq, m_mla_g_ckv, m_mla_w_uq, m_mla_w_ukv, m_mla_g_q, m_mla_g_k, m_na_g_q, m_na_g_k, m_na_rpb, v_c_ctx, v_w_mod, v_b_mod, v_g_norm1, v_g_norm2, v_w_ff1, v_w_ff2, v_e_w_in, v_e_w_out, v_e_g_q, v_e_g_k, v_ssm_lam_re, v_ssm_lam_im, v_ssm_log_dt, v_ssm_b_re, v_ssm_b_im, v_ssm_c_re, v_ssm_c_im, v_ssm_d, v_ssm_w_glu, v_ssm_b_glu, v_o_w_in, v_o_w_out, v_mla_g_cq, v_mla_g_ckv, v_mla_w_uq, v_mla_w_ukv, v_mla_g_q, v_mla_g_k, v_na_g_q, v_na_g_k, v_na_rpb):
    given = dict(locals())
    x, c, ctx, target = given['x'], given['c'], given['ctx'], given['loss_target']
    b_loc, _, d_model = x.shape
    depth = given['w_mod'].shape[0]
    ix, iy, ic = lax.axis_index('x'), lax.axis_index('y'), lax.axis_index('c')
    me = 4 * ix + 2 * iy + ic
    n_batch = N_DEV * b_loc
    mod_w = given['w_mod'].shape[2]

    c_rows = jnp.concatenate([c, jnp.zeros((8 - b_loc, d_model), F32)], axis=0)
    small = _gather_packed([c_rows] + [given[n] for n in SHARDED_SMALL], F32, PACK_SMALL, False, 'gather_small')
    c_all = small[0][:, :b_loc].reshape(n_batch, d_model)
    full = {n: _join_shards(g, SHARDED_SMALL[n]) for n, g in zip(SHARDED_SMALL, small[1:])}
    cuts = {n: (a if given[n].shape[a] % (16 if a == 1 else LANES) == 0 else None) for n, a in BIG.items()}
    big = _exchange_many([given[n].astype(BF16) for n in BIG], [cuts[n] for n in BIG], True, 'gather_weights')
    for n, g in zip(BIG, big):
        g = g if cuts[n] is not None else _join_shards(g, BIG[n])
        full[n] = [g[i] for i in range(g.shape[0])]
    c3 = MLA_Q_RANK + MLA_KV_RANK + MLA_ROPE
    full['o_w_in'] = [jnp.concatenate([w[:, :c3], jnp.zeros((w.shape[0], ODD_NA_AT - c3), BF16), w[:, c3:]], axis=-1)
                      for w in full['o_w_in']]
    for n in REPLICATED:
        full[n] = given[n]

    rows17 = 16 * (-(-(n_batch + 1) // 16))
    cond = jnp.concatenate([jax.nn.silu(c_all), jax.nn.silu(given['c_ctx'])[None],
                            jnp.zeros((rows17 - n_batch - 1, d_model), F32)], axis=0)
    mod_mine = jnp.stack([_matmul(cond, given['w_mod'][i], 'nn', F32) for i in range(depth)])
    b_mine = lax.dynamic_slice_in_dim(given['b_mod'], me * mod_w, mod_w, axis=1)
    mod_mine = mod_mine + b_mine[:, None, :]
    mod_all = _gather_packed([mod_mine], F32, PACK_SMALL, False, 'gather_mod')[0]
    mod_all = jnp.moveaxis(mod_all, 0, 2).reshape(depth, rows17, N_MOD, d_model)
    m_lat = lax.dynamic_slice_in_dim(mod_all, me * b_loc, b_loc, axis=1)
    m_ctx = mod_all[:, n_batch]

    diff = {n: full[n] for n in list(BIG) + list(SHARDED_SMALL) + REPLICATED}
    loss, (g_x, g_p, g_ml, g_mc) = jax.value_and_grad(local_loss, argnums=(0, 1, 2, 3))(x, diff, m_lat, m_ctx, ctx, target)
    loss = lax.psum(loss, ('x', 'y', 'c'))
    g_p['o_w_in'] = [jnp.concatenate([g[:, :c3], g[:, ODD_NA_AT:]], axis=-1) for g in g_p['o_w_in']]

    g_rows = jnp.concatenate([g_ml.reshape(depth, b_loc, N_MOD * d_model), g_mc.reshape(depth, 1, N_MOD * d_model),
                              jnp.zeros((depth, 8 - b_loc - 1, N_MOD * d_model), F32)], axis=1)
    g_mod_all = _gather_packed([g_rows], F32, PACK_SMALL, False, 'gather_mod_grads')[0]
    g_lat_all = jnp.moveaxis(g_mod_all[:, :, :b_loc], 0, 1).reshape(depth, n_batch, N_MOD * d_model)
    g_ctx_all = g_mod_all[0, :, b_loc]
    for dev in range(1, N_DEV):
        g_ctx_all = g_ctx_all + g_mod_all[dev, :, b_loc]
    g_mod17 = jnp.concatenate([g_lat_all, g_ctx_all[:, None], jnp.zeros((depth, rows17 - n_batch - 1, N_MOD * d_model), F32)],
                              axis=1)
    grad_b_mod = jnp.sum(g_mod17, axis=1)
    g_mod_mine = lax.dynamic_slice_in_dim(g_mod17, me * mod_w, mod_w, axis=2)
    grad_w_mod = jnp.stack([_matmul(cond, g_mod_mine[i], 'tn', F32) for i in range(depth)])
    d_cond = _matmul(g_mod_mine[0], given['w_mod'][0], 'nt', F32)
    for i in range(1, depth):
        d_cond = d_cond + _matmul(g_mod_mine[i], given['w_mod'][i], 'nt', F32)
    d_cond_ctx = d_cond[n_batch]

    small_names = REPLICATED + list(SHARDED_SMALL)
    parts = [d_cond_ctx] + [g_p[n] for n in small_names]
    packed = jnp.concatenate([_pack_rows(a.reshape(-1), PACK_SMALL) for a in parts], axis=0)
    summed = _unpack_all(_all_reduce_small(packed), [a.shape for a in parts], PACK_SMALL)
    grads = dict(zip(['c_ctx'] + small_names, summed))
    c_ctx = given['c_ctx']
    sig = jax.nn.sigmoid(c_ctx)
    grads['c_ctx'] = grads['c_ctx'] * (sig * (1 + c_ctx * (1 - sig)))
    for n, axis in SHARDED_SMALL.items():
        width = given[n].shape[axis]
        grads[n] = lax.dynamic_slice_in_dim(grads[n], me * width, width, axis=axis)
    grads['w_mod'], grads['b_mod'] = grad_w_mod, grad_b_mod

    stacked = [jnp.stack(g_p[n]) for n in BIG]
    stacked = [g if cuts[n] is not None else _split_shards(g, BIG[n]) for n, g in zip(BIG, stacked)]
    landed = _exchange_many(stacked, [cuts[n] for n in BIG], False, 'scatter_weight_grads')

    upd = {}
    for n, slots in zip(BIG, landed):
        grads[n], *upd[n] = _adamw_landed(slots, given[n], given['m_' + n], given['v_' + n])
    upd['w_mod'] = _adamw_big(given['w_mod'], grads['w_mod'], given['m_w_mod'], given['v_w_mod'])
    rest = [n for n in WEIGHTS if n not in upd]
    out = _adamw_small([given[n] for n in rest], [grads[n] for n in rest], [given['m_' + n] for n in rest],
                       [given['v_' + n] for n in rest])
    upd.update(dict(zip(rest, out)))
    return (loss, g_x, *[grads[n] for n in WEIGHTS], *[upd[n][0] for n in WEIGHTS], *[upd[n][1] for n in WEIGHTS],
            *[upd[n][2] for n in WEIGHTS])
```

```python
import functools
import math

import jax
import jax.numpy as jnp
from jax import lax
from jax.experimental import pallas as pl
from jax.experimental.pallas import tpu as pltpu

F32, BF16 = jnp.float32, jnp.bfloat16
MESH = pl.DeviceIdType.MESH
N_DEV = 8
VMEM_LIMIT_BYTES = 56 * 1024 * 1024
MM_TILE_BYTES = 6 * 1024 * 1024
LANES = 128
PACK_SMALL = (128, 8)

GRID_W = 64
HEAD_DIM = 64
ROPE_BASE = 10000.0
EPS = 1e-6
N_MOD = 6
GQA_Q_HEADS, GQA_KV_HEADS = 12, 4
GQA_Q_W, GQA_KV_W = GQA_Q_HEADS * HEAD_DIM, GQA_KV_HEADS * HEAD_DIM
SSM_WIDTH, SSM_GROUP, SSM_GROUPS, SSM_STATE = 256, 16, 16, 64
MLA_HEADS, MLA_Q_RANK, MLA_KV_RANK, MLA_NOPE, MLA_ROPE, MLA_V = 8, 512, 256, 64, 32, 64
MLA_QK = MLA_NOPE + MLA_ROPE
NA_HEADS, NA_WIN_R, NA_WIN_C = 8, 8, 16
NA_W = NA_HEADS * HEAD_DIM
ODD_IN_W = MLA_Q_RANK + MLA_KV_RANK + MLA_ROPE + 3 * NA_W
ODD_NA_AT = 1024
ODD_IN_PAD = ODD_NA_AT + 3 * NA_W
NEG = -1e30

ADAM_LR, ADAM_B1, ADAM_B2, ADAM_EPS, ADAM_WD, ADAM_STEP = 0.001, 0.9, 0.999, 1e-08, 0.01, 10

FWD_PARAMS = ['x', 'c', 'ctx', 'c_ctx', 'w_mod', 'b_mod', 'g_norm1', 'g_norm2', 'w_ff1', 'w_ff2', 'e_w_in', 'e_w_out',
              'e_g_q', 'e_g_k', 'ssm_lam_re', 'ssm_lam_im', 'ssm_log_dt', 'ssm_b_re', 'ssm_b_im', 'ssm_c_re', 'ssm_c_im',
              'ssm_d', 'ssm_w_glu', 'ssm_b_glu', 'o_w_in', 'o_w_out', 'mla_g_cq', 'mla_g_ckv', 'mla_w_uq', 'mla_w_ukv',
              'mla_g_q', 'mla_g_k', 'na_g_q', 'na_g_k', 'na_rpb']
WEIGHTS = FWD_PARAMS[3:]
BIG = {'w_ff1': 2, 'w_ff2': 1, 'e_w_in': 2, 'e_w_out': 1, 'o_w_in': 2, 'o_w_out': 1, 'mla_w_uq': 2, 'mla_w_ukv': 2,
       'ssm_w_glu': 1}
SHARDED_SMALL = {'mla_g_cq': 1, 'mla_g_ckv': 1}
REPLICATED = [n for n in WEIGHTS if n not in BIG and n not in SHARDED_SMALL and n not in ('w_mod', 'c_ctx', 'b_mod')]


def _tile(dim, prefs):
    for p in prefs:
        if dim >= p and dim % p == 0:
            return p
    return dim


def _params(*sem):
    return pltpu.CompilerParams(dimension_semantics=sem, vmem_limit_bytes=VMEM_LIMIT_BYTES)


def _dot_nt(a, b):
    return lax.dot_general(a, b, (((1,), (1,)), ((), ())), preferred_element_type=F32)


def _dot_tn(a, b):
    return lax.dot_general(a, b, (((0,), (0,)), ((), ())), preferred_element_type=F32)


def _dot(a, b):
    return jnp.dot(a, b, preferred_element_type=F32)


def _matmul(a, b, kind, out_dtype, finish=None, extra=None, n_out=1):
    a, b = a.astype(BF16), b.astype(BF16)
    if kind == 'nn':
        (m, kd), n = a.shape, b.shape[1]
    elif kind == 'nt':
        (m, kd), n = a.shape, b.shape[0]
    else:
        (kd, m), n = a.shape, b.shape[1]
    if kind == 'tn':
        tm = m if m <= 1024 else _tile(m, (1024, 768, 512, 256, 128))
        tn = _tile(n, (1024, 768, 512, 256, 128))
    else:
        tn = n if kd * n * 2 <= MM_TILE_BYTES else _tile(n, (1024, 768, 512, 256, 128))
        tm = _tile(m, [t for t in (1536, 1024, 768, 512, 256, 128) if t * tn * 4 <= MM_TILE_BYTES])
    whole = kind != 'tn' and tn == n and kd * n * 2 <= MM_TILE_BYTES
    tk = kd if whole else _tile(kd, [t for t in (2048, 1536, 1024, 512, 256, 128) if t * max(tm, tn) * 4 <= MM_TILE_BYTES])
    nk = kd // tk
    dn = {'nn': (((1,), (0,)), ((), ())), 'nt': (((1,), (1,)), ((), ())), 'tn': (((0,), (0,)), ((), ()))}[kind]

    n_in = 2 if extra is None else 3

    def body(*refs):
        a_ref, b_ref = refs[:2]
        o_refs = refs[n_in:n_in + n_out]

        def store(total):
            outs = (total,) if finish is None else finish(total, refs[2][...] if extra is not None else None)
            for o_ref, val in zip(o_refs, outs):
                o_ref[...] = val.astype(o_ref.dtype)

        part = lax.dot_general(a_ref[...], b_ref[...], dn, preferred_element_type=F32)
        if nk == 1:
            store(part)
            return
        acc_ref, k = refs[n_in + n_out], pl.program_id(2)

        @pl.when(k == 0)
        def _():
            acc_ref[...] = part

        @pl.when((k > 0) & (k < nk - 1))
        def _():
            acc_ref[...] += part

        @pl.when(k == nk - 1)
        def _():
            store(acc_ref[...] + part)

    a_spec = pl.BlockSpec((tk, tm), lambda i, j, k: (k, i)) if kind == 'tn' else pl.BlockSpec((tm, tk), lambda i, j, k: (i, k))
    b_spec = pl.BlockSpec((tn, tk), lambda i, j, k: (j, k)) if kind == 'nt' else pl.BlockSpec((tk, tn), lambda i, j, k: (k, j))
    o_spec = pl.BlockSpec((tm, tn), lambda i, j, k: (i, j))
    out = pl.pallas_call(
        body, name='mm_' + kind, grid=(m // tm, n // tn, nk),
        out_shape=[jax.ShapeDtypeStruct((m, n), out_dtype)] * n_out,
        in_specs=[a_spec, b_spec] + ([o_spec] if extra is not None else []), out_specs=[o_spec] * n_out,
        scratch_shapes=[pltpu.VMEM((tm, tn), F32)] if nk > 1 else [],
        compiler_params=_params('parallel', 'parallel', 'arbitrary'),
    )(*((a, b) if extra is None else (a, b, extra)))
    return out[0] if n_out == 1 else out


@jax.custom_vjp
def linear(a, w):
    return _matmul(a, w, 'nn', F32)


def _linear_fwd(a, w):
    ab = a.astype(BF16)
    return _matmul(ab, w, 'nn', F32), (ab, w)


def _linear_bwd(res, g):
    ab, w = res
    gb = g.astype(BF16)
    return _matmul(gb, w, 'nt', F32), _matmul(ab, gb, 'tn', w.dtype)


linear.defvjp(_linear_fwd, _linear_bwd)


def _relu2(z, _):
    r = jnp.maximum(z, 0.0)
    return r, r * r


def _relu2_grad(d_act, r):
    return (d_act * (2.0 * r.astype(F32)),)


@jax.custom_vjp
def mlp_rows(h, w1, w2):
    return _mlp_rows_fwd(h, w1, w2)[0]


def _mlp_rows_fwd(h, w1, w2):
    hb = h.astype(BF16)
    r, act = _matmul(hb, w1, 'nn', BF16, finish=_relu2, n_out=2)
    return _matmul(act, w2, 'nn', F32), (hb, w1, w2, r, act)


def _mlp_rows_bwd(res, g):
    hb, w1, w2, r, act = res
    gb = g.astype(BF16)
    dz = _matmul(gb, w2, 'nt', BF16, finish=_relu2_grad, extra=r)
    return _matmul(dz, w1, 'nt', F32), _matmul(hb, dz, 'tn', w1.dtype), _matmul(act, gb, 'tn', w2.dtype)


mlp_rows.defvjp(_mlp_rows_fwd, _mlp_rows_bwd)


LOG2E = math.log2(math.e)


def _softmax_rows(t):
    m = jnp.max(t, axis=-1, keepdims=True)
    e = jnp.exp2(t - m)
    return e * (1.0 / jnp.sum(e, axis=-1, keepdims=True))


ATTN_SPLIT = 2


def _attn_specs(q, k, v, bq):
    _, h, nq, dq = q.shape
    _, hk, nk, dv = v.shape
    g = h // hk
    q_spec = pl.BlockSpec((None, None, bq, dq), lambda b, j, gi, i: (b, j * g + gi, i, 0))
    k_spec = pl.BlockSpec((None, None, nk, dq), lambda b, j, gi, i: (b, j, 0, 0))
    v_spec = pl.BlockSpec((None, None, nk, dv), lambda b, j, gi, i: (b, j, 0, 0))
    o_spec = pl.BlockSpec((None, None, bq, dv), lambda b, j, gi, i: (b, j * g + gi, i, 0))
    return (q.shape[0], hk, g, nq // bq), q_spec, k_spec, v_spec, o_spec


def _attn_blocks(nq):
    bq = _tile(nq, (512, 256, 128))
    return bq, [pl.ds(s * (bq // ATTN_SPLIT), bq // ATTN_SPLIT) for s in range(ATTN_SPLIT)]


def _attn_fwd_call(q, k, v, scale):
    bq, subs = _attn_blocks(q.shape[2])
    grid, q_spec, k_spec, v_spec, o_spec = _attn_specs(q, k, v, bq)

    def body(q_ref, k_ref, v_ref, o_ref):
        kb, vb = k_ref[...], v_ref[...]
        for rows in subs:
            p = _softmax_rows(_dot_nt(q_ref[rows, :], kb) * (scale * LOG2E))
            o_ref[rows, :] = _dot(p.astype(BF16), vb)

    return pl.pallas_call(
        body, name='attn_fwd', grid=grid, out_shape=jax.ShapeDtypeStruct(q.shape[:3] + (v.shape[3],), F32),
        in_specs=[q_spec, k_spec, v_spec], out_specs=o_spec,
        compiler_params=_params('parallel', 'parallel', 'arbitrary', 'arbitrary'),
    )(q, k, v)


def _attn_bwd_call(q, k, v, do, scale):
    bq, subs = _attn_blocks(q.shape[2])
    grid, q_spec, k_spec, v_spec, o_spec = _attn_specs(q, k, v, bq)

    def body(q_ref, k_ref, v_ref, do_ref, dq_ref, dk_ref, dv_ref):
        @pl.when((pl.program_id(2) == 0) & (pl.program_id(3) == 0))
        def _():
            dk_ref[...] = jnp.zeros_like(dk_ref)
            dv_ref[...] = jnp.zeros_like(dv_ref)

        kb, vb = k_ref[...], v_ref[...]
        dk, dv = [], []
        for rows in subs:
            qb, dob = q_ref[rows, :], do_ref[rows, :]
            p = _softmax_rows(_dot_nt(qb, kb) * (scale * LOG2E))
            dp = _dot_nt(dob, vb)
            ds = p * (dp - jnp.sum(p * dp, axis=-1, keepdims=True))
            dsb = (ds * scale).astype(BF16)
            dq_ref[rows, :] = _dot(dsb, kb)
            dk.append(_dot_tn(dsb, qb))
            dv.append(_dot_tn(p.astype(BF16), dob))
        dk_ref[...] += sum(dk[1:], dk[0])
        dv_ref[...] += sum(dv[1:], dv[0])

    return pl.pallas_call(
        body, name='attn_bwd', grid=grid,
        out_shape=(jax.ShapeDtypeStruct(q.shape, F32), jax.ShapeDtypeStruct(k.shape, F32), jax.ShapeDtypeStruct(v.shape, F32)),
        in_specs=[q_spec, k_spec, v_spec, o_spec], out_specs=(q_spec, k_spec, v_spec),
        compiler_params=_params('parallel', 'parallel', 'arbitrary', 'arbitrary'),
    )(q, k, v, do)


@functools.partial(jax.custom_vjp, nondiff_argnums=(3,))
def attention(q, k, v, scale):
    return _attn_fwd_call(q.astype(BF16), k.astype(BF16), v.astype(BF16), scale)


def _attention_fwd(q, k, v, scale):
    qb, kb, vb = q.astype(BF16), k.astype(BF16), v.astype(BF16)
    return _attn_fwd_call(qb, kb, vb, scale), (qb, kb, vb)


def _attention_bwd(scale, res, g):
    return _attn_bwd_call(*res, g.astype(BF16), scale)


attention.defvjp(_attention_fwd, _attention_bwd)


def _na_window(r, rows):
    start = jnp.clip(r - NA_WIN_R // 2, 0, rows - NA_WIN_R)
    return start, r - start


def _na_scores(q, kw, kc, bias, scale):
    s1 = _dot_nt(q, kw) * scale + bias
    s2 = _dot_nt(q, kc) * scale
    m = jnp.maximum(jnp.max(s1, axis=-1, keepdims=True), jnp.max(s2, axis=-1, keepdims=True))
    e1, e2 = jnp.exp(s1 - m), jnp.exp(s2 - m)
    inv = 1.0 / (jnp.sum(e1, axis=-1, keepdims=True) + jnp.sum(e2, axis=-1, keepdims=True))
    return e1 * inv, e2 * inv


def _na_specs(q, kc):
    _, _, n, d = q.shape
    c = kc.shape[2]
    win = NA_WIN_R * GRID_W
    tok = pl.BlockSpec((None, None, n, d), lambda b, h: (b, h, 0, 0))
    ctx = pl.BlockSpec((None, None, c, d), lambda b, h: (b, h, 0, 0))
    bias = pl.BlockSpec((None, NA_WIN_R, GRID_W, win), lambda b, h: (h, 0, 0, 0))
    dbias = pl.BlockSpec((None, None, NA_WIN_R, GRID_W, win), lambda b, h: (b, h, 0, 0, 0))
    return tok, ctx, bias, dbias


def _na_fwd_call(q, k, v, kc, vc, bias, scale):
    b, h, n, d = q.shape
    rows, win = n // GRID_W, NA_WIN_R * GRID_W
    tok, ctx, bias_spec, _ = _na_specs(q, kc)

    def body(q_ref, k_ref, v_ref, kc_ref, vc_ref, b_ref, o_ref):
        def row(r, carry):
            start, off = _na_window(r, rows)
            at = pl.ds(pl.multiple_of(r * GRID_W, GRID_W), GRID_W)
            wat = pl.ds(pl.multiple_of(start * GRID_W, GRID_W), win)
            p1, p2 = _na_scores(q_ref[at, :], k_ref[wat, :], kc_ref[...], b_ref[off], scale)
            o_ref[at, :] = _dot(p1.astype(BF16), v_ref[wat, :]) + _dot(p2.astype(BF16), vc_ref[...])
            return carry

        lax.fori_loop(0, rows, row, 0, unroll=2)

    return pl.pallas_call(
        body, name='na_fwd', grid=(b, h), out_shape=jax.ShapeDtypeStruct(q.shape, F32),
        in_specs=[tok, tok, tok, ctx, ctx, bias_spec], out_specs=tok,
        compiler_params=_params('parallel', 'parallel'),
    )(q, k, v, kc, vc, bias)


def _na_bwd_call(q, k, v, kc, vc, bias, do, scale):
    b, h, n, d = q.shape
    rows, win = n // GRID_W, NA_WIN_R * GRID_W
    tok, ctx, bias_spec, dbias_spec = _na_specs(q, kc)

    def body(q_ref, k_ref, v_ref, kc_ref, vc_ref, b_ref, do_ref, dq_ref, dk_ref, dv_ref, dkc_ref, dvc_ref, db_ref):
        for ref in (dk_ref, dv_ref, dkc_ref, dvc_ref, db_ref):
            ref[...] = jnp.zeros_like(ref)

        def row(r, carry):
            start, off = _na_window(r, rows)
            at = pl.ds(pl.multiple_of(r * GRID_W, GRID_W), GRID_W)
            wat = pl.ds(pl.multiple_of(start * GRID_W, GRID_W), win)
            qb, kw, vw, dob = q_ref[at, :], k_ref[wat, :], v_ref[wat, :], do_ref[at, :]
            kcb, vcb = kc_ref[...], vc_ref[...]
            p1, p2 = _na_scores(qb, kw, kcb, b_ref[off], scale)
            dp1, dp2 = _dot_nt(dob, vw), _dot_nt(dob, vcb)
            delta = jnp.sum(p1 * dp1, axis=-1, keepdims=True) + jnp.sum(p2 * dp2, axis=-1, keepdims=True)
            ds1, ds2 = p1 * (dp1 - delta), p2 * (dp2 - delta)
            db_ref[off] += ds1
            ds1b, ds2b = (ds1 * scale).astype(BF16), (ds2 * scale).astype(BF16)
            dq_ref[at, :] = _dot(ds1b, kw) + _dot(ds2b, kcb)
            dk_ref[wat, :] += _dot_tn(ds1b, qb)
            dv_ref[wat, :] += _dot_tn(p1.astype(BF16), dob)
            dkc_ref[...] += _dot_tn(ds2b, qb)
            dvc_ref[...] += _dot_tn(p2.astype(BF16), dob)
            return carry

        lax.fori_loop(0, rows, row, 0, unroll=2)

    f = lambda a: jax.ShapeDtypeStruct(a.shape, F32)
    return pl.pallas_call(
        body, name='na_bwd', grid=(b, h),
        out_shape=(f(q), f(k), f(v), f(kc), f(vc), jax.ShapeDtypeStruct((b,) + bias.shape, F32)),
        in_specs=[tok, tok, tok, ctx, ctx, bias_spec, tok], out_specs=(tok, tok, tok, ctx, ctx, dbias_spec),
        compiler_params=_params('parallel', 'parallel'),
    )(q, k, v, kc, vc, bias, do)


@functools.partial(jax.custom_vjp, nondiff_argnums=(6,))
def na_attention(q, k, v, kc, vc, bias, scale):
    return _na_fwd_call(q.astype(BF16), k.astype(BF16), v.astype(BF16), kc.astype(BF16), vc.astype(BF16), bias, scale)


def _na_attention_fwd(q, k, v, kc, vc, bias, scale):
    res = (q.astype(BF16), k.astype(BF16), v.astype(BF16), kc.astype(BF16), vc.astype(BF16), bias)
    return _na_fwd_call(*res, scale), res


def _na_attention_bwd(scale, res, g):
    dq, dk, dv, dkc, dvc, db = _na_bwd_call(*res, g.astype(BF16), scale)
    return dq, dk, dv, dkc, dvc, jnp.sum(db, axis=0)


na_attention.defvjp(_na_attention_fwd, _na_attention_bwd)


def _na_table_index():
    qcol = jnp.arange(GRID_W)
    kcol = jnp.arange(GRID_W)
    cstart = jnp.clip(qcol - NA_WIN_C // 2, 0, GRID_W - NA_WIN_C)
    inside = (kcol[None, :] >= cstart[:, None]) & (kcol[None, :] < cstart[:, None] + NA_WIN_C)
    cidx = jnp.clip(kcol[None, :] - qcol[:, None] + (NA_WIN_C - 1), 0, 2 * NA_WIN_C - 2)
    ridx = jnp.arange(NA_WIN_R)[None, :] - jnp.arange(NA_WIN_R)[:, None] + (NA_WIN_R - 1)
    return inside, cidx, ridx


@jax.custom_vjp
def na_bias_table(rpb):
    inside, cidx, ridx = _na_table_index()
    t = rpb[:, ridx][:, :, :, cidx]
    t = jnp.where(inside[None, None, None], t, NEG)
    return jnp.transpose(t, (0, 1, 3, 2, 4)).reshape(rpb.shape[0], NA_WIN_R, GRID_W, NA_WIN_R * GRID_W)


def _na_bias_table_bwd(_, dt):
    inside, cidx, ridx = _na_table_index()
    pick_c = ((cidx[..., None] == jnp.arange(2 * NA_WIN_C - 1)) & inside[..., None]).astype(F32)
    pick_r = (ridx[..., None] == jnp.arange(2 * NA_WIN_R - 1)).astype(F32)
    d5 = dt.reshape(dt.shape[0], NA_WIN_R, GRID_W, NA_WIN_R, GRID_W)
    part = jnp.einsum('hoqjk,qkb->hojb', d5, pick_c, precision=lax.Precision.HIGHEST)
    return (jnp.einsum('hojb,oja->hab', part, pick_r, precision=lax.Precision.HIGHEST),)


na_bias_table.defvjp(lambda rpb: (na_bias_table(rpb), None), _na_bias_table_bwd)


S5_HALF = SSM_WIDTH // 2
S5_LANES = (SSM_GROUPS // 2) * SSM_STATE
S5_Q = S5_LANES // LANES


def _s5_tiles(a):
    r = a.shape[0]
    return jnp.transpose(a.reshape(r, S5_Q, LANES), (1, 0, 2)).reshape(S5_Q * r, LANES)


def _s5_untiles(a):
    r = a.shape[0] // S5_Q
    return jnp.transpose(a.reshape(S5_Q, r, LANES), (1, 0, 2)).reshape(r, S5_LANES)


def _s5_put(ref, r, rr, tc, val):
    for q in range(S5_Q):
        ref[pl.ds((q * rr + r) * tc, tc), :] = val[:, q * LANES:(q + 1) * LANES]


def _s5_get(ref, r, rr, tc):
    return jnp.concatenate([ref[pl.ds((q * rr + r) * tc, tc), :] for q in range(S5_Q)], axis=1)


def _s5_fwd_call(u, a_re, a_im, b_re, b_im, c_re, c_im):
    rr, t_len, _ = u.shape
    sets = b_re.shape[0]
    per = rr // sets
    tc = _tile(t_len, (256, 128))
    nt = t_len // tc
    qr = S5_Q * rr

    def body(u_ref, ar_ref, ai_ref, br_ref, bi_ref, cr_ref, ci_ref, y_ref, hr_ref, hi_ref, sr_ref, si_ref):
        @pl.when(pl.program_id(0) == 0)
        def _():
            sr_ref[...] = jnp.zeros_like(sr_ref)
            si_ref[...] = jnp.zeros_like(si_ref)

        for r in range(rr):
            ub = u_ref[r]
            _s5_put(hr_ref, r, rr, tc, _dot(ub, br_ref[r // per]))
            _s5_put(hi_ref, r, rr, tc, _dot(ub, bi_ref[r // per]))
        ar, ai = ar_ref[...], ai_ref[...]

        def step(t, carry):
            hr, hi = carry
            at = pl.ds(t, qr, stride=tc)
            nr = ar * hr - ai * hi + hr_ref[at, :]
            ni = ar * hi + ai * hr + hi_ref[at, :]
            hr_ref[at, :] = nr
            hi_ref[at, :] = ni
            return nr, ni

        hr, hi = lax.fori_loop(0, tc, step, (sr_ref[...], si_ref[...]), unroll=8)
        sr_ref[...] = hr
        si_ref[...] = hi
        for r in range(rr):
            y_ref[r] = (_dot(_s5_get(hr_ref, r, rr, tc).astype(BF16), cr_ref[r // per])
                        - _dot(_s5_get(hi_ref, r, rr, tc).astype(BF16), ci_ref[r // per]))

    full = lambda a: pl.BlockSpec(a.shape, lambda i: (0,) * a.ndim)
    h_spec = pl.BlockSpec((None, qr * tc, LANES), lambda i: (i, 0, 0))
    h_shape = jax.ShapeDtypeStruct((nt, qr * tc, LANES), F32)
    a_re, a_im = _s5_tiles(a_re), _s5_tiles(a_im)
    return pl.pallas_call(
        body, name='s5_fwd', grid=(nt,),
        out_shape=(jax.ShapeDtypeStruct((rr, t_len, S5_HALF), F32), h_shape, h_shape),
        in_specs=[pl.BlockSpec((rr, tc, S5_HALF), lambda i: (0, i, 0)), full(a_re), full(a_im), full(b_re), full(b_im),
                  full(c_re), full(c_im)],
        out_specs=(pl.BlockSpec((rr, tc, S5_HALF), lambda i: (0, i, 0)), h_spec, h_spec),
        scratch_shapes=[pltpu.VMEM((qr, LANES), F32), pltpu.VMEM((qr, LANES), F32)],
        compiler_params=_params('arbitrary'),
    )(u, a_re, a_im, b_re, b_im, c_re, c_im)


def _s5_bwd_call(u, a_re, a_im, b_re, b_im, c_re, c_im, h_re, h_im, dy):
    rr, t_len, _ = u.shape
    sets = b_re.shape[0]
    per = rr // sets
    nt, rows, _ = h_re.shape
    qr = S5_Q * rr
    tc = rows // qr

    def body(u_ref, dy_ref, ar_ref, ai_ref, br_ref, bi_ref, cr_ref, ci_ref, hr_ref, hi_ref,
             du_ref, dar_ref, dai_ref, dbr_ref, dbi_ref, dcr_ref, dci_ref, gr_ref, gi_ref, sr_ref, si_ref):
        i = pl.program_id(0)

        @pl.when(i == 0)
        def _():
            for ref in (dar_ref, dai_ref, dbr_ref, dbi_ref, dcr_ref, dci_ref, sr_ref, si_ref):
                ref[...] = jnp.zeros_like(ref)

        for r in range(rr):
            dyb = dy_ref[r]
            _s5_put(gr_ref, r, rr, tc, _dot_nt(dyb, cr_ref[r // per]))
            _s5_put(gi_ref, r, rr, tc, -_dot_nt(dyb, ci_ref[r // per]))
        ar, ai = ar_ref[...], ai_ref[...]
        g_r, g_i = sr_ref[...], si_ref[...]
        last = pl.ds(tc - 1, qr, stride=tc)
        d_r = g_r * hr_ref[last, :] + g_i * hi_ref[last, :]
        d_i = g_i * hr_ref[last, :] - g_r * hi_ref[last, :]

        def advance(t, g_r, g_i):
            at = pl.ds(t, qr, stride=tc)
            n_r = ar * g_r + ai * g_i + gr_ref[at, :]
            n_i = ar * g_i - ai * g_r + gi_ref[at, :]
            gr_ref[at, :] = n_r
            gi_ref[at, :] = n_i
            return n_r, n_i

        def step(k, carry):
            g_r, g_i, d_r, d_i = carry
            t = tc - 1 - k
            g_r, g_i = advance(t, g_r, g_i)
            before = pl.ds(t - 1, qr, stride=tc)
            p_r, p_i = hr_ref[before, :], hi_ref[before, :]
            return g_r, g_i, d_r + g_r * p_r + g_i * p_i, d_i + g_i * p_r - g_r * p_i

        g_r, g_i, d_r, d_i = lax.fori_loop(0, tc - 1, step, (g_r, g_i, d_r, d_i), unroll=8)
        g_r, g_i = advance(0, g_r, g_i)
        sr_ref[...] = g_r
        si_ref[...] = g_i
        dar_ref[...] += d_r
        dai_ref[...] += d_i
        for r in range(rr):
            s = r // per
            ub, dyb = u_ref[r], dy_ref[r]
            grb, gib = _s5_get(gr_ref, r, rr, tc).astype(BF16), _s5_get(gi_ref, r, rr, tc).astype(BF16)
            du_ref[r] = _dot_nt(grb, br_ref[s]) + _dot_nt(gib, bi_ref[s])
            dbr_ref[s] += _dot_tn(ub, grb)
            dbi_ref[s] += _dot_tn(ub, gib)
            dcr_ref[s] += _dot_tn(_s5_get(hr_ref, r, rr, tc).astype(BF16), dyb)
            dci_ref[s] -= _dot_tn(_s5_get(hi_ref, r, rr, tc).astype(BF16), dyb)

    full = lambda a: pl.BlockSpec(a.shape, lambda i: (0,) * a.ndim)
    back = lambda i: nt - 1 - i
    tok = pl.BlockSpec((rr, tc, S5_HALF), lambda i: (0, back(i), 0))
    h_spec = pl.BlockSpec((None, qr * tc, LANES), lambda i: (back(i), 0, 0))
    f = lambda a: jax.ShapeDtypeStruct(a.shape, F32)
    a_re, a_im = _s5_tiles(a_re), _s5_tiles(a_im)
    du, da_re, da_im, db_re, db_im, dc_re, dc_im = pl.pallas_call(
        body, name='s5_bwd', grid=(nt,),
        out_shape=(jax.ShapeDtypeStruct(u.shape, F32), f(a_re), f(a_im), f(b_re), f(b_im), f(c_re), f(c_im)),
        in_specs=[tok, tok, full(a_re), full(a_im), full(b_re), full(b_im), full(c_re), full(c_im), h_spec, h_spec],
        out_specs=(tok, full(a_re), full(a_im), full(b_re), full(b_im), full(c_re), full(c_im)),
        scratch_shapes=[pltpu.VMEM((qr * tc, LANES), F32), pltpu.VMEM((qr * tc, LANES), F32),
                        pltpu.VMEM((qr, LANES), F32), pltpu.VMEM((qr, LANES), F32)],
        compiler_params=_params('arbitrary'),
    )(u, dy, a_re, a_im, b_re, b_im, c_re, c_im, h_re, h_im)
    return du, _s5_untiles(da_re), _s5_untiles(da_im), db_re, db_im, dc_re, dc_im


@jax.custom_vjp
def s5_core(u, a_re, a_im, b_re, b_im, c_re, c_im):
    return _s5_fwd_call(u.astype(BF16), a_re, a_im, b_re.astype(BF16), b_im.astype(BF16), c_re.astype(BF16),
                        c_im.astype(BF16))[0]


def _s5_core_fwd(u, a_re, a_im, b_re, b_im, c_re, c_im):
    args = (u.astype(BF16), a_re, a_im, b_re.astype(BF16), b_im.astype(BF16), c_re.astype(BF16), c_im.astype(BF16))
    y, h_re, h_im = _s5_fwd_call(*args)
    return y, args + (h_re, h_im)


def _s5_core_bwd(res, g):
    return _s5_bwd_call(*res, g.astype(BF16))


s5_core.defvjp(_s5_core_fwd, _s5_core_bwd)


def _exchange(x, gather, hbm, name):
    block = x.shape if gather else x.shape[1:]

    def body(x_ref, out_ref, send_sems, recv_sems, local_sem):
        ix, iy, ic = lax.axis_index('x'), lax.axis_index('y'), lax.axis_index('c')
        me = 4 * ix + 2 * iy + ic

        def flipped(k):
            px = 1 - ix if k & 4 else ix
            py = 1 - iy if k & 2 else iy
            pc = 1 - ic if k & 1 else ic
            return (px, py, pc), 4 * px + 2 * py + pc

        def copy(k, src, dst):
            return pltpu.make_async_remote_copy(src_ref=src, dst_ref=dst, send_sem=send_sems.at[k - 1],
                                                 recv_sem=recv_sems.at[k - 1], device_id=flipped(k)[0], device_id_type=MESH)

        own = pltpu.make_async_copy(x_ref if gather else x_ref.at[me], out_ref.at[me], local_sem)
        own.start()
        sent = []
        for k in range(1, N_DEV):
            src = x_ref if gather else x_ref.at[flipped(k)[1]]
            sent.append(copy(k, src, out_ref.at[me]))
            sent[-1].start()
        for k in range(1, N_DEV):
            src = x_ref if gather else x_ref.at[flipped(k)[1]]
            copy(k, src, out_ref.at[flipped(k)[1]]).wait_recv()
        for cp in sent:
            cp.wait_send()
        own.wait()

    space = pltpu.HBM if hbm else pltpu.VMEM
    return pl.pallas_call(
        body, name=name, out_shape=jax.ShapeDtypeStruct((N_DEV,) + tuple(block), x.dtype),
        in_specs=[pl.BlockSpec(memory_space=space)], out_specs=pl.BlockSpec(memory_space=space),
        scratch_shapes=[pltpu.SemaphoreType.DMA((N_DEV - 1,)), pltpu.SemaphoreType.DMA((N_DEV - 1,)), pltpu.SemaphoreType.DMA],
        compiler_params=pltpu.CompilerParams(vmem_limit_bytes=VMEM_LIMIT_BYTES),
    )(x)


def _shard_view(ref, axis, index, width):
    return ref.at[(slice(None),) * axis + (pl.ds(pl.multiple_of(index * width, width), width),)]


def _exchange_many(xs, cuts, gather, name):
    n = len(xs)
    if gather:
        shards = [x.shape for x in xs]
    else:
        shards = [x.shape[1:] if cut is None else x.shape[:cut] + (x.shape[cut] // N_DEV,) + x.shape[cut + 1:]
                  for x, cut in zip(xs, cuts)]

    def full_shape(shard, cut):
        return shard[:cut] + (N_DEV * shard[cut],) + shard[cut + 1:]

    out_shapes = [jax.ShapeDtypeStruct((N_DEV,) + tuple(s) if (cut is None or not gather) else full_shape(tuple(s), cut), x.dtype)
                  for x, s, cut in zip(xs, shards, cuts)]

    def body(*refs):
        x_refs, out_refs = refs[:n], refs[n:2 * n]
        send_sems, recv_sems, local_sems = refs[2 * n:]
        ix, iy, ic = lax.axis_index('x'), lax.axis_index('y'), lax.axis_index('c')
        me = 4 * ix + 2 * iy + ic

        def flipped(k):
            px = 1 - ix if k & 4 else ix
            py = 1 - iy if k & 2 else iy
            pc = 1 - ic if k & 1 else ic
            return (px, py, pc), 4 * px + 2 * py + pc

        def block(ref, cut, shard, who):
            return ref.at[who] if cut is None else _shard_view(ref, cut, who, shard[cut])

        def ends(i, sender, receiver):
            if gather:
                return x_refs[i], block(out_refs[i], cuts[i], shards[i], sender)
            return block(x_refs[i], cuts[i], shards[i], receiver), out_refs[i].at[sender]

        def copy(i, k, sender, receiver):
            src, dst = ends(i, sender, receiver)
            return pltpu.make_async_remote_copy(src_ref=src, dst_ref=dst, send_sem=send_sems.at[i * (N_DEV - 1) + k - 1],
                                                 recv_sem=recv_sems.at[i * (N_DEV - 1) + k - 1], device_id=flipped(k)[0],
                                                 device_id_type=MESH)

        own = [pltpu.make_async_copy(*ends(i, me, me), local_sems.at[i]) for i in range(n)]
        for cp in own:
            cp.start()
        if gather:
            chips = (2, 4, 6)
            sent = [copy(i, k, me, flipped(k)[1]) for k in (1,) + chips for i in range(n)]
            for cp in sent:
                cp.start()
            for k in chips:
                for i in range(n):
                    copy(i, k, flipped(k)[1], me).wait_recv()
                    src, dst = ends(i, flipped(k)[1], me)
                    sent.append(pltpu.make_async_remote_copy(
                        src_ref=dst, dst_ref=dst, send_sem=send_sems.at[i * (N_DEV - 1) + k], recv_sem=recv_sems.at[i * (N_DEV - 1) + k],
                        device_id=flipped(1)[0], device_id_type=MESH))
                    sent[-1].start()
            for k in (1, 3, 5, 7):
                for i in range(n):
                    src, dst = ends(i, flipped(k)[1], me)
                    pltpu.make_async_remote_copy(
                        src_ref=dst, dst_ref=dst, send_sem=send_sems.at[i * (N_DEV - 1) + k - 1], recv_sem=recv_sems.at[i * (N_DEV - 1) + k - 1],
                        device_id=flipped(1)[0], device_id_type=MESH).wait_recv()
            for cp in sent:
                cp.wait_send()
            for cp in own:
                cp.wait()
            return
        sent = [copy(i, k, me, flipped(k)[1]) for k in range(1, N_DEV) for i in range(n)]
        for cp in sent:
            cp.start()
        for k in range(1, N_DEV):
            for i in range(n):
                copy(i, k, flipped(k)[1], me).wait_recv()
        for cp in sent:
            cp.wait_send()
        for cp in own:
            cp.wait()

    hbm = pl.BlockSpec(memory_space=pltpu.HBM)
    pairs = n * (N_DEV - 1)
    return pl.pallas_call(
        body, name=name, out_shape=out_shapes, in_specs=[hbm] * n, out_specs=[hbm] * n,
        scratch_shapes=[pltpu.SemaphoreType.DMA((pairs,)), pltpu.SemaphoreType.DMA((pairs,)), pltpu.SemaphoreType.DMA((n,))],
    )(*xs)


N_CHIPS = N_DEV // 2


def _block_shapes(xs, cuts):
    return [x.shape[1:] if cut is None else x.shape[:cut] + (x.shape[cut] // N_DEV,) + x.shape[cut + 1:]
            for x, cut in zip(xs, cuts)]


def _scatter_pairs(xs, cuts):
    n = len(xs)
    shards = _block_shapes(xs, cuts)

    def body(*refs):
        x_refs, mine_refs, theirs_refs = refs[:n], refs[n:3 * n:2], refs[n + 1:3 * n:2]
        send_sems, recv_sems, local_sems = refs[3 * n:]
        ix, iy, ic = lax.axis_index('x'), lax.axis_index('y'), lax.axis_index('c')
        sibling = (ix, iy, 1 - ic)

        def block(i, device):
            return x_refs[i].at[device] if cuts[i] is None else _shard_view(x_refs[i], cuts[i], device, shards[i][cuts[i]])

        def to_sibling(i, j):
            return pltpu.make_async_remote_copy(src_ref=block(i, 2 * j + 1 - ic), dst_ref=theirs_refs[i].at[j],
                                                 send_sem=send_sems.at[i * N_CHIPS + j], recv_sem=recv_sems.at[i * N_CHIPS + j],
                                                 device_id=sibling, device_id_type=MESH)

        own = [pltpu.make_async_copy(block(i, 2 * j + ic), mine_refs[i].at[j], local_sems.at[i * N_CHIPS + j])
               for i in range(n) for j in range(N_CHIPS)]
        sent = [to_sibling(i, j) for i in range(n) for j in range(N_CHIPS)]
        for cp in own + sent:
            cp.start()
        for cp in sent:
            cp.wait_recv()
        for cp in sent:
            cp.wait_send()
        for cp in own:
            cp.wait()

    hbm = pl.BlockSpec(memory_space=pltpu.HBM)
    out_shapes = [jax.ShapeDtypeStruct((N_CHIPS,) + tuple(s), x.dtype) for x, s in zip(xs, shards) for _ in range(2)]
    out = pl.pallas_call(
        body, name='scatter_pairs', out_shape=out_shapes, in_specs=[hbm] * n, out_specs=[hbm] * (2 * n),
        scratch_shapes=[pltpu.SemaphoreType.DMA((n * N_CHIPS,))] * 3,
    )(*xs)
    return list(zip(out[::2], out[1::2]))


def _scatter_chips(xs):
    n = len(xs)

    def body(*refs):
        x_refs, out_refs = refs[:n], refs[n:2 * n]
        send_sems, recv_sems, local_sems = refs[2 * n:]
        ix, iy, ic = lax.axis_index('x'), lax.axis_index('y'), lax.axis_index('c')
        chip = 2 * ix + iy
        flips = ((1, 0), (0, 1), (1, 1))

        def other(f):
            px = 1 - ix if f[0] else ix
            py = 1 - iy if f[1] else iy
            return (px, py, ic), 2 * px + py

        def copy(i, e, sender, receiver):
            return pltpu.make_async_remote_copy(src_ref=x_refs[i].at[receiver], dst_ref=out_refs[i].at[sender],
                                                 send_sem=send_sems.at[i * 3 + e], recv_sem=recv_sems.at[i * 3 + e],
                                                 device_id=other(flips[e])[0], device_id_type=MESH)

        own = [pltpu.make_async_copy(x_refs[i].at[chip], out_refs[i].at[chip], local_sems.at[i]) for i in range(n)]
        sent = [copy(i, e, chip, other(f)[1]) for e, f in enumerate(flips) for i in range(n)]
        for cp in own + sent:
            cp.start()
        for e, f in enumerate(flips):
            for i in range(n):
                copy(i, e, other(f)[1], chip).wait_recv()
        for cp in sent:
            cp.wait_send()
        for cp in own:
            cp.wait()

    hbm = pl.BlockSpec(memory_space=pltpu.HBM)
    return pl.pallas_call(
        body, name='scatter_chips', out_shape=[jax.ShapeDtypeStruct(x.shape, x.dtype) for x in xs],
        in_specs=[hbm] * n, out_specs=[hbm] * n,
        scratch_shapes=[pltpu.SemaphoreType.DMA((n * 3,)), pltpu.SemaphoreType.DMA((n * 3,)), pltpu.SemaphoreType.DMA((n,))],
    )(*xs)


def _pair_sum(a, b):
    shape = a.shape
    a, b = _as_rows(a), _as_rows(b)
    rows, cols = a.shape
    tr = _tile(rows, (1024, 512, 256, 128, 64, 32, 16))

    def body(a_ref, b_ref, o_ref):
        o_ref[...] = (a_ref[...].astype(F32) + b_ref[...].astype(F32)).astype(o_ref.dtype)

    spec = pl.BlockSpec((tr, cols), lambda i: (i, 0))
    return pl.pallas_call(
        body, name='pair_sum', grid=(rows // tr,), out_shape=jax.ShapeDtypeStruct(a.shape, a.dtype),
        in_specs=[spec, spec], out_specs=spec, compiler_params=_params('parallel'),
    )(a, b).reshape(shape)


def _adamw_landed(landed, w, m, v):
    shape = w.shape
    slots = landed.shape[0]
    w, m, v = (_as_rows(a) for a in (w, m, v))
    rows, cols = w.shape
    landed = landed.reshape(slots, rows, cols)
    tr = _tile(rows, (256, 128, 64, 32, 16))

    def body(l_ref, w_ref, m_ref, v_ref, g_ref, d_ref, nm_ref, nv_ref):
        g = l_ref[0].astype(F32)
        for d in range(1, slots):
            g = g + l_ref[d].astype(F32)
        g_ref[...] = g
        d_ref[...], nm_ref[...], nv_ref[...] = _adamw_math(w_ref[...], g, m_ref[...], v_ref[...])

    spec = pl.BlockSpec((tr, cols), lambda i: (i, 0))
    out = pl.pallas_call(
        body, name='adamw_landed', grid=(rows // tr,), out_shape=(jax.ShapeDtypeStruct(w.shape, F32),) * 4,
        in_specs=[pl.BlockSpec((slots, tr, cols), lambda i: (0, i, 0))] + [spec] * 3, out_specs=(spec,) * 4,
        compiler_params=_params('parallel'),
    )(landed, w, m, v)
    return tuple(o.reshape(shape) for o in out)


def _all_reduce_small(x):
    g = _exchange(x, True, False, 'gather_small_grads')

    def body(g_ref, o_ref):
        acc = g_ref[0]
        for d in range(1, N_DEV):
            acc = acc + g_ref[d]
        o_ref[...] = acc

    return pl.pallas_call(body, name='sum_small', out_shape=jax.ShapeDtypeStruct(x.shape, F32))(g)


def _adamw_math(w, g, m, v):
    m = ADAM_B1 * m + (1.0 - ADAM_B1) * g
    v = ADAM_B2 * v + (1.0 - ADAM_B2) * (g * g)
    m_hat = m / (1.0 - ADAM_B1 ** ADAM_STEP)
    v_hat = v / (1.0 - ADAM_B2 ** ADAM_STEP)
    return -ADAM_LR * (m_hat / (jnp.sqrt(v_hat) + ADAM_EPS) + ADAM_WD * w), m, v


def _as_rows(a):
    return a.reshape(1, -1) if a.ndim < 2 else a.reshape(-1, a.shape[-1])


def _as_lanes(a):
    return a.reshape(-1, LANES) if a.size % LANES == 0 else a.reshape(1, -1)


def _adamw_big(w, g, m, v):
    shape = w.shape
    w, g, m, v = (_as_rows(a) for a in (w, g, m, v))
    rows, cols = w.shape
    tr = _tile(rows, (512, 256, 128, 64, 32, 16, 8))

    def body(w_ref, g_ref, m_ref, v_ref, d_ref, nm_ref, nv_ref):
        d_ref[...], nm_ref[...], nv_ref[...] = _adamw_math(w_ref[...], g_ref[...], m_ref[...], v_ref[...])

    spec = pl.BlockSpec((tr, cols), lambda i: (i, 0))
    out = pl.pallas_call(
        body, name='adamw', grid=(rows // tr,), out_shape=(jax.ShapeDtypeStruct(w.shape, F32),) * 3,
        in_specs=[spec] * 4, out_specs=(spec,) * 3, compiler_params=_params('parallel'),
    )(w, g, m, v)
    return tuple(o.reshape(shape) for o in out)


def _adamw_small(ws, gs, ms, vs):
    n = len(ws)
    shapes = [w.shape for w in ws]
    flat = [_as_lanes(a) for group in (ws, gs, ms, vs) for a in group]

    def body(*refs):
        ins, outs = refs[:4 * n], refs[4 * n:]
        for i in range(n):
            d, m, v = _adamw_math(ins[i][...], ins[n + i][...], ins[2 * n + i][...], ins[3 * n + i][...])
            outs[i][...], outs[n + i][...], outs[2 * n + i][...] = d, m, v

    out = pl.pallas_call(
        body, name='adamw_small', out_shape=tuple(jax.ShapeDtypeStruct(flat[i].shape, F32) for _ in range(3) for i in range(n)),
    )(*flat)
    return [tuple(out[j * n + i].reshape(shapes[i]) for j in range(3)) for i in range(n)]


def rms_norm(x, g):
    return x * lax.rsqrt(jnp.mean(jnp.square(x), axis=-1, keepdims=True) + EPS) * g


def modulate(x, g, shift, scale):
    return rms_norm(x, g) * (1 + scale) + shift


def rope_tables(n_tokens, rot_dim):
    t = jnp.arange(n_tokens)
    rows = (t // GRID_W).astype(F32)
    cols = (t % GRID_W).astype(F32)
    axis_dim = rot_dim // 2
    freqs = ROPE_BASE ** (-jnp.arange(0, axis_dim, 2, dtype=F32) / axis_dim)
    ang_r, ang_c = rows[:, None] * freqs, cols[:, None] * freqs
    ang = jnp.concatenate([ang_r, ang_r, ang_c, ang_c], axis=-1)
    return jnp.cos(ang), jnp.sin(ang)


def rope(x, cos, sin):
    x1, x2, x3, x4 = jnp.split(x, 4, axis=-1)
    rot = jnp.concatenate([-x2, x1, -x4, x3], axis=-1)
    return x * cos[:, None, :] + rot * sin[:, None, :]


def heads_first(t):
    return jnp.swapaxes(t, 1, 2)


def tokens_matmul(t, w):
    b, n, k = t.shape
    return linear(t.reshape(b * n, k), w).reshape(b, n, w.shape[1])


def s5_discretize(lam_re, lam_im, log_dt, b_re, b_im):
    dt = jnp.exp(log_dt)[:, None]
    mag = jnp.exp(lam_re * dt)
    a_re = mag * jnp.cos(lam_im * dt)
    a_im = mag * jnp.sin(lam_im * dt)
    den = jnp.square(lam_re) + jnp.square(lam_im)
    f_re = ((a_re - 1.0) * lam_re + a_im * lam_im) / den
    f_im = (a_im * lam_re - (a_re - 1.0) * lam_im) / den
    bb_re = f_re[..., None] * b_re - f_im[..., None] * b_im
    bb_im = f_re[..., None] * b_im + f_im[..., None] * b_re
    return a_re, a_im, bb_re, bb_im


def s5_mixer(u_lat, u_ctx, p, j, need_ctx):
    b, n, _ = u_lat.shape
    c = u_ctx.shape[1]
    half_groups = SSM_GROUPS // 2
    eye = jnp.eye(half_groups, dtype=F32)
    a_res, a_ims, b_res, b_ims, c_res, c_ims, seqs = [], [], [], [], [], [], []
    for d in range(2):
        a_re, a_im, bb_re, bb_im = s5_discretize(p['ssm_lam_re'][j, d], p['ssm_lam_im'][j, d], p['ssm_log_dt'][j, d],
                                                 p['ssm_b_re'][j, d], p['ssm_b_im'][j, d])
        for half in range(2):
            grp = slice(half * half_groups, (half + 1) * half_groups)
            a_res.append(a_re[grp].reshape(S5_LANES))
            a_ims.append(a_im[grp].reshape(S5_LANES))
            b_res.append(jnp.einsum('gsp,gh->gphs', bb_re[grp], eye).reshape(S5_HALF, S5_LANES))
            b_ims.append(jnp.einsum('gsp,gh->gphs', bb_im[grp], eye).reshape(S5_HALF, S5_LANES))
            c_res.append(jnp.einsum('gps,gh->gshp', p['ssm_c_re'][j, d][grp], eye).reshape(S5_LANES, S5_HALF))
            c_ims.append(jnp.einsum('gps,gh->gshp', p['ssm_c_im'][j, d][grp], eye).reshape(S5_LANES, S5_HALF))
        flip = (lambda t: t[:, ::-1]) if d == 1 else (lambda t: t)
        seq = jnp.concatenate([flip(u_ctx), flip(u_lat)], axis=1)
        seqs.append(jnp.transpose(seq.reshape(b, c + n, 2, S5_HALF), (2, 0, 1, 3)))
    u = jnp.stack(seqs).reshape(4 * b, c + n, S5_HALF)
    rep = lambda parts: jnp.repeat(jnp.stack(parts), b, axis=0)
    y = s5_core(u, rep(a_res), rep(a_ims), jnp.stack(b_res), jnp.stack(b_ims), jnp.stack(c_res), jnp.stack(c_ims))
    y = jnp.transpose(y.reshape(2, 2, b, c + n, S5_HALF), (0, 2, 3, 1, 4)).reshape(2, b, c + n, SSM_WIDTH)
    d_skip = p['ssm_d'][j]
    y_lat = d_skip * u_lat + y[0, :, c:] + y[1, :, c:][:, ::-1]
    wg, bg = p['ssm_w_glu'][j], p['ssm_b_glu'][j]

    def glu(t):
        t = jax.nn.gelu(t)
        return t * jax.nn.sigmoid(tokens_matmul(t, wg) + bg)

    if not need_ctx:
        return glu(y_lat), None
    y_ctx = d_skip * u_ctx + y[0, :, :c] + y[1, :, :c][:, ::-1]
    return glu(y_lat), glu(y_ctx)


def even_mixer(a_lat, a_ctx, p, j, need_ctx):
    b, n, _ = a_lat.shape
    c = a_ctx.shape[1]
    cos, sin = rope_tables(n, HEAD_DIM)
    proj = tokens_matmul(jnp.concatenate([a_ctx, a_lat], axis=1), p['e_w_in'][j])
    q, k, v, u = jnp.split(proj, [GQA_Q_W, GQA_Q_W + GQA_KV_W, GQA_Q_W + 2 * GQA_KV_W], axis=-1)
    q = rms_norm(q.reshape(b, c + n, GQA_Q_HEADS, HEAD_DIM), p['e_g_q'][j])
    k = rms_norm(k.reshape(b, c + n, GQA_KV_HEADS, HEAD_DIM), p['e_g_k'][j])
    v = v.reshape(b, c + n, GQA_KV_HEADS, HEAD_DIM)
    q_l = rope(q[:, c:], cos, sin)
    k = jnp.concatenate([k[:, :c], rope(k[:, c:], cos, sin)], axis=1)
    scale = HEAD_DIM ** -0.5
    kh, vh = heads_first(k), heads_first(v)
    att_l = heads_first(attention(heads_first(q_l), kh, vh, scale)).reshape(b, n, GQA_Q_W)
    ssm_l, ssm_c = s5_mixer(u[:, c:], u[:, :c], p, j, need_ctx)
    mix_l = jnp.concatenate([att_l, ssm_l], axis=-1)
    if not need_ctx:
        return tokens_matmul(mix_l, p['e_w_out'][j]), None
    att_c = heads_first(attention(heads_first(q[:, :c]), kh[:, :, :c], vh[:, :, :c], scale)).reshape(b, c, GQA_Q_W)
    mix = jnp.concatenate([jnp.concatenate([att_c, ssm_c], axis=-1), mix_l], axis=1)
    out = tokens_matmul(mix, p['e_w_out'][j])
    return out[:, c:], out[:, :c]


def odd_mixer(a_lat, a_ctx, p, j, need_ctx):
    b, n, _ = a_lat.shape
    c = a_ctx.shape[1]
    t = c + n
    cos, sin = rope_tables(n, MLA_ROPE)
    proj = tokens_matmul(jnp.concatenate([a_ctx, a_lat], axis=1), p['o_w_in'][j])
    c1, c2, c3 = MLA_Q_RANK, MLA_Q_RANK + MLA_KV_RANK, MLA_Q_RANK + MLA_KV_RANK + MLA_ROPE
    cq, ckv, kr = proj[..., :c1], proj[..., c1:c2], proj[..., c2:c3]
    nq, nk, nv = jnp.split(proj[..., ODD_NA_AT:], 3, axis=-1)
    q = tokens_matmul(rms_norm(cq, p['mla_g_cq'][j]), p['mla_w_uq'][j]).reshape(b, t, MLA_HEADS, MLA_QK)
    kv = tokens_matmul(rms_norm(ckv, p['mla_g_ckv'][j]), p['mla_w_ukv'][j]).reshape(b, t, MLA_HEADS, MLA_NOPE + MLA_V)
    k = jnp.concatenate([kv[..., :MLA_NOPE], jnp.broadcast_to(kr[:, :, None, :], (b, t, MLA_HEADS, MLA_ROPE))], axis=-1)
    q, k, mv = rms_norm(q, p['mla_g_q'][j]), rms_norm(k, p['mla_g_k'][j]), kv[..., MLA_NOPE:]

    def rope_tail(x):
        tail = jnp.concatenate([x[:, :c, :, MLA_NOPE:], rope(x[:, c:, :, MLA_NOPE:], cos, sin)], axis=1)
        return jnp.concatenate([x[..., :MLA_NOPE], tail], axis=-1)

    q, k = rope_tail(q), rope_tail(k)
    qh, kh, vh = heads_first(q), heads_first(k), heads_first(mv)
    mla_scale = MLA_QK ** -0.5
    mla_l = heads_first(attention(qh[:, :, c:], kh, vh, mla_scale)).reshape(b, n, MLA_HEADS * MLA_V)
    nq = heads_first(rms_norm(nq.reshape(b, t, NA_HEADS, HEAD_DIM), p['na_g_q'][j]))
    nk = heads_first(rms_norm(nk.reshape(b, t, NA_HEADS, HEAD_DIM), p['na_g_k'][j]))
    nv = heads_first(nv.reshape(b, t, NA_HEADS, HEAD_DIM))
    na_scale = HEAD_DIM ** -0.5
    na_l = na_attention(nq[:, :, c:], nk[:, :, c:], nv[:, :, c:], nk[:, :, :c], nv[:, :, :c], na_bias_table(p['na_rpb'][j]),
                        na_scale)
    na_l = heads_first(na_l).reshape(b, n, NA_W)
    mix_l = jnp.concatenate([mla_l, na_l], axis=-1)
    if not need_ctx:
        return tokens_matmul(mix_l, p['o_w_out'][j]), None
    mla_c = heads_first(attention(qh[:, :, :c], kh[:, :, :c], vh[:, :, :c], mla_scale)).reshape(b, c, MLA_HEADS * MLA_V)
    na_c = heads_first(attention(nq[:, :, :c], nk[:, :, :c], nv[:, :, :c], na_scale)).reshape(b, c, NA_W)
    mix = jnp.concatenate([jnp.concatenate([mla_c, na_c], axis=-1), mix_l], axis=1)
    out = tokens_matmul(mix, p['o_w_out'][j])
    return out[:, c:], out[:, :c]


def mlp(h, w1, w2):
    b, n, k = h.shape
    return mlp_rows(h.reshape(b * n, k), w1, w2).reshape(b, n, w2.shape[1])


def local_loss(x, p, m_lat, m_ctx, ctx, target):
    depth = m_lat.shape[0]
    c = ctx.shape[1]
    xc = ctx
    for i in range(depth):
        need_ctx = i < depth - 1
        j = i // 2
        ml = [m_lat[i, :, s][:, None, :] for s in range(N_MOD)]
        mc = [m_ctx[i, s][None, None, :] for s in range(N_MOD)]
        a_lat = modulate(x, p['g_norm1'][i], ml[0], ml[1])
        a_ctx = modulate(xc, p['g_norm1'][i], mc[0], mc[1])
        mixer = even_mixer if i % 2 == 0 else odd_mixer
        o_lat, o_ctx = mixer(a_lat, a_ctx, p, j, need_ctx)
        x = x + ml[2] * o_lat
        h_lat = modulate(x, p['g_norm2'][i], ml[3], ml[4])
        if need_ctx:
            xc = xc + mc[2] * o_ctx
            h_ctx = modulate(xc, p['g_norm2'][i], mc[3], mc[4])
            ff = mlp(jnp.concatenate([h_ctx, h_lat], axis=1), p['w_ff1'][i], p['w_ff2'][i])
            x = x + ml[5] * ff[:, c:]
            xc = xc + mc[5] * ff[:, :c]
        else:
            x = x + ml[5] * mlp(h_lat, p['w_ff1'][i], p['w_ff2'][i])
    return 0.5 * jnp.sum(jnp.mean(jnp.square(x - target), axis=-1))


def _packed_rows(size, layout):
    width, group = layout
    return -(-size // (width * group)) * group


def _pack_rows(flat, layout):
    width = layout[0]
    rows = _packed_rows(flat.shape[-1], layout)
    flat = jnp.pad(flat, [(0, 0)] * (flat.ndim - 1) + [(0, rows * width - flat.shape[-1])])
    return flat.reshape(flat.shape[:-1] + (rows, width))


def _unpack_rows(rows, shape):
    lead = rows.shape[:-2]
    return rows.reshape(lead + (-1,))[..., :math.prod(shape)].reshape(lead + tuple(shape))


def _unpack_all(packed, shapes, layout):
    out, at = [], 0
    for shape in shapes:
        rows = _packed_rows(math.prod(shape), layout)
        out.append(_unpack_rows(packed[..., at:at + rows, :], shape))
        at += rows
    return out


def _join_shards(g, axis):
    g = jnp.moveaxis(g, 0, axis)
    return g.reshape(g.shape[:axis] + (N_DEV * g.shape[axis + 1],) + g.shape[axis + 2:])


def _split_shards(full, axis):
    s = full.shape
    return jnp.moveaxis(full.reshape(s[:axis] + (N_DEV, s[axis] // N_DEV) + s[axis + 1:]), axis, 0)


def _gather_packed(parts, dtype, layout, hbm, name):
    packed = jnp.concatenate([_pack_rows(a.astype(dtype).reshape(-1), layout) for a in parts], axis=0)
    return _unpack_all(_exchange(packed, True, hbm, name), [a.shape for a in parts], layout)


def kernel(x, c, ctx, c_ctx, w_mod, b_mod, g_norm1, g_norm2, w_ff1, w_ff2, e_w_in, e_w_out, e_g_q, e_g_k, ssm_lam_re, ssm_lam_im, ssm_log_dt, ssm_b_re, ssm_b_im, ssm_c_re, ssm_c_im, ssm_d, ssm_w_glu, ssm_b_glu, o_w_in, o_w_out, mla_g_cq, mla_g_ckv, mla_w_uq, mla_w_ukv, mla_g_q, mla_g_k, na_g_q, na_g_k, na_rpb, loss_target, m_c_ctx, m_w_mod, m_b_mod, m_g_norm1, m_g_norm2, m_w_ff1, m_w_ff2, m_e_w_in, m_e_w_out, m_e_g_q, m_e_g_k, m_ssm_lam_re, m_ssm_lam_im, m_ssm_log_dt, m_ssm_b_re, m_ssm_b_im, m_ssm_c_re, m_ssm_c_im, m_ssm_d, m_ssm_w_glu, m_ssm_b_glu, m_o_w_in, m_o_w_out, m_mla_g_cq, m_mla_g_ckv, m_mla_w_uq, m_mla_w_ukv, m_mla_g_q, m_mla_g_k, m_na_g_q, m_na_g_k, m_na_rpb, v_c_ctx, v_w_mod, v_b_mod, v_g_norm1, v_g_norm2, v_w_ff1, v_w_ff2, v_e_w_in, v_e_w_out, v_e_g_q, v_e_g_k, v_ssm_lam_re, v_ssm_lam_im, v_ssm_log_dt, v_ssm_b_re, v_ssm_b_im, v_ssm_c_re, v_ssm_c_im, v_ssm_d, v_ssm_w_glu, v_ssm_b_glu, v_o_w_in, v_o_w_out, v_mla_g_cq, v_mla_g_ckv, v_mla_w_uq, v_mla_w_ukv, v_mla_g_q, v_mla_g_k, v_na_g_q, v_na_g_k, v_na_rpb):
    given = dict(locals())
    x, c, ctx, target = given['x'], given['c'], given['ctx'], given['loss_target']
    b_loc, _, d_model = x.shape
    depth = given['w_mod'].shape[0]
    ix, iy, ic = lax.axis_index('x'), lax.axis_index('y'), lax.axis_index('c')
    me = 4 * ix + 2 * iy + ic
    n_batch = N_DEV * b_loc
    mod_w = given['w_mod'].shape[2]

    c_rows = jnp.concatenate([c, jnp.zeros((8 - b_loc, d_model), F32)], axis=0)
    small = _gather_packed([c_rows] + [given[n] for n in SHARDED_SMALL], F32, PACK_SMALL, False, 'gather_small')
    c_all = small[0][:, :b_loc].reshape(n_batch, d_model)
    full = {n: _join_shards(g, SHARDED_SMALL[n]) for n, g in zip(SHARDED_SMALL, small[1:])}
    cuts = {n: (a if given[n].shape[a] % (16 if a == 1 else LANES) == 0 else None) for n, a in BIG.items()}
    big = _exchange_many([given[n].astype(BF16) for n in BIG], [cuts[n] for n in BIG], True, 'gather_weights')
    for n, g in zip(BIG, big):
        g = g if cuts[n] is not None else _join_shards(g, BIG[n])
        full[n] = [g[i] for i in range(g.shape[0])]
    c3 = MLA_Q_RANK + MLA_KV_RANK + MLA_ROPE
    full['o_w_in'] = [jnp.concatenate([w[:, :c3], jnp.zeros((w.shape[0], ODD_NA_AT - c3), BF16), w[:, c3:]], axis=-1)
                      for w in full['o_w_in']]
    for n in REPLICATED:
        full[n] = given[n]

    rows17 = 16 * (-(-(n_batch + 1) // 16))
    cond = jnp.concatenate([jax.nn.silu(c_all), jax.nn.silu(given['c_ctx'])[None],
                            jnp.zeros((rows17 - n_batch - 1, d_model), F32)], axis=0)
    mod_mine = jnp.stack([_matmul(cond, given['w_mod'][i], 'nn', F32) for i in range(depth)])
    b_mine = lax.dynamic_slice_in_dim(given['b_mod'], me * mod_w, mod_w, axis=1)
    mod_mine = mod_mine + b_mine[:, None, :]
    mod_all = _gather_packed([mod_mine], F32, PACK_SMALL, False, 'gather_mod')[0]
    mod_all = jnp.moveaxis(mod_all, 0, 2).reshape(depth, rows17, N_MOD, d_model)
    m_lat = lax.dynamic_slice_in_dim(mod_all, me * b_loc, b_loc, axis=1)
    m_ctx = mod_all[:, n_batch]

    diff = {n: full[n] for n in list(BIG) + list(SHARDED_SMALL) + REPLICATED}
    loss, (g_x, g_p, g_ml, g_mc) = jax.value_and_grad(local_loss, argnums=(0, 1, 2, 3))(x, diff, m_lat, m_ctx, ctx, target)
    loss = lax.psum(loss, ('x', 'y', 'c'))
    g_p['o_w_in'] = [jnp.concatenate([g[:, :c3], g[:, ODD_NA_AT:]], axis=-1) for g in g_p['o_w_in']]

    g_rows = jnp.concatenate([g_ml.reshape(depth, b_loc, N_MOD * d_model), g_mc.reshape(depth, 1, N_MOD * d_model),
                              jnp.zeros((depth, 8 - b_loc - 1, N_MOD * d_model), F32)], axis=1)
    g_mod_all = _gather_packed([g_rows], F32, PACK_SMALL, False, 'gather_mod_grads')[0]
    g_lat_all = jnp.moveaxis(g_mod_all[:, :, :b_loc], 0, 1).reshape(depth, n_batch, N_MOD * d_model)
    g_ctx_all = g_mod_all[0, :, b_loc]
    for dev in range(1, N_DEV):
        g_ctx_all = g_ctx_all + g_mod_all[dev, :, b_loc]
    g_mod17 = jnp.concatenate([g_lat_all, g_ctx_all[:, None], jnp.zeros((depth, rows17 - n_batch - 1, N_MOD * d_model), F32)],
                              axis=1)
    grad_b_mod = jnp.sum(g_mod17, axis=1)
    g_mod_mine = lax.dynamic_slice_in_dim(g_mod17, me * mod_w, mod_w, axis=2)
    grad_w_mod = jnp.stack([_matmul(cond, g_mod_mine[i], 'tn', F32) for i in range(depth)])
    d_cond = _matmul(g_mod_mine[0], given['w_mod'][0], 'nt', F32)
    for i in range(1, depth):
        d_cond = d_cond + _matmul(g_mod_mine[i], given['w_mod'][i], 'nt', F32)
    d_cond_ctx = d_cond[n_batch]

    small_names = REPLICATED + list(SHARDED_SMALL)
    parts = [d_cond_ctx] + [g_p[n] for n in small_names]
    packed = jnp.concatenate([_pack_rows(a.reshape(-1), PACK_SMALL) for a in parts], axis=0)
    summed = _unpack_all(_all_reduce_small(packed), [a.shape for a in parts], PACK_SMALL)
    grads = dict(zip(['c_ctx'] + small_names, summed))
    c_ctx = given['c_ctx']
    sig = jax.nn.sigmoid(c_ctx)
    grads['c_ctx'] = grads['c_ctx'] * (sig * (1 + c_ctx * (1 - sig)))
    for n, axis in SHARDED_SMALL.items():
        width = given[n].shape[axis]
        grads[n] = lax.dynamic_slice_in_dim(grads[n], me * width, width, axis=axis)
    grads['w_mod'], grads['b_mod'] = grad_w_mod, grad_b_mod

    stacked = [jnp.stack(g_p[n]) for n in BIG]
    stacked = [g if cuts[n] is not None else _split_shards(g, BIG[n]) for n, g in zip(BIG, stacked)]
    pairs = _scatter_pairs(stacked, [cuts[n] for n in BIG])
    landed = _scatter_chips([_pair_sum(mine, theirs) for mine, theirs in pairs])

    upd = {}
    for n, slots in zip(BIG, landed):
        grads[n], *upd[n] = _adamw_landed(slots, given[n], given['m_' + n], given['v_' + n])
    upd['w_mod'] = _adamw_big(given['w_mod'], grads['w_mod'], given['m_w_mod'], given['v_w_mod'])
    rest = [n for n in WEIGHTS if n not in upd]
    out = _adamw_small([given[n] for n in rest], [grads[n] for n in rest], [given['m_' + n] for n in rest],
                       [given['v_' + n] for n in rest])
    upd.update(dict(zip(rest, out)))
    return (loss, g_x, *[grads[n] for n in WEIGHTS], *[upd[n][0] for n in WEIGHTS], *[upd[n][1] for n in WEIGHTS],
            *[upd[n][2] for n in WEIGHTS])
```

```python
import functools
import math

import jax
import jax.numpy as jnp
from jax import lax
from jax.experimental import pallas as pl
from jax.experimental.pallas import tpu as pltpu

F32, BF16 = jnp.float32, jnp.bfloat16
MESH = pl.DeviceIdType.MESH
N_DEV = 8
VMEM_LIMIT_BYTES = 56 * 1024 * 1024
MM_TILE_BYTES = 6 * 1024 * 1024
LANES = 128
PACK_SMALL = (128, 8)

GRID_W = 64
HEAD_DIM = 64
ROPE_BASE = 10000.0
EPS = 1e-6
N_MOD = 6
GQA_Q_HEADS, GQA_KV_HEADS = 12, 4
GQA_Q_W, GQA_KV_W = GQA_Q_HEADS * HEAD_DIM, GQA_KV_HEADS * HEAD_DIM
SSM_WIDTH, SSM_GROUP, SSM_GROUPS, SSM_STATE = 256, 16, 16, 64
MLA_HEADS, MLA_Q_RANK, MLA_KV_RANK, MLA_NOPE, MLA_ROPE, MLA_V = 8, 512, 256, 64, 32, 64
MLA_QK = MLA_NOPE + MLA_ROPE
NA_HEADS, NA_WIN_R, NA_WIN_C = 8, 8, 16
NA_W = NA_HEADS * HEAD_DIM
ODD_IN_W = MLA_Q_RANK + MLA_KV_RANK + MLA_ROPE + 3 * NA_W
ODD_NA_AT = 1024
ODD_IN_PAD = ODD_NA_AT + 3 * NA_W
NEG = -1e30

ADAM_LR, ADAM_B1, ADAM_B2, ADAM_EPS, ADAM_WD, ADAM_STEP = 0.001, 0.9, 0.999, 1e-08, 0.01, 10

FWD_PARAMS = ['x', 'c', 'ctx', 'c_ctx', 'w_mod', 'b_mod', 'g_norm1', 'g_norm2', 'w_ff1', 'w_ff2', 'e_w_in', 'e_w_out',
              'e_g_q', 'e_g_k', 'ssm_lam_re', 'ssm_lam_im', 'ssm_log_dt', 'ssm_b_re', 'ssm_b_im', 'ssm_c_re', 'ssm_c_im',
              'ssm_d', 'ssm_w_glu', 'ssm_b_glu', 'o_w_in', 'o_w_out', 'mla_g_cq', 'mla_g_ckv', 'mla_w_uq', 'mla_w_ukv',
              'mla_g_q', 'mla_g_k', 'na_g_q', 'na_g_k', 'na_rpb']
WEIGHTS = FWD_PARAMS[3:]
BIG = {'w_ff1': 2, 'w_ff2': 1, 'e_w_in': 2, 'e_w_out': 1, 'o_w_in': 2, 'o_w_out': 1, 'mla_w_uq': 2, 'mla_w_ukv': 2,
       'ssm_w_glu': 1}
SHARDED_SMALL = {'mla_g_cq': 1, 'mla_g_ckv': 1}
REPLICATED = [n for n in WEIGHTS if n not in BIG and n not in SHARDED_SMALL and n not in ('w_mod', 'c_ctx', 'b_mod')]


def _tile(dim, prefs):
    for p in prefs:
        if dim >= p and dim % p == 0:
            return p
    return dim


def _params(*sem):
    return pltpu.CompilerParams(dimension_semantics=sem, vmem_limit_bytes=VMEM_LIMIT_BYTES)


def _dot_nt(a, b):
    return lax.dot_general(a, b, (((1,), (1,)), ((), ())), preferred_element_type=F32)


def _dot_tn(a, b):
    return lax.dot_general(a, b, (((0,), (0,)), ((), ())), preferred_element_type=F32)


def _dot(a, b):
    return jnp.dot(a, b, preferred_element_type=F32)


def _matmul(a, b, kind, out_dtype, finish=None, extra=None, n_out=1):
    a, b = a.astype(BF16), b.astype(BF16)
    if kind == 'nn':
        (m, kd), n = a.shape, b.shape[1]
    elif kind == 'nt':
        (m, kd), n = a.shape, b.shape[0]
    else:
        (kd, m), n = a.shape, b.shape[1]
    if kind == 'tn':
        tm = m if m <= 1024 else _tile(m, (1024, 768, 512, 256, 128))
        tn = _tile(n, (1024, 768, 512, 256, 128))
    else:
        tn = n if kd * n * 2 <= MM_TILE_BYTES else _tile(n, (1024, 768, 512, 256, 128))
        tm = _tile(m, [t for t in (1536, 1024, 768, 512, 256, 128) if t * tn * 4 <= MM_TILE_BYTES])
    whole = kind != 'tn' and tn == n and kd * n * 2 <= MM_TILE_BYTES
    tk = kd if whole else _tile(kd, [t for t in (2048, 1536, 1024, 512, 256, 128) if t * max(tm, tn) * 4 <= MM_TILE_BYTES])
    nk = kd // tk
    dn = {'nn': (((1,), (0,)), ((), ())), 'nt': (((1,), (1,)), ((), ())), 'tn': (((0,), (0,)), ((), ()))}[kind]

    n_in = 2 if extra is None else 3

    def body(*refs):
        a_ref, b_ref = refs[:2]
        o_refs = refs[n_in:n_in + n_out]

        def store(total):
            outs = (total,) if finish is None else finish(total, refs[2][...] if extra is not None else None)
            for o_ref, val in zip(o_refs, outs):
                o_ref[...] = val.astype(o_ref.dtype)

        part = lax.dot_general(a_ref[...], b_ref[...], dn, preferred_element_type=F32)
        if nk == 1:
            store(part)
            return
        acc_ref, k = refs[n_in + n_out], pl.program_id(2)

        @pl.when(k == 0)
        def _():
            acc_ref[...] = part

        @pl.when((k > 0) & (k < nk - 1))
        def _():
            acc_ref[...] += part

        @pl.when(k == nk - 1)
        def _():
            store(acc_ref[...] + part)

    a_spec = pl.BlockSpec((tk, tm), lambda i, j, k: (k, i)) if kind == 'tn' else pl.BlockSpec((tm, tk), lambda i, j, k: (i, k))
    b_spec = pl.BlockSpec((tn, tk), lambda i, j, k: (j, k)) if kind == 'nt' else pl.BlockSpec((tk, tn), lambda i, j, k: (k, j))
    o_spec = pl.BlockSpec((tm, tn), lambda i, j, k: (i, j))
    out = pl.pallas_call(
        body, name='mm_' + kind, grid=(m // tm, n // tn, nk),
        out_shape=[jax.ShapeDtypeStruct((m, n), out_dtype)] * n_out,
        in_specs=[a_spec, b_spec] + ([o_spec] if extra is not None else []), out_specs=[o_spec] * n_out,
        scratch_shapes=[pltpu.VMEM((tm, tn), F32)] if nk > 1 else [],
        compiler_params=_params('parallel', 'parallel', 'arbitrary'),
    )(*((a, b) if extra is None else (a, b, extra)))
    return out[0] if n_out == 1 else out


@jax.custom_vjp
def linear(a, w):
    return _matmul(a, w, 'nn', F32)


def _linear_fwd(a, w):
    ab = a.astype(BF16)
    return _matmul(ab, w, 'nn', F32), (ab, w)


def _linear_bwd(res, g):
    ab, w = res
    gb = g.astype(BF16)
    return _matmul(gb, w, 'nt', F32), _matmul(ab, gb, 'tn', w.dtype)


linear.defvjp(_linear_fwd, _linear_bwd)


def _relu2(z, _):
    r = jnp.maximum(z, 0.0)
    return r, r * r


def _relu2_grad(d_act, r):
    return (d_act * (2.0 * r.astype(F32)),)


@jax.custom_vjp
def mlp_rows(h, w1, w2):
    return _mlp_rows_fwd(h, w1, w2)[0]


def _mlp_rows_fwd(h, w1, w2):
    hb = h.astype(BF16)
    r, act = _matmul(hb, w1, 'nn', BF16, finish=_relu2, n_out=2)
    return _matmul(act, w2, 'nn', F32), (hb, w1, w2, r, act)


def _mlp_rows_bwd(res, g):
    hb, w1, w2, r, act = res
    gb = g.astype(BF16)
    dz = _matmul(gb, w2, 'nt', BF16, finish=_relu2_grad, extra=r)
    return _matmul(dz, w1, 'nt', F32), _matmul(hb, dz, 'tn', w1.dtype), _matmul(act, gb, 'tn', w2.dtype)


mlp_rows.defvjp(_mlp_rows_fwd, _mlp_rows_bwd)


LOG2E = math.log2(math.e)


def _softmax_rows(t):
    m = jnp.max(t, axis=-1, keepdims=True)
    e = jnp.exp2(t - m)
    return e * (1.0 / jnp.sum(e, axis=-1, keepdims=True))


ATTN_SPLIT = 2


def _attn_specs(q, k, v, bq):
    _, h, nq, dq = q.shape
    _, hk, nk, dv = v.shape
    g = h // hk
    q_spec = pl.BlockSpec((None, None, bq, dq), lambda b, j, gi, i: (b, j * g + gi, i, 0))
    k_spec = pl.BlockSpec((None, None, nk, dq), lambda b, j, gi, i: (b, j, 0, 0))
    v_spec = pl.BlockSpec((None, None, nk, dv), lambda b, j, gi, i: (b, j, 0, 0))
    o_spec = pl.BlockSpec((None, None, bq, dv), lambda b, j, gi, i: (b, j * g + gi, i, 0))
    t = dict(q=pl.BlockSpec((None, None, dq, bq), lambda b, j, gi, i: (b, j * g + gi, 0, i)),
             o=pl.BlockSpec((None, None, dv, bq), lambda b, j, gi, i: (b, j * g + gi, 0, i)),
             k=pl.BlockSpec((None, None, dq, nk), lambda b, j, gi, i: (b, j, 0, 0)),
             v=pl.BlockSpec((None, None, dv, nk), lambda b, j, gi, i: (b, j, 0, 0)))
    return (q.shape[0], hk, g, nq // bq), q_spec, k_spec, v_spec, o_spec, t


def _attn_blocks(nq):
    bq = _tile(nq, (512, 256, 128))
    return bq, [pl.ds(s * (bq // ATTN_SPLIT), bq // ATTN_SPLIT) for s in range(ATTN_SPLIT)]


def _attn_fwd_call(q, k, vt, scale):
    b, h, nq, _ = q.shape
    dv = vt.shape[2]
    bq, subs = _attn_blocks(nq)
    grid, q_spec, k_spec, _, _, t = _attn_specs(q, k, jnp.swapaxes(vt, 2, 3), bq)

    def body(q_ref, k_ref, vt_ref, ot_ref):
        kb, vtb = k_ref[...], vt_ref[...]
        for rows in subs:
            p = _softmax_rows(_dot_nt(q_ref[rows, :], kb) * (scale * LOG2E))
            ot_ref[:, rows] = _dot_nt(vtb, p.astype(BF16))

    return pl.pallas_call(
        body, name='attn_fwd', grid=grid, out_shape=jax.ShapeDtypeStruct((b, h, dv, nq), F32),
        in_specs=[q_spec, k_spec, t['v']], out_specs=t['o'],
        compiler_params=_params('parallel', 'parallel', 'arbitrary', 'arbitrary'),
    )(q, k, vt)


def _attn_bwd_call(q, k, kt, v, do, scale):
    b, h, nq, dq = q.shape
    bq, subs = _attn_blocks(nq)
    grid, q_spec, k_spec, v_spec, o_spec, t = _attn_specs(q, k, v, bq)

    def body(q_ref, k_ref, kt_ref, v_ref, do_ref, dqt_ref, dk_ref, dv_ref):
        @pl.when((pl.program_id(2) == 0) & (pl.program_id(3) == 0))
        def _():
            dk_ref[...] = jnp.zeros_like(dk_ref)
            dv_ref[...] = jnp.zeros_like(dv_ref)

        kb, ktb, vb = k_ref[...], kt_ref[...], v_ref[...]
        dk, dv = [], []
        for rows in subs:
            qb, dob = q_ref[rows, :], do_ref[rows, :]
            p = _softmax_rows(_dot_nt(qb, kb) * (scale * LOG2E))
            dp = _dot_nt(dob, vb)
            ds = p * (dp - jnp.sum(p * dp, axis=-1, keepdims=True))
            dsb = (ds * scale).astype(BF16)
            dqt_ref[:, rows] = _dot_nt(ktb, dsb)
            dk.append(_dot_tn(dsb, qb))
            dv.append(_dot_tn(p.astype(BF16), dob))
        dk_ref[...] += sum(dk[1:], dk[0])
        dv_ref[...] += sum(dv[1:], dv[0])

    return pl.pallas_call(
        body, name='attn_bwd', grid=grid,
        out_shape=(jax.ShapeDtypeStruct((b, h, dq, nq), F32), jax.ShapeDtypeStruct(k.shape, F32), jax.ShapeDtypeStruct(v.shape, F32)),
        in_specs=[q_spec, k_spec, t['k'], v_spec, o_spec], out_specs=(t['q'], k_spec, v_spec),
        compiler_params=_params('parallel', 'parallel', 'arbitrary', 'arbitrary'),
    )(q, k, kt, v, do)


@functools.partial(jax.custom_vjp, nondiff_argnums=(3,))
def attention(q, k, v, scale):
    return _attention_fwd(q, k, v, scale)[0]


def _attention_fwd(q, k, v, scale):
    qb, kb, vb = q.astype(BF16), k.astype(BF16), v.astype(BF16)
    return jnp.swapaxes(_attn_fwd_call(qb, kb, jnp.swapaxes(vb, 2, 3), scale), 2, 3), (qb, kb, vb)


def _attention_bwd(scale, res, g):
    qb, kb, vb = res
    dqt, dk, dv = _attn_bwd_call(qb, kb, jnp.swapaxes(kb, 2, 3), vb, g.astype(BF16), scale)
    return jnp.swapaxes(dqt, 2, 3), dk, dv


attention.defvjp(_attention_fwd, _attention_bwd)


def _na_window(r, rows):
    start = jnp.clip(r - NA_WIN_R // 2, 0, rows - NA_WIN_R)
    return start, r - start


def _na_scores(q, kw, kc, bias, scale):
    s1 = _dot_nt(q, kw) * scale + bias
    s2 = _dot_nt(q, kc) * scale
    m = jnp.maximum(jnp.max(s1, axis=-1, keepdims=True), jnp.max(s2, axis=-1, keepdims=True))
    e1, e2 = jnp.exp(s1 - m), jnp.exp(s2 - m)
    inv = 1.0 / (jnp.sum(e1, axis=-1, keepdims=True) + jnp.sum(e2, axis=-1, keepdims=True))
    return e1 * inv, e2 * inv


def _na_specs(q, kc):
    _, _, n, d = q.shape
    c = kc.shape[2]
    win = NA_WIN_R * GRID_W
    tok = pl.BlockSpec((None, None, n, d), lambda b, h: (b, h, 0, 0))
    ctx = pl.BlockSpec((None, None, c, d), lambda b, h: (b, h, 0, 0))
    bias = pl.BlockSpec((None, NA_WIN_R, GRID_W, win), lambda b, h: (h, 0, 0, 0))
    dbias = pl.BlockSpec((None, None, NA_WIN_R, GRID_W, win), lambda b, h: (b, h, 0, 0, 0))
    return tok, ctx, bias, dbias


def _na_fwd_call(q, k, v, kc, vc, bias, scale):
    b, h, n, d = q.shape
    rows, win = n // GRID_W, NA_WIN_R * GRID_W
    tok, ctx, bias_spec, _ = _na_specs(q, kc)

    def body(q_ref, k_ref, v_ref, kc_ref, vc_ref, b_ref, o_ref):
        def row(r, carry):
            start, off = _na_window(r, rows)
            at = pl.ds(pl.multiple_of(r * GRID_W, GRID_W), GRID_W)
            wat = pl.ds(pl.multiple_of(start * GRID_W, GRID_W), win)
            p1, p2 = _na_scores(q_ref[at, :], k_ref[wat, :], kc_ref[...], b_ref[off], scale)
            o_ref[at, :] = _dot(p1.astype(BF16), v_ref[wat, :]) + _dot(p2.astype(BF16), vc_ref[...])
            return carry

        lax.fori_loop(0, rows, row, 0, unroll=2)

    return pl.pallas_call(
        body, name='na_fwd', grid=(b, h), out_shape=jax.ShapeDtypeStruct(q.shape, F32),
        in_specs=[tok, tok, tok, ctx, ctx, bias_spec], out_specs=tok,
        compiler_params=_params('parallel', 'parallel'),
    )(q, k, v, kc, vc, bias)


def _na_bwd_call(q, k, v, kc, vc, bias, do, scale):
    b, h, n, d = q.shape
    rows, win = n // GRID_W, NA_WIN_R * GRID_W
    tok, ctx, bias_spec, dbias_spec = _na_specs(q, kc)

    def body(q_ref, k_ref, v_ref, kc_ref, vc_ref, b_ref, do_ref, dq_ref, dk_ref, dv_ref, dkc_ref, dvc_ref, db_ref):
        for ref in (dk_ref, dv_ref, dkc_ref, dvc_ref, db_ref):
            ref[...] = jnp.zeros_like(ref)

        def row(r, carry):
            start, off = _na_window(r, rows)
            at = pl.ds(pl.multiple_of(r * GRID_W, GRID_W), GRID_W)
            wat = pl.ds(pl.multiple_of(start * GRID_W, GRID_W), win)
            qb, kw, vw, dob = q_ref[at, :], k_ref[wat, :], v_ref[wat, :], do_ref[at, :]
            kcb, vcb = kc_ref[...], vc_ref[...]
            p1, p2 = _na_scores(qb, kw, kcb, b_ref[off], scale)
            dp1, dp2 = _dot_nt(dob, vw), _dot_nt(dob, vcb)
            delta = jnp.sum(p1 * dp1, axis=-1, keepdims=True) + jnp.sum(p2 * dp2, axis=-1, keepdims=True)
            ds1, ds2 = p1 * (dp1 - delta), p2 * (dp2 - delta)
            db_ref[off] += ds1
            ds1b, ds2b = (ds1 * scale).astype(BF16), (ds2 * scale).astype(BF16)
            dq_ref[at, :] = _dot(ds1b, kw) + _dot(ds2b, kcb)
            dk_ref[wat, :] += _dot_tn(ds1b, qb)
            dv_ref[wat, :] += _dot_tn(p1.astype(BF16), dob)
            dkc_ref[...] += _dot_tn(ds2b, qb)
            dvc_ref[...] += _dot_tn(p2.astype(BF16), dob)
            return carry

        lax.fori_loop(0, rows, row, 0, unroll=2)

    f = lambda a: jax.ShapeDtypeStruct(a.shape, F32)
    return pl.pallas_call(
        body, name='na_bwd', grid=(b, h),
        out_shape=(f(q), f(k), f(v), f(kc), f(vc), jax.ShapeDtypeStruct((b,) + bias.shape, F32)),
        in_specs=[tok, tok, tok, ctx, ctx, bias_spec, tok], out_specs=(tok, tok, tok, ctx, ctx, dbias_spec),
        compiler_params=_params('parallel', 'parallel'),
    )(q, k, v, kc, vc, bias, do)


@functools.partial(jax.custom_vjp, nondiff_argnums=(6,))
def na_attention(q, k, v, kc, vc, bias, scale):
    return _na_fwd_call(q.astype(BF16), k.astype(BF16), v.astype(BF16), kc.astype(BF16), vc.astype(BF16), bias, scale)


def _na_attention_fwd(q, k, v, kc, vc, bias, scale):
    res = (q.astype(BF16), k.astype(BF16), v.astype(BF16), kc.astype(BF16), vc.astype(BF16), bias)
    return _na_fwd_call(*res, scale), res


def _na_attention_bwd(scale, res, g):
    dq, dk, dv, dkc, dvc, db = _na_bwd_call(*res, g.astype(BF16), scale)
    return dq, dk, dv, dkc, dvc, jnp.sum(db, axis=0)


na_attention.defvjp(_na_attention_fwd, _na_attention_bwd)


def _na_table_index():
    qcol = jnp.arange(GRID_W)
    kcol = jnp.arange(GRID_W)
    cstart = jnp.clip(qcol - NA_WIN_C // 2, 0, GRID_W - NA_WIN_C)
    inside = (kcol[None, :] >= cstart[:, None]) & (kcol[None, :] < cstart[:, None] + NA_WIN_C)
    cidx = jnp.clip(kcol[None, :] - qcol[:, None] + (NA_WIN_C - 1), 0, 2 * NA_WIN_C - 2)
    ridx = jnp.arange(NA_WIN_R)[None, :] - jnp.arange(NA_WIN_R)[:, None] + (NA_WIN_R - 1)
    return inside, cidx, ridx


@jax.custom_vjp
def na_bias_table(rpb):
    inside, cidx, ridx = _na_table_index()
    t = rpb[:, ridx][:, :, :, cidx]
    t = jnp.where(inside[None, None, None], t, NEG)
    return jnp.transpose(t, (0, 1, 3, 2, 4)).reshape(rpb.shape[0], NA_WIN_R, GRID_W, NA_WIN_R * GRID_W)


def _na_bias_table_bwd(_, dt):
    inside, cidx, ridx = _na_table_index()
    pick_c = ((cidx[..., None] == jnp.arange(2 * NA_WIN_C - 1)) & inside[..., None]).astype(F32)
    pick_r = (ridx[..., None] == jnp.arange(2 * NA_WIN_R - 1)).astype(F32)
    d5 = dt.reshape(dt.shape[0], NA_WIN_R, GRID_W, NA_WIN_R, GRID_W)
    part = jnp.einsum('hoqjk,qkb->hojb', d5, pick_c, precision=lax.Precision.HIGHEST)
    return (jnp.einsum('hojb,oja->hab', part, pick_r, precision=lax.Precision.HIGHEST),)


na_bias_table.defvjp(lambda rpb: (na_bias_table(rpb), None), _na_bias_table_bwd)


S5_HALF = SSM_WIDTH // 2
S5_LANES = (SSM_GROUPS // 2) * SSM_STATE
S5_Q = S5_LANES // LANES


def _s5_tiles(a):
    r = a.shape[0]
    return jnp.transpose(a.reshape(r, S5_Q, LANES), (1, 0, 2)).reshape(S5_Q * r, LANES)


def _s5_untiles(a):
    r = a.shape[0] // S5_Q
    return jnp.transpose(a.reshape(S5_Q, r, LANES), (1, 0, 2)).reshape(r, S5_LANES)


def _s5_put(ref, r, rr, tc, val):
    for q in range(S5_Q):
        ref[pl.ds((q * rr + r) * tc, tc), :] = val[:, q * LANES:(q + 1) * LANES]


def _s5_get(ref, r, rr, tc):
    return jnp.concatenate([ref[pl.ds((q * rr + r) * tc, tc), :] for q in range(S5_Q)], axis=1)


def _s5_fwd_call(u, a_re, a_im, b_re, b_im, c_re, c_im):
    rr, t_len, _ = u.shape
    sets = b_re.shape[0]
    per = rr // sets
    tc = _tile(t_len, (256, 128))
    nt = t_len // tc
    qr = S5_Q * rr

    def body(u_ref, ar_ref, ai_ref, br_ref, bi_ref, cr_ref, ci_ref, y_ref, hr_ref, hi_ref, sr_ref, si_ref):
        @pl.when(pl.program_id(0) == 0)
        def _():
            sr_ref[...] = jnp.zeros_like(sr_ref)
            si_ref[...] = jnp.zeros_like(si_ref)

        for r in range(rr):
            ub = u_ref[r]
            _s5_put(hr_ref, r, rr, tc, _dot(ub, br_ref[r // per]))
            _s5_put(hi_ref, r, rr, tc, _dot(ub, bi_ref[r // per]))
        ar, ai = ar_ref[...], ai_ref[...]

        def step(t, carry):
            hr, hi = carry
            at = pl.ds(t, qr, stride=tc)
            nr = ar * hr - ai * hi + hr_ref[at, :]
            ni = ar * hi + ai * hr + hi_ref[at, :]
            hr_ref[at, :] = nr
            hi_ref[at, :] = ni
            return nr, ni

        hr, hi = lax.fori_loop(0, tc, step, (sr_ref[...], si_ref[...]), unroll=8)
        sr_ref[...] = hr
        si_ref[...] = hi
        for r in range(rr):
            y_ref[r] = (_dot(_s5_get(hr_ref, r, rr, tc).astype(BF16), cr_ref[r // per])
                        - _dot(_s5_get(hi_ref, r, rr, tc).astype(BF16), ci_ref[r // per]))

    full = lambda a: pl.BlockSpec(a.shape, lambda i: (0,) * a.ndim)
    h_spec = pl.BlockSpec((None, qr * tc, LANES), lambda i: (i, 0, 0))
    h_shape = jax.ShapeDtypeStruct((nt, qr * tc, LANES), F32)
    a_re, a_im = _s5_tiles(a_re), _s5_tiles(a_im)
    return pl.pallas_call(
        body, name='s5_fwd', grid=(nt,),
        out_shape=(jax.ShapeDtypeStruct((rr, t_len, S5_HALF), F32), h_shape, h_shape),
        in_specs=[pl.BlockSpec((rr, tc, S5_HALF), lambda i: (0, i, 0)), full(a_re), full(a_im), full(b_re), full(b_im),
                  full(c_re), full(c_im)],
        out_specs=(pl.BlockSpec((rr, tc, S5_HALF), lambda i: (0, i, 0)), h_spec, h_spec),
        scratch_shapes=[pltpu.VMEM((qr, LANES), F32), pltpu.VMEM((qr, LANES), F32)],
        compiler_params=_params('arbitrary'),
    )(u, a_re, a_im, b_re, b_im, c_re, c_im)


def _s5_bwd_call(u, a_re, a_im, b_re, b_im, c_re, c_im, h_re, h_im, dy):
    rr, t_len, _ = u.shape
    sets = b_re.shape[0]
    per = rr // sets
    nt, rows, _ = h_re.shape
    qr = S5_Q * rr
    tc = rows // qr

    def body(u_ref, dy_ref, ar_ref, ai_ref, br_ref, bi_ref, cr_ref, ci_ref, hr_ref, hi_ref,
             du_ref, dar_ref, dai_ref, dbr_ref, dbi_ref, dcr_ref, dci_ref, gr_ref, gi_ref, sr_ref, si_ref):
        i = pl.program_id(0)

        @pl.when(i == 0)
        def _():
            for ref in (dar_ref, dai_ref, dbr_ref, dbi_ref, dcr_ref, dci_ref, sr_ref, si_ref):
                ref[...] = jnp.zeros_like(ref)

        for r in range(rr):
            dyb = dy_ref[r]
            _s5_put(gr_ref, r, rr, tc, _dot_nt(dyb, cr_ref[r // per]))
            _s5_put(gi_ref, r, rr, tc, -_dot_nt(dyb, ci_ref[r // per]))
        ar, ai = ar_ref[...], ai_ref[...]
        g_r, g_i = sr_ref[...], si_ref[...]
        last = pl.ds(tc - 1, qr, stride=tc)
        d_r = g_r * hr_ref[last, :] + g_i * hi_ref[last, :]
        d_i = g_i * hr_ref[last, :] - g_r * hi_ref[last, :]

        def advance(t, g_r, g_i):
            at = pl.ds(t, qr, stride=tc)
            n_r = ar * g_r + ai * g_i + gr_ref[at, :]
            n_i = ar * g_i - ai * g_r + gi_ref[at, :]
            gr_ref[at, :] = n_r
            gi_ref[at, :] = n_i
            return n_r, n_i

        def step(k, carry):
            g_r, g_i, d_r, d_i = carry
            t = tc - 1 - k
            g_r, g_i = advance(t, g_r, g_i)
            before = pl.ds(t - 1, qr, stride=tc)
            p_r, p_i = hr_ref[before, :], hi_ref[before, :]
            return g_r, g_i, d_r + g_r * p_r + g_i * p_i, d_i + g_i * p_r - g_r * p_i

        g_r, g_i, d_r, d_i = lax.fori_loop(0, tc - 1, step, (g_r, g_i, d_r, d_i), unroll=8)
        g_r, g_i = advance(0, g_r, g_i)
        sr_ref[...] = g_r
        si_ref[...] = g_i
        dar_ref[...] += d_r
        dai_ref[...] += d_i
        for r in range(rr):
            s = r // per
            ub, dyb = u_ref[r], dy_ref[r]
            grb, gib = _s5_get(gr_ref, r, rr, tc).astype(BF16), _s5_get(gi_ref, r, rr, tc).astype(BF16)
            du_ref[r] = _dot_nt(grb, br_ref[s]) + _dot_nt(gib, bi_ref[s])
            dbr_ref[s] += _dot_tn(ub, grb)
            dbi_ref[s] += _dot_tn(ub, gib)
            dcr_ref[s] += _dot_tn(_s5_get(hr_ref, r, rr, tc).astype(BF16), dyb)
            dci_ref[s] -= _dot_tn(_s5_get(hi_ref, r, rr, tc).astype(BF16), dyb)

    full = lambda a: pl.BlockSpec(a.shape, lambda i: (0,) * a.ndim)
    back = lambda i: nt - 1 - i
    tok = pl.BlockSpec((rr, tc, S5_HALF), lambda i: (0, back(i), 0))
    h_spec = pl.BlockSpec((None, qr * tc, LANES), lambda i: (back(i), 0, 0))
    f = lambda a: jax.ShapeDtypeStruct(a.shape, F32)
    a_re, a_im = _s5_tiles(a_re), _s5_tiles(a_im)
    du, da_re, da_im, db_re, db_im, dc_re, dc_im = pl.pallas_call(
        body, name='s5_bwd', grid=(nt,),
        out_shape=(jax.ShapeDtypeStruct(u.shape, F32), f(a_re), f(a_im), f(b_re), f(b_im), f(c_re), f(c_im)),
        in_specs=[tok, tok, full(a_re), full(a_im), full(b_re), full(b_im), full(c_re), full(c_im), h_spec, h_spec],
        out_specs=(tok, full(a_re), full(a_im), full(b_re), full(b_im), full(c_re), full(c_im)),
        scratch_shapes=[pltpu.VMEM((qr * tc, LANES), F32), pltpu.VMEM((qr * tc, LANES), F32),
                        pltpu.VMEM((qr, LANES), F32), pltpu.VMEM((qr, LANES), F32)],
        compiler_params=_params('arbitrary'),
    )(u, dy, a_re, a_im, b_re, b_im, c_re, c_im, h_re, h_im)
    return du, _s5_untiles(da_re), _s5_untiles(da_im), db_re, db_im, dc_re, dc_im


@jax.custom_vjp
def s5_core(u, a_re, a_im, b_re, b_im, c_re, c_im):
    return _s5_fwd_call(u.astype(BF16), a_re, a_im, b_re.astype(BF16), b_im.astype(BF16), c_re.astype(BF16),
                        c_im.astype(BF16))[0]


def _s5_core_fwd(u, a_re, a_im, b_re, b_im, c_re, c_im):
    args = (u.astype(BF16), a_re, a_im, b_re.astype(BF16), b_im.astype(BF16), c_re.astype(BF16), c_im.astype(BF16))
    y, h_re, h_im = _s5_fwd_call(*args)
    return y, args + (h_re, h_im)


def _s5_core_bwd(res, g):
    return _s5_bwd_call(*res, g.astype(BF16))


s5_core.defvjp(_s5_core_fwd, _s5_core_bwd)


def _exchange(x, gather, hbm, name):
    block = x.shape if gather else x.shape[1:]

    def body(x_ref, out_ref, send_sems, recv_sems, local_sem):
        ix, iy, ic = lax.axis_index('x'), lax.axis_index('y'), lax.axis_index('c')
        me = 4 * ix + 2 * iy + ic

        def flipped(k):
            px = 1 - ix if k & 4 else ix
            py = 1 - iy if k & 2 else iy
            pc = 1 - ic if k & 1 else ic
            return (px, py, pc), 4 * px + 2 * py + pc

        def copy(k, src, dst):
            return pltpu.make_async_remote_copy(src_ref=src, dst_ref=dst, send_sem=send_sems.at[k - 1],
                                                 recv_sem=recv_sems.at[k - 1], device_id=flipped(k)[0], device_id_type=MESH)

        own = pltpu.make_async_copy(x_ref if gather else x_ref.at[me], out_ref.at[me], local_sem)
        own.start()
        sent = []
        for k in range(1, N_DEV):
            src = x_ref if gather else x_ref.at[flipped(k)[1]]
            sent.append(copy(k, src, out_ref.at[me]))
            sent[-1].start()
        for k in range(1, N_DEV):
            src = x_ref if gather else x_ref.at[flipped(k)[1]]
            copy(k, src, out_ref.at[flipped(k)[1]]).wait_recv()
        for cp in sent:
            cp.wait_send()
        own.wait()

    space = pltpu.HBM if hbm else pltpu.VMEM
    return pl.pallas_call(
        body, name=name, out_shape=jax.ShapeDtypeStruct((N_DEV,) + tuple(block), x.dtype),
        in_specs=[pl.BlockSpec(memory_space=space)], out_specs=pl.BlockSpec(memory_space=space),
        scratch_shapes=[pltpu.SemaphoreType.DMA((N_DEV - 1,)), pltpu.SemaphoreType.DMA((N_DEV - 1,)), pltpu.SemaphoreType.DMA],
        compiler_params=pltpu.CompilerParams(vmem_limit_bytes=VMEM_LIMIT_BYTES),
    )(x)


def _shard_view(ref, axis, index, width):
    return ref.at[(slice(None),) * axis + (pl.ds(pl.multiple_of(index * width, width), width),)]


def _exchange_many(xs, cuts, gather, name):
    n = len(xs)
    if gather:
        shards = [x.shape for x in xs]
    else:
        shards = [x.shape[1:] if cut is None else x.shape[:cut] + (x.shape[cut] // N_DEV,) + x.shape[cut + 1:]
                  for x, cut in zip(xs, cuts)]

    def full_shape(shard, cut):
        return shard[:cut] + (N_DEV * shard[cut],) + shard[cut + 1:]

    out_shapes = [jax.ShapeDtypeStruct((N_DEV,) + tuple(s) if (cut is None or not gather) else full_shape(tuple(s), cut), x.dtype)
                  for x, s, cut in zip(xs, shards, cuts)]

    def body(*refs):
        x_refs, out_refs = refs[:n], refs[n:2 * n]
        send_sems, recv_sems, local_sems = refs[2 * n:]
        ix, iy, ic = lax.axis_index('x'), lax.axis_index('y'), lax.axis_index('c')
        me = 4 * ix + 2 * iy + ic

        def flipped(k):
            px = 1 - ix if k & 4 else ix
            py = 1 - iy if k & 2 else iy
            pc = 1 - ic if k & 1 else ic
            return (px, py, pc), 4 * px + 2 * py + pc

        def block(ref, cut, shard, who):
            return ref.at[who] if cut is None else _shard_view(ref, cut, who, shard[cut])

        def ends(i, sender, receiver):
            if gather:
                return x_refs[i], block(out_refs[i], cuts[i], shards[i], sender)
            return block(x_refs[i], cuts[i], shards[i], receiver), out_refs[i].at[sender]

        def copy(i, k, sender, receiver):
            src, dst = ends(i, sender, receiver)
            return pltpu.make_async_remote_copy(src_ref=src, dst_ref=dst, send_sem=send_sems.at[i * (N_DEV - 1) + k - 1],
                                                 recv_sem=recv_sems.at[i * (N_DEV - 1) + k - 1], device_id=flipped(k)[0],
                                                 device_id_type=MESH)

        own = [pltpu.make_async_copy(*ends(i, me, me), local_sems.at[i]) for i in range(n)]
        for cp in own:
            cp.start()
        if gather:
            chips = (2, 4, 6)
            sent = [copy(i, k, me, flipped(k)[1]) for k in (1,) + chips for i in range(n)]
            for cp in sent:
                cp.start()
            for k in chips:
                for i in range(n):
                    copy(i, k, flipped(k)[1], me).wait_recv()
                    src, dst = ends(i, flipped(k)[1], me)
                    sent.append(pltpu.make_async_remote_copy(
                        src_ref=dst, dst_ref=dst, send_sem=send_sems.at[i * (N_DEV - 1) + k], recv_sem=recv_sems.at[i * (N_DEV - 1) + k],
                        device_id=flipped(1)[0], device_id_type=MESH))
                    sent[-1].start()
            for k in (1, 3, 5, 7):
                for i in range(n):
                    src, dst = ends(i, flipped(k)[1], me)
                    pltpu.make_async_remote_copy(
                        src_ref=dst, dst_ref=dst, send_sem=send_sems.at[i * (N_DEV - 1) + k - 1], recv_sem=recv_sems.at[i * (N_DEV - 1) + k - 1],
                        device_id=flipped(1)[0], device_id_type=MESH).wait_recv()
            for cp in sent:
                cp.wait_send()
            for cp in own:
                cp.wait()
            return
        sent = [copy(i, k, me, flipped(k)[1]) for k in range(1, N_DEV) for i in range(n)]
        for cp in sent:
            cp.start()
        for k in range(1, N_DEV):
            for i in range(n):
                copy(i, k, flipped(k)[1], me).wait_recv()
        for cp in sent:
            cp.wait_send()
        for cp in own:
            cp.wait()

    hbm = pl.BlockSpec(memory_space=pltpu.HBM)
    pairs = n * (N_DEV - 1)
    return pl.pallas_call(
        body, name=name, out_shape=out_shapes, in_specs=[hbm] * n, out_specs=[hbm] * n,
        scratch_shapes=[pltpu.SemaphoreType.DMA((pairs,)), pltpu.SemaphoreType.DMA((pairs,)), pltpu.SemaphoreType.DMA((n,))],
    )(*xs)


N_CHIPS = N_DEV // 2


def _block_shapes(xs, cuts):
    return [x.shape[1:] if cut is None else x.shape[:cut] + (x.shape[cut] // N_DEV,) + x.shape[cut + 1:]
            for x, cut in zip(xs, cuts)]


def _scatter_pairs(xs, cuts):
    n = len(xs)
    shards = _block_shapes(xs, cuts)

    def body(*refs):
        x_refs, mine_refs, theirs_refs = refs[:n], refs[n:3 * n:2], refs[n + 1:3 * n:2]
        send_sems, recv_sems, local_sems = refs[3 * n:]
        ix, iy, ic = lax.axis_index('x'), lax.axis_index('y'), lax.axis_index('c')
        sibling = (ix, iy, 1 - ic)

        def block(i, device):
            return x_refs[i].at[device] if cuts[i] is None else _shard_view(x_refs[i], cuts[i], device, shards[i][cuts[i]])

        def to_sibling(i, j):
            return pltpu.make_async_remote_copy(src_ref=block(i, 2 * j + 1 - ic), dst_ref=theirs_refs[i].at[j],
                                                 send_sem=send_sems.at[i * N_CHIPS + j], recv_sem=recv_sems.at[i * N_CHIPS + j],
                                                 device_id=sibling, device_id_type=MESH)

        own = [pltpu.make_async_copy(block(i, 2 * j + ic), mine_refs[i].at[j], local_sems.at[i * N_CHIPS + j])
               for i in range(n) for j in range(N_CHIPS)]
        sent = [to_sibling(i, j) for i in range(n) for j in range(N_CHIPS)]
        for cp in own + sent:
            cp.start()
        for cp in sent:
            cp.wait_recv()
        for cp in sent:
            cp.wait_send()
        for cp in own:
            cp.wait()

    hbm = pl.BlockSpec(memory_space=pltpu.HBM)
    out_shapes = [jax.ShapeDtypeStruct((N_CHIPS,) + tuple(s), x.dtype) for x, s in zip(xs, shards) for _ in range(2)]
    out = pl.pallas_call(
        body, name='scatter_pairs', out_shape=out_shapes, in_specs=[hbm] * n, out_specs=[hbm] * (2 * n),
        scratch_shapes=[pltpu.SemaphoreType.DMA((n * N_CHIPS,))] * 3,
    )(*xs)
    return list(zip(out[::2], out[1::2]))


def _scatter_chips(xs):
    n = len(xs)

    def body(*refs):
        x_refs, out_refs = refs[:n], refs[n:2 * n]
        send_sems, recv_sems, local_sems = refs[2 * n:]
        ix, iy, ic = lax.axis_index('x'), lax.axis_index('y'), lax.axis_index('c')
        chip = 2 * ix + iy
        flips = ((1, 0), (0, 1), (1, 1))

        def other(f):
            px = 1 - ix if f[0] else ix
            py = 1 - iy if f[1] else iy
            return (px, py, ic), 2 * px + py

        def copy(i, e, sender, receiver):
            return pltpu.make_async_remote_copy(src_ref=x_refs[i].at[receiver], dst_ref=out_refs[i].at[sender],
                                                 send_sem=send_sems.at[i * 3 + e], recv_sem=recv_sems.at[i * 3 + e],
                                                 device_id=other(flips[e])[0], device_id_type=MESH)

        own = [pltpu.make_async_copy(x_refs[i].at[chip], out_refs[i].at[chip], local_sems.at[i]) for i in range(n)]
        sent = [copy(i, e, chip, other(f)[1]) for e, f in enumerate(flips) for i in range(n)]
        for cp in own + sent:
            cp.start()
        for e, f in enumerate(flips):
            for i in range(n):
                copy(i, e, other(f)[1], chip).wait_recv()
        for cp in sent:
            cp.wait_send()
        for cp in own:
            cp.wait()

    hbm = pl.BlockSpec(memory_space=pltpu.HBM)
    return pl.pallas_call(
        body, name='scatter_chips', out_shape=[jax.ShapeDtypeStruct(x.shape, x.dtype) for x in xs],
        in_specs=[hbm] * n, out_specs=[hbm] * n,
        scratch_shapes=[pltpu.SemaphoreType.DMA((n * 3,)), pltpu.SemaphoreType.DMA((n * 3,)), pltpu.SemaphoreType.DMA((n,))],
    )(*xs)


def _pair_sum(a, b):
    shape = a.shape
    a, b = _as_rows(a), _as_rows(b)
    rows, cols = a.shape
    tr = _tile(rows, (1024, 512, 256, 128, 64, 32, 16))

    def body(a_ref, b_ref, o_ref):
        o_ref[...] = (a_ref[...].astype(F32) + b_ref[...].astype(F32)).astype(o_ref.dtype)

    spec = pl.BlockSpec((tr, cols), lambda i: (i, 0))
    return pl.pallas_call(
        body, name='pair_sum', grid=(rows // tr,), out_shape=jax.ShapeDtypeStruct(a.shape, a.dtype),
        in_specs=[spec, spec], out_specs=spec, compiler_params=_params('parallel'),
    )(a, b).reshape(shape)


def _adamw_landed(landed, w, m, v):
    shape = w.shape
    slots = landed.shape[0]
    w, m, v = (_as_rows(a) for a in (w, m, v))
    rows, cols = w.shape
    landed = landed.reshape(slots, rows, cols)
    tr = _tile(rows, (256, 128, 64, 32, 16))

    def body(l_ref, w_ref, m_ref, v_ref, g_ref, d_ref, nm_ref, nv_ref):
        g = l_ref[0].astype(F32)
        for d in range(1, slots):
            g = g + l_ref[d].astype(F32)
        g_ref[...] = g
        d_ref[...], nm_ref[...], nv_ref[...] = _adamw_math(w_ref[...], g, m_ref[...], v_ref[...])

    spec = pl.BlockSpec((tr, cols), lambda i: (i, 0))
    out = pl.pallas_call(
        body, name='adamw_landed', grid=(rows // tr,), out_shape=(jax.ShapeDtypeStruct(w.shape, F32),) * 4,
        in_specs=[pl.BlockSpec((slots, tr, cols), lambda i: (0, i, 0))] + [spec] * 3, out_specs=(spec,) * 4,
        compiler_params=_params('parallel'),
    )(landed, w, m, v)
    return tuple(o.reshape(shape) for o in out)


def _all_reduce_small(x):
    g = _exchange(x, True, False, 'gather_small_grads')

    def body(g_ref, o_ref):
        acc = g_ref[0]
        for d in range(1, N_DEV):
            acc = acc + g_ref[d]
        o_ref[...] = acc

    return pl.pallas_call(body, name='sum_small', out_shape=jax.ShapeDtypeStruct(x.shape, F32))(g)


def _adamw_math(w, g, m, v):
    m = ADAM_B1 * m + (1.0 - ADAM_B1) * g
    v = ADAM_B2 * v + (1.0 - ADAM_B2) * (g * g)
    m_hat = m / (1.0 - ADAM_B1 ** ADAM_STEP)
    v_hat = v / (1.0 - ADAM_B2 ** ADAM_STEP)
    return -ADAM_LR * (m_hat / (jnp.sqrt(v_hat) + ADAM_EPS) + ADAM_WD * w), m, v


def _as_rows(a):
    return a.reshape(1, -1) if a.ndim < 2 else a.reshape(-1, a.shape[-1])


def _as_lanes(a):
    return a.reshape(-1, LANES) if a.size % LANES == 0 else a.reshape(1, -1)


def _adamw_big(w, g, m, v):
    shape = w.shape
    w, g, m, v = (_as_rows(a) for a in (w, g, m, v))
    rows, cols = w.shape
    tr = _tile(rows, (512, 256, 128, 64, 32, 16, 8))

    def body(w_ref, g_ref, m_ref, v_ref, d_ref, nm_ref, nv_ref):
        d_ref[...], nm_ref[...], nv_ref[...] = _adamw_math(w_ref[...], g_ref[...], m_ref[...], v_ref[...])

    spec = pl.BlockSpec((tr, cols), lambda i: (i, 0))
    out = pl.pallas_call(
        body, name='adamw', grid=(rows // tr,), out_shape=(jax.ShapeDtypeStruct(w.shape, F32),) * 3,
        in_specs=[spec] * 4, out_specs=(spec,) * 3, compiler_params=_params('parallel'),
    )(w, g, m, v)
    return tuple(o.reshape(shape) for o in out)


def _adamw_small(ws, gs, ms, vs):
    n = len(ws)
    shapes = [w.shape for w in ws]
    flat = [_as_lanes(a) for group in (ws, gs, ms, vs) for a in group]

    def body(*refs):
        ins, outs = refs[:4 * n], refs[4 * n:]
        for i in range(n):
            d, m, v = _adamw_math(ins[i][...], ins[n + i][...], ins[2 * n + i][...], ins[3 * n + i][...])
            outs[i][...], outs[n + i][...], outs[2 * n + i][...] = d, m, v

    out = pl.pallas_call(
        body, name='adamw_small', out_shape=tuple(jax.ShapeDtypeStruct(flat[i].shape, F32) for _ in range(3) for i in range(n)),
    )(*flat)
    return [tuple(out[j * n + i].reshape(shapes[i]) for j in range(3)) for i in range(n)]


def rms_norm(x, g):
    return x * lax.rsqrt(jnp.mean(jnp.square(x), axis=-1, keepdims=True) + EPS) * g


def modulate(x, g, shift, scale):
    return rms_norm(x, g) * (1 + scale) + shift


def rope_tables(n_tokens, rot_dim):
    t = jnp.arange(n_tokens)
    rows = (t // GRID_W).astype(F32)
    cols = (t % GRID_W).astype(F32)
    axis_dim = rot_dim // 2
    freqs = ROPE_BASE ** (-jnp.arange(0, axis_dim, 2, dtype=F32) / axis_dim)
    ang_r, ang_c = rows[:, None] * freqs, cols[:, None] * freqs
    ang = jnp.concatenate([ang_r, ang_r, ang_c, ang_c], axis=-1)
    return jnp.cos(ang), jnp.sin(ang)


def rope(x, cos, sin):
    x1, x2, x3, x4 = jnp.split(x, 4, axis=-1)
    rot = jnp.concatenate([-x2, x1, -x4, x3], axis=-1)
    return x * cos[:, None, :] + rot * sin[:, None, :]


def heads_first(t):
    return jnp.swapaxes(t, 1, 2)


def tokens_matmul(t, w):
    b, n, k = t.shape
    return linear(t.reshape(b * n, k), w).reshape(b, n, w.shape[1])


def s5_discretize(lam_re, lam_im, log_dt, b_re, b_im):
    dt = jnp.exp(log_dt)[:, None]
    mag = jnp.exp(lam_re * dt)
    a_re = mag * jnp.cos(lam_im * dt)
    a_im = mag * jnp.sin(lam_im * dt)
    den = jnp.square(lam_re) + jnp.square(lam_im)
    f_re = ((a_re - 1.0) * lam_re + a_im * lam_im) / den
    f_im = (a_im * lam_re - (a_re - 1.0) * lam_im) / den
    bb_re = f_re[..., None] * b_re - f_im[..., None] * b_im
    bb_im = f_re[..., None] * b_im + f_im[..., None] * b_re
    return a_re, a_im, bb_re, bb_im


def s5_mixer(u_lat, u_ctx, p, j, need_ctx):
    b, n, _ = u_lat.shape
    c = u_ctx.shape[1]
    half_groups = SSM_GROUPS // 2
    eye = jnp.eye(half_groups, dtype=F32)
    a_res, a_ims, b_res, b_ims, c_res, c_ims, seqs = [], [], [], [], [], [], []
    for d in range(2):
        a_re, a_im, bb_re, bb_im = s5_discretize(p['ssm_lam_re'][j, d], p['ssm_lam_im'][j, d], p['ssm_log_dt'][j, d],
                                                 p['ssm_b_re'][j, d], p['ssm_b_im'][j, d])
        for half in range(2):
            grp = slice(half * half_groups, (half + 1) * half_groups)
            a_res.append(a_re[grp].reshape(S5_LANES))
            a_ims.append(a_im[grp].reshape(S5_LANES))
            b_res.append(jnp.einsum('gsp,gh->gphs', bb_re[grp], eye).reshape(S5_HALF, S5_LANES))
            b_ims.append(jnp.einsum('gsp,gh->gphs', bb_im[grp], eye).reshape(S5_HALF, S5_LANES))
            c_res.append(jnp.einsum('gps,gh->gshp', p['ssm_c_re'][j, d][grp], eye).reshape(S5_LANES, S5_HALF))
            c_ims.append(jnp.einsum('gps,gh->gshp', p['ssm_c_im'][j, d][grp], eye).reshape(S5_LANES, S5_HALF))
        flip = (lambda t: t[:, ::-1]) if d == 1 else (lambda t: t)
        seq = jnp.concatenate([flip(u_ctx), flip(u_lat)], axis=1)
        seqs.append(jnp.transpose(seq.reshape(b, c + n, 2, S5_HALF), (2, 0, 1, 3)))
    u = jnp.stack(seqs).reshape(4 * b, c + n, S5_HALF)
    rep = lambda parts: jnp.repeat(jnp.stack(parts), b, axis=0)
    y = s5_core(u, rep(a_res), rep(a_ims), jnp.stack(b_res), jnp.stack(b_ims), jnp.stack(c_res), jnp.stack(c_ims))
    y = jnp.transpose(y.reshape(2, 2, b, c + n, S5_HALF), (0, 2, 3, 1, 4)).reshape(2, b, c + n, SSM_WIDTH)
    d_skip = p['ssm_d'][j]
    y_lat = d_skip * u_lat + y[0, :, c:] + y[1, :, c:][:, ::-1]
    wg, bg = p['ssm_w_glu'][j], p['ssm_b_glu'][j]

    def glu(t):
        t = jax.nn.gelu(t)
        return t * jax.nn.sigmoid(tokens_matmul(t, wg) + bg)

    if not need_ctx:
        return glu(y_lat), None
    y_ctx = d_skip * u_ctx + y[0, :, :c] + y[1, :, :c][:, ::-1]
    return glu(y_lat), glu(y_ctx)


def even_mixer(a_lat, a_ctx, p, j, need_ctx):
    b, n, _ = a_lat.shape
    c = a_ctx.shape[1]
    cos, sin = rope_tables(n, HEAD_DIM)
    proj = tokens_matmul(jnp.concatenate([a_ctx, a_lat], axis=1), p['e_w_in'][j])
    q, k, v, u = jnp.split(proj, [GQA_Q_W, GQA_Q_W + GQA_KV_W, GQA_Q_W + 2 * GQA_KV_W], axis=-1)
    q = rms_norm(q.reshape(b, c + n, GQA_Q_HEADS, HEAD_DIM), p['e_g_q'][j])
    k = rms_norm(k.reshape(b, c + n, GQA_KV_HEADS, HEAD_DIM), p['e_g_k'][j])
    v = v.reshape(b, c + n, GQA_KV_HEADS, HEAD_DIM)
    q_l = rope(q[:, c:], cos, sin)
    k = jnp.concatenate([k[:, :c], rope(k[:, c:], cos, sin)], axis=1)
    scale = HEAD_DIM ** -0.5
    kh, vh = heads_first(k), heads_first(v)
    att_l = heads_first(attention(heads_first(q_l), kh, vh, scale)).reshape(b, n, GQA_Q_W)
    ssm_l, ssm_c = s5_mixer(u[:, c:], u[:, :c], p, j, need_ctx)
    mix_l = jnp.concatenate([att_l, ssm_l], axis=-1)
    if not need_ctx:
        return tokens_matmul(mix_l, p['e_w_out'][j]), None
    att_c = heads_first(attention(heads_first(q[:, :c]), kh[:, :, :c], vh[:, :, :c], scale)).reshape(b, c, GQA_Q_W)
    mix = jnp.concatenate([jnp.concatenate([att_c, ssm_c], axis=-1), mix_l], axis=1)
    out = tokens_matmul(mix, p['e_w_out'][j])
    return out[:, c:], out[:, :c]


def odd_mixer(a_lat, a_ctx, p, j, need_ctx):
    b, n, _ = a_lat.shape
    c = a_ctx.shape[1]
    t = c + n
    cos, sin = rope_tables(n, MLA_ROPE)
    proj = tokens_matmul(jnp.concatenate([a_ctx, a_lat], axis=1), p['o_w_in'][j])
    c1, c2, c3 = MLA_Q_RANK, MLA_Q_RANK + MLA_KV_RANK, MLA_Q_RANK + MLA_KV_RANK + MLA_ROPE
    cq, ckv, kr = proj[..., :c1], proj[..., c1:c2], proj[..., c2:c3]
    nq, nk, nv = jnp.split(proj[..., ODD_NA_AT:], 3, axis=-1)
    q = tokens_matmul(rms_norm(cq, p['mla_g_cq'][j]), p['mla_w_uq'][j]).reshape(b, t, MLA_HEADS, MLA_QK)
    kv = tokens_matmul(rms_norm(ckv, p['mla_g_ckv'][j]), p['mla_w_ukv'][j]).reshape(b, t, MLA_HEADS, MLA_NOPE + MLA_V)
    k = jnp.concatenate([kv[..., :MLA_NOPE], jnp.broadcast_to(kr[:, :, None, :], (b, t, MLA_HEADS, MLA_ROPE))], axis=-1)
    q, k, mv = rms_norm(q, p['mla_g_q'][j]), rms_norm(k, p['mla_g_k'][j]), kv[..., MLA_NOPE:]

    def rope_tail(x):
        tail = jnp.concatenate([x[:, :c, :, MLA_NOPE:], rope(x[:, c:, :, MLA_NOPE:], cos, sin)], axis=1)
        return jnp.concatenate([x[..., :MLA_NOPE], tail], axis=-1)

    q, k = rope_tail(q), rope_tail(k)
    qh, kh, vh = heads_first(q), heads_first(k), heads_first(mv)
    mla_scale = MLA_QK ** -0.5
    mla_l = heads_first(attention(qh[:, :, c:], kh, vh, mla_scale)).reshape(b, n, MLA_HEADS * MLA_V)
    nq = heads_first(rms_norm(nq.reshape(b, t, NA_HEADS, HEAD_DIM), p['na_g_q'][j]))
    nk = heads_first(rms_norm(nk.reshape(b, t, NA_HEADS, HEAD_DIM), p['na_g_k'][j]))
    nv = heads_first(nv.reshape(b, t, NA_HEADS, HEAD_DIM))
    na_scale = HEAD_DIM ** -0.5
    na_l = na_attention(nq[:, :, c:], nk[:, :, c:], nv[:, :, c:], nk[:, :, :c], nv[:, :, :c], na_bias_table(p['na_rpb'][j]),
                        na_scale)
    na_l = heads_first(na_l).reshape(b, n, NA_W)
    mix_l = jnp.concatenate([mla_l, na_l], axis=-1)
    if not need_ctx:
        return tokens_matmul(mix_l, p['o_w_out'][j]), None
    mla_c = heads_first(attention(qh[:, :, :c], kh[:, :, :c], vh[:, :, :c], mla_scale)).reshape(b, c, MLA_HEADS * MLA_V)
    na_c = heads_first(attention(nq[:, :, :c], nk[:, :, :c], nv[:, :, :c], na_scale)).reshape(b, c, NA_W)
    mix = jnp.concatenate([jnp.concatenate([mla_c, na_c], axis=-1), mix_l], axis=1)
    out = tokens_matmul(mix, p['o_w_out'][j])
    return out[:, c:], out[:, :c]


def mlp(h, w1, w2):
    b, n, k = h.shape
    return mlp_rows(h.reshape(b * n, k), w1, w2).reshape(b, n, w2.shape[1])


def local_loss(x, p, m_lat, m_ctx, ctx, target):
    depth = m_lat.shape[0]
    c = ctx.shape[1]
    xc = ctx
    for i in range(depth):
        need_ctx = i < depth - 1
        j = i // 2
        ml = [m_lat[i, :, s][:, None, :] for s in range(N_MOD)]
        mc = [m_ctx[i, s][None, None, :] for s in range(N_MOD)]
        a_lat = modulate(x, p['g_norm1'][i], ml[0], ml[1])
        a_ctx = modulate(xc, p['g_norm1'][i], mc[0], mc[1])
        mixer = even_mixer if i % 2 == 0 else odd_mixer
        o_lat, o_ctx = mixer(a_lat, a_ctx, p, j, need_ctx)
        x = x + ml[2] * o_lat
        h_lat = modulate(x, p['g_norm2'][i], ml[3], ml[4])
        if need_ctx:
            xc = xc + mc[2] * o_ctx
            h_ctx = modulate(xc, p['g_norm2'][i], mc[3], mc[4])
            ff = mlp(jnp.concatenate([h_ctx, h_lat], axis=1), p['w_ff1'][i], p['w_ff2'][i])
            x = x + ml[5] * ff[:, c:]
            xc = xc + mc[5] * ff[:, :c]
        else:
            x = x + ml[5] * mlp(h_lat, p['w_ff1'][i], p['w_ff2'][i])
    return 0.5 * jnp.sum(jnp.mean(jnp.square(x - target), axis=-1))


def _packed_rows(size, layout):
    width, group = layout
    return -(-size // (width * group)) * group


def _pack_rows(flat, layout):
    width = layout[0]
    rows = _packed_rows(flat.shape[-1], layout)
    flat = jnp.pad(flat, [(0, 0)] * (flat.ndim - 1) + [(0, rows * width - flat.shape[-1])])
    return flat.reshape(flat.shape[:-1] + (rows, width))


def _unpack_rows(rows, shape):
    lead = rows.shape[:-2]
    return rows.reshape(lead + (-1,))[..., :math.prod(shape)].reshape(lead + tuple(shape))


def _unpack_all(packed, shapes, layout):
    out, at = [], 0
    for shape in shapes:
        rows = _packed_rows(math.prod(shape), layout)
        out.append(_unpack_rows(packed[..., at:at + rows, :], shape))
        at += rows
    return out


def _join_shards(g, axis):
    g = jnp.moveaxis(g, 0, axis)
    return g.reshape(g.shape[:axis] + (N_DEV * g.shape[axis + 1],) + g.shape[axis + 2:])


def _split_shards(full, axis):
    s = full.shape
    return jnp.moveaxis(full.reshape(s[:axis] + (N_DEV, s[axis] // N_DEV) + s[axis + 1:]), axis, 0)


def _gather_packed(parts, dtype, layout, hbm, name):
    packed = jnp.concatenate([_pack_rows(a.astype(dtype).reshape(-1), layout) for a in parts], axis=0)
    return _unpack_all(_exchange(packed, True, hbm, name), [a.shape for a in parts], layout)


def kernel(x, c, ctx, c_ctx, w_mod, b_mod, g_norm1, g_norm2, w_ff1, w_ff2, e_w_in, e_w_out, e_g_q, e_g_k, ssm_lam_re, ssm_lam_im, ssm_log_dt, ssm_b_re, ssm_b_im, ssm_c_re, ssm_c_im, ssm_d, ssm_w_glu, ssm_b_glu, o_w_in, o_w_out, mla_g_cq, mla_g_ckv, mla_w_uq, mla_w_ukv, mla_g_q, mla_g_k, na_g_q, na_g_k, na_rpb, loss_target, m_c_ctx, m_w_mod, m_b_mod, m_g_norm1, m_g_norm2, m_w_ff1, m_w_ff2, m_e_w_in, m_e_w_out, m_e_g_q, m_e_g_k, m_ssm_lam_re, m_ssm_lam_im, m_ssm_log_dt, m_ssm_b_re, m_ssm_b_im, m_ssm_c_re, m_ssm_c_im, m_ssm_d, m_ssm_w_glu, m_ssm_b_glu, m_o_w_in, m_o_w_out, m_mla_g_cq, m_mla_g_ckv, m_mla_w_uq, m_mla_w_ukv, m_mla_g_q, m_mla_g_k, m_na_g_q, m_na_g_k, m_na_rpb, v_c_ctx, v_w_mod, v_b_mod, v_g_norm1, v_g_norm2, v_w_ff1, v_w_ff2, v_e_w_in, v_e_w_out, v_e_g_q, v_e_g_k, v_ssm_lam_re, v_ssm_lam_im, v_ssm_log_dt, v_ssm_b_re, v_ssm_b_im, v_ssm_c_re, v_ssm_c_im, v_ssm_d, v_ssm_w_glu, v_ssm_b_glu, v_o_w_in, v_o_w_out, v_mla_g_cq, v_mla_g_ckv, v_mla_w_uq, v_mla_w_ukv, v_mla_g_q, v_mla_g_k, v_na_g_q, v_na_g_k, v_na_rpb):
    given = dict(locals())
    x, c, ctx, target = given['x'], given['c'], given['ctx'], given['loss_target']
    b_loc, _, d_model = x.shape
    depth = given['w_mod'].shape[0]
    ix, iy, ic = lax.axis_index('x'), lax.axis_index('y'), lax.axis_index('c')
    me = 4 * ix + 2 * iy + ic
    n_batch = N_DEV * b_loc
    mod_w = given['w_mod'].shape[2]

    c_rows = jnp.concatenate([c, jnp.zeros((8 - b_loc, d_model), F32)], axis=0)
    small = _gather_packed([c_rows] + [given[n] for n in SHARDED_SMALL], F32, PACK_SMALL, False, 'gather_small')
    c_all = small[0][:, :b_loc].reshape(n_batch, d_model)
    full = {n: _join_shards(g, SHARDED_SMALL[n]) for n, g in zip(SHARDED_SMALL, small[1:])}
    cuts = {n: (a if given[n].shape[a] % (16 if a == 1 else LANES) == 0 else None) for n, a in BIG.items()}
    big = _exchange_many([given[n].astype(BF16) for n in BIG], [cuts[n] for n in BIG], True, 'gather_weights')
    for n, g in zip(BIG, big):
        g = g if cuts[n] is not None else _join_shards(g, BIG[n])
        full[n] = [g[i] for i in range(g.shape[0])]
    c3 = MLA_Q_RANK + MLA_KV_RANK + MLA_ROPE
    full['o_w_in'] = [jnp.concatenate([w[:, :c3], jnp.zeros((w.shape[0], ODD_NA_AT - c3), BF16), w[:, c3:]], axis=-1)
                      for w in full['o_w_in']]
    for n in REPLICATED:
        full[n] = given[n]

    rows17 = 16 * (-(-(n_batch + 1) // 16))
    cond = jnp.concatenate([jax.nn.silu(c_all), jax.nn.silu(given['c_ctx'])[None],
                            jnp.zeros((rows17 - n_batch - 1, d_model), F32)], axis=0)
    mod_mine = jnp.stack([_matmul(cond, given['w_mod'][i], 'nn', F32) for i in range(depth)])
    b_mine = lax.dynamic_slice_in_dim(given['b_mod'], me * mod_w, mod_w, axis=1)
    mod_mine = mod_mine + b_mine[:, None, :]
    mod_all = _gather_packed([mod_mine], F32, PACK_SMALL, False, 'gather_mod')[0]
    mod_all = jnp.moveaxis(mod_all, 0, 2).reshape(depth, rows17, N_MOD, d_model)
    m_lat = lax.dynamic_slice_in_dim(mod_all, me * b_loc, b_loc, axis=1)
    m_ctx = mod_all[:, n_batch]

    diff = {n: full[n] for n in list(BIG) + list(SHARDED_SMALL) + REPLICATED}
    loss, (g_x, g_p, g_ml, g_mc) = jax.value_and_grad(local_loss, argnums=(0, 1, 2, 3))(x, diff, m_lat, m_ctx, ctx, target)
    loss = lax.psum(loss, ('x', 'y', 'c'))
    g_p['o_w_in'] = [jnp.concatenate([g[:, :c3], g[:, ODD_NA_AT:]], axis=-1) for g in g_p['o_w_in']]

    g_rows = jnp.concatenate([g_ml.reshape(depth, b_loc, N_MOD * d_model), g_mc.reshape(depth, 1, N_MOD * d_model),
                              jnp.zeros((depth, 8 - b_loc - 1, N_MOD * d_model), F32)], axis=1)
    g_mod_all = _gather_packed([g_rows], F32, PACK_SMALL, False, 'gather_mod_grads')[0]
    g_lat_all = jnp.moveaxis(g_mod_all[:, :, :b_loc], 0, 1).reshape(depth, n_batch, N_MOD * d_model)
    g_ctx_all = g_mod_all[0, :, b_loc]
    for dev in range(1, N_DEV):
        g_ctx_all = g_ctx_all + g_mod_all[dev, :, b_loc]
    g_mod17 = jnp.concatenate([g_lat_all, g_ctx_all[:, None], jnp.zeros((depth, rows17 - n_batch - 1, N_MOD * d_model), F32)],
                              axis=1)
    grad_b_mod = jnp.sum(g_mod17, axis=1)
    g_mod_mine = lax.dynamic_slice_in_dim(g_mod17, me * mod_w, mod_w, axis=2)
    grad_w_mod = jnp.stack([_matmul(cond, g_mod_mine[i], 'tn', F32) for i in range(depth)])
    d_cond = _matmul(g_mod_mine[0], given['w_mod'][0], 'nt', F32)
    for i in range(1, depth):
        d_cond = d_cond + _matmul(g_mod_mine[i], given['w_mod'][i], 'nt', F32)
    d_cond_ctx = d_cond[n_batch]

    small_names = REPLICATED + list(SHARDED_SMALL)
    parts = [d_cond_ctx] + [g_p[n] for n in small_names]
    packed = jnp.concatenate([_pack_rows(a.reshape(-1), PACK_SMALL) for a in parts], axis=0)
    summed = _unpack_all(_all_reduce_small(packed), [a.shape for a in parts], PACK_SMALL)
    grads = dict(zip(['c_ctx'] + small_names, summed))
    c_ctx = given['c_ctx']
    sig = jax.nn.sigmoid(c_ctx)
    grads['c_ctx'] = grads['c_ctx'] * (sig * (1 + c_ctx * (1 - sig)))
    for n, axis in SHARDED_SMALL.items():
        width = given[n].shape[axis]
        grads[n] = lax.dynamic_slice_in_dim(grads[n], me * width, width, axis=axis)
    grads['w_mod'], grads['b_mod'] = grad_w_mod, grad_b_mod

    stacked = [jnp.stack(g_p[n]) for n in BIG]
    stacked = [g if cuts[n] is not None else _split_shards(g, BIG[n]) for n, g in zip(BIG, stacked)]
    landed = _exchange_many(stacked, [cuts[n] for n in BIG], False, 'scatter_weight_grads')

    upd = {}
    for n, slots in zip(BIG, landed):
        grads[n], *upd[n] = _adamw_landed(slots, given[n], given['m_' + n], given['v_' + n])
    upd['w_mod'] = _adamw_big(given['w_mod'], grads['w_mod'], given['m_w_mod'], given['v_w_mod'])
    rest = [n for n in WEIGHTS if n not in upd]
    out = _adamw_small([given[n] for n in rest], [grads[n] for n in rest], [given['m_' + n] for n in rest],
                       [given['v_' + n] for n in rest])
    upd.update(dict(zip(rest, out)))
    return (loss, g_x, *[grads[n] for n in WEIGHTS], *[upd[n][0] for n in WEIGHTS], *[upd[n][1] for n in WEIGHTS],
            *[upd[n][2] for n in WEIGHTS])
```

```python
import functools
import math

import jax
import jax.numpy as jnp
from jax import lax
from jax.experimental import pallas as pl
from jax.experimental.pallas import tpu as pltpu

F32, BF16 = jnp.float32, jnp.bfloat16
MESH = pl.DeviceIdType.MESH
N_DEV = 8
VMEM_LIMIT_BYTES = 56 * 1024 * 1024
MM_TILE_BYTES = 6 * 1024 * 1024
LANES = 128
PACK_SMALL = (128, 8)

GRID_W = 64
HEAD_DIM = 64
ROPE_BASE = 10000.0
EPS = 1e-6
N_MOD = 6
GQA_Q_HEADS, GQA_KV_HEADS = 12, 4
GQA_Q_W, GQA_KV_W = GQA_Q_HEADS * HEAD_DIM, GQA_KV_HEADS * HEAD_DIM
SSM_WIDTH, SSM_GROUP, SSM_GROUPS, SSM_STATE = 256, 16, 16, 64
MLA_HEADS, MLA_Q_RANK, MLA_KV_RANK, MLA_NOPE, MLA_ROPE, MLA_V = 8, 512, 256, 64, 32, 64
MLA_QK = MLA_NOPE + MLA_ROPE
NA_HEADS, NA_WIN_R, NA_WIN_C = 8, 8, 16
NA_W = NA_HEADS * HEAD_DIM
ODD_IN_W = MLA_Q_RANK + MLA_KV_RANK + MLA_ROPE + 3 * NA_W
ODD_NA_AT = 1024
ODD_IN_PAD = ODD_NA_AT + 3 * NA_W
NEG = -1e30

ADAM_LR, ADAM_B1, ADAM_B2, ADAM_EPS, ADAM_WD, ADAM_STEP = 0.001, 0.9, 0.999, 1e-08, 0.01, 10

FWD_PARAMS = ['x', 'c', 'ctx', 'c_ctx', 'w_mod', 'b_mod', 'g_norm1', 'g_norm2', 'w_ff1', 'w_ff2', 'e_w_in', 'e_w_out',
              'e_g_q', 'e_g_k', 'ssm_lam_re', 'ssm_lam_im', 'ssm_log_dt', 'ssm_b_re', 'ssm_b_im', 'ssm_c_re', 'ssm_c_im',
              'ssm_d', 'ssm_w_glu', 'ssm_b_glu', 'o_w_in', 'o_w_out', 'mla_g_cq', 'mla_g_ckv', 'mla_w_uq', 'mla_w_ukv',
              'mla_g_q', 'mla_g_k', 'na_g_q', 'na_g_k', 'na_rpb']
WEIGHTS = FWD_PARAMS[3:]
BIG = {'w_ff1': 2, 'w_ff2': 1, 'e_w_in': 2, 'e_w_out': 1, 'o_w_in': 2, 'o_w_out': 1, 'mla_w_uq': 2, 'mla_w_ukv': 2,
       'ssm_w_glu': 1}
SHARDED_SMALL = {'mla_g_cq': 1, 'mla_g_ckv': 1}
REPLICATED = [n for n in WEIGHTS if n not in BIG and n not in SHARDED_SMALL and n not in ('w_mod', 'c_ctx', 'b_mod')]


def _tile(dim, prefs):
    for p in prefs:
        if dim >= p and dim % p == 0:
            return p
    return dim


def _params(*sem):
    return pltpu.CompilerParams(dimension_semantics=sem, vmem_limit_bytes=VMEM_LIMIT_BYTES)


def _dot_nt(a, b):
    return lax.dot_general(a, b, (((1,), (1,)), ((), ())), preferred_element_type=F32)


def _dot_tn(a, b):
    return lax.dot_general(a, b, (((0,), (0,)), ((), ())), preferred_element_type=F32)


def _dot(a, b):
    return jnp.dot(a, b, preferred_element_type=F32)


def _matmul(a, b, kind, out_dtype, finish=None, extra=None, n_out=1):
    a, b = a.astype(BF16), b.astype(BF16)
    if kind == 'nn':
        (m, kd), n = a.shape, b.shape[1]
    elif kind == 'nt':
        (m, kd), n = a.shape, b.shape[0]
    else:
        (kd, m), n = a.shape, b.shape[1]
    if kind == 'tn':
        tm = m if m <= 1024 else _tile(m, (1024, 768, 512, 256, 128))
        tn = _tile(n, (1024, 768, 512, 256, 128))
    else:
        tn = n if kd * n * 2 <= MM_TILE_BYTES else _tile(n, (1024, 768, 512, 256, 128))
        tm = _tile(m, [t for t in (1536, 1024, 768, 512, 256, 128) if t * tn * 4 <= MM_TILE_BYTES])
    whole = kind != 'tn' and tn == n and kd * n * 2 <= MM_TILE_BYTES
    tk = kd if whole else _tile(kd, [t for t in (2048, 1536, 1024, 512, 256, 128) if t * max(tm, tn) * 4 <= MM_TILE_BYTES])
    nk = kd // tk
    dn = {'nn': (((1,), (0,)), ((), ())), 'nt': (((1,), (1,)), ((), ())), 'tn': (((0,), (0,)), ((), ()))}[kind]

    n_in = 2 if extra is None else 3

    def body(*refs):
        a_ref, b_ref = refs[:2]
        o_refs = refs[n_in:n_in + n_out]

        def store(total):
            outs = (total,) if finish is None else finish(total, refs[2][...] if extra is not None else None)
            for o_ref, val in zip(o_refs, outs):
                o_ref[...] = val.astype(o_ref.dtype)

        part = lax.dot_general(a_ref[...], b_ref[...], dn, preferred_element_type=F32)
        if nk == 1:
            store(part)
            return
        acc_ref, k = refs[n_in + n_out], pl.program_id(2)

        @pl.when(k == 0)
        def _():
            acc_ref[...] = part

        @pl.when((k > 0) & (k < nk - 1))
        def _():
            acc_ref[...] += part

        @pl.when(k == nk - 1)
        def _():
            store(acc_ref[...] + part)

    a_spec = pl.BlockSpec((tk, tm), lambda i, j, k: (k, i)) if kind == 'tn' else pl.BlockSpec((tm, tk), lambda i, j, k: (i, k))
    b_spec = pl.BlockSpec((tn, tk), lambda i, j, k: (j, k)) if kind == 'nt' else pl.BlockSpec((tk, tn), lambda i, j, k: (k, j))
    o_spec = pl.BlockSpec((tm, tn), lambda i, j, k: (i, j))
    out = pl.pallas_call(
        body, name='mm_' + kind, grid=(m // tm, n // tn, nk),
        out_shape=[jax.ShapeDtypeStruct((m, n), out_dtype)] * n_out,
        in_specs=[a_spec, b_spec] + ([o_spec] if extra is not None else []), out_specs=[o_spec] * n_out,
        scratch_shapes=[pltpu.VMEM((tm, tn), F32)] if nk > 1 else [],
        compiler_params=_params('parallel', 'parallel', 'arbitrary'),
    )(*((a, b) if extra is None else (a, b, extra)))
    return out[0] if n_out == 1 else out


@jax.custom_vjp
def linear(a, w):
    return _matmul(a, w, 'nn', F32)


def _linear_fwd(a, w):
    ab = a.astype(BF16)
    return _matmul(ab, w, 'nn', F32), (ab, w)


def _linear_bwd(res, g):
    ab, w = res
    gb = g.astype(BF16)
    return _matmul(gb, w, 'nt', F32), _matmul(ab, gb, 'tn', w.dtype)


linear.defvjp(_linear_fwd, _linear_bwd)


def _relu2(z, _):
    r = jnp.maximum(z, 0.0)
    return r, r * r


def _relu2_grad(d_act, r):
    return (d_act * (2.0 * r.astype(F32)),)


@jax.custom_vjp
def mlp_rows(h, w1, w2):
    return _mlp_rows_fwd(h, w1, w2)[0]


def _mlp_rows_fwd(h, w1, w2):
    hb = h.astype(BF16)
    r, act = _matmul(hb, w1, 'nn', BF16, finish=_relu2, n_out=2)
    return _matmul(act, w2, 'nn', F32), (hb, w1, w2, r, act)


def _mlp_rows_bwd(res, g):
    hb, w1, w2, r, act = res
    gb = g.astype(BF16)
    dz = _matmul(gb, w2, 'nt', BF16, finish=_relu2_grad, extra=r)
    return _matmul(dz, w1, 'nt', F32), _matmul(hb, dz, 'tn', w1.dtype), _matmul(act, gb, 'tn', w2.dtype)


mlp_rows.defvjp(_mlp_rows_fwd, _mlp_rows_bwd)


LOG2E = math.log2(math.e)


def _softmax_rows(t):
    m = jnp.max(t, axis=-1, keepdims=True)
    e = jnp.exp2(t - m)
    return e * (1.0 / jnp.sum(e, axis=-1, keepdims=True))


ATTN_SPLIT = 2


def _attn_specs(q, k, v, bq):
    _, h, nq, dq = q.shape
    _, hk, nk, dv = v.shape
    g = h // hk
    q_spec = pl.BlockSpec((None, None, bq, dq), lambda b, j, gi, i: (b, j * g + gi, i, 0))
    k_spec = pl.BlockSpec((None, None, nk, dq), lambda b, j, gi, i: (b, j, 0, 0))
    v_spec = pl.BlockSpec((None, None, nk, dv), lambda b, j, gi, i: (b, j, 0, 0))
    o_spec = pl.BlockSpec((None, None, bq, dv), lambda b, j, gi, i: (b, j * g + gi, i, 0))
    t = dict(q=pl.BlockSpec((None, None, dq, bq), lambda b, j, gi, i: (b, j * g + gi, 0, i)),
             o=pl.BlockSpec((None, None, dv, bq), lambda b, j, gi, i: (b, j * g + gi, 0, i)),
             k=pl.BlockSpec((None, None, dq, nk), lambda b, j, gi, i: (b, j, 0, 0)),
             v=pl.BlockSpec((None, None, dv, nk), lambda b, j, gi, i: (b, j, 0, 0)))
    return (q.shape[0], hk, g, nq // bq), q_spec, k_spec, v_spec, o_spec, t


def _attn_blocks(nq):
    bq = _tile(nq, (512, 256, 128))
    return bq, [pl.ds(s * (bq // ATTN_SPLIT), bq // ATTN_SPLIT) for s in range(ATTN_SPLIT)]


def _attn_fwd_call(q, k, vt, scale):
    b, h, nq, _ = q.shape
    dv = vt.shape[2]
    bq, subs = _attn_blocks(nq)
    grid, q_spec, k_spec, _, _, t = _attn_specs(q, k, jnp.swapaxes(vt, 2, 3), bq)

    def body(q_ref, k_ref, vt_ref, ot_ref):
        kb, vtb = k_ref[...], vt_ref[...]
        for rows in subs:
            p = _softmax_rows(_dot_nt(q_ref[rows, :], kb) * (scale * LOG2E))
            ot_ref[:, rows] = _dot_nt(vtb, p.astype(BF16))

    return pl.pallas_call(
        body, name='attn_fwd', grid=grid, out_shape=jax.ShapeDtypeStruct((b, h, dv, nq), F32),
        in_specs=[q_spec, k_spec, t['v']], out_specs=t['o'],
        compiler_params=_params('parallel', 'parallel', 'arbitrary', 'arbitrary'),
    )(q, k, vt)


def _attn_bwd_call(q, k, kt, v, do, scale):
    b, h, nq, dq = q.shape
    bq, subs = _attn_blocks(nq)
    grid, q_spec, k_spec, v_spec, o_spec, t = _attn_specs(q, k, v, bq)

    def body(q_ref, k_ref, kt_ref, v_ref, do_ref, dqt_ref, dk_ref, dv_ref):
        @pl.when((pl.program_id(2) == 0) & (pl.program_id(3) == 0))
        def _():
            dk_ref[...] = jnp.zeros_like(dk_ref)
            dv_ref[...] = jnp.zeros_like(dv_ref)

        kb, ktb, vb = k_ref[...], kt_ref[...], v_ref[...]
        dk, dv = [], []
        for rows in subs:
            qb, dob = q_ref[rows, :], do_ref[rows, :]
            p = _softmax_rows(_dot_nt(qb, kb) * (scale * LOG2E))
            dp = _dot_nt(dob, vb)
            ds = p * (dp - jnp.sum(p * dp, axis=-1, keepdims=True))
            dsb = (ds * scale).astype(BF16)
            dqt_ref[:, rows] = _dot_nt(ktb, dsb)
            dk.append(_dot_tn(dsb, qb))
            dv.append(_dot_tn(p.astype(BF16), dob))
        dk_ref[...] += sum(dk[1:], dk[0])
        dv_ref[...] += sum(dv[1:], dv[0])

    return pl.pallas_call(
        body, name='attn_bwd', grid=grid,
        out_shape=(jax.ShapeDtypeStruct((b, h, dq, nq), F32), jax.ShapeDtypeStruct(k.shape, F32), jax.ShapeDtypeStruct(v.shape, F32)),
        in_specs=[q_spec, k_spec, t['k'], v_spec, o_spec], out_specs=(t['q'], k_spec, v_spec),
        compiler_params=_params('parallel', 'parallel', 'arbitrary', 'arbitrary'),
    )(q, k, kt, v, do)


@functools.partial(jax.custom_vjp, nondiff_argnums=(3,))
def attention(q, k, v, scale):
    return _attention_fwd(q, k, v, scale)[0]


def _attention_fwd(q, k, v, scale):
    qb, kb, vb = q.astype(BF16), k.astype(BF16), v.astype(BF16)
    return jnp.swapaxes(_attn_fwd_call(qb, kb, jnp.swapaxes(vb, 2, 3), scale), 2, 3), (qb, kb, vb)


def _attention_bwd(scale, res, g):
    qb, kb, vb = res
    dqt, dk, dv = _attn_bwd_call(qb, kb, jnp.swapaxes(kb, 2, 3), vb, g.astype(BF16), scale)
    return jnp.swapaxes(dqt, 2, 3), dk, dv


attention.defvjp(_attention_fwd, _attention_bwd)


def _na_window(r, rows):
    start = jnp.clip(r - NA_WIN_R // 2, 0, rows - NA_WIN_R)
    return start, r - start


def _na_scores(q, kw, kc, bias, scale):
    s1 = _dot_nt(q, kw) * scale + bias
    s2 = _dot_nt(q, kc) * scale
    m = jnp.maximum(jnp.max(s1, axis=-1, keepdims=True), jnp.max(s2, axis=-1, keepdims=True))
    e1, e2 = jnp.exp(s1 - m), jnp.exp(s2 - m)
    inv = 1.0 / (jnp.sum(e1, axis=-1, keepdims=True) + jnp.sum(e2, axis=-1, keepdims=True))
    return e1 * inv, e2 * inv


def _na_specs(q, kc):
    _, _, n, d = q.shape
    c = kc.shape[2]
    win = NA_WIN_R * GRID_W
    tok = pl.BlockSpec((None, None, n, d), lambda b, h: (b, h, 0, 0))
    ctx = pl.BlockSpec((None, None, c, d), lambda b, h: (b, h, 0, 0))
    bias = pl.BlockSpec((None, NA_WIN_R, GRID_W, win), lambda b, h: (h, 0, 0, 0))
    dbias = pl.BlockSpec((None, None, NA_WIN_R, GRID_W, win), lambda b, h: (b, h, 0, 0, 0))
    return tok, ctx, bias, dbias


def _na_fwd_call(q, k, v, kc, vc, bias, scale):
    b, h, n, d = q.shape
    rows, win = n // GRID_W, NA_WIN_R * GRID_W
    tok, ctx, bias_spec, _ = _na_specs(q, kc)

    def body(q_ref, k_ref, v_ref, kc_ref, vc_ref, b_ref, o_ref):
        def row(r, carry):
            start, off = _na_window(r, rows)
            at = pl.ds(pl.multiple_of(r * GRID_W, GRID_W), GRID_W)
            wat = pl.ds(pl.multiple_of(start * GRID_W, GRID_W), win)
            p1, p2 = _na_scores(q_ref[at, :], k_ref[wat, :], kc_ref[...], b_ref[off], scale)
            o_ref[at, :] = _dot(p1.astype(BF16), v_ref[wat, :]) + _dot(p2.astype(BF16), vc_ref[...])
            return carry

        lax.fori_loop(0, rows, row, 0, unroll=4)

    return pl.pallas_call(
        body, name='na_fwd', grid=(b, h), out_shape=jax.ShapeDtypeStruct(q.shape, F32),
        in_specs=[tok, tok, tok, ctx, ctx, bias_spec], out_specs=tok,
        compiler_params=_params('parallel', 'parallel'),
    )(q, k, v, kc, vc, bias)


def _na_bwd_call(q, k, v, kc, vc, bias, do, scale):
    b, h, n, d = q.shape
    rows, win = n // GRID_W, NA_WIN_R * GRID_W
    tok, ctx, bias_spec, dbias_spec = _na_specs(q, kc)

    def body(q_ref, k_ref, v_ref, kc_ref, vc_ref, b_ref, do_ref, dq_ref, dk_ref, dv_ref, dkc_ref, dvc_ref, db_ref):
        for ref in (dk_ref, dv_ref, dkc_ref, dvc_ref, db_ref):
            ref[...] = jnp.zeros_like(ref)

        def row(r, carry):
            start, off = _na_window(r, rows)
            at = pl.ds(pl.multiple_of(r * GRID_W, GRID_W), GRID_W)
            wat = pl.ds(pl.multiple_of(start * GRID_W, GRID_W), win)
            qb, kw, vw, dob = q_ref[at, :], k_ref[wat, :], v_ref[wat, :], do_ref[at, :]
            kcb, vcb = kc_ref[...], vc_ref[...]
            p1, p2 = _na_scores(qb, kw, kcb, b_ref[off], scale)
            dp1, dp2 = _dot_nt(dob, vw), _dot_nt(dob, vcb)
            delta = jnp.sum(p1 * dp1, axis=-1, keepdims=True) + jnp.sum(p2 * dp2, axis=-1, keepdims=True)
            ds1, ds2 = p1 * (dp1 - delta), p2 * (dp2 - delta)
            db_ref[off] += ds1
            ds1b, ds2b = (ds1 * scale).astype(BF16), (ds2 * scale).astype(BF16)
            dq_ref[at, :] = _dot(ds1b, kw) + _dot(ds2b, kcb)
            dk_ref[wat, :] += _dot_tn(ds1b, qb)
            dv_ref[wat, :] += _dot_tn(p1.astype(BF16), dob)
            dkc_ref[...] += _dot_tn(ds2b, qb)
            dvc_ref[...] += _dot_tn(p2.astype(BF16), dob)
            return carry

        lax.fori_loop(0, rows, row, 0, unroll=2)

    f = lambda a: jax.ShapeDtypeStruct(a.shape, F32)
    return pl.pallas_call(
        body, name='na_bwd', grid=(b, h),
        out_shape=(f(q), f(k), f(v), f(kc), f(vc), jax.ShapeDtypeStruct((b,) + bias.shape, F32)),
        in_specs=[tok, tok, tok, ctx, ctx, bias_spec, tok], out_specs=(tok, tok, tok, ctx, ctx, dbias_spec),
        compiler_params=_params('parallel', 'parallel'),
    )(q, k, v, kc, vc, bias, do)


@functools.partial(jax.custom_vjp, nondiff_argnums=(6,))
def na_attention(q, k, v, kc, vc, bias, scale):
    return _na_fwd_call(q.astype(BF16), k.astype(BF16), v.astype(BF16), kc.astype(BF16), vc.astype(BF16), bias, scale)


def _na_attention_fwd(q, k, v, kc, vc, bias, scale):
    res = (q.astype(BF16), k.astype(BF16), v.astype(BF16), kc.astype(BF16), vc.astype(BF16), bias)
    return _na_fwd_call(*res, scale), res


def _na_attention_bwd(scale, res, g):
    dq, dk, dv, dkc, dvc, db = _na_bwd_call(*res, g.astype(BF16), scale)
    return dq, dk, dv, dkc, dvc, jnp.sum(db, axis=0)


na_attention.defvjp(_na_attention_fwd, _na_attention_bwd)


def _na_table_index():
    qcol = jnp.arange(GRID_W)
    kcol = jnp.arange(GRID_W)
    cstart = jnp.clip(qcol - NA_WIN_C // 2, 0, GRID_W - NA_WIN_C)
    inside = (kcol[None, :] >= cstart[:, None]) & (kcol[None, :] < cstart[:, None] + NA_WIN_C)
    cidx = jnp.clip(kcol[None, :] - qcol[:, None] + (NA_WIN_C - 1), 0, 2 * NA_WIN_C - 2)
    ridx = jnp.arange(NA_WIN_R)[None, :] - jnp.arange(NA_WIN_R)[:, None] + (NA_WIN_R - 1)
    return inside, cidx, ridx


@jax.custom_vjp
def na_bias_table(rpb):
    inside, pick_c, pick_r = _na_table_picks()
    rows = jnp.einsum('hab,oja->hojb', rpb, pick_r, precision=lax.Precision.HIGHEST)
    t = jnp.einsum('hojb,qkb->hoqjk', rows, pick_c, precision=lax.Precision.HIGHEST)
    t = jnp.where(inside[None, None, :, None, :], t, NEG)
    return t.reshape(rpb.shape[0], NA_WIN_R, GRID_W, NA_WIN_R * GRID_W)


def _na_table_picks():
    inside, cidx, ridx = _na_table_index()
    pick_c = ((cidx[..., None] == jnp.arange(2 * NA_WIN_C - 1)) & inside[..., None]).astype(F32)
    pick_r = (ridx[..., None] == jnp.arange(2 * NA_WIN_R - 1)).astype(F32)
    return inside, pick_c, pick_r


def _na_bias_table_bwd(_, dt):
    _, pick_c, pick_r = _na_table_picks()
    d5 = dt.reshape(dt.shape[0], NA_WIN_R, GRID_W, NA_WIN_R, GRID_W)
    part = jnp.einsum('hoqjk,qkb->hojb', d5, pick_c, precision=lax.Precision.HIGHEST)
    return (jnp.einsum('hojb,oja->hab', part, pick_r, precision=lax.Precision.HIGHEST),)


na_bias_table.defvjp(lambda rpb: (na_bias_table(rpb), None), _na_bias_table_bwd)


S5_HALF = SSM_WIDTH // 2
S5_LANES = (SSM_GROUPS // 2) * SSM_STATE
S5_Q = S5_LANES // LANES


def _s5_tiles(a):
    r = a.shape[0]
    return jnp.transpose(a.reshape(r, S5_Q, LANES), (1, 0, 2)).reshape(S5_Q * r, LANES)


def _s5_untiles(a):
    r = a.shape[0] // S5_Q
    return jnp.transpose(a.reshape(S5_Q, r, LANES), (1, 0, 2)).reshape(r, S5_LANES)


def _s5_put(ref, r, rr, tc, val):
    for q in range(S5_Q):
        ref[pl.ds((q * rr + r) * tc, tc), :] = val[:, q * LANES:(q + 1) * LANES]


def _s5_get(ref, r, rr, tc):
    return jnp.concatenate([ref[pl.ds((q * rr + r) * tc, tc), :] for q in range(S5_Q)], axis=1)


S5_BLOCK = 8


def _s5_powers(a_re, a_im, backward):
    a_im = -a_im if backward else a_im
    pr, pi = [a_re], [a_im]
    for _ in range(S5_BLOCK - 1):
        pr, pi = pr + [pr[-1] * a_re - pi[-1] * a_im], pi + [pr[-1] * a_im + pi[-1] * a_re]
    order = range(S5_BLOCK - 1, -1, -1) if backward else range(S5_BLOCK)

    def table(p):
        rows = [jnp.broadcast_to(p[s - 1][:, None, :], (p[0].shape[0], S5_BLOCK, LANES)) for s in (1, 2, 4)]
        return jnp.stack(rows + [jnp.stack([p[t] for t in order], axis=1)], axis=1)

    return table(pr), table(pi)


def _s5_scan_block(xr, xi, pr_ref, pi_ref, chain, carry, backward):
    row = lax.broadcasted_iota(jnp.int32, (S5_BLOCK, LANES), 0)
    for e, s in enumerate((1, 2, 4)):
        ar, ai = pr_ref[chain, e], pi_ref[chain, e]
        keep = (row < S5_BLOCK - s) if backward else (row >= s)
        shift = S5_BLOCK - s if backward else s
        sr = jnp.where(keep, pltpu.roll(xr, shift, 0), 0.0)
        si = jnp.where(keep, pltpu.roll(xi, shift, 0), 0.0)
        xr, xi = xr + (ar * sr - ai * si), xi + (ar * si + ai * sr)
    ar, ai = pr_ref[chain, 3], pi_ref[chain, 3]
    cr, ci = carry
    xr, xi = xr + (ar * cr - ai * ci), xi + (ar * ci + ai * cr)
    edge = slice(0, 1) if backward else slice(S5_BLOCK - 1, S5_BLOCK)
    return xr, xi, (xr[edge], xi[edge])


def _s5_scan_chunk(xr_ref, xi_ref, pr_ref, pi_ref, sr_ref, si_ref, chains, tc, backward):
    blocks = tc // S5_BLOCK
    for chain in range(chains):
        def block(k, carry, chain=chain):
            j = blocks - 1 - k if backward else k
            at = pl.ds(pl.multiple_of(chain * tc + j * S5_BLOCK, S5_BLOCK), S5_BLOCK)
            xr, xi, carry = _s5_scan_block(xr_ref[at, :], xi_ref[at, :], pr_ref, pi_ref, chain, carry, backward)
            xr_ref[at, :] = xr
            xi_ref[at, :] = xi
            return carry

        one = pl.ds(chain, 1)
        cr, ci = lax.fori_loop(0, blocks, block, (sr_ref[one, :], si_ref[one, :]), unroll=4)
        sr_ref[one, :] = cr
        si_ref[one, :] = ci


def _s5_fwd_call(u, a_re, a_im, b_re, b_im, c_re, c_im):
    rr, t_len, _ = u.shape
    sets = b_re.shape[0]
    per = rr // sets
    tc = _tile(t_len, (256, 128))
    nt = t_len // tc
    qr = S5_Q * rr

    def body(u_ref, ar_ref, ai_ref, br_ref, bi_ref, cr_ref, ci_ref, y_ref, hr_ref, hi_ref, sr_ref, si_ref):
        @pl.when(pl.program_id(0) == 0)
        def _():
            sr_ref[...] = jnp.zeros_like(sr_ref)
            si_ref[...] = jnp.zeros_like(si_ref)

        for r in range(rr):
            ub = u_ref[r]
            _s5_put(hr_ref, r, rr, tc, _dot(ub, br_ref[r // per]))
            _s5_put(hi_ref, r, rr, tc, _dot(ub, bi_ref[r // per]))
        _s5_scan_chunk(hr_ref, hi_ref, ar_ref, ai_ref, sr_ref, si_ref, qr, tc, False)
        for r in range(rr):
            y_ref[r] = (_dot(_s5_get(hr_ref, r, rr, tc).astype(BF16), cr_ref[r // per])
                        - _dot(_s5_get(hi_ref, r, rr, tc).astype(BF16), ci_ref[r // per]))

    full = lambda a: pl.BlockSpec(a.shape, lambda i: (0,) * a.ndim)
    h_spec = pl.BlockSpec((None, qr * tc, LANES), lambda i: (i, 0, 0))
    h_shape = jax.ShapeDtypeStruct((nt, qr * tc, LANES), F32)
    a_re, a_im = _s5_powers(_s5_tiles(a_re), _s5_tiles(a_im), False)
    return pl.pallas_call(
        body, name='s5_fwd', grid=(nt,),
        out_shape=(jax.ShapeDtypeStruct((rr, t_len, S5_HALF), F32), h_shape, h_shape),
        in_specs=[pl.BlockSpec((rr, tc, S5_HALF), lambda i: (0, i, 0)), full(a_re), full(a_im), full(b_re), full(b_im),
                  full(c_re), full(c_im)],
        out_specs=(pl.BlockSpec((rr, tc, S5_HALF), lambda i: (0, i, 0)), h_spec, h_spec),
        scratch_shapes=[pltpu.VMEM((qr, LANES), F32), pltpu.VMEM((qr, LANES), F32)],
        compiler_params=_params('arbitrary'),
    )(u, a_re, a_im, b_re, b_im, c_re, c_im)


def _s5_bwd_call(u, a_re, a_im, b_re, b_im, c_re, c_im, h_re, h_im, dy):
    rr, t_len, _ = u.shape
    sets = b_re.shape[0]
    per = rr // sets
    nt, rows, _ = h_re.shape
    qr = S5_Q * rr
    tc = rows // qr

    def body(u_ref, dy_ref, ar_ref, ai_ref, br_ref, bi_ref, cr_ref, ci_ref, hr_ref, hi_ref,
             du_ref, dar_ref, dai_ref, dbr_ref, dbi_ref, dcr_ref, dci_ref, gr_ref, gi_ref, sr_ref, si_ref):
        i = pl.program_id(0)

        @pl.when(i == 0)
        def _():
            for ref in (dar_ref, dai_ref, dbr_ref, dbi_ref, dcr_ref, dci_ref, sr_ref, si_ref):
                ref[...] = jnp.zeros_like(ref)

        for r in range(rr):
            dyb = dy_ref[r]
            _s5_put(gr_ref, r, rr, tc, _dot_nt(dyb, cr_ref[r // per]))
            _s5_put(gi_ref, r, rr, tc, -_dot_nt(dyb, ci_ref[r // per]))
        g_r, g_i = sr_ref[...], si_ref[...]
        last = pl.ds(tc - 1, qr, stride=tc)
        dar_ref[...] += g_r * hr_ref[last, :] + g_i * hi_ref[last, :]
        dai_ref[...] += g_i * hr_ref[last, :] - g_r * hi_ref[last, :]
        _s5_scan_chunk(gr_ref, gi_ref, ar_ref, ai_ref, sr_ref, si_ref, qr, tc, True)
        row = lax.broadcasted_iota(jnp.int32, (tc, LANES), 0)
        for chain in range(qr):
            at, one = pl.ds(chain * tc, tc), pl.ds(chain, 1)
            p_r = jnp.where(row >= 1, pltpu.roll(hr_ref[at, :], 1, 0), 0.0)
            p_i = jnp.where(row >= 1, pltpu.roll(hi_ref[at, :], 1, 0), 0.0)
            g_r, g_i = gr_ref[at, :], gi_ref[at, :]
            dar_ref[one, :] += jnp.sum(g_r * p_r + g_i * p_i, axis=0, keepdims=True)
            dai_ref[one, :] += jnp.sum(g_i * p_r - g_r * p_i, axis=0, keepdims=True)
        for r in range(rr):
            s = r // per
            ub, dyb = u_ref[r], dy_ref[r]
            grb, gib = _s5_get(gr_ref, r, rr, tc).astype(BF16), _s5_get(gi_ref, r, rr, tc).astype(BF16)
            du_ref[r] = _dot_nt(grb, br_ref[s]) + _dot_nt(gib, bi_ref[s])
            dbr_ref[s] += _dot_tn(ub, grb)
            dbi_ref[s] += _dot_tn(ub, gib)
            dcr_ref[s] += _dot_tn(_s5_get(hr_ref, r, rr, tc).astype(BF16), dyb)
            dci_ref[s] -= _dot_tn(_s5_get(hi_ref, r, rr, tc).astype(BF16), dyb)

    full = lambda a: pl.BlockSpec(a.shape, lambda i: (0,) * a.ndim)
    back = lambda i: nt - 1 - i
    tok = pl.BlockSpec((rr, tc, S5_HALF), lambda i: (0, back(i), 0))
    h_spec = pl.BlockSpec((None, qr * tc, LANES), lambda i: (back(i), 0, 0))
    f = lambda a: jax.ShapeDtypeStruct(a.shape, F32)
    a_re, a_im = _s5_powers(_s5_tiles(a_re), _s5_tiles(a_im), True)
    da = jax.ShapeDtypeStruct((qr, LANES), F32)
    du, da_re, da_im, db_re, db_im, dc_re, dc_im = pl.pallas_call(
        body, name='s5_bwd', grid=(nt,),
        out_shape=(jax.ShapeDtypeStruct(u.shape, F32), da, da, f(b_re), f(b_im), f(c_re), f(c_im)),
        in_specs=[tok, tok, full(a_re), full(a_im), full(b_re), full(b_im), full(c_re), full(c_im), h_spec, h_spec],
        out_specs=(tok, full(da), full(da), full(b_re), full(b_im), full(c_re), full(c_im)),
        scratch_shapes=[pltpu.VMEM((qr * tc, LANES), F32), pltpu.VMEM((qr * tc, LANES), F32),
                        pltpu.VMEM((qr, LANES), F32), pltpu.VMEM((qr, LANES), F32)],
        compiler_params=_params('arbitrary'),
    )(u, dy, a_re, a_im, b_re, b_im, c_re, c_im, h_re, h_im)
    return du, _s5_untiles(da_re), _s5_untiles(da_im), db_re, db_im, dc_re, dc_im


@jax.custom_vjp
def s5_core(u, a_re, a_im, b_re, b_im, c_re, c_im):
    return _s5_fwd_call(u.astype(BF16), a_re, a_im, b_re.astype(BF16), b_im.astype(BF16), c_re.astype(BF16),
                        c_im.astype(BF16))[0]


def _s5_core_fwd(u, a_re, a_im, b_re, b_im, c_re, c_im):
    args = (u.astype(BF16), a_re, a_im, b_re.astype(BF16), b_im.astype(BF16), c_re.astype(BF16), c_im.astype(BF16))
    y, h_re, h_im = _s5_fwd_call(*args)
    return y, args + (h_re, h_im)


def _s5_core_bwd(res, g):
    return _s5_bwd_call(*res, g.astype(BF16))


s5_core.defvjp(_s5_core_fwd, _s5_core_bwd)


def _shard_view(ref, axis, index, width):
    return ref.at[(slice(None),) * axis + (pl.ds(pl.multiple_of(index * width, width), width),)]


def _exchange_many(xs, cuts, gather, name):
    n = len(xs)
    if gather:
        shards = [x.shape for x in xs]
    else:
        shards = [x.shape[1:] if cut is None else x.shape[:cut] + (x.shape[cut] // N_DEV,) + x.shape[cut + 1:]
                  for x, cut in zip(xs, cuts)]

    def full_shape(shard, cut):
        return shard[:cut] + (N_DEV * shard[cut],) + shard[cut + 1:]

    out_shapes = [jax.ShapeDtypeStruct((N_DEV,) + tuple(s) if (cut is None or not gather) else full_shape(tuple(s), cut), x.dtype)
                  for x, s, cut in zip(xs, shards, cuts)]

    def body(*refs):
        x_refs, out_refs = refs[:n], refs[n:2 * n]
        send_sems, recv_sems, local_sems = refs[2 * n:]
        ix, iy, ic = lax.axis_index('x'), lax.axis_index('y'), lax.axis_index('c')
        me = 4 * ix + 2 * iy + ic

        def flipped(k):
            px = 1 - ix if k & 4 else ix
            py = 1 - iy if k & 2 else iy
            pc = 1 - ic if k & 1 else ic
            return (px, py, pc), 4 * px + 2 * py + pc

        def block(ref, cut, shard, who):
            return ref.at[who] if cut is None else _shard_view(ref, cut, who, shard[cut])

        def ends(i, sender, receiver):
            if gather:
                return x_refs[i], block(out_refs[i], cuts[i], shards[i], sender)
            return block(x_refs[i], cuts[i], shards[i], receiver), out_refs[i].at[sender]

        def copy(i, k, sender, receiver):
            src, dst = ends(i, sender, receiver)
            return pltpu.make_async_remote_copy(src_ref=src, dst_ref=dst, send_sem=send_sems.at[i * (N_DEV - 1) + k - 1],
                                                 recv_sem=recv_sems.at[i * (N_DEV - 1) + k - 1], device_id=flipped(k)[0],
                                                 device_id_type=MESH)

        own = [pltpu.make_async_copy(*ends(i, me, me), local_sems.at[i]) for i in range(n)]
        for cp in own:
            cp.start()
        if gather:
            chips = (2, 4, 6)
            sent = [copy(i, k, me, flipped(k)[1]) for k in (1,) + chips for i in range(n)]
            for cp in sent:
                cp.start()
            for k in chips:
                for i in range(n):
                    copy(i, k, flipped(k)[1], me).wait_recv()
                    src, dst = ends(i, flipped(k)[1], me)
                    sent.append(pltpu.make_async_remote_copy(
                        src_ref=dst, dst_ref=dst, send_sem=send_sems.at[i * (N_DEV - 1) + k], recv_sem=recv_sems.at[i * (N_DEV - 1) + k],
                        device_id=flipped(1)[0], device_id_type=MESH))
                    sent[-1].start()
            for k in (1, 3, 5, 7):
                for i in range(n):
                    src, dst = ends(i, flipped(k)[1], me)
                    pltpu.make_async_remote_copy(
                        src_ref=dst, dst_ref=dst, send_sem=send_sems.at[i * (N_DEV - 1) + k - 1], recv_sem=recv_sems.at[i * (N_DEV - 1) + k - 1],
                        device_id=flipped(1)[0], device_id_type=MESH).wait_recv()
            for cp in sent:
                cp.wait_send()
            for cp in own:
                cp.wait()
            return
        sent = [copy(i, k, me, flipped(k)[1]) for k in range(1, N_DEV) for i in range(n)]
        for cp in sent:
            cp.start()
        for k in range(1, N_DEV):
            for i in range(n):
                copy(i, k, flipped(k)[1], me).wait_recv()
        for cp in sent:
            cp.wait_send()
        for cp in own:
            cp.wait()

    hbm = pl.BlockSpec(memory_space=pltpu.HBM)
    pairs = n * (N_DEV - 1)
    return pl.pallas_call(
        body, name=name, out_shape=out_shapes, in_specs=[hbm] * n, out_specs=[hbm] * n,
        scratch_shapes=[pltpu.SemaphoreType.DMA((pairs,)), pltpu.SemaphoreType.DMA((pairs,)), pltpu.SemaphoreType.DMA((n,))],
    )(*xs)


def _adamw_landed(landed, w, m, v):
    shape = w.shape
    slots = landed.shape[0]
    w, m, v = (_as_rows(a) for a in (w, m, v))
    rows, cols = w.shape
    landed = landed.reshape(slots, rows, cols)
    tr = _tile(rows, (256, 128, 64, 32, 16))

    def body(l_ref, w_ref, m_ref, v_ref, g_ref, d_ref, nm_ref, nv_ref):
        g = l_ref[0].astype(F32)
        for d in range(1, slots):
            g = g + l_ref[d].astype(F32)
        g_ref[...] = g
        d_ref[...], nm_ref[...], nv_ref[...] = _adamw_math(w_ref[...], g, m_ref[...], v_ref[...])

    spec = pl.BlockSpec((tr, cols), lambda i: (i, 0))
    out = pl.pallas_call(
        body, name='adamw_landed', grid=(rows // tr,), out_shape=(jax.ShapeDtypeStruct(w.shape, F32),) * 4,
        in_specs=[pl.BlockSpec((slots, tr, cols), lambda i: (0, i, 0))] + [spec] * 3, out_specs=(spec,) * 4,
        compiler_params=_params('parallel'),
    )(landed, w, m, v)
    return tuple(o.reshape(shape) for o in out)


def _all_reduce_small(x):
    g = _exchange_many([x], [None], True, 'gather_small_grads')[0]

    def body(g_ref, o_ref):
        acc = g_ref[0]
        for d in range(1, N_DEV):
            acc = acc + g_ref[d]
        o_ref[...] = acc

    return pl.pallas_call(body, name='sum_small', out_shape=jax.ShapeDtypeStruct(x.shape, F32))(g)


def _adamw_math(w, g, m, v):
    m = ADAM_B1 * m + (1.0 - ADAM_B1) * g
    v = ADAM_B2 * v + (1.0 - ADAM_B2) * (g * g)
    m_hat = m / (1.0 - ADAM_B1 ** ADAM_STEP)
    v_hat = v / (1.0 - ADAM_B2 ** ADAM_STEP)
    return -ADAM_LR * (m_hat / (jnp.sqrt(v_hat) + ADAM_EPS) + ADAM_WD * w), m, v


def _as_rows(a):
    return a.reshape(1, -1) if a.ndim < 2 else a.reshape(-1, a.shape[-1])


def _as_lanes(a):
    return a.reshape(-1, LANES) if a.size % LANES == 0 else a.reshape(1, -1)


def _adamw_big(w, g, m, v):
    shape = w.shape
    w, g, m, v = (_as_rows(a) for a in (w, g, m, v))
    rows, cols = w.shape
    tr = _tile(rows, (512, 256, 128, 64, 32, 16, 8))

    def body(w_ref, g_ref, m_ref, v_ref, d_ref, nm_ref, nv_ref):
        d_ref[...], nm_ref[...], nv_ref[...] = _adamw_math(w_ref[...], g_ref[...], m_ref[...], v_ref[...])

    spec = pl.BlockSpec((tr, cols), lambda i: (i, 0))
    out = pl.pallas_call(
        body, name='adamw', grid=(rows // tr,), out_shape=(jax.ShapeDtypeStruct(w.shape, F32),) * 3,
        in_specs=[spec] * 4, out_specs=(spec,) * 3, compiler_params=_params('parallel'),
    )(w, g, m, v)
    return tuple(o.reshape(shape) for o in out)


def _adamw_small(ws, gs, ms, vs):
    n = len(ws)
    shapes = [w.shape for w in ws]
    flat = [_as_lanes(a) for group in (ws, gs, ms, vs) for a in group]

    def body(*refs):
        ins, outs = refs[:4 * n], refs[4 * n:]
        for i in range(n):
            d, m, v = _adamw_math(ins[i][...], ins[n + i][...], ins[2 * n + i][...], ins[3 * n + i][...])
            outs[i][...], outs[n + i][...], outs[2 * n + i][...] = d, m, v

    out = pl.pallas_call(
        body, name='adamw_small', out_shape=tuple(jax.ShapeDtypeStruct(flat[i].shape, F32) for _ in range(3) for i in range(n)),
    )(*flat)
    return [tuple(out[j * n + i].reshape(shapes[i]) for j in range(3)) for i in range(n)]


def rms_norm(x, g):
    return x * lax.rsqrt(jnp.mean(jnp.square(x), axis=-1, keepdims=True) + EPS) * g


def modulate(x, g, shift, scale):
    return rms_norm(x, g) * (1 + scale) + shift


def rope_tables(n_tokens, rot_dim):
    t = jnp.arange(n_tokens)
    rows = (t // GRID_W).astype(F32)
    cols = (t % GRID_W).astype(F32)
    axis_dim = rot_dim // 2
    freqs = ROPE_BASE ** (-jnp.arange(0, axis_dim, 2, dtype=F32) / axis_dim)
    ang_r, ang_c = rows[:, None] * freqs, cols[:, None] * freqs
    ang = jnp.concatenate([ang_r, ang_r, ang_c, ang_c], axis=-1)
    return jnp.cos(ang), jnp.sin(ang)


def rope(x, cos, sin):
    x1, x2, x3, x4 = jnp.split(x, 4, axis=-1)
    rot = jnp.concatenate([-x2, x1, -x4, x3], axis=-1)
    return x * cos[:, None, :] + rot * sin[:, None, :]


def heads_first(t):
    return jnp.swapaxes(t, 1, 2)


def tokens_matmul(t, w):
    b, n, k = t.shape
    return linear(t.reshape(b * n, k), w).reshape(b, n, w.shape[1])


def s5_discretize(lam_re, lam_im, log_dt, b_re, b_im):
    dt = jnp.exp(log_dt)[:, None]
    mag = jnp.exp(lam_re * dt)
    a_re = mag * jnp.cos(lam_im * dt)
    a_im = mag * jnp.sin(lam_im * dt)
    den = jnp.square(lam_re) + jnp.square(lam_im)
    f_re = ((a_re - 1.0) * lam_re + a_im * lam_im) / den
    f_im = (a_im * lam_re - (a_re - 1.0) * lam_im) / den
    bb_re = f_re[..., None] * b_re - f_im[..., None] * b_im
    bb_im = f_re[..., None] * b_im + f_im[..., None] * b_re
    return a_re, a_im, bb_re, bb_im


def s5_mixer(u_lat, u_ctx, p, j, need_ctx):
    b, n, _ = u_lat.shape
    c = u_ctx.shape[1]
    half_groups = SSM_GROUPS // 2
    eye = jnp.eye(half_groups, dtype=F32)
    a_res, a_ims, b_res, b_ims, c_res, c_ims, seqs = [], [], [], [], [], [], []
    for d in range(2):
        a_re, a_im, bb_re, bb_im = s5_discretize(p['ssm_lam_re'][j, d], p['ssm_lam_im'][j, d], p['ssm_log_dt'][j, d],
                                                 p['ssm_b_re'][j, d], p['ssm_b_im'][j, d])
        for half in range(2):
            grp = slice(half * half_groups, (half + 1) * half_groups)
            a_res.append(a_re[grp].reshape(S5_LANES))
            a_ims.append(a_im[grp].reshape(S5_LANES))
            b_res.append(jnp.einsum('gsp,gh->gphs', bb_re[grp], eye).reshape(S5_HALF, S5_LANES))
            b_ims.append(jnp.einsum('gsp,gh->gphs', bb_im[grp], eye).reshape(S5_HALF, S5_LANES))
            c_res.append(jnp.einsum('gps,gh->gshp', p['ssm_c_re'][j, d][grp], eye).reshape(S5_LANES, S5_HALF))
            c_ims.append(jnp.einsum('gps,gh->gshp', p['ssm_c_im'][j, d][grp], eye).reshape(S5_LANES, S5_HALF))
        flip = (lambda t: t[:, ::-1]) if d == 1 else (lambda t: t)
        seq = jnp.concatenate([flip(u_ctx), flip(u_lat)], axis=1)
        seqs.append(jnp.transpose(seq.reshape(b, c + n, 2, S5_HALF), (2, 0, 1, 3)))
    u = jnp.stack(seqs).reshape(4 * b, c + n, S5_HALF)
    rep = lambda parts: jnp.repeat(jnp.stack(parts), b, axis=0)
    y = s5_core(u, rep(a_res), rep(a_ims), jnp.stack(b_res), jnp.stack(b_ims), jnp.stack(c_res), jnp.stack(c_ims))
    y = jnp.transpose(y.reshape(2, 2, b, c + n, S5_HALF), (0, 2, 3, 1, 4)).reshape(2, b, c + n, SSM_WIDTH)
    d_skip = p['ssm_d'][j]
    y_lat = d_skip * u_lat + y[0, :, c:] + y[1, :, c:][:, ::-1]
    wg, bg = p['ssm_w_glu'][j], p['ssm_b_glu'][j]

    def glu(t):
        t = jax.nn.gelu(t)
        return t * jax.nn.sigmoid(tokens_matmul(t, wg) + bg)

    if not need_ctx:
        return glu(y_lat), None
    y_ctx = d_skip * u_ctx + y[0, :, :c] + y[1, :, :c][:, ::-1]
    return glu(y_lat), glu(y_ctx)


def even_mixer(a_lat, a_ctx, p, j, need_ctx):
    b, n, _ = a_lat.shape
    c = a_ctx.shape[1]
    cos, sin = rope_tables(n, HEAD_DIM)
    proj = tokens_matmul(jnp.concatenate([a_ctx, a_lat], axis=1), p['e_w_in'][j])
    q, k, v, u = jnp.split(proj, [GQA_Q_W, GQA_Q_W + GQA_KV_W, GQA_Q_W + 2 * GQA_KV_W], axis=-1)
    q = rms_norm(q.reshape(b, c + n, GQA_Q_HEADS, HEAD_DIM), p['e_g_q'][j])
    k = rms_norm(k.reshape(b, c + n, GQA_KV_HEADS, HEAD_DIM), p['e_g_k'][j])
    v = v.reshape(b, c + n, GQA_KV_HEADS, HEAD_DIM)
    q_l = rope(q[:, c:], cos, sin)
    k = jnp.concatenate([k[:, :c], rope(k[:, c:], cos, sin)], axis=1)
    scale = HEAD_DIM ** -0.5
    kh, vh = heads_first(k), heads_first(v)
    att_l = heads_first(attention(heads_first(q_l), kh, vh, scale)).reshape(b, n, GQA_Q_W)
    ssm_l, ssm_c = s5_mixer(u[:, c:], u[:, :c], p, j, need_ctx)
    mix_l = jnp.concatenate([att_l, ssm_l], axis=-1)
    if not need_ctx:
        return tokens_matmul(mix_l, p['e_w_out'][j]), None
    att_c = heads_first(attention(heads_first(q[:, :c]), kh[:, :, :c], vh[:, :, :c], scale)).reshape(b, c, GQA_Q_W)
    mix = jnp.concatenate([jnp.concatenate([att_c, ssm_c], axis=-1), mix_l], axis=1)
    out = tokens_matmul(mix, p['e_w_out'][j])
    return out[:, c:], out[:, :c]


def odd_mixer(a_lat, a_ctx, p, j, need_ctx):
    b, n, _ = a_lat.shape
    c = a_ctx.shape[1]
    t = c + n
    cos, sin = rope_tables(n, MLA_ROPE)
    proj = tokens_matmul(jnp.concatenate([a_ctx, a_lat], axis=1), p['o_w_in'][j])
    c1, c2, c3 = MLA_Q_RANK, MLA_Q_RANK + MLA_KV_RANK, MLA_Q_RANK + MLA_KV_RANK + MLA_ROPE
    cq, ckv, kr = proj[..., :c1], proj[..., c1:c2], proj[..., c2:c3]
    nq, nk, nv = jnp.split(proj[..., ODD_NA_AT:], 3, axis=-1)
    q = tokens_matmul(rms_norm(cq, p['mla_g_cq'][j]), p['mla_w_uq'][j]).reshape(b, t, MLA_HEADS, MLA_QK)
    kv = tokens_matmul(rms_norm(ckv, p['mla_g_ckv'][j]), p['mla_w_ukv'][j]).reshape(b, t, MLA_HEADS, MLA_NOPE + MLA_V)
    k = jnp.concatenate([kv[..., :MLA_NOPE], jnp.broadcast_to(kr[:, :, None, :], (b, t, MLA_HEADS, MLA_ROPE))], axis=-1)
    q, k, mv = rms_norm(q, p['mla_g_q'][j]), rms_norm(k, p['mla_g_k'][j]), kv[..., MLA_NOPE:]

    def rope_tail(x):
        tail = jnp.concatenate([x[:, :c, :, MLA_NOPE:], rope(x[:, c:, :, MLA_NOPE:], cos, sin)], axis=1)
        return jnp.concatenate([x[..., :MLA_NOPE], tail], axis=-1)

    q, k = rope_tail(q), rope_tail(k)
    qh, kh, vh = heads_first(q), heads_first(k), heads_first(mv)
    mla_scale = MLA_QK ** -0.5
    mla_l = heads_first(attention(qh[:, :, c:], kh, vh, mla_scale)).reshape(b, n, MLA_HEADS * MLA_V)
    nq = heads_first(rms_norm(nq.reshape(b, t, NA_HEADS, HEAD_DIM), p['na_g_q'][j]))
    nk = heads_first(rms_norm(nk.reshape(b, t, NA_HEADS, HEAD_DIM), p['na_g_k'][j]))
    nv = heads_first(nv.reshape(b, t, NA_HEADS, HEAD_DIM))
    na_scale = HEAD_DIM ** -0.5
    na_l = na_attention(nq[:, :, c:], nk[:, :, c:], nv[:, :, c:], nk[:, :, :c], nv[:, :, :c], na_bias_table(p['na_rpb'][j]),
                        na_scale)
    na_l = heads_first(na_l).reshape(b, n, NA_W)
    mix_l = jnp.concatenate([mla_l, na_l], axis=-1)
    if not need_ctx:
        return tokens_matmul(mix_l, p['o_w_out'][j]), None
    mla_c = heads_first(attention(qh[:, :, :c], kh[:, :, :c], vh[:, :, :c], mla_scale)).reshape(b, c, MLA_HEADS * MLA_V)
    na_c = heads_first(attention(nq[:, :, :c], nk[:, :, :c], nv[:, :, :c], na_scale)).reshape(b, c, NA_W)
    mix = jnp.concatenate([jnp.concatenate([mla_c, na_c], axis=-1), mix_l], axis=1)
    out = tokens_matmul(mix, p['o_w_out'][j])
    return out[:, c:], out[:, :c]


def mlp(h, w1, w2):
    b, n, k = h.shape
    return mlp_rows(h.reshape(b * n, k), w1, w2).reshape(b, n, w2.shape[1])


def local_loss(x, p, m_lat, m_ctx, ctx, target):
    depth = m_lat.shape[0]
    c = ctx.shape[1]
    xc = ctx
    for i in range(depth):
        need_ctx = i < depth - 1
        j = i // 2
        ml = [m_lat[i, :, s][:, None, :] for s in range(N_MOD)]
        mc = [m_ctx[i, s][None, None, :] for s in range(N_MOD)]
        a_lat = modulate(x, p['g_norm1'][i], ml[0], ml[1])
        a_ctx = modulate(xc, p['g_norm1'][i], mc[0], mc[1])
        mixer = even_mixer if i % 2 == 0 else odd_mixer
        o_lat, o_ctx = mixer(a_lat, a_ctx, p, j, need_ctx)
        x = x + ml[2] * o_lat
        h_lat = modulate(x, p['g_norm2'][i], ml[3], ml[4])
        if need_ctx:
            xc = xc + mc[2] * o_ctx
            h_ctx = modulate(xc, p['g_norm2'][i], mc[3], mc[4])
            ff = mlp(jnp.concatenate([h_ctx, h_lat], axis=1), p['w_ff1'][i], p['w_ff2'][i])
            x = x + ml[5] * ff[:, c:]
            xc = xc + mc[5] * ff[:, :c]
        else:
            x = x + ml[5] * mlp(h_lat, p['w_ff1'][i], p['w_ff2'][i])
    return 0.5 * jnp.sum(jnp.mean(jnp.square(x - target), axis=-1))


def _packed_rows(size, layout):
    width, group = layout
    return -(-size // (width * group)) * group


def _pack_rows(flat, layout):
    width = layout[0]
    rows = _packed_rows(flat.shape[-1], layout)
    flat = jnp.pad(flat, [(0, 0)] * (flat.ndim - 1) + [(0, rows * width - flat.shape[-1])])
    return flat.reshape(flat.shape[:-1] + (rows, width))


def _unpack_rows(rows, shape):
    lead = rows.shape[:-2]
    return rows.reshape(lead + (-1,))[..., :math.prod(shape)].reshape(lead + tuple(shape))


def _unpack_all(packed, shapes, layout):
    out, at = [], 0
    for shape in shapes:
        rows = _packed_rows(math.prod(shape), layout)
        out.append(_unpack_rows(packed[..., at:at + rows, :], shape))
        at += rows
    return out


def _join_shards(g, axis):
    g = jnp.moveaxis(g, 0, axis)
    return g.reshape(g.shape[:axis] + (N_DEV * g.shape[axis + 1],) + g.shape[axis + 2:])


def _split_shards(full, axis):
    s = full.shape
    return jnp.moveaxis(full.reshape(s[:axis] + (N_DEV, s[axis] // N_DEV) + s[axis + 1:]), axis, 0)


def _gather_packed(parts, dtype, layout, name):
    packed = jnp.concatenate([_pack_rows(a.astype(dtype).reshape(-1), layout) for a in parts], axis=0)
    return _unpack_all(_exchange_many([packed], [None], True, name)[0], [a.shape for a in parts], layout)


def kernel(x, c, ctx, c_ctx, w_mod, b_mod, g_norm1, g_norm2, w_ff1, w_ff2, e_w_in, e_w_out, e_g_q, e_g_k, ssm_lam_re, ssm_lam_im, ssm_log_dt, ssm_b_re, ssm_b_im, ssm_c_re, ssm_c_im, ssm_d, ssm_w_glu, ssm_b_glu, o_w_in, o_w_out, mla_g_cq, mla_g_ckv, mla_w_uq, mla_w_ukv, mla_g_q, mla_g_k, na_g_q, na_g_k, na_rpb, loss_target, m_c_ctx, m_w_mod, m_b_mod, m_g_norm1, m_g_norm2, m_w_ff1, m_w_ff2, m_e_w_in, m_e_w_out, m_e_g_q, m_e_g_k, m_ssm_lam_re, m_ssm_lam_im, m_ssm_log_dt, m_ssm_b_re, m_ssm_b_im, m_ssm_c_re, m_ssm_c_im, m_ssm_d, m_ssm_w_glu, m_ssm_b_glu, m_o_w_in, m_o_w_out, m_mla_g_cq, m_mla_g_ckv, m_mla_w_uq, m_mla_w_ukv, m_mla_g_q, m_mla_g_k, m_na_g_q, m_na_g_k, m_na_rpb, v_c_ctx, v_w_mod, v_b_mod, v_g_norm1, v_g_norm2, v_w_ff1, v_w_ff2, v_e_w_in, v_e_w_out, v_e_g_q, v_e_g_k, v_ssm_lam_re, v_ssm_lam_im, v_ssm_log_dt, v_ssm_b_re, v_ssm_b_im, v_ssm_c_re, v_ssm_c_im, v_ssm_d, v_ssm_w_glu, v_ssm_b_glu, v_o_w_in, v_o_w_out, v_mla_g_cq, v_mla_g_ckv, v_mla_w_uq, v_mla_w_ukv, v_mla_g_q, v_mla_g_k, v_na_g_q, v_na_g_k, v_na_rpb):
    given = dict(locals())
    x, c, ctx, target = given['x'], given['c'], given['ctx'], given['loss_target']
    b_loc, _, d_model = x.shape
    depth = given['w_mod'].shape[0]
    ix, iy, ic = lax.axis_index('x'), lax.axis_index('y'), lax.axis_index('c')
    me = 4 * ix + 2 * iy + ic
    n_batch = N_DEV * b_loc
    mod_w = given['w_mod'].shape[2]

    c_rows = jnp.concatenate([c, jnp.zeros((8 - b_loc, d_model), F32)], axis=0)
    small = _gather_packed([c_rows] + [given[n] for n in SHARDED_SMALL], F32, PACK_SMALL,'gather_small')
    c_all = small[0][:, :b_loc].reshape(n_batch, d_model)
    full = {n: _join_shards(g, SHARDED_SMALL[n]) for n, g in zip(SHARDED_SMALL, small[1:])}
    cuts = {n: (a if given[n].shape[a] % (16 if a == 1 else LANES) == 0 else None) for n, a in BIG.items()}
    big = _exchange_many([given[n].astype(BF16) for n in BIG], [cuts[n] for n in BIG], True, 'gather_weights')
    for n, g in zip(BIG, big):
        g = g if cuts[n] is not None else _join_shards(g, BIG[n])
        full[n] = [g[i] for i in range(g.shape[0])]
    c3 = MLA_Q_RANK + MLA_KV_RANK + MLA_ROPE
    full['o_w_in'] = [jnp.concatenate([w[:, :c3], jnp.zeros((w.shape[0], ODD_NA_AT - c3), BF16), w[:, c3:]], axis=-1)
                      for w in full['o_w_in']]
    for n in REPLICATED:
        full[n] = given[n]

    rows17 = 16 * (-(-(n_batch + 1) // 16))
    cond = jnp.concatenate([jax.nn.silu(c_all), jax.nn.silu(given['c_ctx'])[None],
                            jnp.zeros((rows17 - n_batch - 1, d_model), F32)], axis=0)
    mod_mine = jnp.stack([_matmul(cond, given['w_mod'][i], 'nn', F32) for i in range(depth)])
    b_mine = lax.dynamic_slice_in_dim(given['b_mod'], me * mod_w, mod_w, axis=1)
    mod_mine = mod_mine + b_mine[:, None, :]
    mod_all = _gather_packed([mod_mine], F32, PACK_SMALL,'gather_mod')[0]
    mod_all = jnp.moveaxis(mod_all, 0, 2).reshape(depth, rows17, N_MOD, d_model)
    m_lat = lax.dynamic_slice_in_dim(mod_all, me * b_loc, b_loc, axis=1)
    m_ctx = mod_all[:, n_batch]

    diff = {n: full[n] for n in list(BIG) + list(SHARDED_SMALL) + REPLICATED}
    loss, (g_x, g_p, g_ml, g_mc) = jax.value_and_grad(local_loss, argnums=(0, 1, 2, 3))(x, diff, m_lat, m_ctx, ctx, target)
    loss = lax.psum(loss, ('x', 'y', 'c'))
    g_p['o_w_in'] = [jnp.concatenate([g[:, :c3], g[:, ODD_NA_AT:]], axis=-1) for g in g_p['o_w_in']]

    g_rows = jnp.concatenate([g_ml.reshape(depth, b_loc, N_MOD * d_model), g_mc.reshape(depth, 1, N_MOD * d_model),
                              jnp.zeros((depth, 8 - b_loc - 1, N_MOD * d_model), F32)], axis=1)
    g_mod_all = _gather_packed([g_rows], F32, PACK_SMALL,'gather_mod_grads')[0]
    g_lat_all = jnp.moveaxis(g_mod_all[:, :, :b_loc], 0, 1).reshape(depth, n_batch, N_MOD * d_model)
    g_ctx_all = g_mod_all[0, :, b_loc]
    for dev in range(1, N_DEV):
        g_ctx_all = g_ctx_all + g_mod_all[dev, :, b_loc]
    g_mod17 = jnp.concatenate([g_lat_all, g_ctx_all[:, None], jnp.zeros((depth, rows17 - n_batch - 1, N_MOD * d_model), F32)],
                              axis=1)
    grad_b_mod = jnp.sum(g_mod17, axis=1)
    g_mod_mine = lax.dynamic_slice_in_dim(g_mod17, me * mod_w, mod_w, axis=2)
    grad_w_mod = jnp.stack([_matmul(cond, g_mod_mine[i], 'tn', F32) for i in range(depth)])
    d_cond = _matmul(g_mod_mine[0], given['w_mod'][0], 'nt', F32)
    for i in range(1, depth):
        d_cond = d_cond + _matmul(g_mod_mine[i], given['w_mod'][i], 'nt', F32)
    d_cond_ctx = d_cond[n_batch]

    small_names = REPLICATED + list(SHARDED_SMALL)
    parts = [d_cond_ctx] + [g_p[n] for n in small_names]
    packed = jnp.concatenate([_pack_rows(a.reshape(-1), PACK_SMALL) for a in parts], axis=0)
    summed = _unpack_all(_all_reduce_small(packed), [a.shape for a in parts], PACK_SMALL)
    grads = dict(zip(['c_ctx'] + small_names, summed))
    c_ctx = given['c_ctx']
    sig = jax.nn.sigmoid(c_ctx)
    grads['c_ctx'] = grads['c_ctx'] * (sig * (1 + c_ctx * (1 - sig)))
    for n, axis in SHARDED_SMALL.items():
        width = given[n].shape[axis]
        grads[n] = lax.dynamic_slice_in_dim(grads[n], me * width, width, axis=axis)
    grads['w_mod'], grads['b_mod'] = grad_w_mod, grad_b_mod

    stacked = [jnp.stack(g_p[n]) for n in BIG]
    stacked = [g if cuts[n] is not None else _split_shards(g, BIG[n]) for n, g in zip(BIG, stacked)]
    landed = _exchange_many(stacked, [cuts[n] for n in BIG], False, 'scatter_weight_grads')

    upd = {}
    for n, slots in zip(BIG, landed):
        grads[n], *upd[n] = _adamw_landed(slots, given[n], given['m_' + n], given['v_' + n])
    upd['w_mod'] = _adamw_big(given['w_mod'], grads['w_mod'], given['m_w_mod'], given['v_w_mod'])
    rest = [n for n in WEIGHTS if n not in upd]
    out = _adamw_small([given[n] for n in rest], [grads[n] for n in rest], [given['m_' + n] for n in rest],
                       [given['v_' + n] for n in rest])
    upd.update(dict(zip(rest, out)))
    return (loss, g_x, *[grads[n] for n in WEIGHTS], *[upd[n][0] for n in WEIGHTS], *[upd[n][1] for n in WEIGHTS],
            *[upd[n][2] for n in WEIGHTS])
```

```python
import functools
import math

import jax
import jax.numpy as jnp
from jax import lax
from jax.experimental import pallas as pl
from jax.experimental.pallas import tpu as pltpu

F32, BF16 = jnp.float32, jnp.bfloat16
MESH = pl.DeviceIdType.MESH
N_DEV = 8
VMEM_LIMIT_BYTES = 56 * 1024 * 1024
MM_TILE_BYTES = 6 * 1024 * 1024
LANES = 128
PACK_SMALL = (128, 8)

GRID_W = 64
HEAD_DIM = 64
ROPE_BASE = 10000.0
EPS = 1e-6
N_MOD = 6
GQA_Q_HEADS, GQA_KV_HEADS = 12, 4
GQA_Q_W, GQA_KV_W = GQA_Q_HEADS * HEAD_DIM, GQA_KV_HEADS * HEAD_DIM
SSM_WIDTH, SSM_GROUP, SSM_GROUPS, SSM_STATE = 256, 16, 16, 64
MLA_HEADS, MLA_Q_RANK, MLA_KV_RANK, MLA_NOPE, MLA_ROPE, MLA_V = 8, 512, 256, 64, 32, 64
MLA_QK = MLA_NOPE + MLA_ROPE
NA_HEADS, NA_WIN_R, NA_WIN_C = 8, 8, 16
NA_W = NA_HEADS * HEAD_DIM
ODD_IN_W = MLA_Q_RANK + MLA_KV_RANK + MLA_ROPE + 3 * NA_W
ODD_NA_AT = 1024
ODD_IN_PAD = ODD_NA_AT + 3 * NA_W
NEG = -1e30

ADAM_LR, ADAM_B1, ADAM_B2, ADAM_EPS, ADAM_WD, ADAM_STEP = 0.001, 0.9, 0.999, 1e-08, 0.01, 10

FWD_PARAMS = ['x', 'c', 'ctx', 'c_ctx', 'w_mod', 'b_mod', 'g_norm1', 'g_norm2', 'w_ff1', 'w_ff2', 'e_w_in', 'e_w_out',
              'e_g_q', 'e_g_k', 'ssm_lam_re', 'ssm_lam_im', 'ssm_log_dt', 'ssm_b_re', 'ssm_b_im', 'ssm_c_re', 'ssm_c_im',
              'ssm_d', 'ssm_w_glu', 'ssm_b_glu', 'o_w_in', 'o_w_out', 'mla_g_cq', 'mla_g_ckv', 'mla_w_uq', 'mla_w_ukv',
              'mla_g_q', 'mla_g_k', 'na_g_q', 'na_g_k', 'na_rpb']
WEIGHTS = FWD_PARAMS[3:]
BIG = {'w_ff1': 2, 'w_ff2': 1, 'e_w_in': 2, 'e_w_out': 1, 'o_w_in': 2, 'o_w_out': 1, 'mla_w_uq': 2, 'mla_w_ukv': 2,
       'ssm_w_glu': 1}
SHARDED_SMALL = {'mla_g_cq': 1, 'mla_g_ckv': 1}
REPLICATED = [n for n in WEIGHTS if n not in BIG and n not in SHARDED_SMALL and n not in ('w_mod', 'c_ctx', 'b_mod')]


def _tile(dim, prefs):
    for p in prefs:
        if dim >= p and dim % p == 0:
            return p
    return dim


def _params(*sem):
    return pltpu.CompilerParams(dimension_semantics=sem, vmem_limit_bytes=VMEM_LIMIT_BYTES)


def _dot_nt(a, b):
    return lax.dot_general(a, b, (((1,), (1,)), ((), ())), preferred_element_type=F32)


def _dot_tn(a, b):
    return lax.dot_general(a, b, (((0,), (0,)), ((), ())), preferred_element_type=F32)


def _dot(a, b):
    return jnp.dot(a, b, preferred_element_type=F32)


def _matmul(a, b, kind, out_dtype, finish=None, extra=None, n_out=1):
    a, b = a.astype(BF16), b.astype(BF16)
    if kind == 'nn':
        (m, kd), n = a.shape, b.shape[1]
    elif kind == 'nt':
        (m, kd), n = a.shape, b.shape[0]
    else:
        (kd, m), n = a.shape, b.shape[1]
    if kind == 'tn':
        tm = m if m <= 1024 else _tile(m, (1024, 768, 512, 256, 128))
        tn = _tile(n, (1024, 768, 512, 256, 128))
    else:
        tn = n if kd * n * 2 <= MM_TILE_BYTES else _tile(n, (1024, 768, 512, 256, 128))
        tm = _tile(m, [t for t in (1536, 1024, 768, 512, 256, 128) if t * tn * 4 <= MM_TILE_BYTES])
    whole = kind != 'tn' and tn == n and kd * n * 2 <= MM_TILE_BYTES
    tk = kd if whole else _tile(kd, [t for t in (2048, 1536, 1024, 512, 256, 128) if t * max(tm, tn) * 4 <= MM_TILE_BYTES])
    nk = kd // tk
    dn = {'nn': (((1,), (0,)), ((), ())), 'nt': (((1,), (1,)), ((), ())), 'tn': (((0,), (0,)), ((), ()))}[kind]

    n_in = 2 if extra is None else 3

    def body(*refs):
        a_ref, b_ref = refs[:2]
        o_refs = refs[n_in:n_in + n_out]

        def store(total):
            outs = (total,) if finish is None else finish(total, refs[2][...] if extra is not None else None)
            for o_ref, val in zip(o_refs, outs):
                o_ref[...] = val.astype(o_ref.dtype)

        part = lax.dot_general(a_ref[...], b_ref[...], dn, preferred_element_type=F32)
        if nk == 1:
            store(part)
            return
        acc_ref, k = refs[n_in + n_out], pl.program_id(2)

        @pl.when(k == 0)
        def _():
            acc_ref[...] = part

        @pl.when((k > 0) & (k < nk - 1))
        def _():
            acc_ref[...] += part

        @pl.when(k == nk - 1)
        def _():
            store(acc_ref[...] + part)

    a_spec = pl.BlockSpec((tk, tm), lambda i, j, k: (k, i)) if kind == 'tn' else pl.BlockSpec((tm, tk), lambda i, j, k: (i, k))
    b_spec = pl.BlockSpec((tn, tk), lambda i, j, k: (j, k)) if kind == 'nt' else pl.BlockSpec((tk, tn), lambda i, j, k: (k, j))
    o_spec = pl.BlockSpec((tm, tn), lambda i, j, k: (i, j))
    out = pl.pallas_call(
        body, name='mm_' + kind, grid=(m // tm, n // tn, nk),
        out_shape=[jax.ShapeDtypeStruct((m, n), out_dtype)] * n_out,
        in_specs=[a_spec, b_spec] + ([o_spec] if extra is not None else []), out_specs=[o_spec] * n_out,
        scratch_shapes=[pltpu.VMEM((tm, tn), F32)] if nk > 1 else [],
        compiler_params=_params('parallel', 'parallel', 'arbitrary'),
    )(*((a, b) if extra is None else (a, b, extra)))
    return out[0] if n_out == 1 else out


@jax.custom_vjp
def linear(a, w):
    return _matmul(a, w, 'nn', F32)


def _linear_fwd(a, w):
    ab = a.astype(BF16)
    return _matmul(ab, w, 'nn', F32), (ab, w)


def _linear_bwd(res, g):
    ab, w = res
    gb = g.astype(BF16)
    return _matmul(gb, w, 'nt', F32), _matmul(ab, gb, 'tn', w.dtype)


linear.defvjp(_linear_fwd, _linear_bwd)


def _relu2(z, _):
    r = jnp.maximum(z, 0.0)
    return r, r * r


def _relu2_grad(d_act, r):
    return (d_act * (2.0 * r.astype(F32)),)


@jax.custom_vjp
def mlp_rows(h, w1, w2):
    return _mlp_rows_fwd(h, w1, w2)[0]


def _mlp_rows_fwd(h, w1, w2):
    hb = h.astype(BF16)
    r, act = _matmul(hb, w1, 'nn', BF16, finish=_relu2, n_out=2)
    return _matmul(act, w2, 'nn', F32), (hb, w1, w2, r, act)


def _mlp_rows_bwd(res, g):
    hb, w1, w2, r, act = res
    gb = g.astype(BF16)
    dz = _matmul(gb, w2, 'nt', BF16, finish=_relu2_grad, extra=r)
    return _matmul(dz, w1, 'nt', F32), _matmul(hb, dz, 'tn', w1.dtype), _matmul(act, gb, 'tn', w2.dtype)


mlp_rows.defvjp(_mlp_rows_fwd, _mlp_rows_bwd)


LOG2E = math.log2(math.e)


def _softmax_rows(t):
    m = jnp.max(t, axis=-1, keepdims=True)
    e = jnp.exp2(t - m)
    return e * (1.0 / jnp.sum(e, axis=-1, keepdims=True))


ATTN_SPLIT = 2


def _attn_specs(q, k, v, bq):
    _, h, nq, dq = q.shape
    _, hk, nk, dv = v.shape
    g = h // hk
    q_spec = pl.BlockSpec((None, None, bq, dq), lambda b, j, gi, i: (b, j * g + gi, i, 0))
    k_spec = pl.BlockSpec((None, None, nk, dq), lambda b, j, gi, i: (b, j, 0, 0))
    v_spec = pl.BlockSpec((None, None, nk, dv), lambda b, j, gi, i: (b, j, 0, 0))
    o_spec = pl.BlockSpec((None, None, bq, dv), lambda b, j, gi, i: (b, j * g + gi, i, 0))
    t = dict(q=pl.BlockSpec((None, None, dq, bq), lambda b, j, gi, i: (b, j * g + gi, 0, i)),
             o=pl.BlockSpec((None, None, dv, bq), lambda b, j, gi, i: (b, j * g + gi, 0, i)),
             k=pl.BlockSpec((None, None, dq, nk), lambda b, j, gi, i: (b, j, 0, 0)),
             v=pl.BlockSpec((None, None, dv, nk), lambda b, j, gi, i: (b, j, 0, 0)))
    return (q.shape[0], hk, g, nq // bq), q_spec, k_spec, v_spec, o_spec, t


def _attn_blocks(nq):
    bq = _tile(nq, (512, 256, 128))
    return bq, [pl.ds(s * (bq // ATTN_SPLIT), bq // ATTN_SPLIT) for s in range(ATTN_SPLIT)]


def _attn_fwd_call(q, k, vt, scale):
    b, h, nq, _ = q.shape
    dv = vt.shape[2]
    bq, subs = _attn_blocks(nq)
    grid, q_spec, k_spec, _, _, t = _attn_specs(q, k, jnp.swapaxes(vt, 2, 3), bq)

    def body(q_ref, k_ref, vt_ref, ot_ref):
        kb, vtb = k_ref[...], vt_ref[...]
        for rows in subs:
            p = _softmax_rows(_dot_nt(q_ref[rows, :], kb) * (scale * LOG2E))
            ot_ref[:, rows] = _dot_nt(vtb, p.astype(BF16))

    return pl.pallas_call(
        body, name='attn_fwd', grid=grid, out_shape=jax.ShapeDtypeStruct((b, h, dv, nq), F32),
        in_specs=[q_spec, k_spec, t['v']], out_specs=t['o'],
        compiler_params=_params('parallel', 'parallel', 'arbitrary', 'arbitrary'),
    )(q, k, vt)


def _attn_bwd_call(q, k, kt, v, do, scale):
    b, h, nq, dq = q.shape
    bq, subs = _attn_blocks(nq)
    grid, q_spec, k_spec, v_spec, o_spec, t = _attn_specs(q, k, v, bq)

    def body(q_ref, k_ref, kt_ref, v_ref, do_ref, dqt_ref, dk_ref, dv_ref):
        @pl.when((pl.program_id(2) == 0) & (pl.program_id(3) == 0))
        def _():
            dk_ref[...] = jnp.zeros_like(dk_ref)
            dv_ref[...] = jnp.zeros_like(dv_ref)

        kb, ktb, vb = k_ref[...], kt_ref[...], v_ref[...]
        dk, dv = [], []
        for rows in subs:
            qb, dob = q_ref[rows, :], do_ref[rows, :]
            p = _softmax_rows(_dot_nt(qb, kb) * (scale * LOG2E))
            dp = _dot_nt(dob, vb)
            ds = p * (dp - jnp.sum(p * dp, axis=-1, keepdims=True))
            dsb = (ds * scale).astype(BF16)
            dqt_ref[:, rows] = _dot_nt(ktb, dsb)
            dk.append(_dot_tn(dsb, qb))
            dv.append(_dot_tn(p.astype(BF16), dob))
        dk_ref[...] += sum(dk[1:], dk[0])
        dv_ref[...] += sum(dv[1:], dv[0])

    return pl.pallas_call(
        body, name='attn_bwd', grid=grid,
        out_shape=(jax.ShapeDtypeStruct((b, h, dq, nq), F32), jax.ShapeDtypeStruct(k.shape, F32), jax.ShapeDtypeStruct(v.shape, F32)),
        in_specs=[q_spec, k_spec, t['k'], v_spec, o_spec], out_specs=(t['q'], k_spec, v_spec),
        compiler_params=_params('parallel', 'parallel', 'arbitrary', 'arbitrary'),
    )(q, k, kt, v, do)


@functools.partial(jax.custom_vjp, nondiff_argnums=(3,))
def attention(q, k, v, scale):
    return _attention_fwd(q, k, v, scale)[0]


def _attention_fwd(q, k, v, scale):
    qb, kb, vb = q.astype(BF16), k.astype(BF16), v.astype(BF16)
    return jnp.swapaxes(_attn_fwd_call(qb, kb, jnp.swapaxes(vb, 2, 3), scale), 2, 3), (qb, kb, vb)


def _attention_bwd(scale, res, g):
    qb, kb, vb = res
    dqt, dk, dv = _attn_bwd_call(qb, kb, jnp.swapaxes(kb, 2, 3), vb, g.astype(BF16), scale)
    return jnp.swapaxes(dqt, 2, 3), dk, dv


attention.defvjp(_attention_fwd, _attention_bwd)


def _na_window(r, rows):
    start = jnp.clip(r - NA_WIN_R // 2, 0, rows - NA_WIN_R)
    return start, r - start


def _na_scores(q, kw, kc, bias, scale):
    s1 = _dot_nt(q, kw) * scale + bias
    s2 = _dot_nt(q, kc) * scale
    m = jnp.maximum(jnp.max(s1, axis=-1, keepdims=True), jnp.max(s2, axis=-1, keepdims=True))
    e1, e2 = jnp.exp(s1 - m), jnp.exp(s2 - m)
    inv = 1.0 / (jnp.sum(e1, axis=-1, keepdims=True) + jnp.sum(e2, axis=-1, keepdims=True))
    return e1 * inv, e2 * inv


def _na_specs(q, kc):
    _, _, n, d = q.shape
    c = kc.shape[2]
    win = NA_WIN_R * GRID_W
    tok = pl.BlockSpec((None, None, n, d), lambda b, h: (b, h, 0, 0))
    ctx = pl.BlockSpec((None, None, c, d), lambda b, h: (b, h, 0, 0))
    bias = pl.BlockSpec((None, NA_WIN_R, GRID_W, win), lambda b, h: (h, 0, 0, 0))
    dbias = pl.BlockSpec((None, None, NA_WIN_R, GRID_W, win), lambda b, h: (b, h, 0, 0, 0))
    return tok, ctx, bias, dbias


def _na_fwd_call(q, k, v, kc, vc, bias, scale):
    b, h, n, d = q.shape
    rows, win = n // GRID_W, NA_WIN_R * GRID_W
    tok, ctx, bias_spec, _ = _na_specs(q, kc)

    def body(q_ref, k_ref, v_ref, kc_ref, vc_ref, b_ref, o_ref):
        def row(r, carry):
            start, off = _na_window(r, rows)
            at = pl.ds(pl.multiple_of(r * GRID_W, GRID_W), GRID_W)
            wat = pl.ds(pl.multiple_of(start * GRID_W, GRID_W), win)
            p1, p2 = _na_scores(q_ref[at, :], k_ref[wat, :], kc_ref[...], b_ref[off], scale)
            o_ref[at, :] = _dot(p1.astype(BF16), v_ref[wat, :]) + _dot(p2.astype(BF16), vc_ref[...])
            return carry

        lax.fori_loop(0, rows, row, 0, unroll=4)

    return pl.pallas_call(
        body, name='na_fwd', grid=(b, h), out_shape=jax.ShapeDtypeStruct(q.shape, F32),
        in_specs=[tok, tok, tok, ctx, ctx, bias_spec], out_specs=tok,
        compiler_params=_params('parallel', 'parallel'),
    )(q, k, v, kc, vc, bias)


def _na_bwd_call(q, k, v, kc, vc, bias, do, scale):
    b, h, n, d = q.shape
    rows, win = n // GRID_W, NA_WIN_R * GRID_W
    tok, ctx, bias_spec, dbias_spec = _na_specs(q, kc)

    def body(q_ref, k_ref, v_ref, kc_ref, vc_ref, b_ref, do_ref, dq_ref, dk_ref, dv_ref, dkc_ref, dvc_ref, db_ref):
        for ref in (dk_ref, dv_ref, dkc_ref, dvc_ref, db_ref):
            ref[...] = jnp.zeros_like(ref)

        def row(r, carry):
            start, off = _na_window(r, rows)
            at = pl.ds(pl.multiple_of(r * GRID_W, GRID_W), GRID_W)
            wat = pl.ds(pl.multiple_of(start * GRID_W, GRID_W), win)
            qb, kw, vw, dob = q_ref[at, :], k_ref[wat, :], v_ref[wat, :], do_ref[at, :]
            kcb, vcb = kc_ref[...], vc_ref[...]
            p1, p2 = _na_scores(qb, kw, kcb, b_ref[off], scale)
            dp1, dp2 = _dot_nt(dob, vw), _dot_nt(dob, vcb)
            delta = jnp.sum(p1 * dp1, axis=-1, keepdims=True) + jnp.sum(p2 * dp2, axis=-1, keepdims=True)
            ds1, ds2 = p1 * (dp1 - delta), p2 * (dp2 - delta)
            db_ref[off] += ds1
            ds1b, ds2b = (ds1 * scale).astype(BF16), (ds2 * scale).astype(BF16)
            dq_ref[at, :] = _dot(ds1b, kw) + _dot(ds2b, kcb)
            dk_ref[wat, :] += _dot_tn(ds1b, qb)
            dv_ref[wat, :] += _dot_tn(p1.astype(BF16), dob)
            dkc_ref[...] += _dot_tn(ds2b, qb)
            dvc_ref[...] += _dot_tn(p2.astype(BF16), dob)
            return carry

        lax.fori_loop(0, rows, row, 0, unroll=2)

    f = lambda a: jax.ShapeDtypeStruct(a.shape, F32)
    return pl.pallas_call(
        body, name='na_bwd', grid=(b, h),
        out_shape=(f(q), f(k), f(v), f(kc), f(vc), jax.ShapeDtypeStruct((b,) + bias.shape, F32)),
        in_specs=[tok, tok, tok, ctx, ctx, bias_spec, tok], out_specs=(tok, tok, tok, ctx, ctx, dbias_spec),
        compiler_params=_params('parallel', 'parallel'),
    )(q, k, v, kc, vc, bias, do)


@functools.partial(jax.custom_vjp, nondiff_argnums=(6,))
def na_attention(q, k, v, kc, vc, bias, scale):
    return _na_fwd_call(q.astype(BF16), k.astype(BF16), v.astype(BF16), kc.astype(BF16), vc.astype(BF16), bias, scale)


def _na_attention_fwd(q, k, v, kc, vc, bias, scale):
    res = (q.astype(BF16), k.astype(BF16), v.astype(BF16), kc.astype(BF16), vc.astype(BF16), bias)
    return _na_fwd_call(*res, scale), res


def _na_attention_bwd(scale, res, g):
    dq, dk, dv, dkc, dvc, db = _na_bwd_call(*res, g.astype(BF16), scale)
    return dq, dk, dv, dkc, dvc, jnp.sum(db, axis=0)


na_attention.defvjp(_na_attention_fwd, _na_attention_bwd)


def _na_table_index():
    qcol = jnp.arange(GRID_W)
    kcol = jnp.arange(GRID_W)
    cstart = jnp.clip(qcol - NA_WIN_C // 2, 0, GRID_W - NA_WIN_C)
    inside = (kcol[None, :] >= cstart[:, None]) & (kcol[None, :] < cstart[:, None] + NA_WIN_C)
    cidx = jnp.clip(kcol[None, :] - qcol[:, None] + (NA_WIN_C - 1), 0, 2 * NA_WIN_C - 2)
    ridx = jnp.arange(NA_WIN_R)[None, :] - jnp.arange(NA_WIN_R)[:, None] + (NA_WIN_R - 1)
    return inside, cidx, ridx


@jax.custom_vjp
def na_bias_table(rpb):
    inside, pick_c, pick_r = _na_table_picks()
    rows = jnp.einsum('hab,oja->hojb', rpb, pick_r, precision=lax.Precision.HIGHEST)
    t = jnp.einsum('hojb,qkb->hoqjk', rows, pick_c, precision=lax.Precision.HIGHEST)
    t = jnp.where(inside[None, None, :, None, :], t, NEG)
    return t.reshape(rpb.shape[0], NA_WIN_R, GRID_W, NA_WIN_R * GRID_W)


def _na_table_picks():
    inside, cidx, ridx = _na_table_index()
    pick_c = ((cidx[..., None] == jnp.arange(2 * NA_WIN_C - 1)) & inside[..., None]).astype(F32)
    pick_r = (ridx[..., None] == jnp.arange(2 * NA_WIN_R - 1)).astype(F32)
    return inside, pick_c, pick_r


def _na_bias_table_bwd(_, dt):
    _, pick_c, pick_r = _na_table_picks()
    d5 = dt.reshape(dt.shape[0], NA_WIN_R, GRID_W, NA_WIN_R, GRID_W)
    part = jnp.einsum('hoqjk,qkb->hojb', d5, pick_c, precision=lax.Precision.HIGHEST)
    return (jnp.einsum('hojb,oja->hab', part, pick_r, precision=lax.Precision.HIGHEST),)


na_bias_table.defvjp(lambda rpb: (na_bias_table(rpb), None), _na_bias_table_bwd)


S5_HALF = SSM_WIDTH // 2
S5_LANES = (SSM_GROUPS // 2) * SSM_STATE
S5_Q = S5_LANES // LANES


def _s5_tiles(a):
    r = a.shape[0]
    return jnp.transpose(a.reshape(r, S5_Q, LANES), (1, 0, 2)).reshape(S5_Q * r, LANES)


def _s5_untiles(a):
    r = a.shape[0] // S5_Q
    return jnp.transpose(a.reshape(S5_Q, r, LANES), (1, 0, 2)).reshape(r, S5_LANES)


def _s5_put(ref, r, rr, tc, val):
    for q in range(S5_Q):
        ref[pl.ds((q * rr + r) * tc, tc), :] = val[:, q * LANES:(q + 1) * LANES]


def _s5_get(ref, r, rr, tc):
    return jnp.concatenate([ref[pl.ds((q * rr + r) * tc, tc), :] for q in range(S5_Q)], axis=1)


S5_BLOCK = 8
S5_TOGETHER = 8


def _s5_powers(a_re, a_im, backward):
    a_im = -a_im if backward else a_im
    pr, pi = [a_re], [a_im]
    for _ in range(S5_BLOCK - 1):
        pr, pi = pr + [pr[-1] * a_re - pi[-1] * a_im], pi + [pr[-1] * a_im + pi[-1] * a_re]
    order = range(S5_BLOCK - 1, -1, -1) if backward else range(S5_BLOCK)

    def table(p):
        rows = [jnp.broadcast_to(p[s - 1][:, None, :], (p[0].shape[0], S5_BLOCK, LANES)) for s in (1, 2, 4)]
        return jnp.stack(rows + [jnp.stack([p[t] for t in order], axis=1)], axis=1)

    return table(pr), table(pi)


def _s5_scan_block(xr, xi, pr_ref, pi_ref, chain, carry, backward):
    row = lax.broadcasted_iota(jnp.int32, (S5_BLOCK, LANES), 0)
    for e, s in enumerate((1, 2, 4)):
        ar, ai = pr_ref[chain, e], pi_ref[chain, e]
        keep = (row < S5_BLOCK - s) if backward else (row >= s)
        shift = S5_BLOCK - s if backward else s
        sr = jnp.where(keep, pltpu.roll(xr, shift, 0), 0.0)
        si = jnp.where(keep, pltpu.roll(xi, shift, 0), 0.0)
        xr, xi = xr + (ar * sr - ai * si), xi + (ar * si + ai * sr)
    ar, ai = pr_ref[chain, 3], pi_ref[chain, 3]
    cr, ci = carry
    xr, xi = xr + (ar * cr - ai * ci), xi + (ar * ci + ai * cr)
    edge = slice(0, 1) if backward else slice(S5_BLOCK - 1, S5_BLOCK)
    return xr, xi, (xr[edge], xi[edge])


def _s5_scan_chunk(xr_ref, xi_ref, pr_ref, pi_ref, sr_ref, si_ref, chains, tc, backward):
    blocks = tc // S5_BLOCK
    for first in range(0, chains, S5_TOGETHER):
        group = range(first, min(first + S5_TOGETHER, chains))

        def block(k, carries, group=group):
            j = blocks - 1 - k if backward else k
            out = []
            for chain, carry in zip(group, carries):
                at = pl.ds(pl.multiple_of(chain * tc + j * S5_BLOCK, S5_BLOCK), S5_BLOCK)
                xr, xi, carry = _s5_scan_block(xr_ref[at, :], xi_ref[at, :], pr_ref, pi_ref, chain, carry, backward)
                xr_ref[at, :] = xr
                xi_ref[at, :] = xi
                out.append(carry)
            return tuple(out)

        start = tuple((sr_ref[pl.ds(chain, 1), :], si_ref[pl.ds(chain, 1), :]) for chain in group)
        for chain, (cr, ci) in zip(group, lax.fori_loop(0, blocks, block, start)):
            sr_ref[pl.ds(chain, 1), :] = cr
            si_ref[pl.ds(chain, 1), :] = ci


def _s5_fwd_call(u, a_re, a_im, b_re, b_im, c_re, c_im):
    rr, t_len, _ = u.shape
    sets = b_re.shape[0]
    per = rr // sets
    tc = _tile(t_len, (256, 128))
    nt = t_len // tc
    qr = S5_Q * rr

    def body(u_ref, ar_ref, ai_ref, br_ref, bi_ref, cr_ref, ci_ref, y_ref, hr_ref, hi_ref, sr_ref, si_ref):
        @pl.when(pl.program_id(0) == 0)
        def _():
            sr_ref[...] = jnp.zeros_like(sr_ref)
            si_ref[...] = jnp.zeros_like(si_ref)

        for r in range(rr):
            ub = u_ref[r]
            _s5_put(hr_ref, r, rr, tc, _dot(ub, br_ref[r // per]))
            _s5_put(hi_ref, r, rr, tc, _dot(ub, bi_ref[r // per]))
        _s5_scan_chunk(hr_ref, hi_ref, ar_ref, ai_ref, sr_ref, si_ref, qr, tc, False)
        for r in range(rr):
            y_ref[r] = (_dot(_s5_get(hr_ref, r, rr, tc).astype(BF16), cr_ref[r // per])
                        - _dot(_s5_get(hi_ref, r, rr, tc).astype(BF16), ci_ref[r // per]))

    full = lambda a: pl.BlockSpec(a.shape, lambda i: (0,) * a.ndim)
    h_spec = pl.BlockSpec((None, qr * tc, LANES), lambda i: (i, 0, 0))
    h_shape = jax.ShapeDtypeStruct((nt, qr * tc, LANES), F32)
    a_re, a_im = _s5_powers(_s5_tiles(a_re), _s5_tiles(a_im), False)
    return pl.pallas_call(
        body, name='s5_fwd', grid=(nt,),
        out_shape=(jax.ShapeDtypeStruct((rr, t_len, S5_HALF), F32), h_shape, h_shape),
        in_specs=[pl.BlockSpec((rr, tc, S5_HALF), lambda i: (0, i, 0)), full(a_re), full(a_im), full(b_re), full(b_im),
                  full(c_re), full(c_im)],
        out_specs=(pl.BlockSpec((rr, tc, S5_HALF), lambda i: (0, i, 0)), h_spec, h_spec),
        scratch_shapes=[pltpu.VMEM((qr, LANES), F32), pltpu.VMEM((qr, LANES), F32)],
        compiler_params=_params('arbitrary'),
    )(u, a_re, a_im, b_re, b_im, c_re, c_im)


def _s5_bwd_call(u, a_re, a_im, b_re, b_im, c_re, c_im, h_re, h_im, dy):
    rr, t_len, _ = u.shape
    sets = b_re.shape[0]
    per = rr // sets
    nt, rows, _ = h_re.shape
    qr = S5_Q * rr
    tc = rows // qr

    def body(u_ref, dy_ref, ar_ref, ai_ref, br_ref, bi_ref, cr_ref, ci_ref, hr_ref, hi_ref,
             du_ref, dar_ref, dai_ref, dbr_ref, dbi_ref, dcr_ref, dci_ref, gr_ref, gi_ref, sr_ref, si_ref):
        i = pl.program_id(0)

        @pl.when(i == 0)
        def _():
            for ref in (dar_ref, dai_ref, dbr_ref, dbi_ref, dcr_ref, dci_ref, sr_ref, si_ref):
                ref[...] = jnp.zeros_like(ref)

        for r in range(rr):
            dyb = dy_ref[r]
            _s5_put(gr_ref, r, rr, tc, _dot_nt(dyb, cr_ref[r // per]))
            _s5_put(gi_ref, r, rr, tc, -_dot_nt(dyb, ci_ref[r // per]))
        g_r, g_i = sr_ref[...], si_ref[...]
        last = pl.ds(tc - 1, qr, stride=tc)
        dar_ref[...] += g_r * hr_ref[last, :] + g_i * hi_ref[last, :]
        dai_ref[...] += g_i * hr_ref[last, :] - g_r * hi_ref[last, :]
        _s5_scan_chunk(gr_ref, gi_ref, ar_ref, ai_ref, sr_ref, si_ref, qr, tc, True)
        row = lax.broadcasted_iota(jnp.int32, (tc, LANES), 0)
        for chain in range(qr):
            at, one = pl.ds(chain * tc, tc), pl.ds(chain, 1)
            p_r = jnp.where(row >= 1, pltpu.roll(hr_ref[at, :], 1, 0), 0.0)
            p_i = jnp.where(row >= 1, pltpu.roll(hi_ref[at, :], 1, 0), 0.0)
            g_r, g_i = gr_ref[at, :], gi_ref[at, :]
            dar_ref[one, :] += jnp.sum(g_r * p_r + g_i * p_i, axis=0, keepdims=True)
            dai_ref[one, :] += jnp.sum(g_i * p_r - g_r * p_i, axis=0, keepdims=True)
        for r in range(rr):
            s = r // per
            ub, dyb = u_ref[r], dy_ref[r]
            grb, gib = _s5_get(gr_ref, r, rr, tc).astype(BF16), _s5_get(gi_ref, r, rr, tc).astype(BF16)
            du_ref[r] = _dot_nt(grb, br_ref[s]) + _dot_nt(gib, bi_ref[s])
            dbr_ref[s] += _dot_tn(ub, grb)
            dbi_ref[s] += _dot_tn(ub, gib)
            dcr_ref[s] += _dot_tn(_s5_get(hr_ref, r, rr, tc).astype(BF16), dyb)
            dci_ref[s] -= _dot_tn(_s5_get(hi_ref, r, rr, tc).astype(BF16), dyb)

    full = lambda a: pl.BlockSpec(a.shape, lambda i: (0,) * a.ndim)
    back = lambda i: nt - 1 - i
    tok = pl.BlockSpec((rr, tc, S5_HALF), lambda i: (0, back(i), 0))
    h_spec = pl.BlockSpec((None, qr * tc, LANES), lambda i: (back(i), 0, 0))
    f = lambda a: jax.ShapeDtypeStruct(a.shape, F32)
    a_re, a_im = _s5_powers(_s5_tiles(a_re), _s5_tiles(a_im), True)
    da = jax.ShapeDtypeStruct((qr, LANES), F32)
    du, da_re, da_im, db_re, db_im, dc_re, dc_im = pl.pallas_call(
        body, name='s5_bwd', grid=(nt,),
        out_shape=(jax.ShapeDtypeStruct(u.shape, F32), da, da, f(b_re), f(b_im), f(c_re), f(c_im)),
        in_specs=[tok, tok, full(a_re), full(a_im), full(b_re), full(b_im), full(c_re), full(c_im), h_spec, h_spec],
        out_specs=(tok, full(da), full(da), full(b_re), full(b_im), full(c_re), full(c_im)),
        scratch_shapes=[pltpu.VMEM((qr * tc, LANES), F32), pltpu.VMEM((qr * tc, LANES), F32),
                        pltpu.VMEM((qr, LANES), F32), pltpu.VMEM((qr, LANES), F32)],
        compiler_params=_params('arbitrary'),
    )(u, dy, a_re, a_im, b_re, b_im, c_re, c_im, h_re, h_im)
    return du, _s5_untiles(da_re), _s5_untiles(da_im), db_re, db_im, dc_re, dc_im


@jax.custom_vjp
def s5_core(u, a_re, a_im, b_re, b_im, c_re, c_im):
    return _s5_fwd_call(u.astype(BF16), a_re, a_im, b_re.astype(BF16), b_im.astype(BF16), c_re.astype(BF16),
                        c_im.astype(BF16))[0]


def _s5_core_fwd(u, a_re, a_im, b_re, b_im, c_re, c_im):
    args = (u.astype(BF16), a_re, a_im, b_re.astype(BF16), b_im.astype(BF16), c_re.astype(BF16), c_im.astype(BF16))
    y, h_re, h_im = _s5_fwd_call(*args)
    return y, args + (h_re, h_im)


def _s5_core_bwd(res, g):
    return _s5_bwd_call(*res, g.astype(BF16))


s5_core.defvjp(_s5_core_fwd, _s5_core_bwd)


def _shard_view(ref, axis, index, width):
    return ref.at[(slice(None),) * axis + (pl.ds(pl.multiple_of(index * width, width), width),)]


def _exchange_many(xs, cuts, gather, name):
    n = len(xs)
    if gather:
        shards = [x.shape for x in xs]
    else:
        shards = [x.shape[1:] if cut is None else x.shape[:cut] + (x.shape[cut] // N_DEV,) + x.shape[cut + 1:]
                  for x, cut in zip(xs, cuts)]

    def full_shape(shard, cut):
        return shard[:cut] + (N_DEV * shard[cut],) + shard[cut + 1:]

    out_shapes = [jax.ShapeDtypeStruct((N_DEV,) + tuple(s) if (cut is None or not gather) else full_shape(tuple(s), cut), x.dtype)
                  for x, s, cut in zip(xs, shards, cuts)]

    def body(*refs):
        x_refs, out_refs = refs[:n], refs[n:2 * n]
        send_sems, recv_sems, local_sems = refs[2 * n:]
        ix, iy, ic = lax.axis_index('x'), lax.axis_index('y'), lax.axis_index('c')
        me = 4 * ix + 2 * iy + ic

        def flipped(k):
            px = 1 - ix if k & 4 else ix
            py = 1 - iy if k & 2 else iy
            pc = 1 - ic if k & 1 else ic
            return (px, py, pc), 4 * px + 2 * py + pc

        def block(ref, cut, shard, who):
            return ref.at[who] if cut is None else _shard_view(ref, cut, who, shard[cut])

        def ends(i, sender, receiver):
            if gather:
                return x_refs[i], block(out_refs[i], cuts[i], shards[i], sender)
            return block(x_refs[i], cuts[i], shards[i], receiver), out_refs[i].at[sender]

        def copy(i, k, sender, receiver):
            src, dst = ends(i, sender, receiver)
            return pltpu.make_async_remote_copy(src_ref=src, dst_ref=dst, send_sem=send_sems.at[i * (N_DEV - 1) + k - 1],
                                                 recv_sem=recv_sems.at[i * (N_DEV - 1) + k - 1], device_id=flipped(k)[0],
                                                 device_id_type=MESH)

        own = [pltpu.make_async_copy(*ends(i, me, me), local_sems.at[i]) for i in range(n)]
        for cp in own:
            cp.start()
        if gather:
            chips = (2, 4, 6)
            sent = [copy(i, k, me, flipped(k)[1]) for k in (1,) + chips for i in range(n)]
            for cp in sent:
                cp.start()
            for k in chips:
                for i in range(n):
                    copy(i, k, flipped(k)[1], me).wait_recv()
                    src, dst = ends(i, flipped(k)[1], me)
                    sent.append(pltpu.make_async_remote_copy(
                        src_ref=dst, dst_ref=dst, send_sem=send_sems.at[i * (N_DEV - 1) + k], recv_sem=recv_sems.at[i * (N_DEV - 1) + k],
                        device_id=flipped(1)[0], device_id_type=MESH))
                    sent[-1].start()
            for k in (1, 3, 5, 7):
                for i in range(n):
                    src, dst = ends(i, flipped(k)[1], me)
                    pltpu.make_async_remote_copy(
                        src_ref=dst, dst_ref=dst, send_sem=send_sems.at[i * (N_DEV - 1) + k - 1], recv_sem=recv_sems.at[i * (N_DEV - 1) + k - 1],
                        device_id=flipped(1)[0], device_id_type=MESH).wait_recv()
            for cp in sent:
                cp.wait_send()
            for cp in own:
                cp.wait()
            return
        sent = [copy(i, k, me, flipped(k)[1]) for k in range(1, N_DEV) for i in range(n)]
        for cp in sent:
            cp.start()
        for k in range(1, N_DEV):
            for i in range(n):
                copy(i, k, flipped(k)[1], me).wait_recv()
        for cp in sent:
            cp.wait_send()
        for cp in own:
            cp.wait()

    hbm = pl.BlockSpec(memory_space=pltpu.HBM)
    pairs = n * (N_DEV - 1)
    return pl.pallas_call(
        body, name=name, out_shape=out_shapes, in_specs=[hbm] * n, out_specs=[hbm] * n,
        scratch_shapes=[pltpu.SemaphoreType.DMA((pairs,)), pltpu.SemaphoreType.DMA((pairs,)), pltpu.SemaphoreType.DMA((n,))],
    )(*xs)


def _adamw_landed(landed, w, m, v):
    shape = w.shape
    slots = landed.shape[0]
    w, m, v = (_as_rows(a) for a in (w, m, v))
    rows, cols = w.shape
    landed = landed.reshape(slots, rows, cols)
    tr = _tile(rows, (256, 128, 64, 32, 16))

    def body(l_ref, w_ref, m_ref, v_ref, g_ref, d_ref, nm_ref, nv_ref):
        g = l_ref[0].astype(F32)
        for d in range(1, slots):
            g = g + l_ref[d].astype(F32)
        g_ref[...] = g
        d_ref[...], nm_ref[...], nv_ref[...] = _adamw_math(w_ref[...], g, m_ref[...], v_ref[...])

    spec = pl.BlockSpec((tr, cols), lambda i: (i, 0))
    out = pl.pallas_call(
        body, name='adamw_landed', grid=(rows // tr,), out_shape=(jax.ShapeDtypeStruct(w.shape, F32),) * 4,
        in_specs=[pl.BlockSpec((slots, tr, cols), lambda i: (0, i, 0))] + [spec] * 3, out_specs=(spec,) * 4,
        compiler_params=_params('parallel'),
    )(landed, w, m, v)
    return tuple(o.reshape(shape) for o in out)


def _all_reduce_small(x):
    g = _exchange_many([x], [None], True, 'gather_small_grads')[0]

    def body(g_ref, o_ref):
        acc = g_ref[0]
        for d in range(1, N_DEV):
            acc = acc + g_ref[d]
        o_ref[...] = acc

    return pl.pallas_call(body, name='sum_small', out_shape=jax.ShapeDtypeStruct(x.shape, F32))(g)


def _adamw_math(w, g, m, v):
    m = ADAM_B1 * m + (1.0 - ADAM_B1) * g
    v = ADAM_B2 * v + (1.0 - ADAM_B2) * (g * g)
    m_hat = m / (1.0 - ADAM_B1 ** ADAM_STEP)
    v_hat = v / (1.0 - ADAM_B2 ** ADAM_STEP)
    return -ADAM_LR * (m_hat / (jnp.sqrt(v_hat) + ADAM_EPS) + ADAM_WD * w), m, v


def _as_rows(a):
    return a.reshape(1, -1) if a.ndim < 2 else a.reshape(-1, a.shape[-1])


def _as_lanes(a):
    return a.reshape(-1, LANES) if a.size % LANES == 0 else a.reshape(1, -1)


def _adamw_big(w, g, m, v):
    shape = w.shape
    w, g, m, v = (_as_rows(a) for a in (w, g, m, v))
    rows, cols = w.shape
    tr = _tile(rows, (512, 256, 128, 64, 32, 16, 8))

    def body(w_ref, g_ref, m_ref, v_ref, d_ref, nm_ref, nv_ref):
        d_ref[...], nm_ref[...], nv_ref[...] = _adamw_math(w_ref[...], g_ref[...], m_ref[...], v_ref[...])

    spec = pl.BlockSpec((tr, cols), lambda i: (i, 0))
    out = pl.pallas_call(
        body, name='adamw', grid=(rows // tr,), out_shape=(jax.ShapeDtypeStruct(w.shape, F32),) * 3,
        in_specs=[spec] * 4, out_specs=(spec,) * 3, compiler_params=_params('parallel'),
    )(w, g, m, v)
    return tuple(o.reshape(shape) for o in out)


def _adamw_small(ws, gs, ms, vs):
    n = len(ws)
    shapes = [w.shape for w in ws]
    flat = [_as_lanes(a) for group in (ws, gs, ms, vs) for a in group]

    def body(*refs):
        ins, outs = refs[:4 * n], refs[4 * n:]
        for i in range(n):
            d, m, v = _adamw_math(ins[i][...], ins[n + i][...], ins[2 * n + i][...], ins[3 * n + i][...])
            outs[i][...], outs[n + i][...], outs[2 * n + i][...] = d, m, v

    out = pl.pallas_call(
        body, name='adamw_small', out_shape=tuple(jax.ShapeDtypeStruct(flat[i].shape, F32) for _ in range(3) for i in range(n)),
    )(*flat)
    return [tuple(out[j * n + i].reshape(shapes[i]) for j in range(3)) for i in range(n)]


def rms_norm(x, g):
    return x * lax.rsqrt(jnp.mean(jnp.square(x), axis=-1, keepdims=True) + EPS) * g


def modulate(x, g, shift, scale):
    return rms_norm(x, g) * (1 + scale) + shift


def rope_tables(n_tokens, rot_dim):
    t = jnp.arange(n_tokens)
    rows = (t // GRID_W).astype(F32)
    cols = (t % GRID_W).astype(F32)
    axis_dim = rot_dim // 2
    freqs = ROPE_BASE ** (-jnp.arange(0, axis_dim, 2, dtype=F32) / axis_dim)
    ang_r, ang_c = rows[:, None] * freqs, cols[:, None] * freqs
    ang = jnp.concatenate([ang_r, ang_r, ang_c, ang_c], axis=-1)
    return jnp.cos(ang), jnp.sin(ang)


def rope(x, cos, sin):
    x1, x2, x3, x4 = jnp.split(x, 4, axis=-1)
    rot = jnp.concatenate([-x2, x1, -x4, x3], axis=-1)
    return x * cos[:, None, :] + rot * sin[:, None, :]


def heads_first(t):
    return jnp.swapaxes(t, 1, 2)


def tokens_matmul(t, w):
    b, n, k = t.shape
    return linear(t.reshape(b * n, k), w).reshape(b, n, w.shape[1])


def s5_discretize(lam_re, lam_im, log_dt, b_re, b_im):
    dt = jnp.exp(log_dt)[:, None]
    mag = jnp.exp(lam_re * dt)
    a_re = mag * jnp.cos(lam_im * dt)
    a_im = mag * jnp.sin(lam_im * dt)
    den = jnp.square(lam_re) + jnp.square(lam_im)
    f_re = ((a_re - 1.0) * lam_re + a_im * lam_im) / den
    f_im = (a_im * lam_re - (a_re - 1.0) * lam_im) / den
    bb_re = f_re[..., None] * b_re - f_im[..., None] * b_im
    bb_im = f_re[..., None] * b_im + f_im[..., None] * b_re
    return a_re, a_im, bb_re, bb_im


def s5_mixer(u_lat, u_ctx, p, j, need_ctx):
    b, n, _ = u_lat.shape
    c = u_ctx.shape[1]
    half_groups = SSM_GROUPS // 2
    eye = jnp.eye(half_groups, dtype=F32)
    a_res, a_ims, b_res, b_ims, c_res, c_ims, seqs = [], [], [], [], [], [], []
    for d in range(2):
        a_re, a_im, bb_re, bb_im = s5_discretize(p['ssm_lam_re'][j, d], p['ssm_lam_im'][j, d], p['ssm_log_dt'][j, d],
                                                 p['ssm_b_re'][j, d], p['ssm_b_im'][j, d])
        for half in range(2):
            grp = slice(half * half_groups, (half + 1) * half_groups)
            a_res.append(a_re[grp].reshape(S5_LANES))
            a_ims.append(a_im[grp].reshape(S5_LANES))
            b_res.append(jnp.einsum('gsp,gh->gphs', bb_re[grp], eye).reshape(S5_HALF, S5_LANES))
            b_ims.append(jnp.einsum('gsp,gh->gphs', bb_im[grp], eye).reshape(S5_HALF, S5_LANES))
            c_res.append(jnp.einsum('gps,gh->gshp', p['ssm_c_re'][j, d][grp], eye).reshape(S5_LANES, S5_HALF))
            c_ims.append(jnp.einsum('gps,gh->gshp', p['ssm_c_im'][j, d][grp], eye).reshape(S5_LANES, S5_HALF))
        flip = (lambda t: t[:, ::-1]) if d == 1 else (lambda t: t)
        seq = jnp.concatenate([flip(u_ctx), flip(u_lat)], axis=1)
        seqs.append(jnp.transpose(seq.reshape(b, c + n, 2, S5_HALF), (2, 0, 1, 3)))
    u = jnp.stack(seqs).reshape(4 * b, c + n, S5_HALF)
    rep = lambda parts: jnp.repeat(jnp.stack(parts), b, axis=0)
    y = s5_core(u, rep(a_res), rep(a_ims), jnp.stack(b_res), jnp.stack(b_ims), jnp.stack(c_res), jnp.stack(c_ims))
    y = jnp.transpose(y.reshape(2, 2, b, c + n, S5_HALF), (0, 2, 3, 1, 4)).reshape(2, b, c + n, SSM_WIDTH)
    d_skip = p['ssm_d'][j]
    y_lat = d_skip * u_lat + y[0, :, c:] + y[1, :, c:][:, ::-1]
    wg, bg = p['ssm_w_glu'][j], p['ssm_b_glu'][j]

    def glu(t):
        t = jax.nn.gelu(t)
        return t * jax.nn.sigmoid(tokens_matmul(t, wg) + bg)

    if not need_ctx:
        return glu(y_lat), None
    y_ctx = d_skip * u_ctx + y[0, :, :c] + y[1, :, :c][:, ::-1]
    return glu(y_lat), glu(y_ctx)


def even_mixer(a_lat, a_ctx, p, j, need_ctx):
    b, n, _ = a_lat.shape
    c = a_ctx.shape[1]
    cos, sin = rope_tables(n, HEAD_DIM)
    proj = tokens_matmul(jnp.concatenate([a_ctx, a_lat], axis=1), p['e_w_in'][j])
    q, k, v, u = jnp.split(proj, [GQA_Q_W, GQA_Q_W + GQA_KV_W, GQA_Q_W + 2 * GQA_KV_W], axis=-1)
    q = rms_norm(q.reshape(b, c + n, GQA_Q_HEADS, HEAD_DIM), p['e_g_q'][j])
    k = rms_norm(k.reshape(b, c + n, GQA_KV_HEADS, HEAD_DIM), p['e_g_k'][j])
    v = v.reshape(b, c + n, GQA_KV_HEADS, HEAD_DIM)
    q_l = rope(q[:, c:], cos, sin)
    k = jnp.concatenate([k[:, :c], rope(k[:, c:], cos, sin)], axis=1)
    scale = HEAD_DIM ** -0.5
    kh, vh = heads_first(k), heads_first(v)
    att_l = heads_first(attention(heads_first(q_l), kh, vh, scale)).reshape(b, n, GQA_Q_W)
    ssm_l, ssm_c = s5_mixer(u[:, c:], u[:, :c], p, j, need_ctx)
    mix_l = jnp.concatenate([att_l, ssm_l], axis=-1)
    if not need_ctx:
        return tokens_matmul(mix_l, p['e_w_out'][j]), None
    att_c = heads_first(attention(heads_first(q[:, :c]), kh[:, :, :c], vh[:, :, :c], scale)).reshape(b, c, GQA_Q_W)
    mix = jnp.concatenate([jnp.concatenate([att_c, ssm_c], axis=-1), mix_l], axis=1)
    out = tokens_matmul(mix, p['e_w_out'][j])
    return out[:, c:], out[:, :c]


def odd_mixer(a_lat, a_ctx, p, j, need_ctx):
    b, n, _ = a_lat.shape
    c = a_ctx.shape[1]
    t = c + n
    cos, sin = rope_tables(n, MLA_ROPE)
    proj = tokens_matmul(jnp.concatenate([a_ctx, a_lat], axis=1), p['o_w_in'][j])
    c1, c2, c3 = MLA_Q_RANK, MLA_Q_RANK + MLA_KV_RANK, MLA_Q_RANK + MLA_KV_RANK + MLA_ROPE
    cq, ckv, kr = proj[..., :c1], proj[..., c1:c2], proj[..., c2:c3]
    nq, nk, nv = jnp.split(proj[..., ODD_NA_AT:], 3, axis=-1)
    q = tokens_matmul(rms_norm(cq, p['mla_g_cq'][j]), p['mla_w_uq'][j]).reshape(b, t, MLA_HEADS, MLA_QK)
    kv = tokens_matmul(rms_norm(ckv, p['mla_g_ckv'][j]), p['mla_w_ukv'][j]).reshape(b, t, MLA_HEADS, MLA_NOPE + MLA_V)
    k = jnp.concatenate([kv[..., :MLA_NOPE], jnp.broadcast_to(kr[:, :, None, :], (b, t, MLA_HEADS, MLA_ROPE))], axis=-1)
    q, k, mv = rms_norm(q, p['mla_g_q'][j]), rms_norm(k, p['mla_g_k'][j]), kv[..., MLA_NOPE:]

    def rope_tail(x):
        tail = jnp.concatenate([x[:, :c, :, MLA_NOPE:], rope(x[:, c:, :, MLA_NOPE:], cos, sin)], axis=1)
        return jnp.concatenate([x[..., :MLA_NOPE], tail], axis=-1)

    q, k = rope_tail(q), rope_tail(k)
    qh, kh, vh = heads_first(q), heads_first(k), heads_first(mv)
    mla_scale = MLA_QK ** -0.5
    mla_l = heads_first(attention(qh[:, :, c:], kh, vh, mla_scale)).reshape(b, n, MLA_HEADS * MLA_V)
    nq = heads_first(rms_norm(nq.reshape(b, t, NA_HEADS, HEAD_DIM), p['na_g_q'][j]))
    nk = heads_first(rms_norm(nk.reshape(b, t, NA_HEADS, HEAD_DIM), p['na_g_k'][j]))
    nv = heads_first(nv.reshape(b, t, NA_HEADS, HEAD_DIM))
    na_scale = HEAD_DIM ** -0.5
    na_l = na_attention(nq[:, :, c:], nk[:, :, c:], nv[:, :, c:], nk[:, :, :c], nv[:, :, :c], na_bias_table(p['na_rpb'][j]),
                        na_scale)
    na_l = heads_first(na_l).reshape(b, n, NA_W)
    mix_l = jnp.concatenate([mla_l, na_l], axis=-1)
    if not need_ctx:
        return tokens_matmul(mix_l, p['o_w_out'][j]), None
    mla_c = heads_first(attention(qh[:, :, :c], kh[:, :, :c], vh[:, :, :c], mla_scale)).reshape(b, c, MLA_HEADS * MLA_V)
    na_c = heads_first(attention(nq[:, :, :c], nk[:, :, :c], nv[:, :, :c], na_scale)).reshape(b, c, NA_W)
    mix = jnp.concatenate([jnp.concatenate([mla_c, na_c], axis=-1), mix_l], axis=1)
    out = tokens_matmul(mix, p['o_w_out'][j])
    return out[:, c:], out[:, :c]


def mlp(h, w1, w2):
    b, n, k = h.shape
    return mlp_rows(h.reshape(b * n, k), w1, w2).reshape(b, n, w2.shape[1])


def local_loss(x, p, m_lat, m_ctx, ctx, target):
    depth = m_lat.shape[0]
    c = ctx.shape[1]
    xc = ctx
    for i in range(depth):
        need_ctx = i < depth - 1
        j = i // 2
        ml = [m_lat[i, :, s][:, None, :] for s in range(N_MOD)]
        mc = [m_ctx[i, s][None, None, :] for s in range(N_MOD)]
        a_lat = modulate(x, p['g_norm1'][i], ml[0], ml[1])
        a_ctx = modulate(xc, p['g_norm1'][i], mc[0], mc[1])
        mixer = even_mixer if i % 2 == 0 else odd_mixer
        o_lat, o_ctx = mixer(a_lat, a_ctx, p, j, need_ctx)
        x = x + ml[2] * o_lat
        h_lat = modulate(x, p['g_norm2'][i], ml[3], ml[4])
        if need_ctx:
            xc = xc + mc[2] * o_ctx
            h_ctx = modulate(xc, p['g_norm2'][i], mc[3], mc[4])
            ff = mlp(jnp.concatenate([h_ctx, h_lat], axis=1), p['w_ff1'][i], p['w_ff2'][i])
            x = x + ml[5] * ff[:, c:]
            xc = xc + mc[5] * ff[:, :c]
        else:
            x = x + ml[5] * mlp(h_lat, p['w_ff1'][i], p['w_ff2'][i])
    return 0.5 * jnp.sum(jnp.mean(jnp.square(x - target), axis=-1))


def _packed_rows(size, layout):
    width, group = layout
    return -(-size // (width * group)) * group


def _pack_rows(flat, layout):
    width = layout[0]
    rows = _packed_rows(flat.shape[-1], layout)
    flat = jnp.pad(flat, [(0, 0)] * (flat.ndim - 1) + [(0, rows * width - flat.shape[-1])])
    return flat.reshape(flat.shape[:-1] + (rows, width))


def _unpack_rows(rows, shape):
    lead = rows.shape[:-2]
    return rows.reshape(lead + (-1,))[..., :math.prod(shape)].reshape(lead + tuple(shape))


def _unpack_all(packed, shapes, layout):
    out, at = [], 0
    for shape in shapes:
        rows = _packed_rows(math.prod(shape), layout)
        out.append(_unpack_rows(packed[..., at:at + rows, :], shape))
        at += rows
    return out


def _join_shards(g, axis):
    g = jnp.moveaxis(g, 0, axis)
    return g.reshape(g.shape[:axis] + (N_DEV * g.shape[axis + 1],) + g.shape[axis + 2:])


def _split_shards(full, axis):
    s = full.shape
    return jnp.moveaxis(full.reshape(s[:axis] + (N_DEV, s[axis] // N_DEV) + s[axis + 1:]), axis, 0)


def _gather_packed(parts, dtype, layout, name):
    packed = jnp.concatenate([_pack_rows(a.astype(dtype).reshape(-1), layout) for a in parts], axis=0)
    return _unpack_all(_exchange_many([packed], [None], True, name)[0], [a.shape for a in parts], layout)


def kernel(x, c, ctx, c_ctx, w_mod, b_mod, g_norm1, g_norm2, w_ff1, w_ff2, e_w_in, e_w_out, e_g_q, e_g_k, ssm_lam_re, ssm_lam_im, ssm_log_dt, ssm_b_re, ssm_b_im, ssm_c_re, ssm_c_im, ssm_d, ssm_w_glu, ssm_b_glu, o_w_in, o_w_out, mla_g_cq, mla_g_ckv, mla_w_uq, mla_w_ukv, mla_g_q, mla_g_k, na_g_q, na_g_k, na_rpb, loss_target, m_c_ctx, m_w_mod, m_b_mod, m_g_norm1, m_g_norm2, m_w_ff1, m_w_ff2, m_e_w_in, m_e_w_out, m_e_g_q, m_e_g_k, m_ssm_lam_re, m_ssm_lam_im, m_ssm_log_dt, m_ssm_b_re, m_ssm_b_im, m_ssm_c_re, m_ssm_c_im, m_ssm_d, m_ssm_w_glu, m_ssm_b_glu, m_o_w_in, m_o_w_out, m_mla_g_cq, m_mla_g_ckv, m_mla_w_uq, m_mla_w_ukv, m_mla_g_q, m_mla_g_k, m_na_g_q, m_na_g_k, m_na_rpb, v_c_ctx, v_w_mod, v_b_mod, v_g_norm1, v_g_norm2, v_w_ff1, v_w_ff2, v_e_w_in, v_e_w_out, v_e_g_q, v_e_g_k, v_ssm_lam_re, v_ssm_lam_im, v_ssm_log_dt, v_ssm_b_re, v_ssm_b_im, v_ssm_c_re, v_ssm_c_im, v_ssm_d, v_ssm_w_glu, v_ssm_b_glu, v_o_w_in, v_o_w_out, v_mla_g_cq, v_mla_g_ckv, v_mla_w_uq, v_mla_w_ukv, v_mla_g_q, v_mla_g_k, v_na_g_q, v_na_g_k, v_na_rpb):
    given = dict(locals())
    x, c, ctx, target = given['x'], given['c'], given['ctx'], given['loss_target']
    b_loc, _, d_model = x.shape
    depth = given['w_mod'].shape[0]
    ix, iy, ic = lax.axis_index('x'), lax.axis_index('y'), lax.axis_index('c')
    me = 4 * ix + 2 * iy + ic
    n_batch = N_DEV * b_loc
    mod_w = given['w_mod'].shape[2]

    c_rows = jnp.concatenate([c, jnp.zeros((8 - b_loc, d_model), F32)], axis=0)
    small = _gather_packed([c_rows] + [given[n] for n in SHARDED_SMALL], F32, PACK_SMALL,'gather_small')
    c_all = small[0][:, :b_loc].reshape(n_batch, d_model)
    full = {n: _join_shards(g, SHARDED_SMALL[n]) for n, g in zip(SHARDED_SMALL, small[1:])}
    cuts = {n: (a if given[n].shape[a] % (16 if a == 1 else LANES) == 0 else None) for n, a in BIG.items()}
    big = _exchange_many([given[n].astype(BF16) for n in BIG], [cuts[n] for n in BIG], True, 'gather_weights')
    for n, g in zip(BIG, big):
        g = g if cuts[n] is not None else _join_shards(g, BIG[n])
        full[n] = [g[i] for i in range(g.shape[0])]
    c3 = MLA_Q_RANK + MLA_KV_RANK + MLA_ROPE
    full['o_w_in'] = [jnp.concatenate([w[:, :c3], jnp.zeros((w.shape[0], ODD_NA_AT - c3), BF16), w[:, c3:]], axis=-1)
                      for w in full['o_w_in']]
    for n in REPLICATED:
        full[n] = given[n]

    rows17 = 16 * (-(-(n_batch + 1) // 16))
    cond = jnp.concatenate([jax.nn.silu(c_all), jax.nn.silu(given['c_ctx'])[None],
                            jnp.zeros((rows17 - n_batch - 1, d_model), F32)], axis=0)
    mod_mine = jnp.stack([_matmul(cond, given['w_mod'][i], 'nn', F32) for i in range(depth)])
    b_mine = lax.dynamic_slice_in_dim(given['b_mod'], me * mod_w, mod_w, axis=1)
    mod_mine = mod_mine + b_mine[:, None, :]
    mod_all = _gather_packed([mod_mine], F32, PACK_SMALL,'gather_mod')[0]
    mod_all = jnp.moveaxis(mod_all, 0, 2).reshape(depth, rows17, N_MOD, d_model)
    m_lat = lax.dynamic_slice_in_dim(mod_all, me * b_loc, b_loc, axis=1)
    m_ctx = mod_all[:, n_batch]

    diff = {n: full[n] for n in list(BIG) + list(SHARDED_SMALL) + REPLICATED}
    loss, (g_x, g_p, g_ml, g_mc) = jax.value_and_grad(local_loss, argnums=(0, 1, 2, 3))(x, diff, m_lat, m_ctx, ctx, target)
    loss = lax.psum(loss, ('x', 'y', 'c'))
    g_p['o_w_in'] = [jnp.concatenate([g[:, :c3], g[:, ODD_NA_AT:]], axis=-1) for g in g_p['o_w_in']]

    g_rows = jnp.concatenate([g_ml.reshape(depth, b_loc, N_MOD * d_model), g_mc.reshape(depth, 1, N_MOD * d_model),
                              jnp.zeros((depth, 8 - b_loc - 1, N_MOD * d_model), F32)], axis=1)
    g_mod_all = _gather_packed([g_rows], F32, PACK_SMALL,'gather_mod_grads')[0]
    g_lat_all = jnp.moveaxis(g_mod_all[:, :, :b_loc], 0, 1).reshape(depth, n_batch, N_MOD * d_model)
    g_ctx_all = g_mod_all[0, :, b_loc]
    for dev in range(1, N_DEV):
        g_ctx_all = g_ctx_all + g_mod_all[dev, :, b_loc]
    g_mod17 = jnp.concatenate([g_lat_all, g_ctx_all[:, None], jnp.zeros((depth, rows17 - n_batch - 1, N_MOD * d_model), F32)],
                              axis=1)
    grad_b_mod = jnp.sum(g_mod17, axis=1)
    g_mod_mine = lax.dynamic_slice_in_dim(g_mod17, me * mod_w, mod_w, axis=2)
    grad_w_mod = jnp.stack([_matmul(cond, g_mod_mine[i], 'tn', F32) for i in range(depth)])
    d_cond = _matmul(g_mod_mine[0], given['w_mod'][0], 'nt', F32)
    for i in range(1, depth):
        d_cond = d_cond + _matmul(g_mod_mine[i], given['w_mod'][i], 'nt', F32)
    d_cond_ctx = d_cond[n_batch]

    small_names = REPLICATED + list(SHARDED_SMALL)
    parts = [d_cond_ctx] + [g_p[n] for n in small_names]
    packed = jnp.concatenate([_pack_rows(a.reshape(-1), PACK_SMALL) for a in parts], axis=0)
    summed = _unpack_all(_all_reduce_small(packed), [a.shape for a in parts], PACK_SMALL)
    grads = dict(zip(['c_ctx'] + small_names, summed))
    c_ctx = given['c_ctx']
    sig = jax.nn.sigmoid(c_ctx)
    grads['c_ctx'] = grads['c_ctx'] * (sig * (1 + c_ctx * (1 - sig)))
    for n, axis in SHARDED_SMALL.items():
        width = given[n].shape[axis]
        grads[n] = lax.dynamic_slice_in_dim(grads[n], me * width, width, axis=axis)
    grads['w_mod'], grads['b_mod'] = grad_w_mod, grad_b_mod

    stacked = [jnp.stack(g_p[n]) for n in BIG]
    stacked = [g if cuts[n] is not None else _split_shards(g, BIG[n]) for n, g in zip(BIG, stacked)]
    landed = _exchange_many(stacked, [cuts[n] for n in BIG], False, 'scatter_weight_grads')

    upd = {}
    for n, slots in zip(BIG, landed):
        grads[n], *upd[n] = _adamw_landed(slots, given[n], given['m_' + n], given['v_' + n])
    upd['w_mod'] = _adamw_big(given['w_mod'], grads['w_mod'], given['m_w_mod'], given['v_w_mod'])
    rest = [n for n in WEIGHTS if n not in upd]
    out = _adamw_small([given[n] for n in rest], [grads[n] for n in rest], [given['m_' + n] for n in rest],
                       [given['v_' + n] for n in rest])
    upd.update(dict(zip(rest, out)))
    return (loss, g_x, *[grads[n] for n in WEIGHTS], *[upd[n][0] for n in WEIGHTS], *[upd[n][1] for n in WEIGHTS],
            *[upd[n][2] for n in WEIGHTS])
```

```python
import functools
import math

import jax
import jax.numpy as jnp
from jax import lax
from jax.experimental import pallas as pl
from jax.experimental.pallas import tpu as pltpu

F32, BF16 = jnp.float32, jnp.bfloat16
MESH = pl.DeviceIdType.MESH
N_DEV = 8
VMEM_LIMIT_BYTES = 56 * 1024 * 1024
MM_TILE_BYTES = 6 * 1024 * 1024
LANES = 128
PACK_SMALL = (128, 8)

GRID_W = 64
HEAD_DIM = 64
ROPE_BASE = 10000.0
EPS = 1e-6
N_MOD = 6
GQA_Q_HEADS, GQA_KV_HEADS = 12, 4
GQA_Q_W, GQA_KV_W = GQA_Q_HEADS * HEAD_DIM, GQA_KV_HEADS * HEAD_DIM
SSM_WIDTH, SSM_GROUP, SSM_GROUPS, SSM_STATE = 256, 16, 16, 64
MLA_HEADS, MLA_Q_RANK, MLA_KV_RANK, MLA_NOPE, MLA_ROPE, MLA_V = 8, 512, 256, 64, 32, 64
MLA_QK = MLA_NOPE + MLA_ROPE
NA_HEADS, NA_WIN_R, NA_WIN_C = 8, 8, 16
NA_W = NA_HEADS * HEAD_DIM
ODD_IN_W = MLA_Q_RANK + MLA_KV_RANK + MLA_ROPE + 3 * NA_W
ODD_NA_AT = 1024
ODD_IN_PAD = ODD_NA_AT + 3 * NA_W
NEG = -1e30

ADAM_LR, ADAM_B1, ADAM_B2, ADAM_EPS, ADAM_WD, ADAM_STEP = 0.001, 0.9, 0.999, 1e-08, 0.01, 10

FWD_PARAMS = ['x', 'c', 'ctx', 'c_ctx', 'w_mod', 'b_mod', 'g_norm1', 'g_norm2', 'w_ff1', 'w_ff2', 'e_w_in', 'e_w_out',
              'e_g_q', 'e_g_k', 'ssm_lam_re', 'ssm_lam_im', 'ssm_log_dt', 'ssm_b_re', 'ssm_b_im', 'ssm_c_re', 'ssm_c_im',
              'ssm_d', 'ssm_w_glu', 'ssm_b_glu', 'o_w_in', 'o_w_out', 'mla_g_cq', 'mla_g_ckv', 'mla_w_uq', 'mla_w_ukv',
              'mla_g_q', 'mla_g_k', 'na_g_q', 'na_g_k', 'na_rpb']
WEIGHTS = FWD_PARAMS[3:]
BIG = {'w_ff1': 2, 'w_ff2': 1, 'e_w_in': 2, 'e_w_out': 1, 'o_w_in': 2, 'o_w_out': 1, 'mla_w_uq': 2, 'mla_w_ukv': 2,
       'ssm_w_glu': 1}
SHARDED_SMALL = {'mla_g_cq': 1, 'mla_g_ckv': 1}
REPLICATED = [n for n in WEIGHTS if n not in BIG and n not in SHARDED_SMALL and n not in ('w_mod', 'c_ctx', 'b_mod')]


def _tile(dim, prefs):
    for p in prefs:
        if dim >= p and dim % p == 0:
            return p
    return dim


def _params(*sem):
    return pltpu.CompilerParams(dimension_semantics=sem, vmem_limit_bytes=VMEM_LIMIT_BYTES)


def _dot_nt(a, b):
    return lax.dot_general(a, b, (((1,), (1,)), ((), ())), preferred_element_type=F32)


def _dot_tn(a, b):
    return lax.dot_general(a, b, (((0,), (0,)), ((), ())), preferred_element_type=F32)


def _dot(a, b):
    return jnp.dot(a, b, preferred_element_type=F32)


def _matmul(a, b, kind, out_dtype, finish=None, extra=None, n_out=1):
    a, b = a.astype(BF16), b.astype(BF16)
    if kind == 'nn':
        (m, kd), n = a.shape, b.shape[1]
    elif kind == 'nt':
        (m, kd), n = a.shape, b.shape[0]
    else:
        (kd, m), n = a.shape, b.shape[1]
    if kind == 'tn':
        tm = m if m <= 1024 else _tile(m, (1024, 768, 512, 256, 128))
        tn = _tile(n, (1024, 768, 512, 256, 128))
    else:
        tn = n if kd * n * 2 <= MM_TILE_BYTES else _tile(n, (1024, 768, 512, 256, 128))
        tm = _tile(m, [t for t in (1536, 1024, 768, 512, 256, 128) if t * tn * 4 <= MM_TILE_BYTES])
    whole = kind != 'tn' and tn == n and kd * n * 2 <= MM_TILE_BYTES
    tk = kd if whole else _tile(kd, [t for t in (2048, 1536, 1024, 512, 256, 128) if t * max(tm, tn) * 4 <= MM_TILE_BYTES])
    nk = kd // tk
    dn = {'nn': (((1,), (0,)), ((), ())), 'nt': (((1,), (1,)), ((), ())), 'tn': (((0,), (0,)), ((), ()))}[kind]

    n_in = 2 if extra is None else 3

    def body(*refs):
        a_ref, b_ref = refs[:2]
        o_refs = refs[n_in:n_in + n_out]

        def store(total):
            outs = (total,) if finish is None else finish(total, refs[2][...] if extra is not None else None)
            for o_ref, val in zip(o_refs, outs):
                o_ref[...] = val.astype(o_ref.dtype)

        part = lax.dot_general(a_ref[...], b_ref[...], dn, preferred_element_type=F32)
        if nk == 1:
            store(part)
            return
        acc_ref, k = refs[n_in + n_out], pl.program_id(2)

        @pl.when(k == 0)
        def _():
            acc_ref[...] = part

        @pl.when((k > 0) & (k < nk - 1))
        def _():
            acc_ref[...] += part

        @pl.when(k == nk - 1)
        def _():
            store(acc_ref[...] + part)

    a_spec = pl.BlockSpec((tk, tm), lambda i, j, k: (k, i)) if kind == 'tn' else pl.BlockSpec((tm, tk), lambda i, j, k: (i, k))
    b_spec = pl.BlockSpec((tn, tk), lambda i, j, k: (j, k)) if kind == 'nt' else pl.BlockSpec((tk, tn), lambda i, j, k: (k, j))
    o_spec = pl.BlockSpec((tm, tn), lambda i, j, k: (i, j))
    out = pl.pallas_call(
        body, name='mm_' + kind, grid=(m // tm, n // tn, nk),
        out_shape=[jax.ShapeDtypeStruct((m, n), out_dtype)] * n_out,
        in_specs=[a_spec, b_spec] + ([o_spec] if extra is not None else []), out_specs=[o_spec] * n_out,
        scratch_shapes=[pltpu.VMEM((tm, tn), F32)] if nk > 1 else [],
        compiler_params=_params('parallel', 'parallel', 'arbitrary'),
    )(*((a, b) if extra is None else (a, b, extra)))
    return out[0] if n_out == 1 else out


@jax.custom_vjp
def linear(a, w):
    return _matmul(a, w, 'nn', F32)


def _linear_fwd(a, w):
    ab = a.astype(BF16)
    return _matmul(ab, w, 'nn', F32), (ab, w)


def _linear_bwd(res, g):
    ab, w = res
    gb = g.astype(BF16)
    return _matmul(gb, w, 'nt', F32), _matmul(ab, gb, 'tn', w.dtype)


linear.defvjp(_linear_fwd, _linear_bwd)


def _relu2(z, _):
    r = jnp.maximum(z, 0.0)
    return r, r * r


def _relu2_grad(d_act, r):
    return (d_act * (2.0 * r.astype(F32)),)


@jax.custom_vjp
def mlp_rows(h, w1, w2):
    return _mlp_rows_fwd(h, w1, w2)[0]


def _mlp_rows_fwd(h, w1, w2):
    hb = h.astype(BF16)
    r, act = _matmul(hb, w1, 'nn', BF16, finish=_relu2, n_out=2)
    return _matmul(act, w2, 'nn', F32), (hb, w1, w2, r, act)


def _mlp_rows_bwd(res, g):
    hb, w1, w2, r, act = res
    gb = g.astype(BF16)
    dz = _matmul(gb, w2, 'nt', BF16, finish=_relu2_grad, extra=r)
    return _matmul(dz, w1, 'nt', F32), _matmul(hb, dz, 'tn', w1.dtype), _matmul(act, gb, 'tn', w2.dtype)


mlp_rows.defvjp(_mlp_rows_fwd, _mlp_rows_bwd)


LOG2E = math.log2(math.e)


def _softmax_rows(t):
    m = jnp.max(t, axis=-1, keepdims=True)
    e = jnp.exp2(t - m)
    return e * (1.0 / jnp.sum(e, axis=-1, keepdims=True))


ATTN_SPLIT = 2


def _attn_specs(q, k, v, bq):
    _, h, nq, dq = q.shape
    _, hk, nk, dv = v.shape
    g = h // hk
    q_spec = pl.BlockSpec((None, None, bq, dq), lambda b, j, gi, i: (b, j * g + gi, i, 0))
    k_spec = pl.BlockSpec((None, None, nk, dq), lambda b, j, gi, i: (b, j, 0, 0))
    v_spec = pl.BlockSpec((None, None, nk, dv), lambda b, j, gi, i: (b, j, 0, 0))
    o_spec = pl.BlockSpec((None, None, bq, dv), lambda b, j, gi, i: (b, j * g + gi, i, 0))
    t = dict(q=pl.BlockSpec((None, None, dq, bq), lambda b, j, gi, i: (b, j * g + gi, 0, i)),
             o=pl.BlockSpec((None, None, dv, bq), lambda b, j, gi, i: (b, j * g + gi, 0, i)),
             k=pl.BlockSpec((None, None, dq, nk), lambda b, j, gi, i: (b, j, 0, 0)),
             v=pl.BlockSpec((None, None, dv, nk), lambda b, j, gi, i: (b, j, 0, 0)))
    return (q.shape[0], hk, g, nq // bq), q_spec, k_spec, v_spec, o_spec, t


def _attn_blocks(nq):
    bq = _tile(nq, (512, 256, 128))
    return bq, [pl.ds(s * (bq // ATTN_SPLIT), bq // ATTN_SPLIT) for s in range(ATTN_SPLIT)]


def _attn_fwd_call(q, k, vt, scale):
    b, h, nq, _ = q.shape
    dv = vt.shape[2]
    bq, subs = _attn_blocks(nq)
    grid, q_spec, k_spec, _, _, t = _attn_specs(q, k, jnp.swapaxes(vt, 2, 3), bq)

    def body(q_ref, k_ref, vt_ref, ot_ref):
        kb, vtb = k_ref[...], vt_ref[...]
        for rows in subs:
            p = _softmax_rows(_dot_nt(q_ref[rows, :], kb) * (scale * LOG2E))
            ot_ref[:, rows] = _dot_nt(vtb, p.astype(BF16))

    return pl.pallas_call(
        body, name='attn_fwd', grid=grid, out_shape=jax.ShapeDtypeStruct((b, h, dv, nq), F32),
        in_specs=[q_spec, k_spec, t['v']], out_specs=t['o'],
        compiler_params=_params('parallel', 'parallel', 'arbitrary', 'arbitrary'),
    )(q, k, vt)


def _attn_bwd_call(q, k, kt, v, do, scale):
    b, h, nq, dq = q.shape
    bq, subs = _attn_blocks(nq)
    grid, q_spec, k_spec, v_spec, o_spec, t = _attn_specs(q, k, v, bq)

    def body(q_ref, k_ref, kt_ref, v_ref, do_ref, dqt_ref, dk_ref, dv_ref):
        @pl.when((pl.program_id(2) == 0) & (pl.program_id(3) == 0))
        def _():
            dk_ref[...] = jnp.zeros_like(dk_ref)
            dv_ref[...] = jnp.zeros_like(dv_ref)

        kb, ktb, vb = k_ref[...], kt_ref[...], v_ref[...]
        dk, dv = [], []
        for rows in subs:
            qb, dob = q_ref[rows, :], do_ref[rows, :]
            p = _softmax_rows(_dot_nt(qb, kb) * (scale * LOG2E))
            dp = _dot_nt(dob, vb)
            ds = p * (dp - jnp.sum(p * dp, axis=-1, keepdims=True))
            dsb = (ds * scale).astype(BF16)
            dqt_ref[:, rows] = _dot_nt(ktb, dsb)
            dk.append(_dot_tn(dsb, qb))
            dv.append(_dot_tn(p.astype(BF16), dob))
        dk_ref[...] += sum(dk[1:], dk[0])
        dv_ref[...] += sum(dv[1:], dv[0])

    return pl.pallas_call(
        body, name='attn_bwd', grid=grid,
        out_shape=(jax.ShapeDtypeStruct((b, h, dq, nq), F32), jax.ShapeDtypeStruct(k.shape, F32), jax.ShapeDtypeStruct(v.shape, F32)),
        in_specs=[q_spec, k_spec, t['k'], v_spec, o_spec], out_specs=(t['q'], k_spec, v_spec),
        compiler_params=_params('parallel', 'parallel', 'arbitrary', 'arbitrary'),
    )(q, k, kt, v, do)


@functools.partial(jax.custom_vjp, nondiff_argnums=(3,))
def attention(q, k, v, scale):
    return _attention_fwd(q, k, v, scale)[0]


def _attention_fwd(q, k, v, scale):
    qb, kb, vb = q.astype(BF16), k.astype(BF16), v.astype(BF16)
    return jnp.swapaxes(_attn_fwd_call(qb, kb, jnp.swapaxes(vb, 2, 3), scale), 2, 3), (qb, kb, vb)


def _attention_bwd(scale, res, g):
    qb, kb, vb = res
    dqt, dk, dv = _attn_bwd_call(qb, kb, jnp.swapaxes(kb, 2, 3), vb, g.astype(BF16), scale)
    return jnp.swapaxes(dqt, 2, 3), dk, dv


attention.defvjp(_attention_fwd, _attention_bwd)


def _na_window(r, rows):
    start = jnp.clip(r - NA_WIN_R // 2, 0, rows - NA_WIN_R)
    return start, r - start


NA_TOGETHER = 4


def _na_group(r0, rows):
    out = []
    for g in range(NA_TOGETHER):
        start, off = _na_window(r0 + g, rows)
        out.append((pl.ds(pl.multiple_of(start * GRID_W, GRID_W), NA_WIN_R * GRID_W), off))
    return out


def _na_rows(x, g):
    return x[g * GRID_W:(g + 1) * GRID_W]


def _na_scores(qs, kws, kc, biases, scale):
    n = len(kws)
    s1 = jnp.stack([_dot_nt(_na_rows(qs, g), kws[g]) * scale + biases[g] for g in range(n)])
    s2 = (_dot_nt(qs, kc) * scale).reshape(n, GRID_W, kc.shape[0])
    m = jnp.maximum(jnp.max(s1, axis=-1, keepdims=True), jnp.max(s2, axis=-1, keepdims=True))
    e1, e2 = jnp.exp(s1 - m), jnp.exp(s2 - m)
    inv = 1.0 / (jnp.sum(e1, axis=-1, keepdims=True) + jnp.sum(e2, axis=-1, keepdims=True))
    return e1 * inv, e2 * inv


def _na_specs(q, kc):
    _, _, n, d = q.shape
    c = kc.shape[2]
    win = NA_WIN_R * GRID_W
    tok = pl.BlockSpec((None, None, n, d), lambda b, h: (b, h, 0, 0))
    ctx = pl.BlockSpec((None, None, c, d), lambda b, h: (b, h, 0, 0))
    bias = pl.BlockSpec((None, NA_WIN_R, GRID_W, win), lambda b, h: (h, 0, 0, 0))
    dbias = pl.BlockSpec((None, None, NA_WIN_R, GRID_W, win), lambda b, h: (b, h, 0, 0, 0))
    return tok, ctx, bias, dbias


def _na_fwd_call(q, k, v, kc, vc, bias, scale):
    b, h, n, d = q.shape
    rows, span = n // GRID_W, NA_TOGETHER * GRID_W
    tok, ctx, bias_spec, _ = _na_specs(q, kc)

    def body(q_ref, k_ref, v_ref, kc_ref, vc_ref, b_ref, o_ref):
        def step(i, carry):
            at = pl.ds(pl.multiple_of(i * span, span), span)
            wins = _na_group(i * NA_TOGETHER, rows)
            p1, p2 = _na_scores(q_ref[at, :], [k_ref[w, :] for w, _ in wins], kc_ref[...], [b_ref[off] for _, off in wins],
                                scale)
            p1, p2 = p1.astype(BF16), p2.astype(BF16)
            local = jnp.concatenate([_dot(p1[g], v_ref[w, :]) for g, (w, _) in enumerate(wins)], axis=0)
            o_ref[at, :] = local + _dot(p2.reshape(span, p2.shape[2]), vc_ref[...])
            return carry

        lax.fori_loop(0, rows // NA_TOGETHER, step, 0)

    return pl.pallas_call(
        body, name='na_fwd', grid=(b, h), out_shape=jax.ShapeDtypeStruct(q.shape, F32),
        in_specs=[tok, tok, tok, ctx, ctx, bias_spec], out_specs=tok,
        compiler_params=_params('parallel', 'parallel'),
    )(q, k, v, kc, vc, bias)


def _na_bwd_call(q, k, v, kc, vc, bias, do, scale):
    b, h, n, d = q.shape
    rows, span = n // GRID_W, NA_TOGETHER * GRID_W
    tok, ctx, bias_spec, dbias_spec = _na_specs(q, kc)

    def body(q_ref, k_ref, v_ref, kc_ref, vc_ref, b_ref, do_ref, dq_ref, dk_ref, dv_ref, dkc_ref, dvc_ref, db_ref):
        for ref in (dk_ref, dv_ref, dkc_ref, dvc_ref, db_ref):
            ref[...] = jnp.zeros_like(ref)

        def step(i, carry):
            at = pl.ds(pl.multiple_of(i * span, span), span)
            wins = _na_group(i * NA_TOGETHER, rows)
            qs, dob, kcb, vcb = q_ref[at, :], do_ref[at, :], kc_ref[...], vc_ref[...]
            kws, vws = [k_ref[w, :] for w, _ in wins], [v_ref[w, :] for w, _ in wins]
            p1, p2 = _na_scores(qs, kws, kcb, [b_ref[off] for _, off in wins], scale)
            dp1 = jnp.stack([_dot_nt(_na_rows(dob, g), vws[g]) for g in range(NA_TOGETHER)])
            dp2 = _dot_nt(dob, vcb).reshape(p2.shape)
            delta = jnp.sum(p1 * dp1, axis=-1, keepdims=True) + jnp.sum(p2 * dp2, axis=-1, keepdims=True)
            ds1, ds2 = p1 * (dp1 - delta), p2 * (dp2 - delta)
            ds1b, p1b = (ds1 * scale).astype(BF16), p1.astype(BF16)
            ds2b = (ds2 * scale).astype(BF16).reshape(span, p2.shape[2])
            p2b = p2.astype(BF16).reshape(span, p2.shape[2])
            dq_ref[at, :] = jnp.concatenate([_dot(ds1b[g], kws[g]) for g in range(NA_TOGETHER)], axis=0) + _dot(ds2b, kcb)
            for g, (w, off) in enumerate(wins):
                db_ref[off] += ds1[g]
                dk_ref[w, :] += _dot_tn(ds1b[g], _na_rows(qs, g))
                dv_ref[w, :] += _dot_tn(p1b[g], _na_rows(dob, g))
            dkc_ref[...] += _dot_tn(ds2b, qs)
            dvc_ref[...] += _dot_tn(p2b, dob)
            return carry

        lax.fori_loop(0, rows // NA_TOGETHER, step, 0)

    f = lambda a: jax.ShapeDtypeStruct(a.shape, F32)
    return pl.pallas_call(
        body, name='na_bwd', grid=(b, h),
        out_shape=(f(q), f(k), f(v), f(kc), f(vc), jax.ShapeDtypeStruct((b,) + bias.shape, F32)),
        in_specs=[tok, tok, tok, ctx, ctx, bias_spec, tok], out_specs=(tok, tok, tok, ctx, ctx, dbias_spec),
        compiler_params=_params('parallel', 'parallel'),
    )(q, k, v, kc, vc, bias, do)


@functools.partial(jax.custom_vjp, nondiff_argnums=(6,))
def na_attention(q, k, v, kc, vc, bias, scale):
    return _na_fwd_call(q.astype(BF16), k.astype(BF16), v.astype(BF16), kc.astype(BF16), vc.astype(BF16), bias, scale)


def _na_attention_fwd(q, k, v, kc, vc, bias, scale):
    res = (q.astype(BF16), k.astype(BF16), v.astype(BF16), kc.astype(BF16), vc.astype(BF16), bias)
    return _na_fwd_call(*res, scale), res


def _na_attention_bwd(scale, res, g):
    dq, dk, dv, dkc, dvc, db = _na_bwd_call(*res, g.astype(BF16), scale)
    return dq, dk, dv, dkc, dvc, jnp.sum(db, axis=0)


na_attention.defvjp(_na_attention_fwd, _na_attention_bwd)


def _na_table_index():
    qcol = jnp.arange(GRID_W)
    kcol = jnp.arange(GRID_W)
    cstart = jnp.clip(qcol - NA_WIN_C // 2, 0, GRID_W - NA_WIN_C)
    inside = (kcol[None, :] >= cstart[:, None]) & (kcol[None, :] < cstart[:, None] + NA_WIN_C)
    cidx = jnp.clip(kcol[None, :] - qcol[:, None] + (NA_WIN_C - 1), 0, 2 * NA_WIN_C - 2)
    ridx = jnp.arange(NA_WIN_R)[None, :] - jnp.arange(NA_WIN_R)[:, None] + (NA_WIN_R - 1)
    return inside, cidx, ridx


@jax.custom_vjp
def na_bias_table(rpb):
    inside, pick_c, pick_r = _na_table_picks()
    rows = jnp.einsum('hab,oja->hojb', rpb, pick_r, precision=lax.Precision.HIGHEST)
    t = jnp.einsum('hojb,qkb->hoqjk', rows, pick_c, precision=lax.Precision.HIGHEST)
    t = jnp.where(inside[None, None, :, None, :], t, NEG)
    return t.reshape(rpb.shape[0], NA_WIN_R, GRID_W, NA_WIN_R * GRID_W)


def _na_table_picks():
    inside, cidx, ridx = _na_table_index()
    pick_c = ((cidx[..., None] == jnp.arange(2 * NA_WIN_C - 1)) & inside[..., None]).astype(F32)
    pick_r = (ridx[..., None] == jnp.arange(2 * NA_WIN_R - 1)).astype(F32)
    return inside, pick_c, pick_r


def _na_bias_table_bwd(_, dt):
    _, pick_c, pick_r = _na_table_picks()
    d5 = dt.reshape(dt.shape[0], NA_WIN_R, GRID_W, NA_WIN_R, GRID_W)
    part = jnp.einsum('hoqjk,qkb->hojb', d5, pick_c, precision=lax.Precision.HIGHEST)
    return (jnp.einsum('hojb,oja->hab', part, pick_r, precision=lax.Precision.HIGHEST),)


na_bias_table.defvjp(lambda rpb: (na_bias_table(rpb), None), _na_bias_table_bwd)


S5_HALF = SSM_WIDTH // 2
S5_LANES = (SSM_GROUPS // 2) * SSM_STATE
S5_Q = S5_LANES // LANES


def _s5_tiles(a):
    r = a.shape[0]
    return jnp.transpose(a.reshape(r, S5_Q, LANES), (1, 0, 2)).reshape(S5_Q * r, LANES)


def _s5_untiles(a):
    r = a.shape[0] // S5_Q
    return jnp.transpose(a.reshape(S5_Q, r, LANES), (1, 0, 2)).reshape(r, S5_LANES)


def _s5_put(ref, r, rr, tc, val):
    for q in range(S5_Q):
        ref[pl.ds((q * rr + r) * tc, tc), :] = val[:, q * LANES:(q + 1) * LANES]


def _s5_get(ref, r, rr, tc):
    return jnp.concatenate([ref[pl.ds((q * rr + r) * tc, tc), :] for q in range(S5_Q)], axis=1)


S5_BLOCK = 8
S5_TOGETHER = 8


def _s5_powers(a_re, a_im, backward):
    a_im = -a_im if backward else a_im
    pr, pi = [a_re], [a_im]
    for _ in range(S5_BLOCK - 1):
        pr, pi = pr + [pr[-1] * a_re - pi[-1] * a_im], pi + [pr[-1] * a_im + pi[-1] * a_re]
    order = range(S5_BLOCK - 1, -1, -1) if backward else range(S5_BLOCK)

    def table(p):
        rows = [jnp.broadcast_to(p[s - 1][:, None, :], (p[0].shape[0], S5_BLOCK, LANES)) for s in (1, 2, 4)]
        return jnp.stack(rows + [jnp.stack([p[t] for t in order], axis=1)], axis=1)

    return table(pr), table(pi)


def _s5_scan_block(xr, xi, pr_ref, pi_ref, chain, carry, backward):
    row = lax.broadcasted_iota(jnp.int32, (S5_BLOCK, LANES), 0)
    for e, s in enumerate((1, 2, 4)):
        ar, ai = pr_ref[chain, e], pi_ref[chain, e]
        keep = (row < S5_BLOCK - s) if backward else (row >= s)
        shift = S5_BLOCK - s if backward else s
        sr = jnp.where(keep, pltpu.roll(xr, shift, 0), 0.0)
        si = jnp.where(keep, pltpu.roll(xi, shift, 0), 0.0)
        xr, xi = xr + (ar * sr - ai * si), xi + (ar * si + ai * sr)
    ar, ai = pr_ref[chain, 3], pi_ref[chain, 3]
    cr, ci = carry
    xr, xi = xr + (ar * cr - ai * ci), xi + (ar * ci + ai * cr)
    edge = slice(0, 1) if backward else slice(S5_BLOCK - 1, S5_BLOCK)
    return xr, xi, (xr[edge], xi[edge])


def _s5_scan_chunk(xr_ref, xi_ref, pr_ref, pi_ref, sr_ref, si_ref, chains, tc, backward):
    blocks = tc // S5_BLOCK
    for first in range(0, chains, S5_TOGETHER):
        group = range(first, min(first + S5_TOGETHER, chains))

        def block(k, carries, group=group):
            j = blocks - 1 - k if backward else k
            out = []
            for chain, carry in zip(group, carries):
                at = pl.ds(pl.multiple_of(chain * tc + j * S5_BLOCK, S5_BLOCK), S5_BLOCK)
                xr, xi, carry = _s5_scan_block(xr_ref[at, :], xi_ref[at, :], pr_ref, pi_ref, chain, carry, backward)
                xr_ref[at, :] = xr
                xi_ref[at, :] = xi
                out.append(carry)
            return tuple(out)

        start = tuple((sr_ref[pl.ds(chain, 1), :], si_ref[pl.ds(chain, 1), :]) for chain in group)
        for chain, (cr, ci) in zip(group, lax.fori_loop(0, blocks, block, start)):
            sr_ref[pl.ds(chain, 1), :] = cr
            si_ref[pl.ds(chain, 1), :] = ci


def _s5_fwd_call(u, a_re, a_im, b_re, b_im, c_re, c_im):
    rr, t_len, _ = u.shape
    sets = b_re.shape[0]
    per = rr // sets
    tc = _tile(t_len, (256, 128))
    nt = t_len // tc
    qr = S5_Q * rr

    def body(u_ref, ar_ref, ai_ref, br_ref, bi_ref, cr_ref, ci_ref, y_ref, hr_ref, hi_ref, sr_ref, si_ref):
        @pl.when(pl.program_id(0) == 0)
        def _():
            sr_ref[...] = jnp.zeros_like(sr_ref)
            si_ref[...] = jnp.zeros_like(si_ref)

        for r in range(rr):
            ub = u_ref[r]
            _s5_put(hr_ref, r, rr, tc, _dot(ub, br_ref[r // per]))
            _s5_put(hi_ref, r, rr, tc, _dot(ub, bi_ref[r // per]))
        _s5_scan_chunk(hr_ref, hi_ref, ar_ref, ai_ref, sr_ref, si_ref, qr, tc, False)
        for r in range(rr):
            y_ref[r] = (_dot(_s5_get(hr_ref, r, rr, tc).astype(BF16), cr_ref[r // per])
                        - _dot(_s5_get(hi_ref, r, rr, tc).astype(BF16), ci_ref[r // per]))

    full = lambda a: pl.BlockSpec(a.shape, lambda i: (0,) * a.ndim)
    h_spec = pl.BlockSpec((None, qr * tc, LANES), lambda i: (i, 0, 0))
    h_shape = jax.ShapeDtypeStruct((nt, qr * tc, LANES), F32)
    a_re, a_im = _s5_powers(_s5_tiles(a_re), _s5_tiles(a_im), False)
    return pl.pallas_call(
        body, name='s5_fwd', grid=(nt,),
        out_shape=(jax.ShapeDtypeStruct((rr, t_len, S5_HALF), F32), h_shape, h_shape),
        in_specs=[pl.BlockSpec((rr, tc, S5_HALF), lambda i: (0, i, 0)), full(a_re), full(a_im), full(b_re), full(b_im),
                  full(c_re), full(c_im)],
        out_specs=(pl.BlockSpec((rr, tc, S5_HALF), lambda i: (0, i, 0)), h_spec, h_spec),
        scratch_shapes=[pltpu.VMEM((qr, LANES), F32), pltpu.VMEM((qr, LANES), F32)],
        compiler_params=_params('arbitrary'),
    )(u, a_re, a_im, b_re, b_im, c_re, c_im)


def _s5_bwd_call(u, a_re, a_im, b_re, b_im, c_re, c_im, h_re, h_im, dy):
    rr, t_len, _ = u.shape
    sets = b_re.shape[0]
    per = rr // sets
    nt, rows, _ = h_re.shape
    qr = S5_Q * rr
    tc = rows // qr

    def body(u_ref, dy_ref, ar_ref, ai_ref, br_ref, bi_ref, cr_ref, ci_ref, hr_ref, hi_ref,
             du_ref, dar_ref, dai_ref, dbr_ref, dbi_ref, dcr_ref, dci_ref, gr_ref, gi_ref, sr_ref, si_ref):
        i = pl.program_id(0)

        @pl.when(i == 0)
        def _():
            for ref in (dar_ref, dai_ref, dbr_ref, dbi_ref, dcr_ref, dci_ref, sr_ref, si_ref):
                ref[...] = jnp.zeros_like(ref)

        for r in range(rr):
            dyb = dy_ref[r]
            _s5_put(gr_ref, r, rr, tc, _dot_nt(dyb, cr_ref[r // per]))
            _s5_put(gi_ref, r, rr, tc, -_dot_nt(dyb, ci_ref[r // per]))
        g_r, g_i = sr_ref[...], si_ref[...]
        last = pl.ds(tc - 1, qr, stride=tc)
        dar_ref[...] += g_r * hr_ref[last, :] + g_i * hi_ref[last, :]
        dai_ref[...] += g_i * hr_ref[last, :] - g_r * hi_ref[last, :]
        _s5_scan_chunk(gr_ref, gi_ref, ar_ref, ai_ref, sr_ref, si_ref, qr, tc, True)
        row = lax.broadcasted_iota(jnp.int32, (tc, LANES), 0)
        for chain in range(qr):
            at, one = pl.ds(chain * tc, tc), pl.ds(chain, 1)
            p_r = jnp.where(row >= 1, pltpu.roll(hr_ref[at, :], 1, 0), 0.0)
            p_i = jnp.where(row >= 1, pltpu.roll(hi_ref[at, :], 1, 0), 0.0)
            g_r, g_i = gr_ref[at, :], gi_ref[at, :]
            dar_ref[one, :] += jnp.sum(g_r * p_r + g_i * p_i, axis=0, keepdims=True)
            dai_ref[one, :] += jnp.sum(g_i * p_r - g_r * p_i, axis=0, keepdims=True)
        for r in range(rr):
            s = r // per
            ub, dyb = u_ref[r], dy_ref[r]
            grb, gib = _s5_get(gr_ref, r, rr, tc).astype(BF16), _s5_get(gi_ref, r, rr, tc).astype(BF16)
            du_ref[r] = _dot_nt(grb, br_ref[s]) + _dot_nt(gib, bi_ref[s])
            dbr_ref[s] += _dot_tn(ub, grb)
            dbi_ref[s] += _dot_tn(ub, gib)
            dcr_ref[s] += _dot_tn(_s5_get(hr_ref, r, rr, tc).astype(BF16), dyb)
            dci_ref[s] -= _dot_tn(_s5_get(hi_ref, r, rr, tc).astype(BF16), dyb)

    full = lambda a: pl.BlockSpec(a.shape, lambda i: (0,) * a.ndim)
    back = lambda i: nt - 1 - i
    tok = pl.BlockSpec((rr, tc, S5_HALF), lambda i: (0, back(i), 0))
    h_spec = pl.BlockSpec((None, qr * tc, LANES), lambda i: (back(i), 0, 0))
    f = lambda a: jax.ShapeDtypeStruct(a.shape, F32)
    a_re, a_im = _s5_powers(_s5_tiles(a_re), _s5_tiles(a_im), True)
    da = jax.ShapeDtypeStruct((qr, LANES), F32)
    du, da_re, da_im, db_re, db_im, dc_re, dc_im = pl.pallas_call(
        body, name='s5_bwd', grid=(nt,),
        out_shape=(jax.ShapeDtypeStruct(u.shape, F32), da, da, f(b_re), f(b_im), f(c_re), f(c_im)),
        in_specs=[tok, tok, full(a_re), full(a_im), full(b_re), full(b_im), full(c_re), full(c_im), h_spec, h_spec],
        out_specs=(tok, full(da), full(da), full(b_re), full(b_im), full(c_re), full(c_im)),
        scratch_shapes=[pltpu.VMEM((qr * tc, LANES), F32), pltpu.VMEM((qr * tc, LANES), F32),
                        pltpu.VMEM((qr, LANES), F32), pltpu.VMEM((qr, LANES), F32)],
        compiler_params=_params('arbitrary'),
    )(u, dy, a_re, a_im, b_re, b_im, c_re, c_im, h_re, h_im)
    return du, _s5_untiles(da_re), _s5_untiles(da_im), db_re, db_im, dc_re, dc_im


@jax.custom_vjp
def s5_core(u, a_re, a_im, b_re, b_im, c_re, c_im):
    return _s5_fwd_call(u.astype(BF16), a_re, a_im, b_re.astype(BF16), b_im.astype(BF16), c_re.astype(BF16),
                        c_im.astype(BF16))[0]


def _s5_core_fwd(u, a_re, a_im, b_re, b_im, c_re, c_im):
    args = (u.astype(BF16), a_re, a_im, b_re.astype(BF16), b_im.astype(BF16), c_re.astype(BF16), c_im.astype(BF16))
    y, h_re, h_im = _s5_fwd_call(*args)
    return y, args + (h_re, h_im)


def _s5_core_bwd(res, g):
    return _s5_bwd_call(*res, g.astype(BF16))


s5_core.defvjp(_s5_core_fwd, _s5_core_bwd)


def _shard_view(ref, axis, index, width):
    return ref.at[(slice(None),) * axis + (pl.ds(pl.multiple_of(index * width, width), width),)]


def _exchange_many(xs, cuts, gather, name):
    n = len(xs)
    if gather:
        shards = [x.shape for x in xs]
    else:
        shards = [x.shape[1:] if cut is None else x.shape[:cut] + (x.shape[cut] // N_DEV,) + x.shape[cut + 1:]
                  for x, cut in zip(xs, cuts)]

    def full_shape(shard, cut):
        return shard[:cut] + (N_DEV * shard[cut],) + shard[cut + 1:]

    out_shapes = [jax.ShapeDtypeStruct((N_DEV,) + tuple(s) if (cut is None or not gather) else full_shape(tuple(s), cut), x.dtype)
                  for x, s, cut in zip(xs, shards, cuts)]

    def body(*refs):
        x_refs, out_refs = refs[:n], refs[n:2 * n]
        send_sems, recv_sems, local_sems = refs[2 * n:]
        ix, iy, ic = lax.axis_index('x'), lax.axis_index('y'), lax.axis_index('c')
        me = 4 * ix + 2 * iy + ic

        def flipped(k):
            px = 1 - ix if k & 4 else ix
            py = 1 - iy if k & 2 else iy
            pc = 1 - ic if k & 1 else ic
            return (px, py, pc), 4 * px + 2 * py + pc

        def block(ref, cut, shard, who):
            return ref.at[who] if cut is None else _shard_view(ref, cut, who, shard[cut])

        def ends(i, sender, receiver):
            if gather:
                return x_refs[i], block(out_refs[i], cuts[i], shards[i], sender)
            return block(x_refs[i], cuts[i], shards[i], receiver), out_refs[i].at[sender]

        def copy(i, k, sender, receiver):
            src, dst = ends(i, sender, receiver)
            return pltpu.make_async_remote_copy(src_ref=src, dst_ref=dst, send_sem=send_sems.at[i * (N_DEV - 1) + k - 1],
                                                 recv_sem=recv_sems.at[i * (N_DEV - 1) + k - 1], device_id=flipped(k)[0],
                                                 device_id_type=MESH)

        own = [pltpu.make_async_copy(*ends(i, me, me), local_sems.at[i]) for i in range(n)]
        for cp in own:
            cp.start()
        if gather:
            chips = (2, 4, 6)
            sent = [copy(i, k, me, flipped(k)[1]) for k in (1,) + chips for i in range(n)]
            for cp in sent:
                cp.start()
            for k in chips:
                for i in range(n):
                    copy(i, k, flipped(k)[1], me).wait_recv()
                    src, dst = ends(i, flipped(k)[1], me)
                    sent.append(pltpu.make_async_remote_copy(
                        src_ref=dst, dst_ref=dst, send_sem=send_sems.at[i * (N_DEV - 1) + k], recv_sem=recv_sems.at[i * (N_DEV - 1) + k],
                        device_id=flipped(1)[0], device_id_type=MESH))
                    sent[-1].start()
            for k in (1, 3, 5, 7):
                for i in range(n):
                    src, dst = ends(i, flipped(k)[1], me)
                    pltpu.make_async_remote_copy(
                        src_ref=dst, dst_ref=dst, send_sem=send_sems.at[i * (N_DEV - 1) + k - 1], recv_sem=recv_sems.at[i * (N_DEV - 1) + k - 1],
                        device_id=flipped(1)[0], device_id_type=MESH).wait_recv()
            for cp in sent:
                cp.wait_send()
            for cp in own:
                cp.wait()
            return
        sent = [copy(i, k, me, flipped(k)[1]) for k in range(1, N_DEV) for i in range(n)]
        for cp in sent:
            cp.start()
        for k in range(1, N_DEV):
            for i in range(n):
                copy(i, k, flipped(k)[1], me).wait_recv()
        for cp in sent:
            cp.wait_send()
        for cp in own:
            cp.wait()

    hbm = pl.BlockSpec(memory_space=pltpu.HBM)
    pairs = n * (N_DEV - 1)
    return pl.pallas_call(
        body, name=name, out_shape=out_shapes, in_specs=[hbm] * n, out_specs=[hbm] * n,
        scratch_shapes=[pltpu.SemaphoreType.DMA((pairs,)), pltpu.SemaphoreType.DMA((pairs,)), pltpu.SemaphoreType.DMA((n,))],
    )(*xs)


def _adamw_landed(landed, w, m, v):
    shape = w.shape
    slots = landed.shape[0]
    w, m, v = (_as_rows(a) for a in (w, m, v))
    rows, cols = w.shape
    landed = landed.reshape(slots, rows, cols)
    tr = _tile(rows, (256, 128, 64, 32, 16))

    def body(l_ref, w_ref, m_ref, v_ref, g_ref, d_ref, nm_ref, nv_ref):
        g = l_ref[0].astype(F32)
        for d in range(1, slots):
            g = g + l_ref[d].astype(F32)
        g_ref[...] = g
        d_ref[...], nm_ref[...], nv_ref[...] = _adamw_math(w_ref[...], g, m_ref[...], v_ref[...])

    spec = pl.BlockSpec((tr, cols), lambda i: (i, 0))
    out = pl.pallas_call(
        body, name='adamw_landed', grid=(rows // tr,), out_shape=(jax.ShapeDtypeStruct(w.shape, F32),) * 4,
        in_specs=[pl.BlockSpec((slots, tr, cols), lambda i: (0, i, 0))] + [spec] * 3, out_specs=(spec,) * 4,
        compiler_params=_params('parallel'),
    )(landed, w, m, v)
    return tuple(o.reshape(shape) for o in out)


def _all_reduce_small(x):
    g = _exchange_many([x], [None], True, 'gather_small_grads')[0]

    def body(g_ref, o_ref):
        acc = g_ref[0]
        for d in range(1, N_DEV):
            acc = acc + g_ref[d]
        o_ref[...] = acc

    return pl.pallas_call(body, name='sum_small', out_shape=jax.ShapeDtypeStruct(x.shape, F32))(g)


def _adamw_math(w, g, m, v):
    m = ADAM_B1 * m + (1.0 - ADAM_B1) * g
    v = ADAM_B2 * v + (1.0 - ADAM_B2) * (g * g)
    m_hat = m / (1.0 - ADAM_B1 ** ADAM_STEP)
    v_hat = v / (1.0 - ADAM_B2 ** ADAM_STEP)
    return -ADAM_LR * (m_hat / (jnp.sqrt(v_hat) + ADAM_EPS) + ADAM_WD * w), m, v


def _as_rows(a):
    return a.reshape(1, -1) if a.ndim < 2 else a.reshape(-1, a.shape[-1])


def _as_lanes(a):
    return a.reshape(-1, LANES) if a.size % LANES == 0 else a.reshape(1, -1)


def _adamw_big(w, g, m, v):
    shape = w.shape
    w, g, m, v = (_as_rows(a) for a in (w, g, m, v))
    rows, cols = w.shape
    tr = _tile(rows, (512, 256, 128, 64, 32, 16, 8))

    def body(w_ref, g_ref, m_ref, v_ref, d_ref, nm_ref, nv_ref):
        d_ref[...], nm_ref[...], nv_ref[...] = _adamw_math(w_ref[...], g_ref[...], m_ref[...], v_ref[...])

    spec = pl.BlockSpec((tr, cols), lambda i: (i, 0))
    out = pl.pallas_call(
        body, name='adamw', grid=(rows // tr,), out_shape=(jax.ShapeDtypeStruct(w.shape, F32),) * 3,
        in_specs=[spec] * 4, out_specs=(spec,) * 3, compiler_params=_params('parallel'),
    )(w, g, m, v)
    return tuple(o.reshape(shape) for o in out)


def _adamw_small(ws, gs, ms, vs):
    n = len(ws)
    shapes = [w.shape for w in ws]
    flat = [_as_lanes(a) for group in (ws, gs, ms, vs) for a in group]

    def body(*refs):
        ins, outs = refs[:4 * n], refs[4 * n:]
        for i in range(n):
            d, m, v = _adamw_math(ins[i][...], ins[n + i][...], ins[2 * n + i][...], ins[3 * n + i][...])
            outs[i][...], outs[n + i][...], outs[2 * n + i][...] = d, m, v

    out = pl.pallas_call(
        body, name='adamw_small', out_shape=tuple(jax.ShapeDtypeStruct(flat[i].shape, F32) for _ in range(3) for i in range(n)),
    )(*flat)
    return [tuple(out[j * n + i].reshape(shapes[i]) for j in range(3)) for i in range(n)]


def rms_norm(x, g):
    return x * lax.rsqrt(jnp.mean(jnp.square(x), axis=-1, keepdims=True) + EPS) * g


def modulate(x, g, shift, scale):
    return rms_norm(x, g) * (1 + scale) + shift


def rope_tables(n_tokens, rot_dim):
    t = jnp.arange(n_tokens)
    rows = (t // GRID_W).astype(F32)
    cols = (t % GRID_W).astype(F32)
    axis_dim = rot_dim // 2
    freqs = ROPE_BASE ** (-jnp.arange(0, axis_dim, 2, dtype=F32) / axis_dim)
    ang_r, ang_c = rows[:, None] * freqs, cols[:, None] * freqs
    ang = jnp.concatenate([ang_r, ang_r, ang_c, ang_c], axis=-1)
    return jnp.cos(ang), jnp.sin(ang)


def rope(x, cos, sin):
    x1, x2, x3, x4 = jnp.split(x, 4, axis=-1)
    rot = jnp.concatenate([-x2, x1, -x4, x3], axis=-1)
    return x * cos[:, None, :] + rot * sin[:, None, :]


def heads_first(t):
    return jnp.swapaxes(t, 1, 2)


def tokens_matmul(t, w):
    b, n, k = t.shape
    return linear(t.reshape(b * n, k), w).reshape(b, n, w.shape[1])


def s5_discretize(lam_re, lam_im, log_dt, b_re, b_im):
    dt = jnp.exp(log_dt)[:, None]
    mag = jnp.exp(lam_re * dt)
    a_re = mag * jnp.cos(lam_im * dt)
    a_im = mag * jnp.sin(lam_im * dt)
    den = jnp.square(lam_re) + jnp.square(lam_im)
    f_re = ((a_re - 1.0) * lam_re + a_im * lam_im) / den
    f_im = (a_im * lam_re - (a_re - 1.0) * lam_im) / den
    bb_re = f_re[..., None] * b_re - f_im[..., None] * b_im
    bb_im = f_re[..., None] * b_im + f_im[..., None] * b_re
    return a_re, a_im, bb_re, bb_im


def s5_mixer(u_lat, u_ctx, p, j, need_ctx):
    b, n, _ = u_lat.shape
    c = u_ctx.shape[1]
    half_groups = SSM_GROUPS // 2
    eye = jnp.eye(half_groups, dtype=F32)
    a_res, a_ims, b_res, b_ims, c_res, c_ims, seqs = [], [], [], [], [], [], []
    for d in range(2):
        a_re, a_im, bb_re, bb_im = s5_discretize(p['ssm_lam_re'][j, d], p['ssm_lam_im'][j, d], p['ssm_log_dt'][j, d],
                                                 p['ssm_b_re'][j, d], p['ssm_b_im'][j, d])
        for half in range(2):
            grp = slice(half * half_groups, (half + 1) * half_groups)
            a_res.append(a_re[grp].reshape(S5_LANES))
            a_ims.append(a_im[grp].reshape(S5_LANES))
            b_res.append(jnp.einsum('gsp,gh->gphs', bb_re[grp], eye).reshape(S5_HALF, S5_LANES))
            b_ims.append(jnp.einsum('gsp,gh->gphs', bb_im[grp], eye).reshape(S5_HALF, S5_LANES))
            c_res.append(jnp.einsum('gps,gh->gshp', p['ssm_c_re'][j, d][grp], eye).reshape(S5_LANES, S5_HALF))
            c_ims.append(jnp.einsum('gps,gh->gshp', p['ssm_c_im'][j, d][grp], eye).reshape(S5_LANES, S5_HALF))
        flip = (lambda t: t[:, ::-1]) if d == 1 else (lambda t: t)
        seq = jnp.concatenate([flip(u_ctx), flip(u_lat)], axis=1)
        seqs.append(jnp.transpose(seq.reshape(b, c + n, 2, S5_HALF), (2, 0, 1, 3)))
    u = jnp.stack(seqs).reshape(4 * b, c + n, S5_HALF)
    rep = lambda parts: jnp.repeat(jnp.stack(parts), b, axis=0)
    y = s5_core(u, rep(a_res), rep(a_ims), jnp.stack(b_res), jnp.stack(b_ims), jnp.stack(c_res), jnp.stack(c_ims))
    y = jnp.transpose(y.reshape(2, 2, b, c + n, S5_HALF), (0, 2, 3, 1, 4)).reshape(2, b, c + n, SSM_WIDTH)
    d_skip = p['ssm_d'][j]
    y_lat = d_skip * u_lat + y[0, :, c:] + y[1, :, c:][:, ::-1]
    wg, bg = p['ssm_w_glu'][j], p['ssm_b_glu'][j]

    def glu(t):
        t = jax.nn.gelu(t)
        return t * jax.nn.sigmoid(tokens_matmul(t, wg) + bg)

    if not need_ctx:
        return glu(y_lat), None
    y_ctx = d_skip * u_ctx + y[0, :, :c] + y[1, :, :c][:, ::-1]
    return glu(y_lat), glu(y_ctx)


def even_mixer(a_lat, a_ctx, p, j, need_ctx):
    b, n, _ = a_lat.shape
    c = a_ctx.shape[1]
    cos, sin = rope_tables(n, HEAD_DIM)
    proj = tokens_matmul(jnp.concatenate([a_ctx, a_lat], axis=1), p['e_w_in'][j])
    q, k, v, u = jnp.split(proj, [GQA_Q_W, GQA_Q_W + GQA_KV_W, GQA_Q_W + 2 * GQA_KV_W], axis=-1)
    q = rms_norm(q.reshape(b, c + n, GQA_Q_HEADS, HEAD_DIM), p['e_g_q'][j])
    k = rms_norm(k.reshape(b, c + n, GQA_KV_HEADS, HEAD_DIM), p['e_g_k'][j])
    v = v.reshape(b, c + n, GQA_KV_HEADS, HEAD_DIM)
    q_l = rope(q[:, c:], cos, sin)
    k = jnp.concatenate([k[:, :c], rope(k[:, c:], cos, sin)], axis=1)
    scale = HEAD_DIM ** -0.5
    kh, vh = heads_first(k), heads_first(v)
    att_l = heads_first(attention(heads_first(q_l), kh, vh, scale)).reshape(b, n, GQA_Q_W)
    ssm_l, ssm_c = s5_mixer(u[:, c:], u[:, :c], p, j, need_ctx)
    mix_l = jnp.concatenate([att_l, ssm_l], axis=-1)
    if not need_ctx:
        return tokens_matmul(mix_l, p['e_w_out'][j]), None
    att_c = heads_first(attention(heads_first(q[:, :c]), kh[:, :, :c], vh[:, :, :c], scale)).reshape(b, c, GQA_Q_W)
    mix = jnp.concatenate([jnp.concatenate([att_c, ssm_c], axis=-1), mix_l], axis=1)
    out = tokens_matmul(mix, p['e_w_out'][j])
    return out[:, c:], out[:, :c]


def odd_mixer(a_lat, a_ctx, p, j, need_ctx):
    b, n, _ = a_lat.shape
    c = a_ctx.shape[1]
    t = c + n
    cos, sin = rope_tables(n, MLA_ROPE)
    proj = tokens_matmul(jnp.concatenate([a_ctx, a_lat], axis=1), p['o_w_in'][j])
    c1, c2, c3 = MLA_Q_RANK, MLA_Q_RANK + MLA_KV_RANK, MLA_Q_RANK + MLA_KV_RANK + MLA_ROPE
    cq, ckv, kr = proj[..., :c1], proj[..., c1:c2], proj[..., c2:c3]
    nq, nk, nv = jnp.split(proj[..., ODD_NA_AT:], 3, axis=-1)
    q = tokens_matmul(rms_norm(cq, p['mla_g_cq'][j]), p['mla_w_uq'][j]).reshape(b, t, MLA_HEADS, MLA_QK)
    kv = tokens_matmul(rms_norm(ckv, p['mla_g_ckv'][j]), p['mla_w_ukv'][j]).reshape(b, t, MLA_HEADS, MLA_NOPE + MLA_V)
    k = jnp.concatenate([kv[..., :MLA_NOPE], jnp.broadcast_to(kr[:, :, None, :], (b, t, MLA_HEADS, MLA_ROPE))], axis=-1)
    q, k, mv = rms_norm(q, p['mla_g_q'][j]), rms_norm(k, p['mla_g_k'][j]), kv[..., MLA_NOPE:]

    def rope_tail(x):
        tail = jnp.concatenate([x[:, :c, :, MLA_NOPE:], rope(x[:, c:, :, MLA_NOPE:], cos, sin)], axis=1)
        return jnp.concatenate([x[..., :MLA_NOPE], tail], axis=-1)

    q, k = rope_tail(q), rope_tail(k)
    qh, kh, vh = heads_first(q), heads_first(k), heads_first(mv)
    mla_scale = MLA_QK ** -0.5
    mla_l = heads_first(attention(qh[:, :, c:], kh, vh, mla_scale)).reshape(b, n, MLA_HEADS * MLA_V)
    nq = heads_first(rms_norm(nq.reshape(b, t, NA_HEADS, HEAD_DIM), p['na_g_q'][j]))
    nk = heads_first(rms_norm(nk.reshape(b, t, NA_HEADS, HEAD_DIM), p['na_g_k'][j]))
    nv = heads_first(nv.reshape(b, t, NA_HEADS, HEAD_DIM))
    na_scale = HEAD_DIM ** -0.5
    na_l = na_attention(nq[:, :, c:], nk[:, :, c:], nv[:, :, c:], nk[:, :, :c], nv[:, :, :c], na_bias_table(p['na_rpb'][j]),
                        na_scale)
    na_l = heads_first(na_l).reshape(b, n, NA_W)
    mix_l = jnp.concatenate([mla_l, na_l], axis=-1)
    if not need_ctx:
        return tokens_matmul(mix_l, p['o_w_out'][j]), None
    mla_c = heads_first(attention(qh[:, :, :c], kh[:, :, :c], vh[:, :, :c], mla_scale)).reshape(b, c, MLA_HEADS * MLA_V)
    na_c = heads_first(attention(nq[:, :, :c], nk[:, :, :c], nv[:, :, :c], na_scale)).reshape(b, c, NA_W)
    mix = jnp.concatenate([jnp.concatenate([mla_c, na_c], axis=-1), mix_l], axis=1)
    out = tokens_matmul(mix, p['o_w_out'][j])
    return out[:, c:], out[:, :c]


def mlp(h, w1, w2):
    b, n, k = h.shape
    return mlp_rows(h.reshape(b * n, k), w1, w2).reshape(b, n, w2.shape[1])


def local_loss(x, p, m_lat, m_ctx, ctx, target):
    depth = m_lat.shape[0]
    c = ctx.shape[1]
    xc = ctx
    for i in range(depth):
        need_ctx = i < depth - 1
        j = i // 2
        ml = [m_lat[i, :, s][:, None, :] for s in range(N_MOD)]
        mc = [m_ctx[i, s][None, None, :] for s in range(N_MOD)]
        a_lat = modulate(x, p['g_norm1'][i], ml[0], ml[1])
        a_ctx = modulate(xc, p['g_norm1'][i], mc[0], mc[1])
        mixer = even_mixer if i % 2 == 0 else odd_mixer
        o_lat, o_ctx = mixer(a_lat, a_ctx, p, j, need_ctx)
        x = x + ml[2] * o_lat
        h_lat = modulate(x, p['g_norm2'][i], ml[3], ml[4])
        if need_ctx:
            xc = xc + mc[2] * o_ctx
            h_ctx = modulate(xc, p['g_norm2'][i], mc[3], mc[4])
            ff = mlp(jnp.concatenate([h_ctx, h_lat], axis=1), p['w_ff1'][i], p['w_ff2'][i])
            x = x + ml[5] * ff[:, c:]
            xc = xc + mc[5] * ff[:, :c]
        else:
            x = x + ml[5] * mlp(h_lat, p['w_ff1'][i], p['w_ff2'][i])
    return 0.5 * jnp.sum(jnp.mean(jnp.square(x - target), axis=-1))


def _packed_rows(size, layout):
    width, group = layout
    return -(-size // (width * group)) * group


def _pack_rows(flat, layout):
    width = layout[0]
    rows = _packed_rows(flat.shape[-1], layout)
    flat = jnp.pad(flat, [(0, 0)] * (flat.ndim - 1) + [(0, rows * width - flat.shape[-1])])
    return flat.reshape(flat.shape[:-1] + (rows, width))


def _unpack_rows(rows, shape):
    lead = rows.shape[:-2]
    return rows.reshape(lead + (-1,))[..., :math.prod(shape)].reshape(lead + tuple(shape))


def _unpack_all(packed, shapes, layout):
    out, at = [], 0
    for shape in shapes:
        rows = _packed_rows(math.prod(shape), layout)
        out.append(_unpack_rows(packed[..., at:at + rows, :], shape))
        at += rows
    return out


def _join_shards(g, axis):
    g = jnp.moveaxis(g, 0, axis)
    return g.reshape(g.shape[:axis] + (N_DEV * g.shape[axis + 1],) + g.shape[axis + 2:])


def _split_shards(full, axis):
    s = full.shape
    return jnp.moveaxis(full.reshape(s[:axis] + (N_DEV, s[axis] // N_DEV) + s[axis + 1:]), axis, 0)


def _gather_packed(parts, dtype, layout, name):
    packed = jnp.concatenate([_pack_rows(a.astype(dtype).reshape(-1), layout) for a in parts], axis=0)
    return _unpack_all(_exchange_many([packed], [None], True, name)[0], [a.shape for a in parts], layout)


def kernel(x, c, ctx, c_ctx, w_mod, b_mod, g_norm1, g_norm2, w_ff1, w_ff2, e_w_in, e_w_out, e_g_q, e_g_k, ssm_lam_re, ssm_lam_im, ssm_log_dt, ssm_b_re, ssm_b_im, ssm_c_re, ssm_c_im, ssm_d, ssm_w_glu, ssm_b_glu, o_w_in, o_w_out, mla_g_cq, mla_g_ckv, mla_w_uq, mla_w_ukv, mla_g_q, mla_g_k, na_g_q, na_g_k, na_rpb, loss_target, m_c_ctx, m_w_mod, m_b_mod, m_g_norm1, m_g_norm2, m_w_ff1, m_w_ff2, m_e_w_in, m_e_w_out, m_e_g_q, m_e_g_k, m_ssm_lam_re, m_ssm_lam_im, m_ssm_log_dt, m_ssm_b_re, m_ssm_b_im, m_ssm_c_re, m_ssm_c_im, m_ssm_d, m_ssm_w_glu, m_ssm_b_glu, m_o_w_in, m_o_w_out, m_mla_g_cq, m_mla_g_ckv, m_mla_w_uq, m_mla_w_ukv, m_mla_g_q, m_mla_g_k, m_na_g_q, m_na_g_k, m_na_rpb, v_c_ctx, v_w_mod, v_b_mod, v_g_norm1, v_g_norm2, v_w_ff1, v_w_ff2, v_e_w_in, v_e_w_out, v_e_g_q, v_e_g_k, v_ssm_lam_re, v_ssm_lam_im, v_ssm_log_dt, v_ssm_b_re, v_ssm_b_im, v_ssm_c_re, v_ssm_c_im, v_ssm_d, v_ssm_w_glu, v_ssm_b_glu, v_o_w_in, v_o_w_out, v_mla_g_cq, v_mla_g_ckv, v_mla_w_uq, v_mla_w_ukv, v_mla_g_q, v_mla_g_k, v_na_g_q, v_na_g_k, v_na_rpb):
    given = dict(locals())
    x, c, ctx, target = given['x'], given['c'], given['ctx'], given['loss_target']
    b_loc, _, d_model = x.shape
    depth = given['w_mod'].shape[0]
    ix, iy, ic = lax.axis_index('x'), lax.axis_index('y'), lax.axis_index('c')
    me = 4 * ix + 2 * iy + ic
    n_batch = N_DEV * b_loc
    mod_w = given['w_mod'].shape[2]

    c_rows = jnp.concatenate([c, jnp.zeros((8 - b_loc, d_model), F32)], axis=0)
    small = _gather_packed([c_rows] + [given[n] for n in SHARDED_SMALL], F32, PACK_SMALL,'gather_small')
    c_all = small[0][:, :b_loc].reshape(n_batch, d_model)
    full = {n: _join_shards(g, SHARDED_SMALL[n]) for n, g in zip(SHARDED_SMALL, small[1:])}
    cuts = {n: (a if given[n].shape[a] % (16 if a == 1 else LANES) == 0 else None) for n, a in BIG.items()}
    big = _exchange_many([given[n].astype(BF16) for n in BIG], [cuts[n] for n in BIG], True, 'gather_weights')
    for n, g in zip(BIG, big):
        g = g if cuts[n] is not None else _join_shards(g, BIG[n])
        full[n] = [g[i] for i in range(g.shape[0])]
    c3 = MLA_Q_RANK + MLA_KV_RANK + MLA_ROPE
    full['o_w_in'] = [jnp.concatenate([w[:, :c3], jnp.zeros((w.shape[0], ODD_NA_AT - c3), BF16), w[:, c3:]], axis=-1)
                      for w in full['o_w_in']]
    for n in REPLICATED:
        full[n] = given[n]

    rows17 = 16 * (-(-(n_batch + 1) // 16))
    cond = jnp.concatenate([jax.nn.silu(c_all), jax.nn.silu(given['c_ctx'])[None],
                            jnp.zeros((rows17 - n_batch - 1, d_model), F32)], axis=0)
    mod_mine = jnp.stack([_matmul(cond, given['w_mod'][i], 'nn', F32) for i in range(depth)])
    b_mine = lax.dynamic_slice_in_dim(given['b_mod'], me * mod_w, mod_w, axis=1)
    mod_mine = mod_mine + b_mine[:, None, :]
    mod_all = _gather_packed([mod_mine], F32, PACK_SMALL,'gather_mod')[0]
    mod_all = jnp.moveaxis(mod_all, 0, 2).reshape(depth, rows17, N_MOD, d_model)
    m_lat = lax.dynamic_slice_in_dim(mod_all, me * b_loc, b_loc, axis=1)
    m_ctx = mod_all[:, n_batch]

    diff = {n: full[n] for n in list(BIG) + list(SHARDED_SMALL) + REPLICATED}
    loss, (g_x, g_p, g_ml, g_mc) = jax.value_and_grad(local_loss, argnums=(0, 1, 2, 3))(x, diff, m_lat, m_ctx, ctx, target)
    loss = lax.psum(loss, ('x', 'y', 'c'))
    g_p['o_w_in'] = [jnp.concatenate([g[:, :c3], g[:, ODD_NA_AT:]], axis=-1) for g in g_p['o_w_in']]

    g_rows = jnp.concatenate([g_ml.reshape(depth, b_loc, N_MOD * d_model), g_mc.reshape(depth, 1, N_MOD * d_model),
                              jnp.zeros((depth, 8 - b_loc - 1, N_MOD * d_model), F32)], axis=1)
    g_mod_all = _gather_packed([g_rows], F32, PACK_SMALL,'gather_mod_grads')[0]
    g_lat_all = jnp.moveaxis(g_mod_all[:, :, :b_loc], 0, 1).reshape(depth, n_batch, N_MOD * d_model)
    g_ctx_all = g_mod_all[0, :, b_loc]
    for dev in range(1, N_DEV):
        g_ctx_all = g_ctx_all + g_mod_all[dev, :, b_loc]
    g_mod17 = jnp.concatenate([g_lat_all, g_ctx_all[:, None], jnp.zeros((depth, rows17 - n_batch - 1, N_MOD * d_model), F32)],
                              axis=1)
    grad_b_mod = jnp.sum(g_mod17, axis=1)
    g_mod_mine = lax.dynamic_slice_in_dim(g_mod17, me * mod_w, mod_w, axis=2)
    grad_w_mod = jnp.stack([_matmul(cond, g_mod_mine[i], 'tn', F32) for i in range(depth)])
    d_cond = _matmul(g_mod_mine[0], given['w_mod'][0], 'nt', F32)
    for i in range(1, depth):
        d_cond = d_cond + _matmul(g_mod_mine[i], given['w_mod'][i], 'nt', F32)
    d_cond_ctx = d_cond[n_batch]

    small_names = REPLICATED + list(SHARDED_SMALL)
    parts = [d_cond_ctx] + [g_p[n] for n in small_names]
    packed = jnp.concatenate([_pack_rows(a.reshape(-1), PACK_SMALL) for a in parts], axis=0)
    summed = _unpack_all(_all_reduce_small(packed), [a.shape for a in parts], PACK_SMALL)
    grads = dict(zip(['c_ctx'] + small_names, summed))
    c_ctx = given['c_ctx']
    sig = jax.nn.sigmoid(c_ctx)
    grads['c_ctx'] = grads['c_ctx'] * (sig * (1 + c_ctx * (1 - sig)))
    for n, axis in SHARDED_SMALL.items():
        width = given[n].shape[axis]
        grads[n] = lax.dynamic_slice_in_dim(grads[n], me * width, width, axis=axis)
    grads['w_mod'], grads['b_mod'] = grad_w_mod, grad_b_mod

    stacked = [jnp.stack(g_p[n]) for n in BIG]
    stacked = [g if cuts[n] is not None else _split_shards(g, BIG[n]) for n, g in zip(BIG, stacked)]
    landed = _exchange_many(stacked, [cuts[n] for n in BIG], False, 'scatter_weight_grads')

    upd = {}
    for n, slots in zip(BIG, landed):
        grads[n], *upd[n] = _adamw_landed(slots, given[n], given['m_' + n], given['v_' + n])
    upd['w_mod'] = _adamw_big(given['w_mod'], grads['w_mod'], given['m_w_mod'], given['v_w_mod'])
    rest = [n for n in WEIGHTS if n not in upd]
    out = _adamw_small([given[n] for n in rest], [grads[n] for n in rest], [given['m_' + n] for n in rest],
                       [given['v_' + n] for n in rest])
    upd.update(dict(zip(rest, out)))
    return (loss, g_x, *[grads[n] for n in WEIGHTS], *[upd[n][0] for n in WEIGHTS], *[upd[n][1] for n in WEIGHTS],
            *[upd[n][2] for n in WEIGHTS])
```

```python
import functools
import math

import jax
import jax.numpy as jnp
from jax import lax
from jax.experimental import pallas as pl
from jax.experimental.pallas import tpu as pltpu

F32, BF16 = jnp.float32, jnp.bfloat16
MESH = pl.DeviceIdType.MESH
N_DEV = 8
VMEM_LIMIT_BYTES = 56 * 1024 * 1024
MM_TILE_BYTES = 6 * 1024 * 1024
LANES = 128
PACK_SMALL = (128, 8)

GRID_W = 64
HEAD_DIM = 64
ROPE_BASE = 10000.0
EPS = 1e-6
N_MOD = 6
GQA_Q_HEADS, GQA_KV_HEADS = 12, 4
GQA_Q_W, GQA_KV_W = GQA_Q_HEADS * HEAD_DIM, GQA_KV_HEADS * HEAD_DIM
SSM_WIDTH, SSM_GROUP, SSM_GROUPS, SSM_STATE = 256, 16, 16, 64
MLA_HEADS, MLA_Q_RANK, MLA_KV_RANK, MLA_NOPE, MLA_ROPE, MLA_V = 8, 512, 256, 64, 32, 64
MLA_QK = MLA_NOPE + MLA_ROPE
NA_HEADS, NA_WIN_R, NA_WIN_C = 8, 8, 16
NA_W = NA_HEADS * HEAD_DIM
ODD_IN_W = MLA_Q_RANK + MLA_KV_RANK + MLA_ROPE + 3 * NA_W
ODD_NA_AT = 1024
ODD_IN_PAD = ODD_NA_AT + 3 * NA_W
NEG = -1e30

ADAM_LR, ADAM_B1, ADAM_B2, ADAM_EPS, ADAM_WD, ADAM_STEP = 0.001, 0.9, 0.999, 1e-08, 0.01, 10

FWD_PARAMS = ['x', 'c', 'ctx', 'c_ctx', 'w_mod', 'b_mod', 'g_norm1', 'g_norm2', 'w_ff1', 'w_ff2', 'e_w_in', 'e_w_out',
              'e_g_q', 'e_g_k', 'ssm_lam_re', 'ssm_lam_im', 'ssm_log_dt', 'ssm_b_re', 'ssm_b_im', 'ssm_c_re', 'ssm_c_im',
              'ssm_d', 'ssm_w_glu', 'ssm_b_glu', 'o_w_in', 'o_w_out', 'mla_g_cq', 'mla_g_ckv', 'mla_w_uq', 'mla_w_ukv',
              'mla_g_q', 'mla_g_k', 'na_g_q', 'na_g_k', 'na_rpb']
WEIGHTS = FWD_PARAMS[3:]
BIG = {'w_ff1': 2, 'w_ff2': 1, 'e_w_in': 2, 'e_w_out': 1, 'o_w_in': 2, 'o_w_out': 1, 'mla_w_uq': 2, 'mla_w_ukv': 2,
       'ssm_w_glu': 1}
SHARDED_SMALL = {'mla_g_cq': 1, 'mla_g_ckv': 1}
REPLICATED = [n for n in WEIGHTS if n not in BIG and n not in SHARDED_SMALL and n not in ('w_mod', 'c_ctx', 'b_mod')]


def _tile(dim, prefs):
    for p in prefs:
        if dim >= p and dim % p == 0:
            return p
    return dim


def _params(*sem):
    return pltpu.CompilerParams(dimension_semantics=sem, vmem_limit_bytes=VMEM_LIMIT_BYTES)


def _dot_nt(a, b):
    return lax.dot_general(a, b, (((1,), (1,)), ((), ())), preferred_element_type=F32)


def _dot_tn(a, b):
    return lax.dot_general(a, b, (((0,), (0,)), ((), ())), preferred_element_type=F32)


def _dot(a, b):
    return jnp.dot(a, b, preferred_element_type=F32)


def _matmul(a, b, kind, out_dtype, finish=None, extra=None, n_out=1):
    a, b = a.astype(BF16), b.astype(BF16)
    if kind == 'nn':
        (m, kd), n = a.shape, b.shape[1]
    elif kind == 'nt':
        (m, kd), n = a.shape, b.shape[0]
    else:
        (kd, m), n = a.shape, b.shape[1]
    if kind == 'tn':
        tm = m if m <= 1024 else _tile(m, (1024, 768, 512, 256, 128))
        tn = _tile(n, (1024, 768, 512, 256, 128))
    else:
        tn = n if kd * n * 2 <= MM_TILE_BYTES else _tile(n, (1024, 768, 512, 256, 128))
        tm = _tile(m, [t for t in (1536, 1024, 768, 512, 256, 128) if t * tn * 4 <= MM_TILE_BYTES])
    whole = kind != 'tn' and tn == n and kd * n * 2 <= MM_TILE_BYTES
    tk = kd if whole else _tile(kd, [t for t in (2048, 1536, 1024, 512, 256, 128) if t * max(tm, tn) * 4 <= MM_TILE_BYTES])
    nk = kd // tk
    dn = {'nn': (((1,), (0,)), ((), ())), 'nt': (((1,), (1,)), ((), ())), 'tn': (((0,), (0,)), ((), ()))}[kind]

    n_in = 2 if extra is None else 3

    def body(*refs):
        a_ref, b_ref = refs[:2]
        o_refs = refs[n_in:n_in + n_out]

        def store(total):
            outs = (total,) if finish is None else finish(total, refs[2][...] if extra is not None else None)
            for o_ref, val in zip(o_refs, outs):
                o_ref[...] = val.astype(o_ref.dtype)

        part = lax.dot_general(a_ref[...], b_ref[...], dn, preferred_element_type=F32)
        if nk == 1:
            store(part)
            return
        acc_ref, k = refs[n_in + n_out], pl.program_id(2)

        @pl.when(k == 0)
        def _():
            acc_ref[...] = part

        @pl.when((k > 0) & (k < nk - 1))
        def _():
            acc_ref[...] += part

        @pl.when(k == nk - 1)
        def _():
            store(acc_ref[...] + part)

    a_spec = pl.BlockSpec((tk, tm), lambda i, j, k: (k, i)) if kind == 'tn' else pl.BlockSpec((tm, tk), lambda i, j, k: (i, k))
    b_spec = pl.BlockSpec((tn, tk), lambda i, j, k: (j, k)) if kind == 'nt' else pl.BlockSpec((tk, tn), lambda i, j, k: (k, j))
    o_spec = pl.BlockSpec((tm, tn), lambda i, j, k: (i, j))
    out = pl.pallas_call(
        body, name='mm_' + kind, grid=(m // tm, n // tn, nk),
        out_shape=[jax.ShapeDtypeStruct((m, n), out_dtype)] * n_out,
        in_specs=[a_spec, b_spec] + ([o_spec] if extra is not None else []), out_specs=[o_spec] * n_out,
        scratch_shapes=[pltpu.VMEM((tm, tn), F32)] if nk > 1 else [],
        compiler_params=_params('parallel', 'parallel', 'arbitrary'),
    )(*((a, b) if extra is None else (a, b, extra)))
    return out[0] if n_out == 1 else out


@jax.custom_vjp
def linear(a, w):
    return _matmul(a, w, 'nn', F32)


def _linear_fwd(a, w):
    ab = a.astype(BF16)
    return _matmul(ab, w, 'nn', F32), (ab, w)


def _linear_bwd(res, g):
    ab, w = res
    gb = g.astype(BF16)
    return _matmul(gb, w, 'nt', F32), _matmul(ab, gb, 'tn', w.dtype)


linear.defvjp(_linear_fwd, _linear_bwd)


def _relu2(z, _):
    r = jnp.maximum(z, 0.0)
    return r, r * r


def _relu2_grad(d_act, r):
    return (d_act * (2.0 * r.astype(F32)),)


@jax.custom_vjp
def mlp_rows(h, w1, w2):
    return _mlp_rows_fwd(h, w1, w2)[0]


def _mlp_rows_fwd(h, w1, w2):
    hb = h.astype(BF16)
    r, act = _matmul(hb, w1, 'nn', BF16, finish=_relu2, n_out=2)
    return _matmul(act, w2, 'nn', F32), (hb, w1, w2, r, act)


def _mlp_rows_bwd(res, g):
    hb, w1, w2, r, act = res
    gb = g.astype(BF16)
    dz = _matmul(gb, w2, 'nt', BF16, finish=_relu2_grad, extra=r)
    return _matmul(dz, w1, 'nt', F32), _matmul(hb, dz, 'tn', w1.dtype), _matmul(act, gb, 'tn', w2.dtype)


mlp_rows.defvjp(_mlp_rows_fwd, _mlp_rows_bwd)


LOG2E = math.log2(math.e)


def _softmax_rows(t):
    m = jnp.max(t, axis=-1, keepdims=True)
    e = jnp.exp2(t - m)
    return e * (1.0 / jnp.sum(e, axis=-1, keepdims=True))


ATTN_SPLIT = 2


def _attn_specs(q, k, v, bq):
    _, h, nq, dq = q.shape
    _, hk, nk, dv = v.shape
    g = h // hk
    q_spec = pl.BlockSpec((None, None, bq, dq), lambda b, j, gi, i: (b, j * g + gi, i, 0))
    k_spec = pl.BlockSpec((None, None, nk, dq), lambda b, j, gi, i: (b, j, 0, 0))
    v_spec = pl.BlockSpec((None, None, nk, dv), lambda b, j, gi, i: (b, j, 0, 0))
    o_spec = pl.BlockSpec((None, None, bq, dv), lambda b, j, gi, i: (b, j * g + gi, i, 0))
    t = dict(q=pl.BlockSpec((None, None, dq, bq), lambda b, j, gi, i: (b, j * g + gi, 0, i)),
             o=pl.BlockSpec((None, None, dv, bq), lambda b, j, gi, i: (b, j * g + gi, 0, i)),
             k=pl.BlockSpec((None, None, dq, nk), lambda b, j, gi, i: (b, j, 0, 0)),
             v=pl.BlockSpec((None, None, dv, nk), lambda b, j, gi, i: (b, j, 0, 0)))
    return (q.shape[0], hk, g, nq // bq), q_spec, k_spec, v_spec, o_spec, t


def _attn_blocks(nq):
    bq = _tile(nq, (512, 256, 128))
    return bq, [pl.ds(s * (bq // ATTN_SPLIT), bq // ATTN_SPLIT) for s in range(ATTN_SPLIT)]


def _attn_fwd_call(q, k, vt, scale):
    b, h, nq, _ = q.shape
    dv = vt.shape[2]
    bq, subs = _attn_blocks(nq)
    grid, q_spec, k_spec, _, _, t = _attn_specs(q, k, jnp.swapaxes(vt, 2, 3), bq)

    def body(q_ref, k_ref, vt_ref, ot_ref):
        kb, vtb = k_ref[...], vt_ref[...]
        for rows in subs:
            p = _softmax_rows(_dot_nt(q_ref[rows, :], kb) * (scale * LOG2E))
            ot_ref[:, rows] = _dot_nt(vtb, p.astype(BF16))

    return pl.pallas_call(
        body, name='attn_fwd', grid=grid, out_shape=jax.ShapeDtypeStruct((b, h, dv, nq), F32),
        in_specs=[q_spec, k_spec, t['v']], out_specs=t['o'],
        compiler_params=_params('parallel', 'parallel', 'arbitrary', 'arbitrary'),
    )(q, k, vt)


def _attn_bwd_call(q, k, kt, v, do, scale):
    b, h, nq, dq = q.shape
    bq, subs = _attn_blocks(nq)
    grid, q_spec, k_spec, v_spec, o_spec, t = _attn_specs(q, k, v, bq)

    def body(q_ref, k_ref, kt_ref, v_ref, do_ref, dqt_ref, dk_ref, dv_ref):
        @pl.when((pl.program_id(2) == 0) & (pl.program_id(3) == 0))
        def _():
            dk_ref[...] = jnp.zeros_like(dk_ref)
            dv_ref[...] = jnp.zeros_like(dv_ref)

        kb, ktb, vb = k_ref[...], kt_ref[...], v_ref[...]
        dk, dv = [], []
        for rows in subs:
            qb, dob = q_ref[rows, :], do_ref[rows, :]
            p = _softmax_rows(_dot_nt(qb, kb) * (scale * LOG2E))
            dp = _dot_nt(dob, vb)
            ds = p * (dp - jnp.sum(p * dp, axis=-1, keepdims=True))
            dsb = (ds * scale).astype(BF16)
            dqt_ref[:, rows] = _dot_nt(ktb, dsb)
            dk.append(_dot_tn(dsb, qb))
            dv.append(_dot_tn(p.astype(BF16), dob))
        dk_ref[...] += sum(dk[1:], dk[0])
        dv_ref[...] += sum(dv[1:], dv[0])

    return pl.pallas_call(
        body, name='attn_bwd', grid=grid,
        out_shape=(jax.ShapeDtypeStruct((b, h, dq, nq), F32), jax.ShapeDtypeStruct(k.shape, F32), jax.ShapeDtypeStruct(v.shape, F32)),
        in_specs=[q_spec, k_spec, t['k'], v_spec, o_spec], out_specs=(t['q'], k_spec, v_spec),
        compiler_params=_params('parallel', 'parallel', 'arbitrary', 'arbitrary'),
    )(q, k, kt, v, do)


@functools.partial(jax.custom_vjp, nondiff_argnums=(3,))
def attention(q, k, v, scale):
    return _attention_fwd(q, k, v, scale)[0]


def _attention_fwd(q, k, v, scale):
    qb, kb, vb = q.astype(BF16), k.astype(BF16), v.astype(BF16)
    return jnp.swapaxes(_attn_fwd_call(qb, kb, jnp.swapaxes(vb, 2, 3), scale), 2, 3), (qb, kb, vb)


def _attention_bwd(scale, res, g):
    qb, kb, vb = res
    dqt, dk, dv = _attn_bwd_call(qb, kb, jnp.swapaxes(kb, 2, 3), vb, g.astype(BF16), scale)
    return jnp.swapaxes(dqt, 2, 3), dk, dv


attention.defvjp(_attention_fwd, _attention_bwd)


def _na_window(r, rows):
    start = jnp.clip(r - NA_WIN_R // 2, 0, rows - NA_WIN_R)
    return start, r - start


NA_TOGETHER = 8


def _na_group(r0, rows):
    out = []
    for g in range(NA_TOGETHER):
        start, off = _na_window(r0 + g, rows)
        out.append((pl.ds(pl.multiple_of(start * GRID_W, GRID_W), NA_WIN_R * GRID_W), off))
    return out


def _na_rows(x, g):
    return x[g * GRID_W:(g + 1) * GRID_W]


def _na_scores(qs, kws, kc, biases, scale):
    n = len(kws)
    s1 = jnp.stack([_dot_nt(_na_rows(qs, g), kws[g]) * scale + biases[g] for g in range(n)])
    s2 = (_dot_nt(qs, kc) * scale).reshape(n, GRID_W, kc.shape[0])
    m = jnp.maximum(jnp.max(s1, axis=-1, keepdims=True), jnp.max(s2, axis=-1, keepdims=True))
    e1, e2 = jnp.exp(s1 - m), jnp.exp(s2 - m)
    inv = 1.0 / (jnp.sum(e1, axis=-1, keepdims=True) + jnp.sum(e2, axis=-1, keepdims=True))
    return e1 * inv, e2 * inv


def _na_specs(q, kc):
    _, _, n, d = q.shape
    c = kc.shape[2]
    win = NA_WIN_R * GRID_W
    tok = pl.BlockSpec((None, None, n, d), lambda b, h: (b, h, 0, 0))
    ctx = pl.BlockSpec((None, None, c, d), lambda b, h: (b, h, 0, 0))
    bias = pl.BlockSpec((None, NA_WIN_R, GRID_W, win), lambda b, h: (h, 0, 0, 0))
    dbias = pl.BlockSpec((None, None, NA_WIN_R, GRID_W, win), lambda b, h: (b, h, 0, 0, 0))
    return tok, ctx, bias, dbias


def _na_fwd_call(q, k, v, kc, vc, bias, scale):
    b, h, n, d = q.shape
    rows, span = n // GRID_W, NA_TOGETHER * GRID_W
    tok, ctx, bias_spec, _ = _na_specs(q, kc)

    def body(q_ref, k_ref, v_ref, kc_ref, vc_ref, b_ref, o_ref):
        def step(i, carry):
            at = pl.ds(pl.multiple_of(i * span, span), span)
            wins = _na_group(i * NA_TOGETHER, rows)
            p1, p2 = _na_scores(q_ref[at, :], [k_ref[w, :] for w, _ in wins], kc_ref[...], [b_ref[off] for _, off in wins],
                                scale)
            p1, p2 = p1.astype(BF16), p2.astype(BF16)
            local = jnp.concatenate([_dot(p1[g], v_ref[w, :]) for g, (w, _) in enumerate(wins)], axis=0)
            o_ref[at, :] = local + _dot(p2.reshape(span, p2.shape[2]), vc_ref[...])
            return carry

        lax.fori_loop(0, rows // NA_TOGETHER, step, 0)

    return pl.pallas_call(
        body, name='na_fwd', grid=(b, h), out_shape=jax.ShapeDtypeStruct(q.shape, F32),
        in_specs=[tok, tok, tok, ctx, ctx, bias_spec], out_specs=tok,
        compiler_params=_params('parallel', 'parallel'),
    )(q, k, v, kc, vc, bias)


def _na_bwd_call(q, k, v, kc, vc, bias, do, scale):
    b, h, n, d = q.shape
    rows, span = n // GRID_W, NA_TOGETHER * GRID_W
    tok, ctx, bias_spec, dbias_spec = _na_specs(q, kc)

    def body(q_ref, k_ref, v_ref, kc_ref, vc_ref, b_ref, do_ref, dq_ref, dk_ref, dv_ref, dkc_ref, dvc_ref, db_ref):
        for ref in (dk_ref, dv_ref, dkc_ref, dvc_ref, db_ref):
            ref[...] = jnp.zeros_like(ref)

        def step(i, carry):
            at = pl.ds(pl.multiple_of(i * span, span), span)
            wins = _na_group(i * NA_TOGETHER, rows)
            qs, dob, kcb, vcb = q_ref[at, :], do_ref[at, :], kc_ref[...], vc_ref[...]
            kws, vws = [k_ref[w, :] for w, _ in wins], [v_ref[w, :] for w, _ in wins]
            p1, p2 = _na_scores(qs, kws, kcb, [b_ref[off] for _, off in wins], scale)
            dp1 = jnp.stack([_dot_nt(_na_rows(dob, g), vws[g]) for g in range(NA_TOGETHER)])
            dp2 = _dot_nt(dob, vcb).reshape(p2.shape)
            delta = jnp.sum(p1 * dp1, axis=-1, keepdims=True) + jnp.sum(p2 * dp2, axis=-1, keepdims=True)
            ds1, ds2 = p1 * (dp1 - delta), p2 * (dp2 - delta)
            ds1b, p1b = (ds1 * scale).astype(BF16), p1.astype(BF16)
            ds2b = (ds2 * scale).astype(BF16).reshape(span, p2.shape[2])
            p2b = p2.astype(BF16).reshape(span, p2.shape[2])
            dq_ref[at, :] = jnp.concatenate([_dot(ds1b[g], kws[g]) for g in range(NA_TOGETHER)], axis=0) + _dot(ds2b, kcb)
            for g, (w, off) in enumerate(wins):
                db_ref[off] += ds1[g]
                dk_ref[w, :] += _dot_tn(ds1b[g], _na_rows(qs, g))
                dv_ref[w, :] += _dot_tn(p1b[g], _na_rows(dob, g))
            dkc_ref[...] += _dot_tn(ds2b, qs)
            dvc_ref[...] += _dot_tn(p2b, dob)
            return carry

        lax.fori_loop(0, rows // NA_TOGETHER, step, 0)

    f = lambda a: jax.ShapeDtypeStruct(a.shape, F32)
    return pl.pallas_call(
        body, name='na_bwd', grid=(b, h),
        out_shape=(f(q), f(k), f(v), f(kc), f(vc), jax.ShapeDtypeStruct((b,) + bias.shape, F32)),
        in_specs=[tok, tok, tok, ctx, ctx, bias_spec, tok], out_specs=(tok, tok, tok, ctx, ctx, dbias_spec),
        compiler_params=_params('parallel', 'parallel'),
    )(q, k, v, kc, vc, bias, do)


@functools.partial(jax.custom_vjp, nondiff_argnums=(6,))
def na_attention(q, k, v, kc, vc, bias, scale):
    return _na_fwd_call(q.astype(BF16), k.astype(BF16), v.astype(BF16), kc.astype(BF16), vc.astype(BF16), bias, scale)


def _na_attention_fwd(q, k, v, kc, vc, bias, scale):
    res = (q.astype(BF16), k.astype(BF16), v.astype(BF16), kc.astype(BF16), vc.astype(BF16), bias)
    return _na_fwd_call(*res, scale), res


def _na_attention_bwd(scale, res, g):
    dq, dk, dv, dkc, dvc, db = _na_bwd_call(*res, g.astype(BF16), scale)
    return dq, dk, dv, dkc, dvc, jnp.sum(db, axis=0)


na_attention.defvjp(_na_attention_fwd, _na_attention_bwd)


def _na_table_index():
    qcol = jnp.arange(GRID_W)
    kcol = jnp.arange(GRID_W)
    cstart = jnp.clip(qcol - NA_WIN_C // 2, 0, GRID_W - NA_WIN_C)
    inside = (kcol[None, :] >= cstart[:, None]) & (kcol[None, :] < cstart[:, None] + NA_WIN_C)
    cidx = jnp.clip(kcol[None, :] - qcol[:, None] + (NA_WIN_C - 1), 0, 2 * NA_WIN_C - 2)
    ridx = jnp.arange(NA_WIN_R)[None, :] - jnp.arange(NA_WIN_R)[:, None] + (NA_WIN_R - 1)
    return inside, cidx, ridx


@jax.custom_vjp
def na_bias_table(rpb):
    inside, pick_c, pick_r = _na_table_picks()
    rows = jnp.einsum('hab,oja->hojb', rpb, pick_r, precision=lax.Precision.HIGHEST)
    t = jnp.einsum('hojb,qkb->hoqjk', rows, pick_c, precision=lax.Precision.HIGHEST)
    t = jnp.where(inside[None, None, :, None, :], t, NEG)
    return t.reshape(rpb.shape[0], NA_WIN_R, GRID_W, NA_WIN_R * GRID_W)


def _na_table_picks():
    inside, cidx, ridx = _na_table_index()
    pick_c = ((cidx[..., None] == jnp.arange(2 * NA_WIN_C - 1)) & inside[..., None]).astype(F32)
    pick_r = (ridx[..., None] == jnp.arange(2 * NA_WIN_R - 1)).astype(F32)
    return inside, pick_c, pick_r


def _na_bias_table_bwd(_, dt):
    _, pick_c, pick_r = _na_table_picks()
    d5 = dt.reshape(dt.shape[0], NA_WIN_R, GRID_W, NA_WIN_R, GRID_W)
    part = jnp.einsum('hoqjk,qkb->hojb', d5, pick_c, precision=lax.Precision.HIGHEST)
    return (jnp.einsum('hojb,oja->hab', part, pick_r, precision=lax.Precision.HIGHEST),)


na_bias_table.defvjp(lambda rpb: (na_bias_table(rpb), None), _na_bias_table_bwd)


S5_HALF = SSM_WIDTH // 2
S5_LANES = (SSM_GROUPS // 2) * SSM_STATE
S5_Q = S5_LANES // LANES


def _s5_tiles(a):
    r = a.shape[0]
    return jnp.transpose(a.reshape(r, S5_Q, LANES), (1, 0, 2)).reshape(S5_Q * r, LANES)


def _s5_untiles(a):
    r = a.shape[0] // S5_Q
    return jnp.transpose(a.reshape(S5_Q, r, LANES), (1, 0, 2)).reshape(r, S5_LANES)


def _s5_put(ref, r, rr, tc, val):
    for q in range(S5_Q):
        ref[pl.ds((q * rr + r) * tc, tc), :] = val[:, q * LANES:(q + 1) * LANES]


def _s5_get(ref, r, rr, tc):
    return jnp.concatenate([ref[pl.ds((q * rr + r) * tc, tc), :] for q in range(S5_Q)], axis=1)


S5_BLOCK = 8
S5_TOGETHER = 8


def _s5_powers(a_re, a_im, backward):
    a_im = -a_im if backward else a_im
    pr, pi = [a_re], [a_im]
    for _ in range(S5_BLOCK - 1):
        pr, pi = pr + [pr[-1] * a_re - pi[-1] * a_im], pi + [pr[-1] * a_im + pi[-1] * a_re]
    order = range(S5_BLOCK - 1, -1, -1) if backward else range(S5_BLOCK)

    def table(p):
        rows = [jnp.broadcast_to(p[s - 1][:, None, :], (p[0].shape[0], S5_BLOCK, LANES)) for s in (1, 2, 4)]
        return jnp.stack(rows + [jnp.stack([p[t] for t in order], axis=1)], axis=1)

    return table(pr), table(pi)


def _s5_scan_block(xr, xi, pr_ref, pi_ref, chain, carry, backward):
    row = lax.broadcasted_iota(jnp.int32, (S5_BLOCK, LANES), 0)
    for e, s in enumerate((1, 2, 4)):
        ar, ai = pr_ref[chain, e], pi_ref[chain, e]
        keep = (row < S5_BLOCK - s) if backward else (row >= s)
        shift = S5_BLOCK - s if backward else s
        sr = jnp.where(keep, pltpu.roll(xr, shift, 0), 0.0)
        si = jnp.where(keep, pltpu.roll(xi, shift, 0), 0.0)
        xr, xi = xr + (ar * sr - ai * si), xi + (ar * si + ai * sr)
    ar, ai = pr_ref[chain, 3], pi_ref[chain, 3]
    cr, ci = carry
    xr, xi = xr + (ar * cr - ai * ci), xi + (ar * ci + ai * cr)
    edge = slice(0, 1) if backward else slice(S5_BLOCK - 1, S5_BLOCK)
    return xr, xi, (xr[edge], xi[edge])


def _s5_scan_chunk(xr_ref, xi_ref, pr_ref, pi_ref, sr_ref, si_ref, chains, tc, backward):
    blocks = tc // S5_BLOCK
    for first in range(0, chains, S5_TOGETHER):
        group = range(first, min(first + S5_TOGETHER, chains))

        def block(k, carries, group=group):
            j = blocks - 1 - k if backward else k
            out = []
            for chain, carry in zip(group, carries):
                at = pl.ds(pl.multiple_of(chain * tc + j * S5_BLOCK, S5_BLOCK), S5_BLOCK)
                xr, xi, carry = _s5_scan_block(xr_ref[at, :], xi_ref[at, :], pr_ref, pi_ref, chain, carry, backward)
                xr_ref[at, :] = xr
                xi_ref[at, :] = xi
                out.append(carry)
            return tuple(out)

        start = tuple((sr_ref[pl.ds(chain, 1), :], si_ref[pl.ds(chain, 1), :]) for chain in group)
        for chain, (cr, ci) in zip(group, lax.fori_loop(0, blocks, block, start)):
            sr_ref[pl.ds(chain, 1), :] = cr
            si_ref[pl.ds(chain, 1), :] = ci


def _s5_fwd_call(u, a_re, a_im, b_re, b_im, c_re, c_im):
    rr, t_len, _ = u.shape
    sets = b_re.shape[0]
    per = rr // sets
    tc = _tile(t_len, (256, 128))
    nt = t_len // tc
    qr = S5_Q * rr

    def body(u_ref, ar_ref, ai_ref, br_ref, bi_ref, cr_ref, ci_ref, y_ref, hr_ref, hi_ref, sr_ref, si_ref):
        @pl.when(pl.program_id(0) == 0)
        def _():
            sr_ref[...] = jnp.zeros_like(sr_ref)
            si_ref[...] = jnp.zeros_like(si_ref)

        for r in range(rr):
            ub = u_ref[r]
            _s5_put(hr_ref, r, rr, tc, _dot(ub, br_ref[r // per]))
            _s5_put(hi_ref, r, rr, tc, _dot(ub, bi_ref[r // per]))
        _s5_scan_chunk(hr_ref, hi_ref, ar_ref, ai_ref, sr_ref, si_ref, qr, tc, False)
        for r in range(rr):
            y_ref[r] = (_dot(_s5_get(hr_ref, r, rr, tc).astype(BF16), cr_ref[r // per])
                        - _dot(_s5_get(hi_ref, r, rr, tc).astype(BF16), ci_ref[r // per]))

    full = lambda a: pl.BlockSpec(a.shape, lambda i: (0,) * a.ndim)
    h_spec = pl.BlockSpec((None, qr * tc, LANES), lambda i: (i, 0, 0))
    h_shape = jax.ShapeDtypeStruct((nt, qr * tc, LANES), F32)
    a_re, a_im = _s5_powers(_s5_tiles(a_re), _s5_tiles(a_im), False)
    return pl.pallas_call(
        body, name='s5_fwd', grid=(nt,),
        out_shape=(jax.ShapeDtypeStruct((rr, t_len, S5_HALF), F32), h_shape, h_shape),
        in_specs=[pl.BlockSpec((rr, tc, S5_HALF), lambda i: (0, i, 0)), full(a_re), full(a_im), full(b_re), full(b_im),
                  full(c_re), full(c_im)],
        out_specs=(pl.BlockSpec((rr, tc, S5_HALF), lambda i: (0, i, 0)), h_spec, h_spec),
        scratch_shapes=[pltpu.VMEM((qr, LANES), F32), pltpu.VMEM((qr, LANES), F32)],
        compiler_params=_params('arbitrary'),
    )(u, a_re, a_im, b_re, b_im, c_re, c_im)


def _s5_bwd_call(u, a_re, a_im, b_re, b_im, c_re, c_im, h_re, h_im, dy):
    rr, t_len, _ = u.shape
    sets = b_re.shape[0]
    per = rr // sets
    nt, rows, _ = h_re.shape
    qr = S5_Q * rr
    tc = rows // qr

    def body(u_ref, dy_ref, ar_ref, ai_ref, br_ref, bi_ref, cr_ref, ci_ref, hr_ref, hi_ref,
             du_ref, dar_ref, dai_ref, dbr_ref, dbi_ref, dcr_ref, dci_ref, gr_ref, gi_ref, sr_ref, si_ref):
        i = pl.program_id(0)

        @pl.when(i == 0)
        def _():
            for ref in (dar_ref, dai_ref, dbr_ref, dbi_ref, dcr_ref, dci_ref, sr_ref, si_ref):
                ref[...] = jnp.zeros_like(ref)

        for r in range(rr):
            dyb = dy_ref[r]
            _s5_put(gr_ref, r, rr, tc, _dot_nt(dyb, cr_ref[r // per]))
            _s5_put(gi_ref, r, rr, tc, -_dot_nt(dyb, ci_ref[r // per]))
        g_r, g_i = sr_ref[...], si_ref[...]
        last = pl.ds(tc - 1, qr, stride=tc)
        dar_ref[...] += g_r * hr_ref[last, :] + g_i * hi_ref[last, :]
        dai_ref[...] += g_i * hr_ref[last, :] - g_r * hi_ref[last, :]
        _s5_scan_chunk(gr_ref, gi_ref, ar_ref, ai_ref, sr_ref, si_ref, qr, tc, True)
        row = lax.broadcasted_iota(jnp.int32, (tc, LANES), 0)
        for chain in range(qr):
            at, one = pl.ds(chain * tc, tc), pl.ds(chain, 1)
            p_r = jnp.where(row >= 1, pltpu.roll(hr_ref[at, :], 1, 0), 0.0)
            p_i = jnp.where(row >= 1, pltpu.roll(hi_ref[at, :], 1, 0), 0.0)
            g_r, g_i = gr_ref[at, :], gi_ref[at, :]
            dar_ref[one, :] += jnp.sum(g_r * p_r + g_i * p_i, axis=0, keepdims=True)
            dai_ref[one, :] += jnp.sum(g_i * p_r - g_r * p_i, axis=0, keepdims=True)
        for r in range(rr):
            s = r // per
            ub, dyb = u_ref[r], dy_ref[r]
            grb, gib = _s5_get(gr_ref, r, rr, tc).astype(BF16), _s5_get(gi_ref, r, rr, tc).astype(BF16)
            du_ref[r] = _dot_nt(grb, br_ref[s]) + _dot_nt(gib, bi_ref[s])
            dbr_ref[s] += _dot_tn(ub, grb)
            dbi_ref[s] += _dot_tn(ub, gib)
            dcr_ref[s] += _dot_tn(_s5_get(hr_ref, r, rr, tc).astype(BF16), dyb)
            dci_ref[s] -= _dot_tn(_s5_get(hi_ref, r, rr, tc).astype(BF16), dyb)

    full = lambda a: pl.BlockSpec(a.shape, lambda i: (0,) * a.ndim)
    back = lambda i: nt - 1 - i
    tok = pl.BlockSpec((rr, tc, S5_HALF), lambda i: (0, back(i), 0))
    h_spec = pl.BlockSpec((None, qr * tc, LANES), lambda i: (back(i), 0, 0))
    f = lambda a: jax.ShapeDtypeStruct(a.shape, F32)
    a_re, a_im = _s5_powers(_s5_tiles(a_re), _s5_tiles(a_im), True)
    da = jax.ShapeDtypeStruct((qr, LANES), F32)
    du, da_re, da_im, db_re, db_im, dc_re, dc_im = pl.pallas_call(
        body, name='s5_bwd', grid=(nt,),
        out_shape=(jax.ShapeDtypeStruct(u.shape, F32), da, da, f(b_re), f(b_im), f(c_re), f(c_im)),
        in_specs=[tok, tok, full(a_re), full(a_im), full(b_re), full(b_im), full(c_re), full(c_im), h_spec, h_spec],
        out_specs=(tok, full(da), full(da), full(b_re), full(b_im), full(c_re), full(c_im)),
        scratch_shapes=[pltpu.VMEM((qr * tc, LANES), F32), pltpu.VMEM((qr * tc, LANES), F32),
                        pltpu.VMEM((qr, LANES), F32), pltpu.VMEM((qr, LANES), F32)],
        compiler_params=_params('arbitrary'),
    )(u, dy, a_re, a_im, b_re, b_im, c_re, c_im, h_re, h_im)
    return du, _s5_untiles(da_re), _s5_untiles(da_im), db_re, db_im, dc_re, dc_im


@jax.custom_vjp
def s5_core(u, a_re, a_im, b_re, b_im, c_re, c_im):
    return _s5_fwd_call(u.astype(BF16), a_re, a_im, b_re.astype(BF16), b_im.astype(BF16), c_re.astype(BF16),
                        c_im.astype(BF16))[0]


def _s5_core_fwd(u, a_re, a_im, b_re, b_im, c_re, c_im):
    args = (u.astype(BF16), a_re, a_im, b_re.astype(BF16), b_im.astype(BF16), c_re.astype(BF16), c_im.astype(BF16))
    y, h_re, h_im = _s5_fwd_call(*args)
    return y, args + (h_re, h_im)


def _s5_core_bwd(res, g):
    return _s5_bwd_call(*res, g.astype(BF16))


s5_core.defvjp(_s5_core_fwd, _s5_core_bwd)


def _shard_view(ref, axis, index, width):
    return ref.at[(slice(None),) * axis + (pl.ds(pl.multiple_of(index * width, width), width),)]


def _exchange_many(xs, cuts, gather, name):
    n = len(xs)
    if gather:
        shards = [x.shape for x in xs]
    else:
        shards = [x.shape[1:] if cut is None else x.shape[:cut] + (x.shape[cut] // N_DEV,) + x.shape[cut + 1:]
                  for x, cut in zip(xs, cuts)]

    def full_shape(shard, cut):
        return shard[:cut] + (N_DEV * shard[cut],) + shard[cut + 1:]

    out_shapes = [jax.ShapeDtypeStruct((N_DEV,) + tuple(s) if (cut is None or not gather) else full_shape(tuple(s), cut), x.dtype)
                  for x, s, cut in zip(xs, shards, cuts)]

    def body(*refs):
        x_refs, out_refs = refs[:n], refs[n:2 * n]
        send_sems, recv_sems, local_sems = refs[2 * n:]
        ix, iy, ic = lax.axis_index('x'), lax.axis_index('y'), lax.axis_index('c')
        me = 4 * ix + 2 * iy + ic

        def flipped(k):
            px = 1 - ix if k & 4 else ix
            py = 1 - iy if k & 2 else iy
            pc = 1 - ic if k & 1 else ic
            return (px, py, pc), 4 * px + 2 * py + pc

        def block(ref, cut, shard, who):
            return ref.at[who] if cut is None else _shard_view(ref, cut, who, shard[cut])

        def ends(i, sender, receiver):
            if gather:
                return x_refs[i], block(out_refs[i], cuts[i], shards[i], sender)
            return block(x_refs[i], cuts[i], shards[i], receiver), out_refs[i].at[sender]

        def copy(i, k, sender, receiver):
            src, dst = ends(i, sender, receiver)
            return pltpu.make_async_remote_copy(src_ref=src, dst_ref=dst, send_sem=send_sems.at[i * (N_DEV - 1) + k - 1],
                                                 recv_sem=recv_sems.at[i * (N_DEV - 1) + k - 1], device_id=flipped(k)[0],
                                                 device_id_type=MESH)

        own = [pltpu.make_async_copy(*ends(i, me, me), local_sems.at[i]) for i in range(n)]
        for cp in own:
            cp.start()
        if gather:
            chips = (2, 4, 6)
            sent = [copy(i, k, me, flipped(k)[1]) for k in (1,) + chips for i in range(n)]
            for cp in sent:
                cp.start()
            for k in chips:
                for i in range(n):
                    copy(i, k, flipped(k)[1], me).wait_recv()
                    src, dst = ends(i, flipped(k)[1], me)
                    sent.append(pltpu.make_async_remote_copy(
                        src_ref=dst, dst_ref=dst, send_sem=send_sems.at[i * (N_DEV - 1) + k], recv_sem=recv_sems.at[i * (N_DEV - 1) + k],
                        device_id=flipped(1)[0], device_id_type=MESH))
                    sent[-1].start()
            for k in (1, 3, 5, 7):
                for i in range(n):
                    src, dst = ends(i, flipped(k)[1], me)
                    pltpu.make_async_remote_copy(
                        src_ref=dst, dst_ref=dst, send_sem=send_sems.at[i * (N_DEV - 1) + k - 1], recv_sem=recv_sems.at[i * (N_DEV - 1) + k - 1],
                        device_id=flipped(1)[0], device_id_type=MESH).wait_recv()
            for cp in sent:
                cp.wait_send()
            for cp in own:
                cp.wait()
            return
        sent = [copy(i, k, me, flipped(k)[1]) for k in range(1, N_DEV) for i in range(n)]
        for cp in sent:
            cp.start()
        for k in range(1, N_DEV):
            for i in range(n):
                copy(i, k, flipped(k)[1], me).wait_recv()
        for cp in sent:
            cp.wait_send()
        for cp in own:
            cp.wait()

    hbm = pl.BlockSpec(memory_space=pltpu.HBM)
    pairs = n * (N_DEV - 1)
    return pl.pallas_call(
        body, name=name, out_shape=out_shapes, in_specs=[hbm] * n, out_specs=[hbm] * n,
        scratch_shapes=[pltpu.SemaphoreType.DMA((pairs,)), pltpu.SemaphoreType.DMA((pairs,)), pltpu.SemaphoreType.DMA((n,))],
    )(*xs)


def _adamw_landed(landed, w, m, v):
    shape = w.shape
    slots = landed.shape[0]
    w, m, v = (_as_rows(a) for a in (w, m, v))
    rows, cols = w.shape
    landed = landed.reshape(slots, rows, cols)
    tr = _tile(rows, (256, 128, 64, 32, 16))

    def body(l_ref, w_ref, m_ref, v_ref, g_ref, d_ref, nm_ref, nv_ref):
        g = l_ref[0].astype(F32)
        for d in range(1, slots):
            g = g + l_ref[d].astype(F32)
        g_ref[...] = g
        d_ref[...], nm_ref[...], nv_ref[...] = _adamw_math(w_ref[...], g, m_ref[...], v_ref[...])

    spec = pl.BlockSpec((tr, cols), lambda i: (i, 0))
    out = pl.pallas_call(
        body, name='adamw_landed', grid=(rows // tr,), out_shape=(jax.ShapeDtypeStruct(w.shape, F32),) * 4,
        in_specs=[pl.BlockSpec((slots, tr, cols), lambda i: (0, i, 0))] + [spec] * 3, out_specs=(spec,) * 4,
        compiler_params=_params('parallel'),
    )(landed, w, m, v)
    return tuple(o.reshape(shape) for o in out)


def _all_reduce_small(x):
    g = _exchange_many([x], [None], True, 'gather_small_grads')[0]

    def body(g_ref, o_ref):
        acc = g_ref[0]
        for d in range(1, N_DEV):
            acc = acc + g_ref[d]
        o_ref[...] = acc

    return pl.pallas_call(body, name='sum_small', out_shape=jax.ShapeDtypeStruct(x.shape, F32))(g)


def _adamw_math(w, g, m, v):
    m = ADAM_B1 * m + (1.0 - ADAM_B1) * g
    v = ADAM_B2 * v + (1.0 - ADAM_B2) * (g * g)
    m_hat = m / (1.0 - ADAM_B1 ** ADAM_STEP)
    v_hat = v / (1.0 - ADAM_B2 ** ADAM_STEP)
    return -ADAM_LR * (m_hat / (jnp.sqrt(v_hat) + ADAM_EPS) + ADAM_WD * w), m, v


def _as_rows(a):
    return a.reshape(1, -1) if a.ndim < 2 else a.reshape(-1, a.shape[-1])


def _as_lanes(a):
    return a.reshape(-1, LANES) if a.size % LANES == 0 else a.reshape(1, -1)


def _adamw_big(w, g, m, v):
    shape = w.shape
    w, g, m, v = (_as_rows(a) for a in (w, g, m, v))
    rows, cols = w.shape
    tr = _tile(rows, (512, 256, 128, 64, 32, 16, 8))

    def body(w_ref, g_ref, m_ref, v_ref, d_ref, nm_ref, nv_ref):
        d_ref[...], nm_ref[...], nv_ref[...] = _adamw_math(w_ref[...], g_ref[...], m_ref[...], v_ref[...])

    spec = pl.BlockSpec((tr, cols), lambda i: (i, 0))
    out = pl.pallas_call(
        body, name='adamw', grid=(rows // tr,), out_shape=(jax.ShapeDtypeStruct(w.shape, F32),) * 3,
        in_specs=[spec] * 4, out_specs=(spec,) * 3, compiler_params=_params('parallel'),
    )(w, g, m, v)
    return tuple(o.reshape(shape) for o in out)


def _adamw_small(ws, gs, ms, vs):
    n = len(ws)
    shapes = [w.shape for w in ws]
    flat = [_as_lanes(a) for group in (ws, gs, ms, vs) for a in group]

    def body(*refs):
        ins, outs = refs[:4 * n], refs[4 * n:]
        for i in range(n):
            d, m, v = _adamw_math(ins[i][...], ins[n + i][...], ins[2 * n + i][...], ins[3 * n + i][...])
            outs[i][...], outs[n + i][...], outs[2 * n + i][...] = d, m, v

    out = pl.pallas_call(
        body, name='adamw_small', out_shape=tuple(jax.ShapeDtypeStruct(flat[i].shape, F32) for _ in range(3) for i in range(n)),
    )(*flat)
    return [tuple(out[j * n + i].reshape(shapes[i]) for j in range(3)) for i in range(n)]


def rms_norm(x, g):
    return x * lax.rsqrt(jnp.mean(jnp.square(x), axis=-1, keepdims=True) + EPS) * g


def modulate(x, g, shift, scale):
    return rms_norm(x, g) * (1 + scale) + shift


def rope_tables(n_tokens, rot_dim):
    t = jnp.arange(n_tokens)
    rows = (t // GRID_W).astype(F32)
    cols = (t % GRID_W).astype(F32)
    axis_dim = rot_dim // 2
    freqs = ROPE_BASE ** (-jnp.arange(0, axis_dim, 2, dtype=F32) / axis_dim)
    ang_r, ang_c = rows[:, None] * freqs, cols[:, None] * freqs
    ang = jnp.concatenate([ang_r, ang_r, ang_c, ang_c], axis=-1)
    return jnp.cos(ang), jnp.sin(ang)


def rope(x, cos, sin):
    x1, x2, x3, x4 = jnp.split(x, 4, axis=-1)
    rot = jnp.concatenate([-x2, x1, -x4, x3], axis=-1)
    return x * cos[:, None, :] + rot * sin[:, None, :]


def heads_first(t):
    return jnp.swapaxes(t, 1, 2)


def tokens_matmul(t, w):
    b, n, k = t.shape
    return linear(t.reshape(b * n, k), w).reshape(b, n, w.shape[1])


def s5_discretize(lam_re, lam_im, log_dt, b_re, b_im):
    dt = jnp.exp(log_dt)[:, None]
    mag = jnp.exp(lam_re * dt)
    a_re = mag * jnp.cos(lam_im * dt)
    a_im = mag * jnp.sin(lam_im * dt)
    den = jnp.square(lam_re) + jnp.square(lam_im)
    f_re = ((a_re - 1.0) * lam_re + a_im * lam_im) / den
    f_im = (a_im * lam_re - (a_re - 1.0) * lam_im) / den
    bb_re = f_re[..., None] * b_re - f_im[..., None] * b_im
    bb_im = f_re[..., None] * b_im + f_im[..., None] * b_re
    return a_re, a_im, bb_re, bb_im


def s5_mixer(u_lat, u_ctx, p, j, need_ctx):
    b, n, _ = u_lat.shape
    c = u_ctx.shape[1]
    half_groups = SSM_GROUPS // 2
    eye = jnp.eye(half_groups, dtype=F32)
    a_res, a_ims, b_res, b_ims, c_res, c_ims, seqs = [], [], [], [], [], [], []
    for d in range(2):
        a_re, a_im, bb_re, bb_im = s5_discretize(p['ssm_lam_re'][j, d], p['ssm_lam_im'][j, d], p['ssm_log_dt'][j, d],
                                                 p['ssm_b_re'][j, d], p['ssm_b_im'][j, d])
        for half in range(2):
            grp = slice(half * half_groups, (half + 1) * half_groups)
            a_res.append(a_re[grp].reshape(S5_LANES))
            a_ims.append(a_im[grp].reshape(S5_LANES))
            b_res.append(jnp.einsum('gsp,gh->gphs', bb_re[grp], eye).reshape(S5_HALF, S5_LANES))
            b_ims.append(jnp.einsum('gsp,gh->gphs', bb_im[grp], eye).reshape(S5_HALF, S5_LANES))
            c_res.append(jnp.einsum('gps,gh->gshp', p['ssm_c_re'][j, d][grp], eye).reshape(S5_LANES, S5_HALF))
            c_ims.append(jnp.einsum('gps,gh->gshp', p['ssm_c_im'][j, d][grp], eye).reshape(S5_LANES, S5_HALF))
        flip = (lambda t: t[:, ::-1]) if d == 1 else (lambda t: t)
        seq = jnp.concatenate([flip(u_ctx), flip(u_lat)], axis=1)
        seqs.append(jnp.transpose(seq.reshape(b, c + n, 2, S5_HALF), (2, 0, 1, 3)))
    u = jnp.stack(seqs).reshape(4 * b, c + n, S5_HALF)
    rep = lambda parts: jnp.repeat(jnp.stack(parts), b, axis=0)
    y = s5_core(u, rep(a_res), rep(a_ims), jnp.stack(b_res), jnp.stack(b_ims), jnp.stack(c_res), jnp.stack(c_ims))
    y = jnp.transpose(y.reshape(2, 2, b, c + n, S5_HALF), (0, 2, 3, 1, 4)).reshape(2, b, c + n, SSM_WIDTH)
    d_skip = p['ssm_d'][j]
    y_lat = d_skip * u_lat + y[0, :, c:] + y[1, :, c:][:, ::-1]
    wg, bg = p['ssm_w_glu'][j], p['ssm_b_glu'][j]

    def glu(t):
        t = jax.nn.gelu(t)
        return t * jax.nn.sigmoid(tokens_matmul(t, wg) + bg)

    if not need_ctx:
        return glu(y_lat), None
    y_ctx = d_skip * u_ctx + y[0, :, :c] + y[1, :, :c][:, ::-1]
    return glu(y_lat), glu(y_ctx)


def even_mixer(a_lat, a_ctx, p, j, need_ctx):
    b, n, _ = a_lat.shape
    c = a_ctx.shape[1]
    cos, sin = rope_tables(n, HEAD_DIM)
    proj = tokens_matmul(jnp.concatenate([a_ctx, a_lat], axis=1), p['e_w_in'][j])
    q, k, v, u = jnp.split(proj, [GQA_Q_W, GQA_Q_W + GQA_KV_W, GQA_Q_W + 2 * GQA_KV_W], axis=-1)
    q = rms_norm(q.reshape(b, c + n, GQA_Q_HEADS, HEAD_DIM), p['e_g_q'][j])
    k = rms_norm(k.reshape(b, c + n, GQA_KV_HEADS, HEAD_DIM), p['e_g_k'][j])
    v = v.reshape(b, c + n, GQA_KV_HEADS, HEAD_DIM)
    q_l = rope(q[:, c:], cos, sin)
    k = jnp.concatenate([k[:, :c], rope(k[:, c:], cos, sin)], axis=1)
    scale = HEAD_DIM ** -0.5
    kh, vh = heads_first(k), heads_first(v)
    att_l = heads_first(attention(heads_first(q_l), kh, vh, scale)).reshape(b, n, GQA_Q_W)
    ssm_l, ssm_c = s5_mixer(u[:, c:], u[:, :c], p, j, need_ctx)
    mix_l = jnp.concatenate([att_l, ssm_l], axis=-1)
    if not need_ctx:
        return tokens_matmul(mix_l, p['e_w_out'][j]), None
    att_c = heads_first(attention(heads_first(q[:, :c]), kh[:, :, :c], vh[:, :, :c], scale)).reshape(b, c, GQA_Q_W)
    mix = jnp.concatenate([jnp.concatenate([att_c, ssm_c], axis=-1), mix_l], axis=1)
    out = tokens_matmul(mix, p['e_w_out'][j])
    return out[:, c:], out[:, :c]


def odd_mixer(a_lat, a_ctx, p, j, need_ctx):
    b, n, _ = a_lat.shape
    c = a_ctx.shape[1]
    t = c + n
    cos, sin = rope_tables(n, MLA_ROPE)
    proj = tokens_matmul(jnp.concatenate([a_ctx, a_lat], axis=1), p['o_w_in'][j])
    c1, c2, c3 = MLA_Q_RANK, MLA_Q_RANK + MLA_KV_RANK, MLA_Q_RANK + MLA_KV_RANK + MLA_ROPE
    cq, ckv, kr = proj[..., :c1], proj[..., c1:c2], proj[..., c2:c3]
    nq, nk, nv = jnp.split(proj[..., ODD_NA_AT:], 3, axis=-1)
    q = tokens_matmul(rms_norm(cq, p['mla_g_cq'][j]), p['mla_w_uq'][j]).reshape(b, t, MLA_HEADS, MLA_QK)
    kv = tokens_matmul(rms_norm(ckv, p['mla_g_ckv'][j]), p['mla_w_ukv'][j]).reshape(b, t, MLA_HEADS, MLA_NOPE + MLA_V)
    k = jnp.concatenate([kv[..., :MLA_NOPE], jnp.broadcast_to(kr[:, :, None, :], (b, t, MLA_HEADS, MLA_ROPE))], axis=-1)
    q, k, mv = rms_norm(q, p['mla_g_q'][j]), rms_norm(k, p['mla_g_k'][j]), kv[..., MLA_NOPE:]

    def rope_tail(x):
        tail = jnp.concatenate([x[:, :c, :, MLA_NOPE:], rope(x[:, c:, :, MLA_NOPE:], cos, sin)], axis=1)
        return jnp.concatenate([x[..., :MLA_NOPE], tail], axis=-1)

    q, k = rope_tail(q), rope_tail(k)
    qh, kh, vh = heads_first(q), heads_first(k), heads_first(mv)
    mla_scale = MLA_QK ** -0.5
    mla_l = heads_first(attention(qh[:, :, c:], kh, vh, mla_scale)).reshape(b, n, MLA_HEADS * MLA_V)
    nq = heads_first(rms_norm(nq.reshape(b, t, NA_HEADS, HEAD_DIM), p['na_g_q'][j]))
    nk = heads_first(rms_norm(nk.reshape(b, t, NA_HEADS, HEAD_DIM), p['na_g_k'][j]))
    nv = heads_first(nv.reshape(b, t, NA_HEADS, HEAD_DIM))
    na_scale = HEAD_DIM ** -0.5
    na_l = na_attention(nq[:, :, c:], nk[:, :, c:], nv[:, :, c:], nk[:, :, :c], nv[:, :, :c], na_bias_table(p['na_rpb'][j]),
                        na_scale)
    na_l = heads_first(na_l).reshape(b, n, NA_W)
    mix_l = jnp.concatenate([mla_l, na_l], axis=-1)
    if not need_ctx:
        return tokens_matmul(mix_l, p['o_w_out'][j]), None
    mla_c = heads_first(attention(qh[:, :, :c], kh[:, :, :c], vh[:, :, :c], mla_scale)).reshape(b, c, MLA_HEADS * MLA_V)
    na_c = heads_first(attention(nq[:, :, :c], nk[:, :, :c], nv[:, :, :c], na_scale)).reshape(b, c, NA_W)
    mix = jnp.concatenate([jnp.concatenate([mla_c, na_c], axis=-1), mix_l], axis=1)
    out = tokens_matmul(mix, p['o_w_out'][j])
    return out[:, c:], out[:, :c]


def mlp(h, w1, w2):
    b, n, k = h.shape
    return mlp_rows(h.reshape(b * n, k), w1, w2).reshape(b, n, w2.shape[1])


def local_loss(x, p, m_lat, m_ctx, ctx, target):
    depth = m_lat.shape[0]
    c = ctx.shape[1]
    xc = ctx
    for i in range(depth):
        need_ctx = i < depth - 1
        j = i // 2
        ml = [m_lat[i, :, s][:, None, :] for s in range(N_MOD)]
        mc = [m_ctx[i, s][None, None, :] for s in range(N_MOD)]
        a_lat = modulate(x, p['g_norm1'][i], ml[0], ml[1])
        a_ctx = modulate(xc, p['g_norm1'][i], mc[0], mc[1])
        mixer = even_mixer if i % 2 == 0 else odd_mixer
        o_lat, o_ctx = mixer(a_lat, a_ctx, p, j, need_ctx)
        x = x + ml[2] * o_lat
        h_lat = modulate(x, p['g_norm2'][i], ml[3], ml[4])
        if need_ctx:
            xc = xc + mc[2] * o_ctx
            h_ctx = modulate(xc, p['g_norm2'][i], mc[3], mc[4])
            ff = mlp(jnp.concatenate([h_ctx, h_lat], axis=1), p['w_ff1'][i], p['w_ff2'][i])
            x = x + ml[5] * ff[:, c:]
            xc = xc + mc[5] * ff[:, :c]
        else:
            x = x + ml[5] * mlp(h_lat, p['w_ff1'][i], p['w_ff2'][i])
    return 0.5 * jnp.sum(jnp.mean(jnp.square(x - target), axis=-1))


def _packed_rows(size, layout):
    width, group = layout
    return -(-size // (width * group)) * group


def _pack_rows(flat, layout):
    width = layout[0]
    rows = _packed_rows(flat.shape[-1], layout)
    flat = jnp.pad(flat, [(0, 0)] * (flat.ndim - 1) + [(0, rows * width - flat.shape[-1])])
    return flat.reshape(flat.shape[:-1] + (rows, width))


def _unpack_rows(rows, shape):
    lead = rows.shape[:-2]
    return rows.reshape(lead + (-1,))[..., :math.prod(shape)].reshape(lead + tuple(shape))


def _unpack_all(packed, shapes, layout):
    out, at = [], 0
    for shape in shapes:
        rows = _packed_rows(math.prod(shape), layout)
        out.append(_unpack_rows(packed[..., at:at + rows, :], shape))
        at += rows
    return out


def _join_shards(g, axis):
    g = jnp.moveaxis(g, 0, axis)
    return g.reshape(g.shape[:axis] + (N_DEV * g.shape[axis + 1],) + g.shape[axis + 2:])


def _split_shards(full, axis):
    s = full.shape
    return jnp.moveaxis(full.reshape(s[:axis] + (N_DEV, s[axis] // N_DEV) + s[axis + 1:]), axis, 0)


def _gather_packed(parts, dtype, layout, name):
    packed = jnp.concatenate([_pack_rows(a.astype(dtype).reshape(-1), layout) for a in parts], axis=0)
    return _unpack_all(_exchange_many([packed], [None], True, name)[0], [a.shape for a in parts], layout)


def kernel(x, c, ctx, c_ctx, w_mod, b_mod, g_norm1, g_norm2, w_ff1, w_ff2, e_w_in, e_w_out, e_g_q, e_g_k, ssm_lam_re, ssm_lam_im, ssm_log_dt, ssm_b_re, ssm_b_im, ssm_c_re, ssm_c_im, ssm_d, ssm_w_glu, ssm_b_glu, o_w_in, o_w_out, mla_g_cq, mla_g_ckv, mla_w_uq, mla_w_ukv, mla_g_q, mla_g_k, na_g_q, na_g_k, na_rpb, loss_target, m_c_ctx, m_w_mod, m_b_mod, m_g_norm1, m_g_norm2, m_w_ff1, m_w_ff2, m_e_w_in, m_e_w_out, m_e_g_q, m_e_g_k, m_ssm_lam_re, m_ssm_lam_im, m_ssm_log_dt, m_ssm_b_re, m_ssm_b_im, m_ssm_c_re, m_ssm_c_im, m_ssm_d, m_ssm_w_glu, m_ssm_b_glu, m_o_w_in, m_o_w_out, m_mla_g_cq, m_mla_g_ckv, m_mla_w_uq, m_mla_w_ukv, m_mla_g_q, m_mla_g_k, m_na_g_q, m_na_g_k, m_na_rpb, v_c_ctx, v_w_mod, v_b_mod, v_g_norm1, v_g_norm2, v_w_ff1, v_w_ff2, v_e_w_in, v_e_w_out, v_e_g_q, v_e_g_k, v_ssm_lam_re, v_ssm_lam_im, v_ssm_log_dt, v_ssm_b_re, v_ssm_b_im, v_ssm_c_re, v_ssm_c_im, v_ssm_d, v_ssm_w_glu, v_ssm_b_glu, v_o_w_in, v_o_w_out, v_mla_g_cq, v_mla_g_ckv, v_mla_w_uq, v_mla_w_ukv, v_mla_g_q, v_mla_g_k, v_na_g_q, v_na_g_k, v_na_rpb):
    given = dict(locals())
    x, c, ctx, target = given['x'], given['c'], given['ctx'], given['loss_target']
    b_loc, _, d_model = x.shape
    depth = given['w_mod'].shape[0]
    ix, iy, ic = lax.axis_index('x'), lax.axis_index('y'), lax.axis_index('c')
    me = 4 * ix + 2 * iy + ic
    n_batch = N_DEV * b_loc
    mod_w = given['w_mod'].shape[2]

    c_rows = jnp.concatenate([c, jnp.zeros((8 - b_loc, d_model), F32)], axis=0)
    small = _gather_packed([c_rows] + [given[n] for n in SHARDED_SMALL], F32, PACK_SMALL,'gather_small')
    c_all = small[0][:, :b_loc].reshape(n_batch, d_model)
    full = {n: _join_shards(g, SHARDED_SMALL[n]) for n, g in zip(SHARDED_SMALL, small[1:])}
    cuts = {n: (a if given[n].shape[a] % (16 if a == 1 else LANES) == 0 else None) for n, a in BIG.items()}
    big = _exchange_many([given[n].astype(BF16) for n in BIG], [cuts[n] for n in BIG], True, 'gather_weights')
    for n, g in zip(BIG, big):
        g = g if cuts[n] is not None else _join_shards(g, BIG[n])
        full[n] = [g[i] for i in range(g.shape[0])]
    c3 = MLA_Q_RANK + MLA_KV_RANK + MLA_ROPE
    full['o_w_in'] = [jnp.concatenate([w[:, :c3], jnp.zeros((w.shape[0], ODD_NA_AT - c3), BF16), w[:, c3:]], axis=-1)
                      for w in full['o_w_in']]
    for n in REPLICATED:
        full[n] = given[n]

    rows17 = 16 * (-(-(n_batch + 1) // 16))
    cond = jnp.concatenate([jax.nn.silu(c_all), jax.nn.silu(given['c_ctx'])[None],
                            jnp.zeros((rows17 - n_batch - 1, d_model), F32)], axis=0)
    mod_mine = jnp.stack([_matmul(cond, given['w_mod'][i], 'nn', F32) for i in range(depth)])
    b_mine = lax.dynamic_slice_in_dim(given['b_mod'], me * mod_w, mod_w, axis=1)
    mod_mine = mod_mine + b_mine[:, None, :]
    mod_all = _gather_packed([mod_mine], F32, PACK_SMALL,'gather_mod')[0]
    mod_all = jnp.moveaxis(mod_all, 0, 2).reshape(depth, rows17, N_MOD, d_model)
    m_lat = lax.dynamic_slice_in_dim(mod_all, me * b_loc, b_loc, axis=1)
    m_ctx = mod_all[:, n_batch]

    diff = {n: full[n] for n in list(BIG) + list(SHARDED_SMALL) + REPLICATED}
    loss, (g_x, g_p, g_ml, g_mc) = jax.value_and_grad(local_loss, argnums=(0, 1, 2, 3))(x, diff, m_lat, m_ctx, ctx, target)
    loss = lax.psum(loss, ('x', 'y', 'c'))
    g_p['o_w_in'] = [jnp.concatenate([g[:, :c3], g[:, ODD_NA_AT:]], axis=-1) for g in g_p['o_w_in']]

    g_rows = jnp.concatenate([g_ml.reshape(depth, b_loc, N_MOD * d_model), g_mc.reshape(depth, 1, N_MOD * d_model),
                              jnp.zeros((depth, 8 - b_loc - 1, N_MOD * d_model), F32)], axis=1)
    g_mod_all = _gather_packed([g_rows], F32, PACK_SMALL,'gather_mod_grads')[0]
    g_lat_all = jnp.moveaxis(g_mod_all[:, :, :b_loc], 0, 1).reshape(depth, n_batch, N_MOD * d_model)
    g_ctx_all = g_mod_all[0, :, b_loc]
    for dev in range(1, N_DEV):
        g_ctx_all = g_ctx_all + g_mod_all[dev, :, b_loc]
    g_mod17 = jnp.concatenate([g_lat_all, g_ctx_all[:, None], jnp.zeros((depth, rows17 - n_batch - 1, N_MOD * d_model), F32)],
                              axis=1)
    grad_b_mod = jnp.sum(g_mod17, axis=1)
    g_mod_mine = lax.dynamic_slice_in_dim(g_mod17, me * mod_w, mod_w, axis=2)
    grad_w_mod = jnp.stack([_matmul(cond, g_mod_mine[i], 'tn', F32) for i in range(depth)])
    d_cond = _matmul(g_mod_mine[0], given['w_mod'][0], 'nt', F32)
    for i in range(1, depth):
        d_cond = d_cond + _matmul(g_mod_mine[i], given['w_mod'][i], 'nt', F32)
    d_cond_ctx = d_cond[n_batch]

    small_names = REPLICATED + list(SHARDED_SMALL)
    parts = [d_cond_ctx] + [g_p[n] for n in small_names]
    packed = jnp.concatenate([_pack_rows(a.reshape(-1), PACK_SMALL) for a in parts], axis=0)
    summed = _unpack_all(_all_reduce_small(packed), [a.shape for a in parts], PACK_SMALL)
    grads = dict(zip(['c_ctx'] + small_names, summed))
    c_ctx = given['c_ctx']
    sig = jax.nn.sigmoid(c_ctx)
    grads['c_ctx'] = grads['c_ctx'] * (sig * (1 + c_ctx * (1 - sig)))
    for n, axis in SHARDED_SMALL.items():
        width = given[n].shape[axis]
        grads[n] = lax.dynamic_slice_in_dim(grads[n], me * width, width, axis=axis)
    grads['w_mod'], grads['b_mod'] = grad_w_mod, grad_b_mod

    stacked = [jnp.stack(g_p[n]) for n in BIG]
    stacked = [g if cuts[n] is not None else _split_shards(g, BIG[n]) for n, g in zip(BIG, stacked)]
    landed = _exchange_many(stacked, [cuts[n] for n in BIG], False, 'scatter_weight_grads')

    upd = {}
    for n, slots in zip(BIG, landed):
        grads[n], *upd[n] = _adamw_landed(slots, given[n], given['m_' + n], given['v_' + n])
    upd['w_mod'] = _adamw_big(given['w_mod'], grads['w_mod'], given['m_w_mod'], given['v_w_mod'])
    rest = [n for n in WEIGHTS if n not in upd]
    out = _adamw_small([given[n] for n in rest], [grads[n] for n in rest], [given['m_' + n] for n in rest],
                       [given['v_' + n] for n in rest])
    upd.update(dict(zip(rest, out)))
    return (loss, g_x, *[grads[n] for n in WEIGHTS], *[upd[n][0] for n in WEIGHTS], *[upd[n][1] for n in WEIGHTS],
            *[upd[n][2] for n in WEIGHTS])
```

```python
import functools
import math

import jax
import jax.numpy as jnp
from jax import lax
from jax.experimental import pallas as pl
from jax.experimental.pallas import tpu as pltpu

F32, BF16 = jnp.float32, jnp.bfloat16
MESH = pl.DeviceIdType.MESH
N_DEV = 8
VMEM_LIMIT_BYTES = 56 * 1024 * 1024
MM_TILE_BYTES = 6 * 1024 * 1024
LANES = 128
PACK_SMALL = (128, 8)

GRID_W = 64
HEAD_DIM = 64
ROPE_BASE = 10000.0
EPS = 1e-6
N_MOD = 6
GQA_Q_HEADS, GQA_KV_HEADS = 12, 4
GQA_Q_W, GQA_KV_W = GQA_Q_HEADS * HEAD_DIM, GQA_KV_HEADS * HEAD_DIM
SSM_WIDTH, SSM_GROUP, SSM_GROUPS, SSM_STATE = 256, 16, 16, 64
MLA_HEADS, MLA_Q_RANK, MLA_KV_RANK, MLA_NOPE, MLA_ROPE, MLA_V = 8, 512, 256, 64, 32, 64
MLA_QK = MLA_NOPE + MLA_ROPE
NA_HEADS, NA_WIN_R, NA_WIN_C = 8, 8, 16
NA_W = NA_HEADS * HEAD_DIM
ODD_IN_W = MLA_Q_RANK + MLA_KV_RANK + MLA_ROPE + 3 * NA_W
ODD_NA_AT = 1024
ODD_IN_PAD = ODD_NA_AT + 3 * NA_W
NEG = -1e30

ADAM_LR, ADAM_B1, ADAM_B2, ADAM_EPS, ADAM_WD, ADAM_STEP = 0.001, 0.9, 0.999, 1e-08, 0.01, 10

FWD_PARAMS = ['x', 'c', 'ctx', 'c_ctx', 'w_mod', 'b_mod', 'g_norm1', 'g_norm2', 'w_ff1', 'w_ff2', 'e_w_in', 'e_w_out',
              'e_g_q', 'e_g_k', 'ssm_lam_re', 'ssm_lam_im', 'ssm_log_dt', 'ssm_b_re', 'ssm_b_im', 'ssm_c_re', 'ssm_c_im',
              'ssm_d', 'ssm_w_glu', 'ssm_b_glu', 'o_w_in', 'o_w_out', 'mla_g_cq', 'mla_g_ckv', 'mla_w_uq', 'mla_w_ukv',
              'mla_g_q', 'mla_g_k', 'na_g_q', 'na_g_k', 'na_rpb']
WEIGHTS = FWD_PARAMS[3:]
BIG = {'w_ff1': 2, 'w_ff2': 1, 'e_w_in': 2, 'e_w_out': 1, 'o_w_in': 2, 'o_w_out': 1, 'mla_w_uq': 2, 'mla_w_ukv': 2,
       'ssm_w_glu': 1}
SHARDED_SMALL = {'mla_g_cq': 1, 'mla_g_ckv': 1}
REPLICATED = [n for n in WEIGHTS if n not in BIG and n not in SHARDED_SMALL and n not in ('w_mod', 'c_ctx', 'b_mod')]


def _tile(dim, prefs):
    for p in prefs:
        if dim >= p and dim % p == 0:
            return p
    return dim


def _params(*sem):
    return pltpu.CompilerParams(dimension_semantics=sem, vmem_limit_bytes=VMEM_LIMIT_BYTES)


def _dot_nt(a, b):
    return lax.dot_general(a, b, (((1,), (1,)), ((), ())), preferred_element_type=F32)


def _dot_tn(a, b):
    return lax.dot_general(a, b, (((0,), (0,)), ((), ())), preferred_element_type=F32)


def _dot(a, b):
    return jnp.dot(a, b, preferred_element_type=F32)


def _matmul(a, b, kind, out_dtype, finish=None, extra=None, n_out=1):
    a, b = a.astype(BF16), b.astype(BF16)
    if kind == 'nn':
        (m, kd), n = a.shape, b.shape[1]
    elif kind == 'nt':
        (m, kd), n = a.shape, b.shape[0]
    else:
        (kd, m), n = a.shape, b.shape[1]
    if kind == 'tn':
        tm = m if m <= 1024 else _tile(m, (1024, 768, 512, 256, 128))
        tn = _tile(n, (1024, 768, 512, 256, 128))
    else:
        tn = n if kd * n * 2 <= MM_TILE_BYTES else _tile(n, (1024, 768, 512, 256, 128))
        tm = _tile(m, [t for t in (1536, 1024, 768, 512, 256, 128) if t * tn * 4 <= MM_TILE_BYTES])
    whole = kind != 'tn' and tn == n and kd * n * 2 <= MM_TILE_BYTES
    tk = kd if whole else _tile(kd, [t for t in (2048, 1536, 1024, 512, 256, 128) if t * max(tm, tn) * 4 <= MM_TILE_BYTES])
    nk = kd // tk
    dn = {'nn': (((1,), (0,)), ((), ())), 'nt': (((1,), (1,)), ((), ())), 'tn': (((0,), (0,)), ((), ()))}[kind]

    n_in = 2 if extra is None else 3

    def body(*refs):
        a_ref, b_ref = refs[:2]
        o_refs = refs[n_in:n_in + n_out]

        def store(total):
            outs = (total,) if finish is None else finish(total, refs[2][...] if extra is not None else None)
            for o_ref, val in zip(o_refs, outs):
                o_ref[...] = val.astype(o_ref.dtype)

        part = lax.dot_general(a_ref[...], b_ref[...], dn, preferred_element_type=F32)
        if nk == 1:
            store(part)
            return
        acc_ref, k = refs[n_in + n_out], pl.program_id(2)

        @pl.when(k == 0)
        def _():
            acc_ref[...] = part

        @pl.when((k > 0) & (k < nk - 1))
        def _():
            acc_ref[...] += part

        @pl.when(k == nk - 1)
        def _():
            store(acc_ref[...] + part)

    a_spec = pl.BlockSpec((tk, tm), lambda i, j, k: (k, i)) if kind == 'tn' else pl.BlockSpec((tm, tk), lambda i, j, k: (i, k))
    b_spec = pl.BlockSpec((tn, tk), lambda i, j, k: (j, k)) if kind == 'nt' else pl.BlockSpec((tk, tn), lambda i, j, k: (k, j))
    o_spec = pl.BlockSpec((tm, tn), lambda i, j, k: (i, j))
    out = pl.pallas_call(
        body, name='mm_' + kind, grid=(m // tm, n // tn, nk),
        out_shape=[jax.ShapeDtypeStruct((m, n), out_dtype)] * n_out,
        in_specs=[a_spec, b_spec] + ([o_spec] if extra is not None else []), out_specs=[o_spec] * n_out,
        scratch_shapes=[pltpu.VMEM((tm, tn), F32)] if nk > 1 else [],
        compiler_params=_params('parallel', 'parallel', 'arbitrary'),
    )(*((a, b) if extra is None else (a, b, extra)))
    return out[0] if n_out == 1 else out


@jax.custom_vjp
def linear(a, w):
    return _matmul(a, w, 'nn', F32)


def _linear_fwd(a, w):
    ab = a.astype(BF16)
    return _matmul(ab, w, 'nn', F32), (ab, w)


def _linear_bwd(res, g):
    ab, w = res
    gb = g.astype(BF16)
    return _matmul(gb, w, 'nt', F32), _matmul(ab, gb, 'tn', w.dtype)


linear.defvjp(_linear_fwd, _linear_bwd)


def _relu2(z, _):
    r = jnp.maximum(z, 0.0)
    return r, r * r


def _relu2_grad(d_act, r):
    return (d_act * (2.0 * r.astype(F32)),)


@jax.custom_vjp
def mlp_rows(h, w1, w2):
    return _mlp_rows_fwd(h, w1, w2)[0]


def _mlp_rows_fwd(h, w1, w2):
    hb = h.astype(BF16)
    r, act = _matmul(hb, w1, 'nn', BF16, finish=_relu2, n_out=2)
    return _matmul(act, w2, 'nn', F32), (hb, w1, w2, r, act)


def _mlp_rows_bwd(res, g):
    hb, w1, w2, r, act = res
    gb = g.astype(BF16)
    dz = _matmul(gb, w2, 'nt', BF16, finish=_relu2_grad, extra=r)
    return _matmul(dz, w1, 'nt', F32), _matmul(hb, dz, 'tn', w1.dtype), _matmul(act, gb, 'tn', w2.dtype)


mlp_rows.defvjp(_mlp_rows_fwd, _mlp_rows_bwd)


LOG2E = math.log2(math.e)


def _softmax_rows(t):
    m = jnp.max(t, axis=-1, keepdims=True)
    e = jnp.exp2(t - m)
    return e * (1.0 / jnp.sum(e, axis=-1, keepdims=True))


ATTN_SPLIT = 2


def _attn_specs(q, kt, vt, bq):
    _, h, nq, dq = q.shape
    _, hk, dv, nk = vt.shape
    g = h // hk
    q_spec = pl.BlockSpec((None, None, bq, dq), lambda b, j, gi, i: (b, j * g + gi, i, 0))
    k_spec = pl.BlockSpec((None, None, nk, dq), lambda b, j, gi, i: (b, j, 0, 0))
    v_spec = pl.BlockSpec((None, None, nk, dv), lambda b, j, gi, i: (b, j, 0, 0))
    o_spec = pl.BlockSpec((None, None, bq, dv), lambda b, j, gi, i: (b, j * g + gi, i, 0))
    t = dict(q=pl.BlockSpec((None, None, dq, bq), lambda b, j, gi, i: (b, j * g + gi, 0, i)),
             o=pl.BlockSpec((None, None, dv, bq), lambda b, j, gi, i: (b, j * g + gi, 0, i)),
             k=pl.BlockSpec((None, None, dq, nk), lambda b, j, gi, i: (b, j, 0, 0)),
             v=pl.BlockSpec((None, None, dv, nk), lambda b, j, gi, i: (b, j, 0, 0)))
    return (q.shape[0], hk, g, nq // bq), q_spec, k_spec, v_spec, o_spec, t


def _attn_blocks(nq):
    bq = _tile(nq, (512, 256, 128))
    return bq, [pl.ds(s * (bq // ATTN_SPLIT), bq // ATTN_SPLIT) for s in range(ATTN_SPLIT)]


def _attn_fwd_call(q, kt, vt, scale):
    b, h, nq, _ = q.shape
    dv = vt.shape[2]
    bq, subs = _attn_blocks(nq)
    grid, q_spec, _, _, _, t = _attn_specs(q, kt, vt, bq)

    def body(q_ref, kt_ref, vt_ref, ot_ref):
        ktb, vtb = kt_ref[...], vt_ref[...]
        for rows in subs:
            p = _softmax_rows(_dot(q_ref[rows, :], ktb) * (scale * LOG2E))
            ot_ref[:, rows] = _dot_nt(vtb, p.astype(BF16))

    return pl.pallas_call(
        body, name='attn_fwd', grid=grid, out_shape=jax.ShapeDtypeStruct((b, h, dv, nq), F32),
        in_specs=[q_spec, t['k'], t['v']], out_specs=t['o'],
        compiler_params=_params('parallel', 'parallel', 'arbitrary', 'arbitrary'),
    )(q, kt, vt)


def _attn_bwd_call(q, kt, vt, do, scale):
    b, h, nq, dq = q.shape
    _, hk, dv_width, nk = vt.shape
    bq, subs = _attn_blocks(nq)
    grid, q_spec, k_spec, v_spec, o_spec, t = _attn_specs(q, kt, vt, bq)

    def body(q_ref, kt_ref, vt_ref, do_ref, dqt_ref, dk_ref, dv_ref):
        @pl.when((pl.program_id(2) == 0) & (pl.program_id(3) == 0))
        def _():
            dk_ref[...] = jnp.zeros_like(dk_ref)
            dv_ref[...] = jnp.zeros_like(dv_ref)

        ktb, vtb = kt_ref[...], vt_ref[...]
        dk, dv = [], []
        for rows in subs:
            qb, dob = q_ref[rows, :], do_ref[rows, :]
            p = _softmax_rows(_dot(qb, ktb) * (scale * LOG2E))
            dp = _dot(dob, vtb)
            ds = p * (dp - jnp.sum(p * dp, axis=-1, keepdims=True))
            dsb = (ds * scale).astype(BF16)
            dqt_ref[:, rows] = _dot_nt(ktb, dsb)
            dk.append(_dot_tn(dsb, qb))
            dv.append(_dot_tn(p.astype(BF16), dob))
        dk_ref[...] += sum(dk[1:], dk[0])
        dv_ref[...] += sum(dv[1:], dv[0])

    return pl.pallas_call(
        body, name='attn_bwd', grid=grid,
        out_shape=(jax.ShapeDtypeStruct((b, h, dq, nq), F32), jax.ShapeDtypeStruct((b, hk, nk, dq), F32),
                   jax.ShapeDtypeStruct((b, hk, nk, dv_width), F32)),
        in_specs=[q_spec, t['k'], t['v'], o_spec], out_specs=(t['q'], k_spec, v_spec),
        compiler_params=_params('parallel', 'parallel', 'arbitrary', 'arbitrary'),
    )(q, kt, vt, do)


@functools.partial(jax.custom_vjp, nondiff_argnums=(3,))
def attention(q, k, v, scale):
    return _attention_fwd(q, k, v, scale)[0]


def _attention_fwd(q, k, v, scale):
    qb, kt, vt = q.astype(BF16), jnp.swapaxes(k.astype(BF16), 2, 3), jnp.swapaxes(v.astype(BF16), 2, 3)
    return jnp.swapaxes(_attn_fwd_call(qb, kt, vt, scale), 2, 3), (qb, kt, vt)


def _attention_bwd(scale, res, g):
    dqt, dk, dv = _attn_bwd_call(*res, g.astype(BF16), scale)
    return jnp.swapaxes(dqt, 2, 3), dk, dv


attention.defvjp(_attention_fwd, _attention_bwd)


def _na_window(r, rows):
    start = jnp.clip(r - NA_WIN_R // 2, 0, rows - NA_WIN_R)
    return start, r - start


NA_TOGETHER = 8


def _na_group(r0, rows):
    out = []
    for g in range(NA_TOGETHER):
        start, off = _na_window(r0 + g, rows)
        out.append((pl.ds(pl.multiple_of(start * GRID_W, GRID_W), NA_WIN_R * GRID_W), off))
    return out


def _na_rows(x, g):
    return x[g * GRID_W:(g + 1) * GRID_W]


def _na_scores(qs, kws, kc, biases, scale):
    n = len(kws)
    s1 = jnp.stack([_dot_nt(_na_rows(qs, g), kws[g]) * scale + biases[g] for g in range(n)])
    s2 = (_dot_nt(qs, kc) * scale).reshape(n, GRID_W, kc.shape[0])
    m = jnp.maximum(jnp.max(s1, axis=-1, keepdims=True), jnp.max(s2, axis=-1, keepdims=True))
    e1, e2 = jnp.exp(s1 - m), jnp.exp(s2 - m)
    inv = 1.0 / (jnp.sum(e1, axis=-1, keepdims=True) + jnp.sum(e2, axis=-1, keepdims=True))
    return e1 * inv, e2 * inv


def _na_specs(q, kc):
    _, _, n, d = q.shape
    c = kc.shape[2]
    win = NA_WIN_R * GRID_W
    tok = pl.BlockSpec((None, None, n, d), lambda b, h: (b, h, 0, 0))
    ctx = pl.BlockSpec((None, None, c, d), lambda b, h: (b, h, 0, 0))
    bias = pl.BlockSpec((None, NA_WIN_R, GRID_W, win), lambda b, h: (h, 0, 0, 0))
    dbias = pl.BlockSpec((None, None, NA_WIN_R, GRID_W, win), lambda b, h: (b, h, 0, 0, 0))
    return tok, ctx, bias, dbias


def _na_fwd_call(q, k, v, kc, vc, bias, scale):
    b, h, n, d = q.shape
    rows, span = n // GRID_W, NA_TOGETHER * GRID_W
    tok, ctx, bias_spec, _ = _na_specs(q, kc)

    def body(q_ref, k_ref, v_ref, kc_ref, vc_ref, b_ref, o_ref):
        def step(i, carry):
            at = pl.ds(pl.multiple_of(i * span, span), span)
            wins = _na_group(i * NA_TOGETHER, rows)
            p1, p2 = _na_scores(q_ref[at, :], [k_ref[w, :] for w, _ in wins], kc_ref[...], [b_ref[off] for _, off in wins],
                                scale)
            p1, p2 = p1.astype(BF16), p2.astype(BF16)
            local = jnp.concatenate([_dot(p1[g], v_ref[w, :]) for g, (w, _) in enumerate(wins)], axis=0)
            o_ref[at, :] = local + _dot(p2.reshape(span, p2.shape[2]), vc_ref[...])
            return carry

        lax.fori_loop(0, rows // NA_TOGETHER, step, 0)

    return pl.pallas_call(
        body, name='na_fwd', grid=(b, h), out_shape=jax.ShapeDtypeStruct(q.shape, F32),
        in_specs=[tok, tok, tok, ctx, ctx, bias_spec], out_specs=tok,
        compiler_params=_params('parallel', 'parallel'),
    )(q, k, v, kc, vc, bias)


def _na_bwd_call(q, k, v, kc, vc, bias, do, scale):
    b, h, n, d = q.shape
    rows, span = n // GRID_W, NA_TOGETHER * GRID_W
    tok, ctx, bias_spec, dbias_spec = _na_specs(q, kc)

    def body(q_ref, k_ref, v_ref, kc_ref, vc_ref, b_ref, do_ref, dq_ref, dk_ref, dv_ref, dkc_ref, dvc_ref, db_ref):
        for ref in (dk_ref, dv_ref, dkc_ref, dvc_ref, db_ref):
            ref[...] = jnp.zeros_like(ref)

        def step(i, carry):
            at = pl.ds(pl.multiple_of(i * span, span), span)
            wins = _na_group(i * NA_TOGETHER, rows)
            qs, dob, kcb, vcb = q_ref[at, :], do_ref[at, :], kc_ref[...], vc_ref[...]
            kws, vws = [k_ref[w, :] for w, _ in wins], [v_ref[w, :] for w, _ in wins]
            p1, p2 = _na_scores(qs, kws, kcb, [b_ref[off] for _, off in wins], scale)
            dp1 = jnp.stack([_dot_nt(_na_rows(dob, g), vws[g]) for g in range(NA_TOGETHER)])
            dp2 = _dot_nt(dob, vcb).reshape(p2.shape)
            delta = jnp.sum(p1 * dp1, axis=-1, keepdims=True) + jnp.sum(p2 * dp2, axis=-1, keepdims=True)
            ds1, ds2 = p1 * (dp1 - delta), p2 * (dp2 - delta)
            ds1b, p1b = (ds1 * scale).astype(BF16), p1.astype(BF16)
            ds2b = (ds2 * scale).astype(BF16).reshape(span, p2.shape[2])
            p2b = p2.astype(BF16).reshape(span, p2.shape[2])
            dq_ref[at, :] = jnp.concatenate([_dot(ds1b[g], kws[g]) for g in range(NA_TOGETHER)], axis=0) + _dot(ds2b, kcb)
            for g, (w, off) in enumerate(wins):
                db_ref[off] += ds1[g]
                dk_ref[w, :] += _dot_tn(ds1b[g], _na_rows(qs, g))
                dv_ref[w, :] += _dot_tn(p1b[g], _na_rows(dob, g))
            dkc_ref[...] += _dot_tn(ds2b, qs)
            dvc_ref[...] += _dot_tn(p2b, dob)
            return carry

        lax.fori_loop(0, rows // NA_TOGETHER, step, 0)

    f = lambda a: jax.ShapeDtypeStruct(a.shape, F32)
    return pl.pallas_call(
        body, name='na_bwd', grid=(b, h),
        out_shape=(f(q), f(k), f(v), f(kc), f(vc), jax.ShapeDtypeStruct((b,) + bias.shape, F32)),
        in_specs=[tok, tok, tok, ctx, ctx, bias_spec, tok], out_specs=(tok, tok, tok, ctx, ctx, dbias_spec),
        compiler_params=_params('parallel', 'parallel'),
    )(q, k, v, kc, vc, bias, do)


@functools.partial(jax.custom_vjp, nondiff_argnums=(6,))
def na_attention(q, k, v, kc, vc, bias, scale):
    return _na_fwd_call(q.astype(BF16), k.astype(BF16), v.astype(BF16), kc.astype(BF16), vc.astype(BF16), bias, scale)


def _na_attention_fwd(q, k, v, kc, vc, bias, scale):
    res = (q.astype(BF16), k.astype(BF16), v.astype(BF16), kc.astype(BF16), vc.astype(BF16), bias)
    return _na_fwd_call(*res, scale), res


def _na_attention_bwd(scale, res, g):
    dq, dk, dv, dkc, dvc, db = _na_bwd_call(*res, g.astype(BF16), scale)
    return dq, dk, dv, dkc, dvc, jnp.sum(db, axis=0)


na_attention.defvjp(_na_attention_fwd, _na_attention_bwd)


def _na_table_index():
    qcol = jnp.arange(GRID_W)
    kcol = jnp.arange(GRID_W)
    cstart = jnp.clip(qcol - NA_WIN_C // 2, 0, GRID_W - NA_WIN_C)
    inside = (kcol[None, :] >= cstart[:, None]) & (kcol[None, :] < cstart[:, None] + NA_WIN_C)
    cidx = jnp.clip(kcol[None, :] - qcol[:, None] + (NA_WIN_C - 1), 0, 2 * NA_WIN_C - 2)
    ridx = jnp.arange(NA_WIN_R)[None, :] - jnp.arange(NA_WIN_R)[:, None] + (NA_WIN_R - 1)
    return inside, cidx, ridx


@jax.custom_vjp
def na_bias_table(rpb):
    inside, pick_c, pick_r = _na_table_picks()
    rows = jnp.einsum('hab,oja->hojb', rpb, pick_r, precision=lax.Precision.HIGHEST)
    t = jnp.einsum('hojb,qkb->hoqjk', rows, pick_c, precision=lax.Precision.HIGHEST)
    t = jnp.where(inside[None, None, :, None, :], t, NEG)
    return t.reshape(rpb.shape[0], NA_WIN_R, GRID_W, NA_WIN_R * GRID_W)


def _na_table_picks():
    inside, cidx, ridx = _na_table_index()
    pick_c = ((cidx[..., None] == jnp.arange(2 * NA_WIN_C - 1)) & inside[..., None]).astype(F32)
    pick_r = (ridx[..., None] == jnp.arange(2 * NA_WIN_R - 1)).astype(F32)
    return inside, pick_c, pick_r


def _na_bias_table_bwd(_, dt):
    _, pick_c, pick_r = _na_table_picks()
    d5 = dt.reshape(dt.shape[0], NA_WIN_R, GRID_W, NA_WIN_R, GRID_W)
    part = jnp.einsum('hoqjk,qkb->hojb', d5, pick_c, precision=lax.Precision.HIGHEST)
    return (jnp.einsum('hojb,oja->hab', part, pick_r, precision=lax.Precision.HIGHEST),)


na_bias_table.defvjp(lambda rpb: (na_bias_table(rpb), None), _na_bias_table_bwd)


S5_HALF = SSM_WIDTH // 2
S5_LANES = (SSM_GROUPS // 2) * SSM_STATE
S5_Q = S5_LANES // LANES


def _s5_tiles(a):
    r = a.shape[0]
    return jnp.transpose(a.reshape(r, S5_Q, LANES), (1, 0, 2)).reshape(S5_Q * r, LANES)


def _s5_untiles(a):
    r = a.shape[0] // S5_Q
    return jnp.transpose(a.reshape(S5_Q, r, LANES), (1, 0, 2)).reshape(r, S5_LANES)


def _s5_put(ref, r, rr, tc, val):
    for q in range(S5_Q):
        ref[pl.ds((q * rr + r) * tc, tc), :] = val[:, q * LANES:(q + 1) * LANES]


def _s5_get(ref, r, rr, tc):
    return jnp.concatenate([ref[pl.ds((q * rr + r) * tc, tc), :] for q in range(S5_Q)], axis=1)


S5_BLOCK = 8
S5_TOGETHER = 8


def _s5_powers(a_re, a_im, backward):
    a_im = -a_im if backward else a_im
    pr, pi = [a_re], [a_im]
    for _ in range(S5_BLOCK - 1):
        pr, pi = pr + [pr[-1] * a_re - pi[-1] * a_im], pi + [pr[-1] * a_im + pi[-1] * a_re]
    order = range(S5_BLOCK - 1, -1, -1) if backward else range(S5_BLOCK)

    def table(p):
        rows = [jnp.broadcast_to(p[s - 1][:, None, :], (p[0].shape[0], S5_BLOCK, LANES)) for s in (1, 2, 4)]
        return jnp.stack(rows + [jnp.stack([p[t] for t in order], axis=1)], axis=1)

    return table(pr), table(pi)


def _s5_scan_block(xr, xi, pr_ref, pi_ref, chain, carry, backward):
    row = lax.broadcasted_iota(jnp.int32, (S5_BLOCK, LANES), 0)
    for e, s in enumerate((1, 2, 4)):
        ar, ai = pr_ref[chain, e], pi_ref[chain, e]
        keep = (row < S5_BLOCK - s) if backward else (row >= s)
        shift = S5_BLOCK - s if backward else s
        sr = jnp.where(keep, pltpu.roll(xr, shift, 0), 0.0)
        si = jnp.where(keep, pltpu.roll(xi, shift, 0), 0.0)
        xr, xi = xr + (ar * sr - ai * si), xi + (ar * si + ai * sr)
    ar, ai = pr_ref[chain, 3], pi_ref[chain, 3]
    cr, ci = carry
    xr, xi = xr + (ar * cr - ai * ci), xi + (ar * ci + ai * cr)
    edge = slice(0, 1) if backward else slice(S5_BLOCK - 1, S5_BLOCK)
    return xr, xi, (xr[edge], xi[edge])


def _s5_scan_chunk(xr_ref, xi_ref, pr_ref, pi_ref, sr_ref, si_ref, chains, tc, backward):
    blocks = tc // S5_BLOCK
    for first in range(0, chains, S5_TOGETHER):
        group = range(first, min(first + S5_TOGETHER, chains))

        def block(k, carries, group=group):
            j = blocks - 1 - k if backward else k
            out = []
            for chain, carry in zip(group, carries):
                at = pl.ds(pl.multiple_of(chain * tc + j * S5_BLOCK, S5_BLOCK), S5_BLOCK)
                xr, xi, carry = _s5_scan_block(xr_ref[at, :], xi_ref[at, :], pr_ref, pi_ref, chain, carry, backward)
                xr_ref[at, :] = xr
                xi_ref[at, :] = xi
                out.append(carry)
            return tuple(out)

        start = tuple((sr_ref[pl.ds(chain, 1), :], si_ref[pl.ds(chain, 1), :]) for chain in group)
        for chain, (cr, ci) in zip(group, lax.fori_loop(0, blocks, block, start)):
            sr_ref[pl.ds(chain, 1), :] = cr
            si_ref[pl.ds(chain, 1), :] = ci


def _s5_fwd_call(u, a_re, a_im, b_re, b_im, c_re, c_im):
    rr, t_len, _ = u.shape
    sets = b_re.shape[0]
    per = rr // sets
    tc = _tile(t_len, (256, 128))
    nt = t_len // tc
    qr = S5_Q * rr

    def body(u_ref, ar_ref, ai_ref, br_ref, bi_ref, cr_ref, ci_ref, y_ref, hr_ref, hi_ref, sr_ref, si_ref):
        @pl.when(pl.program_id(0) == 0)
        def _():
            sr_ref[...] = jnp.zeros_like(sr_ref)
            si_ref[...] = jnp.zeros_like(si_ref)

        for r in range(rr):
            ub = u_ref[r]
            _s5_put(hr_ref, r, rr, tc, _dot(ub, br_ref[r // per]))
            _s5_put(hi_ref, r, rr, tc, _dot(ub, bi_ref[r // per]))
        _s5_scan_chunk(hr_ref, hi_ref, ar_ref, ai_ref, sr_ref, si_ref, qr, tc, False)
        for r in range(rr):
            y_ref[r] = (_dot(_s5_get(hr_ref, r, rr, tc).astype(BF16), cr_ref[r // per])
                        - _dot(_s5_get(hi_ref, r, rr, tc).astype(BF16), ci_ref[r // per]))

    full = lambda a: pl.BlockSpec(a.shape, lambda i: (0,) * a.ndim)
    h_spec = pl.BlockSpec((None, qr * tc, LANES), lambda i: (i, 0, 0))
    h_shape = jax.ShapeDtypeStruct((nt, qr * tc, LANES), F32)
    a_re, a_im = _s5_powers(_s5_tiles(a_re), _s5_tiles(a_im), False)
    return pl.pallas_call(
        body, name='s5_fwd', grid=(nt,),
        out_shape=(jax.ShapeDtypeStruct((rr, t_len, S5_HALF), F32), h_shape, h_shape),
        in_specs=[pl.BlockSpec((rr, tc, S5_HALF), lambda i: (0, i, 0)), full(a_re), full(a_im), full(b_re), full(b_im),
                  full(c_re), full(c_im)],
        out_specs=(pl.BlockSpec((rr, tc, S5_HALF), lambda i: (0, i, 0)), h_spec, h_spec),
        scratch_shapes=[pltpu.VMEM((qr, LANES), F32), pltpu.VMEM((qr, LANES), F32)],
        compiler_params=_params('arbitrary'),
    )(u, a_re, a_im, b_re, b_im, c_re, c_im)


def _s5_bwd_call(u, a_re, a_im, b_re, b_im, c_re, c_im, h_re, h_im, dy):
    rr, t_len, _ = u.shape
    sets = b_re.shape[0]
    per = rr // sets
    nt, rows, _ = h_re.shape
    qr = S5_Q * rr
    tc = rows // qr

    def body(u_ref, dy_ref, ar_ref, ai_ref, br_ref, bi_ref, cr_ref, ci_ref, hr_ref, hi_ref,
             du_ref, dar_ref, dai_ref, dbr_ref, dbi_ref, dcr_ref, dci_ref, gr_ref, gi_ref, sr_ref, si_ref):
        i = pl.program_id(0)

        @pl.when(i == 0)
        def _():
            for ref in (dar_ref, dai_ref, dbr_ref, dbi_ref, dcr_ref, dci_ref, sr_ref, si_ref):
                ref[...] = jnp.zeros_like(ref)

        for r in range(rr):
            dyb = dy_ref[r]
            _s5_put(gr_ref, r, rr, tc, _dot_nt(dyb, cr_ref[r // per]))
            _s5_put(gi_ref, r, rr, tc, -_dot_nt(dyb, ci_ref[r // per]))
        g_r, g_i = sr_ref[...], si_ref[...]
        last = pl.ds(tc - 1, qr, stride=tc)
        dar_ref[...] += g_r * hr_ref[last, :] + g_i * hi_ref[last, :]
        dai_ref[...] += g_i * hr_ref[last, :] - g_r * hi_ref[last, :]
        _s5_scan_chunk(gr_ref, gi_ref, ar_ref, ai_ref, sr_ref, si_ref, qr, tc, True)
        row = lax.broadcasted_iota(jnp.int32, (tc, LANES), 0)
        for chain in range(qr):
            at, one = pl.ds(chain * tc, tc), pl.ds(chain, 1)
            p_r = jnp.where(row >= 1, pltpu.roll(hr_ref[at, :], 1, 0), 0.0)
            p_i = jnp.where(row >= 1, pltpu.roll(hi_ref[at, :], 1, 0), 0.0)
            g_r, g_i = gr_ref[at, :], gi_ref[at, :]
            dar_ref[one, :] += jnp.sum(g_r * p_r + g_i * p_i, axis=0, keepdims=True)
            dai_ref[one, :] += jnp.sum(g_i * p_r - g_r * p_i, axis=0, keepdims=True)
        for r in range(rr):
            s = r // per
            ub, dyb = u_ref[r], dy_ref[r]
            grb, gib = _s5_get(gr_ref, r, rr, tc).astype(BF16), _s5_get(gi_ref, r, rr, tc).astype(BF16)
            du_ref[r] = _dot_nt(grb, br_ref[s]) + _dot_nt(gib, bi_ref[s])
            dbr_ref[s] += _dot_tn(ub, grb)
            dbi_ref[s] += _dot_tn(ub, gib)
            dcr_ref[s] += _dot_tn(_s5_get(hr_ref, r, rr, tc).astype(BF16), dyb)
            dci_ref[s] -= _dot_tn(_s5_get(hi_ref, r, rr, tc).astype(BF16), dyb)

    full = lambda a: pl.BlockSpec(a.shape, lambda i: (0,) * a.ndim)
    back = lambda i: nt - 1 - i
    tok = pl.BlockSpec((rr, tc, S5_HALF), lambda i: (0, back(i), 0))
    h_spec = pl.BlockSpec((None, qr * tc, LANES), lambda i: (back(i), 0, 0))
    f = lambda a: jax.ShapeDtypeStruct(a.shape, F32)
    a_re, a_im = _s5_powers(_s5_tiles(a_re), _s5_tiles(a_im), True)
    da = jax.ShapeDtypeStruct((qr, LANES), F32)
    du, da_re, da_im, db_re, db_im, dc_re, dc_im = pl.pallas_call(
        body, name='s5_bwd', grid=(nt,),
        out_shape=(jax.ShapeDtypeStruct(u.shape, F32), da, da, f(b_re), f(b_im), f(c_re), f(c_im)),
        in_specs=[tok, tok, full(a_re), full(a_im), full(b_re), full(b_im), full(c_re), full(c_im), h_spec, h_spec],
        out_specs=(tok, full(da), full(da), full(b_re), full(b_im), full(c_re), full(c_im)),
        scratch_shapes=[pltpu.VMEM((qr * tc, LANES), F32), pltpu.VMEM((qr * tc, LANES), F32),
                        pltpu.VMEM((qr, LANES), F32), pltpu.VMEM((qr, LANES), F32)],
        compiler_params=_params('arbitrary'),
    )(u, dy, a_re, a_im, b_re, b_im, c_re, c_im, h_re, h_im)
    return du, _s5_untiles(da_re), _s5_untiles(da_im), db_re, db_im, dc_re, dc_im


@jax.custom_vjp
def s5_core(u, a_re, a_im, b_re, b_im, c_re, c_im):
    return _s5_fwd_call(u.astype(BF16), a_re, a_im, b_re.astype(BF16), b_im.astype(BF16), c_re.astype(BF16),
                        c_im.astype(BF16))[0]


def _s5_core_fwd(u, a_re, a_im, b_re, b_im, c_re, c_im):
    args = (u.astype(BF16), a_re, a_im, b_re.astype(BF16), b_im.astype(BF16), c_re.astype(BF16), c_im.astype(BF16))
    y, h_re, h_im = _s5_fwd_call(*args)
    return y, args + (h_re, h_im)


def _s5_core_bwd(res, g):
    return _s5_bwd_call(*res, g.astype(BF16))


s5_core.defvjp(_s5_core_fwd, _s5_core_bwd)


def _shard_view(ref, axis, index, width):
    return ref.at[(slice(None),) * axis + (pl.ds(pl.multiple_of(index * width, width), width),)]


def _exchange_many(xs, cuts, gather, name):
    n = len(xs)
    if gather:
        shards = [x.shape for x in xs]
    else:
        shards = [x.shape[1:] if cut is None else x.shape[:cut] + (x.shape[cut] // N_DEV,) + x.shape[cut + 1:]
                  for x, cut in zip(xs, cuts)]

    def full_shape(shard, cut):
        return shard[:cut] + (N_DEV * shard[cut],) + shard[cut + 1:]

    out_shapes = [jax.ShapeDtypeStruct((N_DEV,) + tuple(s) if (cut is None or not gather) else full_shape(tuple(s), cut), x.dtype)
                  for x, s, cut in zip(xs, shards, cuts)]

    def body(*refs):
        x_refs, out_refs = refs[:n], refs[n:2 * n]
        send_sems, recv_sems, local_sems = refs[2 * n:]
        ix, iy, ic = lax.axis_index('x'), lax.axis_index('y'), lax.axis_index('c')
        me = 4 * ix + 2 * iy + ic

        def flipped(k):
            px = 1 - ix if k & 4 else ix
            py = 1 - iy if k & 2 else iy
            pc = 1 - ic if k & 1 else ic
            return (px, py, pc), 4 * px + 2 * py + pc

        def block(ref, cut, shard, who):
            return ref.at[who] if cut is None else _shard_view(ref, cut, who, shard[cut])

        def ends(i, sender, receiver):
            if gather:
                return x_refs[i], block(out_refs[i], cuts[i], shards[i], sender)
            return block(x_refs[i], cuts[i], shards[i], receiver), out_refs[i].at[sender]

        def copy(i, k, sender, receiver):
            src, dst = ends(i, sender, receiver)
            return pltpu.make_async_remote_copy(src_ref=src, dst_ref=dst, send_sem=send_sems.at[i * (N_DEV - 1) + k - 1],
                                                 recv_sem=recv_sems.at[i * (N_DEV - 1) + k - 1], device_id=flipped(k)[0],
                                                 device_id_type=MESH)

        own = [pltpu.make_async_copy(*ends(i, me, me), local_sems.at[i]) for i in range(n)]
        for cp in own:
            cp.start()
        if gather:
            chips = (2, 4, 6)
            sent = [copy(i, k, me, flipped(k)[1]) for k in (1,) + chips for i in range(n)]
            for cp in sent:
                cp.start()
            for k in chips:
                for i in range(n):
                    copy(i, k, flipped(k)[1], me).wait_recv()
                    src, dst = ends(i, flipped(k)[1], me)
                    sent.append(pltpu.make_async_remote_copy(
                        src_ref=dst, dst_ref=dst, send_sem=send_sems.at[i * (N_DEV - 1) + k], recv_sem=recv_sems.at[i * (N_DEV - 1) + k],
                        device_id=flipped(1)[0], device_id_type=MESH))
                    sent[-1].start()
            for k in (1, 3, 5, 7):
                for i in range(n):
                    src, dst = ends(i, flipped(k)[1], me)
                    pltpu.make_async_remote_copy(
                        src_ref=dst, dst_ref=dst, send_sem=send_sems.at[i * (N_DEV - 1) + k - 1], recv_sem=recv_sems.at[i * (N_DEV - 1) + k - 1],
                        device_id=flipped(1)[0], device_id_type=MESH).wait_recv()
            for cp in sent:
                cp.wait_send()
            for cp in own:
                cp.wait()
            return
        sent = [copy(i, k, me, flipped(k)[1]) for k in range(1, N_DEV) for i in range(n)]
        for cp in sent:
            cp.start()
        for k in range(1, N_DEV):
            for i in range(n):
                copy(i, k, flipped(k)[1], me).wait_recv()
        for cp in sent:
            cp.wait_send()
        for cp in own:
            cp.wait()

    hbm = pl.BlockSpec(memory_space=pltpu.HBM)
    pairs = n * (N_DEV - 1)
    return pl.pallas_call(
        body, name=name, out_shape=out_shapes, in_specs=[hbm] * n, out_specs=[hbm] * n,
        scratch_shapes=[pltpu.SemaphoreType.DMA((pairs,)), pltpu.SemaphoreType.DMA((pairs,)), pltpu.SemaphoreType.DMA((n,))],
    )(*xs)


def _adamw_landed(landed, w, m, v):
    shape = w.shape
    slots = landed.shape[0]
    w, m, v = (_as_rows(a) for a in (w, m, v))
    rows, cols = w.shape
    landed = landed.reshape(slots, rows, cols)
    tr = _tile(rows, (256, 128, 64, 32, 16))

    def body(l_ref, w_ref, m_ref, v_ref, g_ref, d_ref, nm_ref, nv_ref):
        g = l_ref[0].astype(F32)
        for d in range(1, slots):
            g = g + l_ref[d].astype(F32)
        g_ref[...] = g
        d_ref[...], nm_ref[...], nv_ref[...] = _adamw_math(w_ref[...], g, m_ref[...], v_ref[...])

    spec = pl.BlockSpec((tr, cols), lambda i: (i, 0))
    out = pl.pallas_call(
        body, name='adamw_landed', grid=(rows // tr,), out_shape=(jax.ShapeDtypeStruct(w.shape, F32),) * 4,
        in_specs=[pl.BlockSpec((slots, tr, cols), lambda i: (0, i, 0))] + [spec] * 3, out_specs=(spec,) * 4,
        compiler_params=_params('parallel'),
    )(landed, w, m, v)
    return tuple(o.reshape(shape) for o in out)


def _all_reduce_small(x):
    g = _exchange_many([x], [None], True, 'gather_small_grads')[0]

    def body(g_ref, o_ref):
        acc = g_ref[0]
        for d in range(1, N_DEV):
            acc = acc + g_ref[d]
        o_ref[...] = acc

    return pl.pallas_call(body, name='sum_small', out_shape=jax.ShapeDtypeStruct(x.shape, F32))(g)


def _adamw_math(w, g, m, v):
    m = ADAM_B1 * m + (1.0 - ADAM_B1) * g
    v = ADAM_B2 * v + (1.0 - ADAM_B2) * (g * g)
    m_hat = m / (1.0 - ADAM_B1 ** ADAM_STEP)
    v_hat = v / (1.0 - ADAM_B2 ** ADAM_STEP)
    return -ADAM_LR * (m_hat / (jnp.sqrt(v_hat) + ADAM_EPS) + ADAM_WD * w), m, v


def _as_rows(a):
    return a.reshape(1, -1) if a.ndim < 2 else a.reshape(-1, a.shape[-1])


def _as_lanes(a):
    return a.reshape(-1, LANES) if a.size % LANES == 0 else a.reshape(1, -1)


def _adamw_big(w, g, m, v):
    shape = w.shape
    w, g, m, v = (_as_rows(a) for a in (w, g, m, v))
    rows, cols = w.shape
    tr = _tile(rows, (512, 256, 128, 64, 32, 16, 8))

    def body(w_ref, g_ref, m_ref, v_ref, d_ref, nm_ref, nv_ref):
        d_ref[...], nm_ref[...], nv_ref[...] = _adamw_math(w_ref[...], g_ref[...], m_ref[...], v_ref[...])

    spec = pl.BlockSpec((tr, cols), lambda i: (i, 0))
    out = pl.pallas_call(
        body, name='adamw', grid=(rows // tr,), out_shape=(jax.ShapeDtypeStruct(w.shape, F32),) * 3,
        in_specs=[spec] * 4, out_specs=(spec,) * 3, compiler_params=_params('parallel'),
    )(w, g, m, v)
    return tuple(o.reshape(shape) for o in out)


def _adamw_small(ws, gs, ms, vs):
    n = len(ws)
    shapes = [w.shape for w in ws]
    flat = [_as_lanes(a) for group in (ws, gs, ms, vs) for a in group]

    def body(*refs):
        ins, outs = refs[:4 * n], refs[4 * n:]
        for i in range(n):
            d, m, v = _adamw_math(ins[i][...], ins[n + i][...], ins[2 * n + i][...], ins[3 * n + i][...])
            outs[i][...], outs[n + i][...], outs[2 * n + i][...] = d, m, v

    out = pl.pallas_call(
        body, name='adamw_small', out_shape=tuple(jax.ShapeDtypeStruct(flat[i].shape, F32) for _ in range(3) for i in range(n)),
    )(*flat)
    return [tuple(out[j * n + i].reshape(shapes[i]) for j in range(3)) for i in range(n)]


def rms_norm(x, g):
    return x * lax.rsqrt(jnp.mean(jnp.square(x), axis=-1, keepdims=True) + EPS) * g


def modulate(x, g, shift, scale):
    return rms_norm(x, g) * (1 + scale) + shift


def rope_tables(n_tokens, rot_dim):
    t = jnp.arange(n_tokens)
    rows = (t // GRID_W).astype(F32)
    cols = (t % GRID_W).astype(F32)
    axis_dim = rot_dim // 2
    freqs = ROPE_BASE ** (-jnp.arange(0, axis_dim, 2, dtype=F32) / axis_dim)
    ang_r, ang_c = rows[:, None] * freqs, cols[:, None] * freqs
    ang = jnp.concatenate([ang_r, ang_r, ang_c, ang_c], axis=-1)
    return jnp.cos(ang), jnp.sin(ang)


def rope(x, cos, sin):
    x1, x2, x3, x4 = jnp.split(x, 4, axis=-1)
    rot = jnp.concatenate([-x2, x1, -x4, x3], axis=-1)
    return x * cos[:, None, :] + rot * sin[:, None, :]


def heads_first(t):
    return jnp.swapaxes(t, 1, 2)


def tokens_matmul(t, w):
    b, n, k = t.shape
    return linear(t.reshape(b * n, k), w).reshape(b, n, w.shape[1])


def s5_discretize(lam_re, lam_im, log_dt, b_re, b_im):
    dt = jnp.exp(log_dt)[:, None]
    mag = jnp.exp(lam_re * dt)
    a_re = mag * jnp.cos(lam_im * dt)
    a_im = mag * jnp.sin(lam_im * dt)
    den = jnp.square(lam_re) + jnp.square(lam_im)
    f_re = ((a_re - 1.0) * lam_re + a_im * lam_im) / den
    f_im = (a_im * lam_re - (a_re - 1.0) * lam_im) / den
    bb_re = f_re[..., None] * b_re - f_im[..., None] * b_im
    bb_im = f_re[..., None] * b_im + f_im[..., None] * b_re
    return a_re, a_im, bb_re, bb_im


def s5_mixer(u_lat, u_ctx, p, j, need_ctx):
    b, n, _ = u_lat.shape
    c = u_ctx.shape[1]
    half_groups = SSM_GROUPS // 2
    eye = jnp.eye(half_groups, dtype=F32)
    a_res, a_ims, b_res, b_ims, c_res, c_ims, seqs = [], [], [], [], [], [], []
    for d in range(2):
        a_re, a_im, bb_re, bb_im = s5_discretize(p['ssm_lam_re'][j, d], p['ssm_lam_im'][j, d], p['ssm_log_dt'][j, d],
                                                 p['ssm_b_re'][j, d], p['ssm_b_im'][j, d])
        for half in range(2):
            grp = slice(half * half_groups, (half + 1) * half_groups)
            a_res.append(a_re[grp].reshape(S5_LANES))
            a_ims.append(a_im[grp].reshape(S5_LANES))
            b_res.append(jnp.einsum('gsp,gh->gphs', bb_re[grp], eye).reshape(S5_HALF, S5_LANES))
            b_ims.append(jnp.einsum('gsp,gh->gphs', bb_im[grp], eye).reshape(S5_HALF, S5_LANES))
            c_res.append(jnp.einsum('gps,gh->gshp', p['ssm_c_re'][j, d][grp], eye).reshape(S5_LANES, S5_HALF))
            c_ims.append(jnp.einsum('gps,gh->gshp', p['ssm_c_im'][j, d][grp], eye).reshape(S5_LANES, S5_HALF))
        flip = (lambda t: t[:, ::-1]) if d == 1 else (lambda t: t)
        seq = jnp.concatenate([flip(u_ctx), flip(u_lat)], axis=1)
        seqs.append(jnp.transpose(seq.reshape(b, c + n, 2, S5_HALF), (2, 0, 1, 3)))
    u = jnp.stack(seqs).reshape(4 * b, c + n, S5_HALF)
    rep = lambda parts: jnp.repeat(jnp.stack(parts), b, axis=0)
    y = s5_core(u, rep(a_res), rep(a_ims), jnp.stack(b_res), jnp.stack(b_ims), jnp.stack(c_res), jnp.stack(c_ims))
    y = jnp.transpose(y.reshape(2, 2, b, c + n, S5_HALF), (0, 2, 3, 1, 4)).reshape(2, b, c + n, SSM_WIDTH)
    d_skip = p['ssm_d'][j]
    y_lat = d_skip * u_lat + y[0, :, c:] + y[1, :, c:][:, ::-1]
    wg, bg = p['ssm_w_glu'][j], p['ssm_b_glu'][j]

    def glu(t):
        t = jax.nn.gelu(t)
        return t * jax.nn.sigmoid(tokens_matmul(t, wg) + bg)

    if not need_ctx:
        return glu(y_lat), None
    y_ctx = d_skip * u_ctx + y[0, :, :c] + y[1, :, :c][:, ::-1]
    return glu(y_lat), glu(y_ctx)


def even_mixer(a_lat, a_ctx, p, j, need_ctx):
    b, n, _ = a_lat.shape
    c = a_ctx.shape[1]
    cos, sin = rope_tables(n, HEAD_DIM)
    proj = tokens_matmul(jnp.concatenate([a_ctx, a_lat], axis=1), p['e_w_in'][j])
    q, k, v, u = jnp.split(proj, [GQA_Q_W, GQA_Q_W + GQA_KV_W, GQA_Q_W + 2 * GQA_KV_W], axis=-1)
    q = rms_norm(q.reshape(b, c + n, GQA_Q_HEADS, HEAD_DIM), p['e_g_q'][j])
    k = rms_norm(k.reshape(b, c + n, GQA_KV_HEADS, HEAD_DIM), p['e_g_k'][j])
    v = v.reshape(b, c + n, GQA_KV_HEADS, HEAD_DIM)
    q_l = rope(q[:, c:], cos, sin)
    k = jnp.concatenate([k[:, :c], rope(k[:, c:], cos, sin)], axis=1)
    scale = HEAD_DIM ** -0.5
    kh, vh = heads_first(k), heads_first(v)
    att_l = heads_first(attention(heads_first(q_l), kh, vh, scale)).reshape(b, n, GQA_Q_W)
    ssm_l, ssm_c = s5_mixer(u[:, c:], u[:, :c], p, j, need_ctx)
    mix_l = jnp.concatenate([att_l, ssm_l], axis=-1)
    if not need_ctx:
        return tokens_matmul(mix_l, p['e_w_out'][j]), None
    att_c = heads_first(attention(heads_first(q[:, :c]), kh[:, :, :c], vh[:, :, :c], scale)).reshape(b, c, GQA_Q_W)
    mix = jnp.concatenate([jnp.concatenate([att_c, ssm_c], axis=-1), mix_l], axis=1)
    out = tokens_matmul(mix, p['e_w_out'][j])
    return out[:, c:], out[:, :c]


def odd_mixer(a_lat, a_ctx, p, j, need_ctx):
    b, n, _ = a_lat.shape
    c = a_ctx.shape[1]
    t = c + n
    cos, sin = rope_tables(n, MLA_ROPE)
    proj = tokens_matmul(jnp.concatenate([a_ctx, a_lat], axis=1), p['o_w_in'][j])
    c1, c2, c3 = MLA_Q_RANK, MLA_Q_RANK + MLA_KV_RANK, MLA_Q_RANK + MLA_KV_RANK + MLA_ROPE
    cq, ckv, kr = proj[..., :c1], proj[..., c1:c2], proj[..., c2:c3]
    nq, nk, nv = jnp.split(proj[..., ODD_NA_AT:], 3, axis=-1)
    q = tokens_matmul(rms_norm(cq, p['mla_g_cq'][j]), p['mla_w_uq'][j]).reshape(b, t, MLA_HEADS, MLA_QK)
    kv = tokens_matmul(rms_norm(ckv, p['mla_g_ckv'][j]), p['mla_w_ukv'][j]).reshape(b, t, MLA_HEADS, MLA_NOPE + MLA_V)
    k = jnp.concatenate([kv[..., :MLA_NOPE], jnp.broadcast_to(kr[:, :, None, :], (b, t, MLA_HEADS, MLA_ROPE))], axis=-1)
    q, k, mv = rms_norm(q, p['mla_g_q'][j]), rms_norm(k, p['mla_g_k'][j]), kv[..., MLA_NOPE:]

    def rope_tail(x):
        tail = jnp.concatenate([x[:, :c, :, MLA_NOPE:], rope(x[:, c:, :, MLA_NOPE:], cos, sin)], axis=1)
        return jnp.concatenate([x[..., :MLA_NOPE], tail], axis=-1)

    q, k = rope_tail(q), rope_tail(k)
    qh, kh, vh = heads_first(q), heads_first(k), heads_first(mv)
    mla_scale = MLA_QK ** -0.5
    mla_l = heads_first(attention(qh[:, :, c:], kh, vh, mla_scale)).reshape(b, n, MLA_HEADS * MLA_V)
    nq = heads_first(rms_norm(nq.reshape(b, t, NA_HEADS, HEAD_DIM), p['na_g_q'][j]))
    nk = heads_first(rms_norm(nk.reshape(b, t, NA_HEADS, HEAD_DIM), p['na_g_k'][j]))
    nv = heads_first(nv.reshape(b, t, NA_HEADS, HEAD_DIM))
    na_scale = HEAD_DIM ** -0.5
    na_l = na_attention(nq[:, :, c:], nk[:, :, c:], nv[:, :, c:], nk[:, :, :c], nv[:, :, :c], na_bias_table(p['na_rpb'][j]),
                        na_scale)
    na_l = heads_first(na_l).reshape(b, n, NA_W)
    mix_l = jnp.concatenate([mla_l, na_l], axis=-1)
    if not need_ctx:
        return tokens_matmul(mix_l, p['o_w_out'][j]), None
    mla_c = heads_first(attention(qh[:, :, :c], kh[:, :, :c], vh[:, :, :c], mla_scale)).reshape(b, c, MLA_HEADS * MLA_V)
    na_c = heads_first(attention(nq[:, :, :c], nk[:, :, :c], nv[:, :, :c], na_scale)).reshape(b, c, NA_W)
    mix = jnp.concatenate([jnp.concatenate([mla_c, na_c], axis=-1), mix_l], axis=1)
    out = tokens_matmul(mix, p['o_w_out'][j])
    return out[:, c:], out[:, :c]


def mlp(h, w1, w2):
    b, n, k = h.shape
    return mlp_rows(h.reshape(b * n, k), w1, w2).reshape(b, n, w2.shape[1])


def local_loss(x, p, m_lat, m_ctx, ctx, target):
    depth = m_lat.shape[0]
    c = ctx.shape[1]
    xc = ctx
    for i in range(depth):
        need_ctx = i < depth - 1
        j = i // 2
        ml = [m_lat[i, :, s][:, None, :] for s in range(N_MOD)]
        mc = [m_ctx[i, s][None, None, :] for s in range(N_MOD)]
        a_lat = modulate(x, p['g_norm1'][i], ml[0], ml[1])
        a_ctx = modulate(xc, p['g_norm1'][i], mc[0], mc[1])
        mixer = even_mixer if i % 2 == 0 else odd_mixer
        o_lat, o_ctx = mixer(a_lat, a_ctx, p, j, need_ctx)
        x = x + ml[2] * o_lat
        h_lat = modulate(x, p['g_norm2'][i], ml[3], ml[4])
        if need_ctx:
            xc = xc + mc[2] * o_ctx
            h_ctx = modulate(xc, p['g_norm2'][i], mc[3], mc[4])
            ff = mlp(jnp.concatenate([h_ctx, h_lat], axis=1), p['w_ff1'][i], p['w_ff2'][i])
            x = x + ml[5] * ff[:, c:]
            xc = xc + mc[5] * ff[:, :c]
        else:
            x = x + ml[5] * mlp(h_lat, p['w_ff1'][i], p['w_ff2'][i])
    return 0.5 * jnp.sum(jnp.mean(jnp.square(x - target), axis=-1))


def _packed_rows(size, layout):
    width, group = layout
    return -(-size // (width * group)) * group


def _pack_rows(flat, layout):
    width = layout[0]
    rows = _packed_rows(flat.shape[-1], layout)
    flat = jnp.pad(flat, [(0, 0)] * (flat.ndim - 1) + [(0, rows * width - flat.shape[-1])])
    return flat.reshape(flat.shape[:-1] + (rows, width))


def _unpack_rows(rows, shape):
    lead = rows.shape[:-2]
    return rows.reshape(lead + (-1,))[..., :math.prod(shape)].reshape(lead + tuple(shape))


def _unpack_all(packed, shapes, layout):
    out, at = [], 0
    for shape in shapes:
        rows = _packed_rows(math.prod(shape), layout)
        out.append(_unpack_rows(packed[..., at:at + rows, :], shape))
        at += rows
    return out


def _join_shards(g, axis):
    g = jnp.moveaxis(g, 0, axis)
    return g.reshape(g.shape[:axis] + (N_DEV * g.shape[axis + 1],) + g.shape[axis + 2:])


def _split_shards(full, axis):
    s = full.shape
    return jnp.moveaxis(full.reshape(s[:axis] + (N_DEV, s[axis] // N_DEV) + s[axis + 1:]), axis, 0)


def _gather_packed(parts, dtype, layout, name):
    packed = jnp.concatenate([_pack_rows(a.astype(dtype).reshape(-1), layout) for a in parts], axis=0)
    return _unpack_all(_exchange_many([packed], [None], True, name)[0], [a.shape for a in parts], layout)


def kernel(x, c, ctx, c_ctx, w_mod, b_mod, g_norm1, g_norm2, w_ff1, w_ff2, e_w_in, e_w_out, e_g_q, e_g_k, ssm_lam_re, ssm_lam_im, ssm_log_dt, ssm_b_re, ssm_b_im, ssm_c_re, ssm_c_im, ssm_d, ssm_w_glu, ssm_b_glu, o_w_in, o_w_out, mla_g_cq, mla_g_ckv, mla_w_uq, mla_w_ukv, mla_g_q, mla_g_k, na_g_q, na_g_k, na_rpb, loss_target, m_c_ctx, m_w_mod, m_b_mod, m_g_norm1, m_g_norm2, m_w_ff1, m_w_ff2, m_e_w_in, m_e_w_out, m_e_g_q, m_e_g_k, m_ssm_lam_re, m_ssm_lam_im, m_ssm_log_dt, m_ssm_b_re, m_ssm_b_im, m_ssm_c_re, m_ssm_c_im, m_ssm_d, m_ssm_w_glu, m_ssm_b_glu, m_o_w_in, m_o_w_out, m_mla_g_cq, m_mla_g_ckv, m_mla_w_uq, m_mla_w_ukv, m_mla_g_q, m_mla_g_k, m_na_g_q, m_na_g_k, m_na_rpb, v_c_ctx, v_w_mod, v_b_mod, v_g_norm1, v_g_norm2, v_w_ff1, v_w_ff2, v_e_w_in, v_e_w_out, v_e_g_q, v_e_g_k, v_ssm_lam_re, v_ssm_lam_im, v_ssm_log_dt, v_ssm_b_re, v_ssm_b_im, v_ssm_c_re, v_ssm_c_im, v_ssm_d, v_ssm_w_glu, v_ssm_b_glu, v_o_w_in, v_o_w_out, v_mla_g_cq, v_mla_g_ckv, v_mla_w_uq, v_mla_w_ukv, v_mla_g_q, v_mla_g_k, v_na_g_q, v_na_g_k, v_na_rpb):
    given = dict(locals())
    x, c, ctx, target = given['x'], given['c'], given['ctx'], given['loss_target']
    b_loc, _, d_model = x.shape
    depth = given['w_mod'].shape[0]
    ix, iy, ic = lax.axis_index('x'), lax.axis_index('y'), lax.axis_index('c')
    me = 4 * ix + 2 * iy + ic
    n_batch = N_DEV * b_loc
    mod_w = given['w_mod'].shape[2]

    c_rows = jnp.concatenate([c, jnp.zeros((8 - b_loc, d_model), F32)], axis=0)
    small = _gather_packed([c_rows] + [given[n] for n in SHARDED_SMALL], F32, PACK_SMALL,'gather_small')
    c_all = small[0][:, :b_loc].reshape(n_batch, d_model)
    full = {n: _join_shards(g, SHARDED_SMALL[n]) for n, g in zip(SHARDED_SMALL, small[1:])}
    cuts = {n: (a if given[n].shape[a] % (16 if a == 1 else LANES) == 0 else None) for n, a in BIG.items()}
    big = _exchange_many([given[n].astype(BF16) for n in BIG], [cuts[n] for n in BIG], True, 'gather_weights')
    for n, g in zip(BIG, big):
        g = g if cuts[n] is not None else _join_shards(g, BIG[n])
        full[n] = [g[i] for i in range(g.shape[0])]
    c3 = MLA_Q_RANK + MLA_KV_RANK + MLA_ROPE
    full['o_w_in'] = [jnp.concatenate([w[:, :c3], jnp.zeros((w.shape[0], ODD_NA_AT - c3), BF16), w[:, c3:]], axis=-1)
                      for w in full['o_w_in']]
    for n in REPLICATED:
        full[n] = given[n]

    rows17 = 16 * (-(-(n_batch + 1) // 16))
    cond = jnp.concatenate([jax.nn.silu(c_all), jax.nn.silu(given['c_ctx'])[None],
                            jnp.zeros((rows17 - n_batch - 1, d_model), F32)], axis=0)
    mod_mine = jnp.stack([_matmul(cond, given['w_mod'][i], 'nn', F32) for i in range(depth)])
    b_mine = lax.dynamic_slice_in_dim(given['b_mod'], me * mod_w, mod_w, axis=1)
    mod_mine = mod_mine + b_mine[:, None, :]
    mod_all = _gather_packed([mod_mine], F32, PACK_SMALL,'gather_mod')[0]
    mod_all = jnp.moveaxis(mod_all, 0, 2).reshape(depth, rows17, N_MOD, d_model)
    m_lat = lax.dynamic_slice_in_dim(mod_all, me * b_loc, b_loc, axis=1)
    m_ctx = mod_all[:, n_batch]

    diff = {n: full[n] for n in list(BIG) + list(SHARDED_SMALL) + REPLICATED}
    loss, (g_x, g_p, g_ml, g_mc) = jax.value_and_grad(local_loss, argnums=(0, 1, 2, 3))(x, diff, m_lat, m_ctx, ctx, target)
    loss = lax.psum(loss, ('x', 'y', 'c'))
    g_p['o_w_in'] = [jnp.concatenate([g[:, :c3], g[:, ODD_NA_AT:]], axis=-1) for g in g_p['o_w_in']]

    g_rows = jnp.concatenate([g_ml.reshape(depth, b_loc, N_MOD * d_model), g_mc.reshape(depth, 1, N_MOD * d_model),
                              jnp.zeros((depth, 8 - b_loc - 1, N_MOD * d_model), F32)], axis=1)
    g_mod_all = _gather_packed([g_rows], F32, PACK_SMALL,'gather_mod_grads')[0]
    g_lat_all = jnp.moveaxis(g_mod_all[:, :, :b_loc], 0, 1).reshape(depth, n_batch, N_MOD * d_model)
    g_ctx_all = g_mod_all[0, :, b_loc]
    for dev in range(1, N_DEV):
        g_ctx_all = g_ctx_all + g_mod_all[dev, :, b_loc]
    g_mod17 = jnp.concatenate([g_lat_all, g_ctx_all[:, None], jnp.zeros((depth, rows17 - n_batch - 1, N_MOD * d_model), F32)],
                              axis=1)
    grad_b_mod = jnp.sum(g_mod17, axis=1)
    g_mod_mine = lax.dynamic_slice_in_dim(g_mod17, me * mod_w, mod_w, axis=2)
    grad_w_mod = jnp.stack([_matmul(cond, g_mod_mine[i], 'tn', F32) for i in range(depth)])
    d_cond = _matmul(g_mod_mine[0], given['w_mod'][0], 'nt', F32)
    for i in range(1, depth):
        d_cond = d_cond + _matmul(g_mod_mine[i], given['w_mod'][i], 'nt', F32)
    d_cond_ctx = d_cond[n_batch]

    small_names = REPLICATED + list(SHARDED_SMALL)
    parts = [d_cond_ctx] + [g_p[n] for n in small_names]
    packed = jnp.concatenate([_pack_rows(a.reshape(-1), PACK_SMALL) for a in parts], axis=0)
    summed = _unpack_all(_all_reduce_small(packed), [a.shape for a in parts], PACK_SMALL)
    grads = dict(zip(['c_ctx'] + small_names, summed))
    c_ctx = given['c_ctx']
    sig = jax.nn.sigmoid(c_ctx)
    grads['c_ctx'] = grads['c_ctx'] * (sig * (1 + c_ctx * (1 - sig)))
    for n, axis in SHARDED_SMALL.items():
        width = given[n].shape[axis]
        grads[n] = lax.dynamic_slice_in_dim(grads[n], me * width, width, axis=axis)
    grads['w_mod'], grads['b_mod'] = grad_w_mod, grad_b_mod

    stacked = [jnp.stack(g_p[n]) for n in BIG]
    stacked = [g if cuts[n] is not None else _split_shards(g, BIG[n]) for n, g in zip(BIG, stacked)]
    landed = _exchange_many(stacked, [cuts[n] for n in BIG], False, 'scatter_weight_grads')

    upd = {}
    for n, slots in zip(BIG, landed):
        grads[n], *upd[n] = _adamw_landed(slots, given[n], given['m_' + n], given['v_' + n])
    upd['w_mod'] = _adamw_big(given['w_mod'], grads['w_mod'], given['m_w_mod'], given['v_w_mod'])
    rest = [n for n in WEIGHTS if n not in upd]
    out = _adamw_small([given[n] for n in rest], [grads[n] for n in rest], [given['m_' + n] for n in rest],
                       [given['v_' + n] for n in rest])
    upd.update(dict(zip(rest, out)))
    return (loss, g_x, *[grads[n] for n in WEIGHTS], *[upd[n][0] for n in WEIGHTS], *[upd[n][1] for n in WEIGHTS],
            *[upd[n][2] for n in WEIGHTS])
```

```python
import functools
import math

import jax
import jax.numpy as jnp
from jax import lax
from jax.experimental import pallas as pl
from jax.experimental.pallas import tpu as pltpu

F32, BF16 = jnp.float32, jnp.bfloat16
MESH = pl.DeviceIdType.MESH
N_DEV = 8
VMEM_LIMIT_BYTES = 56 * 1024 * 1024
MM_TILE_BYTES = 6 * 1024 * 1024
LANES = 128
PACK_SMALL = (128, 8)

GRID_W = 64
HEAD_DIM = 64
ROPE_BASE = 10000.0
EPS = 1e-6
N_MOD = 6
GQA_Q_HEADS, GQA_KV_HEADS = 12, 4
GQA_Q_W, GQA_KV_W = GQA_Q_HEADS * HEAD_DIM, GQA_KV_HEADS * HEAD_DIM
SSM_WIDTH, SSM_GROUP, SSM_GROUPS, SSM_STATE = 256, 16, 16, 64
MLA_HEADS, MLA_Q_RANK, MLA_KV_RANK, MLA_NOPE, MLA_ROPE, MLA_V = 8, 512, 256, 64, 32, 64
MLA_QK = MLA_NOPE + MLA_ROPE
NA_HEADS, NA_WIN_R, NA_WIN_C = 8, 8, 16
NA_W = NA_HEADS * HEAD_DIM
ODD_IN_W = MLA_Q_RANK + MLA_KV_RANK + MLA_ROPE + 3 * NA_W
ODD_NA_AT = 1024
ODD_IN_PAD = ODD_NA_AT + 3 * NA_W
NEG = -1e30

ADAM_LR, ADAM_B1, ADAM_B2, ADAM_EPS, ADAM_WD, ADAM_STEP = 0.001, 0.9, 0.999, 1e-08, 0.01, 10

FWD_PARAMS = ['x', 'c', 'ctx', 'c_ctx', 'w_mod', 'b_mod', 'g_norm1', 'g_norm2', 'w_ff1', 'w_ff2', 'e_w_in', 'e_w_out',
              'e_g_q', 'e_g_k', 'ssm_lam_re', 'ssm_lam_im', 'ssm_log_dt', 'ssm_b_re', 'ssm_b_im', 'ssm_c_re', 'ssm_c_im',
              'ssm_d', 'ssm_w_glu', 'ssm_b_glu', 'o_w_in', 'o_w_out', 'mla_g_cq', 'mla_g_ckv', 'mla_w_uq', 'mla_w_ukv',
              'mla_g_q', 'mla_g_k', 'na_g_q', 'na_g_k', 'na_rpb']
WEIGHTS = FWD_PARAMS[3:]
BIG = {'w_ff1': 2, 'w_ff2': 1, 'e_w_in': 2, 'e_w_out': 1, 'o_w_in': 2, 'o_w_out': 1, 'mla_w_uq': 2, 'mla_w_ukv': 2,
       'ssm_w_glu': 1}
SHARDED_SMALL = {'mla_g_cq': 1, 'mla_g_ckv': 1}
REPLICATED = [n for n in WEIGHTS if n not in BIG and n not in SHARDED_SMALL and n not in ('w_mod', 'c_ctx', 'b_mod')]


def _tile(dim, prefs):
    for p in prefs:
        if dim >= p and dim % p == 0:
            return p
    return dim


def _params(*sem):
    return pltpu.CompilerParams(dimension_semantics=sem, vmem_limit_bytes=VMEM_LIMIT_BYTES)


def _dot_nt(a, b):
    return lax.dot_general(a, b, (((1,), (1,)), ((), ())), preferred_element_type=F32)


def _dot_tn(a, b):
    return lax.dot_general(a, b, (((0,), (0,)), ((), ())), preferred_element_type=F32)


def _dot(a, b):
    return jnp.dot(a, b, preferred_element_type=F32)


def _matmul(a, b, kind, out_dtype, finish=None, extra=None, n_out=1):
    a, b = a.astype(BF16), b.astype(BF16)
    if kind == 'nn':
        (m, kd), n = a.shape, b.shape[1]
    elif kind == 'nt':
        (m, kd), n = a.shape, b.shape[0]
    else:
        (kd, m), n = a.shape, b.shape[1]
    if kind == 'tn':
        tm = m if m <= 1024 else _tile(m, (1024, 768, 512, 256, 128))
        tn = _tile(n, (1024, 768, 512, 256, 128))
    else:
        tn = n if kd * n * 2 <= MM_TILE_BYTES else _tile(n, (1024, 768, 512, 256, 128))
        tm = _tile(m, [t for t in (1536, 1024, 768, 512, 256, 128) if t * tn * 4 <= MM_TILE_BYTES])
    whole = kind != 'tn' and tn == n and kd * n * 2 <= MM_TILE_BYTES
    tk = kd if whole else _tile(kd, [t for t in (2048, 1536, 1024, 512, 256, 128) if t * max(tm, tn) * 4 <= MM_TILE_BYTES])
    nk = kd // tk
    dn = {'nn': (((1,), (0,)), ((), ())), 'nt': (((1,), (1,)), ((), ())), 'tn': (((0,), (0,)), ((), ()))}[kind]

    n_in = 2 if extra is None else 3

    def body(*refs):
        a_ref, b_ref = refs[:2]
        o_refs = refs[n_in:n_in + n_out]

        def store(total):
            outs = (total,) if finish is None else finish(total, refs[2][...] if extra is not None else None)
            for o_ref, val in zip(o_refs, outs):
                o_ref[...] = val.astype(o_ref.dtype)

        part = lax.dot_general(a_ref[...], b_ref[...], dn, preferred_element_type=F32)
        if nk == 1:
            store(part)
            return
        acc_ref, k = refs[n_in + n_out], pl.program_id(2)

        @pl.when(k == 0)
        def _():
            acc_ref[...] = part

        @pl.when((k > 0) & (k < nk - 1))
        def _():
            acc_ref[...] += part

        @pl.when(k == nk - 1)
        def _():
            store(acc_ref[...] + part)

    a_spec = pl.BlockSpec((tk, tm), lambda i, j, k: (k, i)) if kind == 'tn' else pl.BlockSpec((tm, tk), lambda i, j, k: (i, k))
    b_spec = pl.BlockSpec((tn, tk), lambda i, j, k: (j, k)) if kind == 'nt' else pl.BlockSpec((tk, tn), lambda i, j, k: (k, j))
    o_spec = pl.BlockSpec((tm, tn), lambda i, j, k: (i, j))
    out = pl.pallas_call(
        body, name='mm_' + kind, grid=(m // tm, n // tn, nk),
        out_shape=[jax.ShapeDtypeStruct((m, n), out_dtype)] * n_out,
        in_specs=[a_spec, b_spec] + ([o_spec] if extra is not None else []), out_specs=[o_spec] * n_out,
        scratch_shapes=[pltpu.VMEM((tm, tn), F32)] if nk > 1 else [],
        compiler_params=_params('parallel', 'parallel', 'arbitrary'),
    )(*((a, b) if extra is None else (a, b, extra)))
    return out[0] if n_out == 1 else out


@jax.custom_vjp
def linear(a, w):
    return _matmul(a, w, 'nn', F32)


def _linear_fwd(a, w):
    ab = a.astype(BF16)
    return _matmul(ab, w, 'nn', F32), (ab, w)


def _linear_bwd(res, g):
    ab, w = res
    gb = g.astype(BF16)
    return _matmul(gb, w, 'nt', F32), _matmul(ab, gb, 'tn', w.dtype)


linear.defvjp(_linear_fwd, _linear_bwd)


def _relu2(z, _):
    r = jnp.maximum(z, 0.0)
    return r, r * r


def _relu2_grad(d_act, r):
    return (d_act * (2.0 * r.astype(F32)),)


@jax.custom_vjp
def mlp_rows(h, w1, w2):
    return _mlp_rows_fwd(h, w1, w2)[0]


def _mlp_rows_fwd(h, w1, w2):
    hb = h.astype(BF16)
    r, act = _matmul(hb, w1, 'nn', BF16, finish=_relu2, n_out=2)
    return _matmul(act, w2, 'nn', F32), (hb, w1, w2, r, act)


def _mlp_rows_bwd(res, g):
    hb, w1, w2, r, act = res
    gb = g.astype(BF16)
    dz = _matmul(gb, w2, 'nt', BF16, finish=_relu2_grad, extra=r)
    return _matmul(dz, w1, 'nt', F32), _matmul(hb, dz, 'tn', w1.dtype), _matmul(act, gb, 'tn', w2.dtype)


mlp_rows.defvjp(_mlp_rows_fwd, _mlp_rows_bwd)


LOG2E = math.log2(math.e)


def _softmax_rows(t):
    m = jnp.max(t, axis=-1, keepdims=True)
    e = jnp.exp2(t - m)
    return e * (1.0 / jnp.sum(e, axis=-1, keepdims=True))


ATTN_SPLIT = 2


def _attn_specs(q, kt, vt, bq):
    _, h, nq, dq = q.shape
    _, hk, dv, nk = vt.shape
    g = h // hk
    q_spec = pl.BlockSpec((None, None, bq, dq), lambda b, j, gi, i: (b, j * g + gi, i, 0))
    k_spec = pl.BlockSpec((None, None, nk, dq), lambda b, j, gi, i: (b, j, 0, 0))
    v_spec = pl.BlockSpec((None, None, nk, dv), lambda b, j, gi, i: (b, j, 0, 0))
    o_spec = pl.BlockSpec((None, None, bq, dv), lambda b, j, gi, i: (b, j * g + gi, i, 0))
    t = dict(q=pl.BlockSpec((None, None, dq, bq), lambda b, j, gi, i: (b, j * g + gi, 0, i)),
             o=pl.BlockSpec((None, None, dv, bq), lambda b, j, gi, i: (b, j * g + gi, 0, i)),
             k=pl.BlockSpec((None, None, dq, nk), lambda b, j, gi, i: (b, j, 0, 0)),
             v=pl.BlockSpec((None, None, dv, nk), lambda b, j, gi, i: (b, j, 0, 0)))
    return (q.shape[0], hk, g, nq // bq), q_spec, k_spec, v_spec, o_spec, t


def _attn_blocks(nq):
    bq = _tile(nq, (512, 256, 128))
    return bq, [pl.ds(s * (bq // ATTN_SPLIT), bq // ATTN_SPLIT) for s in range(ATTN_SPLIT)]


def _attn_fwd_call(q, kt, vt, scale):
    b, h, nq, _ = q.shape
    dv = vt.shape[2]
    bq, subs = _attn_blocks(nq)
    grid, q_spec, _, _, _, t = _attn_specs(q, kt, vt, bq)

    def body(q_ref, kt_ref, vt_ref, ot_ref):
        ktb, vtb = kt_ref[...], vt_ref[...]
        for rows in subs:
            p = _softmax_rows(_dot(q_ref[rows, :], ktb) * (scale * LOG2E))
            ot_ref[:, rows] = _dot_nt(vtb, p.astype(BF16))

    return pl.pallas_call(
        body, name='attn_fwd', grid=grid, out_shape=jax.ShapeDtypeStruct((b, h, dv, nq), F32),
        in_specs=[q_spec, t['k'], t['v']], out_specs=t['o'],
        compiler_params=_params('parallel', 'parallel', 'arbitrary', 'arbitrary'),
    )(q, kt, vt)


def _attn_bwd_call(q, kt, vt, do, scale):
    b, h, nq, dq = q.shape
    _, hk, dv_width, nk = vt.shape
    bq, subs = _attn_blocks(nq)
    grid, q_spec, k_spec, v_spec, o_spec, t = _attn_specs(q, kt, vt, bq)

    def body(q_ref, kt_ref, vt_ref, do_ref, dqt_ref, dk_ref, dv_ref):
        @pl.when((pl.program_id(2) == 0) & (pl.program_id(3) == 0))
        def _():
            dk_ref[...] = jnp.zeros_like(dk_ref)
            dv_ref[...] = jnp.zeros_like(dv_ref)

        ktb, vtb = kt_ref[...], vt_ref[...]
        dk, dv = [], []
        for rows in subs:
            qb, dob = q_ref[rows, :], do_ref[rows, :]
            p = _softmax_rows(_dot(qb, ktb) * (scale * LOG2E))
            dp = _dot(dob, vtb)
            ds = p * (dp - jnp.sum(p * dp, axis=-1, keepdims=True))
            dsb = (ds * scale).astype(BF16)
            dqt_ref[:, rows] = _dot_nt(ktb, dsb)
            dk.append(_dot_tn(dsb, qb))
            dv.append(_dot_tn(p.astype(BF16), dob))
        dk_ref[...] += sum(dk[1:], dk[0])
        dv_ref[...] += sum(dv[1:], dv[0])

    return pl.pallas_call(
        body, name='attn_bwd', grid=grid,
        out_shape=(jax.ShapeDtypeStruct((b, h, dq, nq), F32), jax.ShapeDtypeStruct((b, hk, nk, dq), F32),
                   jax.ShapeDtypeStruct((b, hk, nk, dv_width), F32)),
        in_specs=[q_spec, t['k'], t['v'], o_spec], out_specs=(t['q'], k_spec, v_spec),
        compiler_params=_params('parallel', 'parallel', 'arbitrary', 'arbitrary'),
    )(q, kt, vt, do)


@functools.partial(jax.custom_vjp, nondiff_argnums=(3,))
def attention(q, k, v, scale):
    return _attention_fwd(q, k, v, scale)[0]


def _attention_fwd(q, k, v, scale):
    qb, kt, vt = q.astype(BF16), jnp.swapaxes(k.astype(BF16), 2, 3), jnp.swapaxes(v.astype(BF16), 2, 3)
    return jnp.swapaxes(_attn_fwd_call(qb, kt, vt, scale), 2, 3), (qb, kt, vt)


def _attention_bwd(scale, res, g):
    dqt, dk, dv = _attn_bwd_call(*res, g.astype(BF16), scale)
    return jnp.swapaxes(dqt, 2, 3), dk, dv


attention.defvjp(_attention_fwd, _attention_bwd)


def _na_window(r, rows):
    start = jnp.clip(r - NA_WIN_R // 2, 0, rows - NA_WIN_R)
    return start, r - start


NA_TOGETHER = 16


def _na_group(r0, rows):
    out = []
    for g in range(NA_TOGETHER):
        start, off = _na_window(r0 + g, rows)
        out.append((pl.ds(pl.multiple_of(start * GRID_W, GRID_W), NA_WIN_R * GRID_W), off))
    return out


def _na_rows(x, g):
    return x[g * GRID_W:(g + 1) * GRID_W]


def _na_scores(qs, kws, kc, biases, scale):
    n = len(kws)
    s1 = jnp.stack([_dot_nt(_na_rows(qs, g), kws[g]) * scale + biases[g] for g in range(n)])
    s2 = (_dot_nt(qs, kc) * scale).reshape(n, GRID_W, kc.shape[0])
    m = jnp.maximum(jnp.max(s1, axis=-1, keepdims=True), jnp.max(s2, axis=-1, keepdims=True))
    e1, e2 = jnp.exp(s1 - m), jnp.exp(s2 - m)
    inv = 1.0 / (jnp.sum(e1, axis=-1, keepdims=True) + jnp.sum(e2, axis=-1, keepdims=True))
    return e1 * inv, e2 * inv


def _na_specs(q, kc):
    _, _, n, d = q.shape
    c = kc.shape[2]
    win = NA_WIN_R * GRID_W
    tok = pl.BlockSpec((None, None, n, d), lambda b, h: (b, h, 0, 0))
    ctx = pl.BlockSpec((None, None, c, d), lambda b, h: (b, h, 0, 0))
    bias = pl.BlockSpec((None, NA_WIN_R, GRID_W, win), lambda b, h: (h, 0, 0, 0))
    dbias = pl.BlockSpec((None, None, NA_WIN_R, GRID_W, win), lambda b, h: (b, h, 0, 0, 0))
    return tok, ctx, bias, dbias


def _na_fwd_call(q, k, v, kc, vc, bias, scale):
    b, h, n, d = q.shape
    rows, span = n // GRID_W, NA_TOGETHER * GRID_W
    tok, ctx, bias_spec, _ = _na_specs(q, kc)

    def body(q_ref, k_ref, v_ref, kc_ref, vc_ref, b_ref, o_ref):
        def step(i, carry):
            at = pl.ds(pl.multiple_of(i * span, span), span)
            wins = _na_group(i * NA_TOGETHER, rows)
            p1, p2 = _na_scores(q_ref[at, :], [k_ref[w, :] for w, _ in wins], kc_ref[...], [b_ref[off] for _, off in wins],
                                scale)
            p1, p2 = p1.astype(BF16), p2.astype(BF16)
            local = jnp.concatenate([_dot(p1[g], v_ref[w, :]) for g, (w, _) in enumerate(wins)], axis=0)
            o_ref[at, :] = local + _dot(p2.reshape(span, p2.shape[2]), vc_ref[...])
            return carry

        lax.fori_loop(0, rows // NA_TOGETHER, step, 0)

    return pl.pallas_call(
        body, name='na_fwd', grid=(b, h), out_shape=jax.ShapeDtypeStruct(q.shape, F32),
        in_specs=[tok, tok, tok, ctx, ctx, bias_spec], out_specs=tok,
        compiler_params=_params('parallel', 'parallel'),
    )(q, k, v, kc, vc, bias)


def _na_bwd_call(q, k, v, kc, vc, bias, do, scale):
    b, h, n, d = q.shape
    rows, span = n // GRID_W, NA_TOGETHER * GRID_W
    tok, ctx, bias_spec, dbias_spec = _na_specs(q, kc)

    def body(q_ref, k_ref, v_ref, kc_ref, vc_ref, b_ref, do_ref, dq_ref, dk_ref, dv_ref, dkc_ref, dvc_ref, db_ref):
        for ref in (dk_ref, dv_ref, dkc_ref, dvc_ref, db_ref):
            ref[...] = jnp.zeros_like(ref)

        def step(i, carry):
            at = pl.ds(pl.multiple_of(i * span, span), span)
            wins = _na_group(i * NA_TOGETHER, rows)
            qs, dob, kcb, vcb = q_ref[at, :], do_ref[at, :], kc_ref[...], vc_ref[...]
            kws, vws = [k_ref[w, :] for w, _ in wins], [v_ref[w, :] for w, _ in wins]
            p1, p2 = _na_scores(qs, kws, kcb, [b_ref[off] for _, off in wins], scale)
            dp1 = jnp.stack([_dot_nt(_na_rows(dob, g), vws[g]) for g in range(NA_TOGETHER)])
            dp2 = _dot_nt(dob, vcb).reshape(p2.shape)
            delta = jnp.sum(p1 * dp1, axis=-1, keepdims=True) + jnp.sum(p2 * dp2, axis=-1, keepdims=True)
            ds1, ds2 = p1 * (dp1 - delta), p2 * (dp2 - delta)
            ds1b, p1b = (ds1 * scale).astype(BF16), p1.astype(BF16)
            ds2b = (ds2 * scale).astype(BF16).reshape(span, p2.shape[2])
            p2b = p2.astype(BF16).reshape(span, p2.shape[2])
            dq_ref[at, :] = jnp.concatenate([_dot(ds1b[g], kws[g]) for g in range(NA_TOGETHER)], axis=0) + _dot(ds2b, kcb)
            for g, (w, off) in enumerate(wins):
                db_ref[off] += ds1[g]
                dk_ref[w, :] += _dot_tn(ds1b[g], _na_rows(qs, g))
                dv_ref[w, :] += _dot_tn(p1b[g], _na_rows(dob, g))
            dkc_ref[...] += _dot_tn(ds2b, qs)
            dvc_ref[...] += _dot_tn(p2b, dob)
            return carry

        lax.fori_loop(0, rows // NA_TOGETHER, step, 0)

    f = lambda a: jax.ShapeDtypeStruct(a.shape, F32)
    return pl.pallas_call(
        body, name='na_bwd', grid=(b, h),
        out_shape=(f(q), f(k), f(v), f(kc), f(vc), jax.ShapeDtypeStruct((b,) + bias.shape, F32)),
        in_specs=[tok, tok, tok, ctx, ctx, bias_spec, tok], out_specs=(tok, tok, tok, ctx, ctx, dbias_spec),
        compiler_params=_params('parallel', 'parallel'),
    )(q, k, v, kc, vc, bias, do)


@functools.partial(jax.custom_vjp, nondiff_argnums=(6,))
def na_attention(q, k, v, kc, vc, bias, scale):
    return _na_fwd_call(q.astype(BF16), k.astype(BF16), v.astype(BF16), kc.astype(BF16), vc.astype(BF16), bias, scale)


def _na_attention_fwd(q, k, v, kc, vc, bias, scale):
    res = (q.astype(BF16), k.astype(BF16), v.astype(BF16), kc.astype(BF16), vc.astype(BF16), bias)
    return _na_fwd_call(*res, scale), res


def _na_attention_bwd(scale, res, g):
    dq, dk, dv, dkc, dvc, db = _na_bwd_call(*res, g.astype(BF16), scale)
    return dq, dk, dv, dkc, dvc, jnp.sum(db, axis=0)


na_attention.defvjp(_na_attention_fwd, _na_attention_bwd)


def _na_table_index():
    qcol = jnp.arange(GRID_W)
    kcol = jnp.arange(GRID_W)
    cstart = jnp.clip(qcol - NA_WIN_C // 2, 0, GRID_W - NA_WIN_C)
    inside = (kcol[None, :] >= cstart[:, None]) & (kcol[None, :] < cstart[:, None] + NA_WIN_C)
    cidx = jnp.clip(kcol[None, :] - qcol[:, None] + (NA_WIN_C - 1), 0, 2 * NA_WIN_C - 2)
    ridx = jnp.arange(NA_WIN_R)[None, :] - jnp.arange(NA_WIN_R)[:, None] + (NA_WIN_R - 1)
    return inside, cidx, ridx


@jax.custom_vjp
def na_bias_table(rpb):
    inside, pick_c, pick_r = _na_table_picks()
    rows = jnp.einsum('hab,oja->hojb', rpb, pick_r, precision=lax.Precision.HIGHEST)
    t = jnp.einsum('hojb,qkb->hoqjk', rows, pick_c, precision=lax.Precision.HIGHEST)
    t = jnp.where(inside[None, None, :, None, :], t, NEG)
    return t.reshape(rpb.shape[0], NA_WIN_R, GRID_W, NA_WIN_R * GRID_W)


def _na_table_picks():
    inside, cidx, ridx = _na_table_index()
    pick_c = ((cidx[..., None] == jnp.arange(2 * NA_WIN_C - 1)) & inside[..., None]).astype(F32)
    pick_r = (ridx[..., None] == jnp.arange(2 * NA_WIN_R - 1)).astype(F32)
    return inside, pick_c, pick_r


def _na_bias_table_bwd(_, dt):
    _, pick_c, pick_r = _na_table_picks()
    d5 = dt.reshape(dt.shape[0], NA_WIN_R, GRID_W, NA_WIN_R, GRID_W)
    part = jnp.einsum('hoqjk,qkb->hojb', d5, pick_c, precision=lax.Precision.HIGHEST)
    return (jnp.einsum('hojb,oja->hab', part, pick_r, precision=lax.Precision.HIGHEST),)


na_bias_table.defvjp(lambda rpb: (na_bias_table(rpb), None), _na_bias_table_bwd)


S5_HALF = SSM_WIDTH // 2
S5_LANES = (SSM_GROUPS // 2) * SSM_STATE
S5_Q = S5_LANES // LANES


def _s5_tiles(a):
    r = a.shape[0]
    return jnp.transpose(a.reshape(r, S5_Q, LANES), (1, 0, 2)).reshape(S5_Q * r, LANES)


def _s5_untiles(a):
    r = a.shape[0] // S5_Q
    return jnp.transpose(a.reshape(S5_Q, r, LANES), (1, 0, 2)).reshape(r, S5_LANES)


def _s5_put(ref, r, rr, tc, val):
    for q in range(S5_Q):
        ref[pl.ds((q * rr + r) * tc, tc), :] = val[:, q * LANES:(q + 1) * LANES]


def _s5_get(ref, r, rr, tc):
    return jnp.concatenate([ref[pl.ds((q * rr + r) * tc, tc), :] for q in range(S5_Q)], axis=1)


S5_BLOCK = 8
S5_TOGETHER = 8


def _s5_powers(a_re, a_im, backward):
    a_im = -a_im if backward else a_im
    pr, pi = [a_re], [a_im]
    for _ in range(S5_BLOCK - 1):
        pr, pi = pr + [pr[-1] * a_re - pi[-1] * a_im], pi + [pr[-1] * a_im + pi[-1] * a_re]
    order = range(S5_BLOCK - 1, -1, -1) if backward else range(S5_BLOCK)

    def table(p):
        rows = [jnp.broadcast_to(p[s - 1][:, None, :], (p[0].shape[0], S5_BLOCK, LANES)) for s in (1, 2, 4)]
        return jnp.stack(rows + [jnp.stack([p[t] for t in order], axis=1)], axis=1)

    return table(pr), table(pi)


def _s5_scan_block(xr, xi, pr_ref, pi_ref, chain, carry, backward):
    row = lax.broadcasted_iota(jnp.int32, (S5_BLOCK, LANES), 0)
    for e, s in enumerate((1, 2, 4)):
        ar, ai = pr_ref[chain, e], pi_ref[chain, e]
        keep = (row < S5_BLOCK - s) if backward else (row >= s)
        shift = S5_BLOCK - s if backward else s
        sr = jnp.where(keep, pltpu.roll(xr, shift, 0), 0.0)
        si = jnp.where(keep, pltpu.roll(xi, shift, 0), 0.0)
        xr, xi = xr + (ar * sr - ai * si), xi + (ar * si + ai * sr)
    ar, ai = pr_ref[chain, 3], pi_ref[chain, 3]
    cr, ci = carry
    xr, xi = xr + (ar * cr - ai * ci), xi + (ar * ci + ai * cr)
    edge = slice(0, 1) if backward else slice(S5_BLOCK - 1, S5_BLOCK)
    return xr, xi, (xr[edge], xi[edge])


def _s5_scan_chunk(xr_ref, xi_ref, pr_ref, pi_ref, sr_ref, si_ref, chains, tc, backward):
    blocks = tc // S5_BLOCK
    for first in range(0, chains, S5_TOGETHER):
        group = range(first, min(first + S5_TOGETHER, chains))

        def block(k, carries, group=group):
            j = blocks - 1 - k if backward else k
            out = []
            for chain, carry in zip(group, carries):
                at = pl.ds(pl.multiple_of(chain * tc + j * S5_BLOCK, S5_BLOCK), S5_BLOCK)
                xr, xi, carry = _s5_scan_block(xr_ref[at, :], xi_ref[at, :], pr_ref, pi_ref, chain, carry, backward)
                xr_ref[at, :] = xr
                xi_ref[at, :] = xi
                out.append(carry)
            return tuple(out)

        start = tuple((sr_ref[pl.ds(chain, 1), :], si_ref[pl.ds(chain, 1), :]) for chain in group)
        for chain, (cr, ci) in zip(group, lax.fori_loop(0, blocks, block, start)):
            sr_ref[pl.ds(chain, 1), :] = cr
            si_ref[pl.ds(chain, 1), :] = ci


def _s5_fwd_call(u, a_re, a_im, b_re, b_im, c_re, c_im):
    rr, t_len, _ = u.shape
    sets = b_re.shape[0]
    per = rr // sets
    tc = _tile(t_len, (256, 128))
    nt = t_len // tc
    qr = S5_Q * rr

    def body(u_ref, ar_ref, ai_ref, br_ref, bi_ref, cr_ref, ci_ref, y_ref, hr_ref, hi_ref, sr_ref, si_ref):
        @pl.when(pl.program_id(0) == 0)
        def _():
            sr_ref[...] = jnp.zeros_like(sr_ref)
            si_ref[...] = jnp.zeros_like(si_ref)

        for r in range(rr):
            ub = u_ref[r]
            _s5_put(hr_ref, r, rr, tc, _dot(ub, br_ref[r // per]))
            _s5_put(hi_ref, r, rr, tc, _dot(ub, bi_ref[r // per]))
        _s5_scan_chunk(hr_ref, hi_ref, ar_ref, ai_ref, sr_ref, si_ref, qr, tc, False)
        for r in range(rr):
            y_ref[r] = (_dot(_s5_get(hr_ref, r, rr, tc).astype(BF16), cr_ref[r // per])
                        - _dot(_s5_get(hi_ref, r, rr, tc).astype(BF16), ci_ref[r // per]))

    full = lambda a: pl.BlockSpec(a.shape, lambda i: (0,) * a.ndim)
    h_spec = pl.BlockSpec((None, qr * tc, LANES), lambda i: (i, 0, 0))
    h_shape = jax.ShapeDtypeStruct((nt, qr * tc, LANES), F32)
    a_re, a_im = _s5_powers(_s5_tiles(a_re), _s5_tiles(a_im), False)
    return pl.pallas_call(
        body, name='s5_fwd', grid=(nt,),
        out_shape=(jax.ShapeDtypeStruct((rr, t_len, S5_HALF), F32), h_shape, h_shape),
        in_specs=[pl.BlockSpec((rr, tc, S5_HALF), lambda i: (0, i, 0)), full(a_re), full(a_im), full(b_re), full(b_im),
                  full(c_re), full(c_im)],
        out_specs=(pl.BlockSpec((rr, tc, S5_HALF), lambda i: (0, i, 0)), h_spec, h_spec),
        scratch_shapes=[pltpu.VMEM((qr, LANES), F32), pltpu.VMEM((qr, LANES), F32)],
        compiler_params=_params('arbitrary'),
    )(u, a_re, a_im, b_re, b_im, c_re, c_im)


def _s5_bwd_call(u, a_re, a_im, b_re, b_im, c_re, c_im, h_re, h_im, dy):
    rr, t_len, _ = u.shape
    sets = b_re.shape[0]
    per = rr // sets
    nt, rows, _ = h_re.shape
    qr = S5_Q * rr
    tc = rows // qr

    def body(u_ref, dy_ref, ar_ref, ai_ref, br_ref, bi_ref, cr_ref, ci_ref, hr_ref, hi_ref,
             du_ref, dar_ref, dai_ref, dbr_ref, dbi_ref, dcr_ref, dci_ref, gr_ref, gi_ref, sr_ref, si_ref):
        i = pl.program_id(0)

        @pl.when(i == 0)
        def _():
            for ref in (dar_ref, dai_ref, dbr_ref, dbi_ref, dcr_ref, dci_ref, sr_ref, si_ref):
                ref[...] = jnp.zeros_like(ref)

        for r in range(rr):
            dyb = dy_ref[r]
            _s5_put(gr_ref, r, rr, tc, _dot_nt(dyb, cr_ref[r // per]))
            _s5_put(gi_ref, r, rr, tc, -_dot_nt(dyb, ci_ref[r // per]))
        g_r, g_i = sr_ref[...], si_ref[...]
        last = pl.ds(tc - 1, qr, stride=tc)
        dar_ref[...] += g_r * hr_ref[last, :] + g_i * hi_ref[last, :]
        dai_ref[...] += g_i * hr_ref[last, :] - g_r * hi_ref[last, :]
        _s5_scan_chunk(gr_ref, gi_ref, ar_ref, ai_ref, sr_ref, si_ref, qr, tc, True)
        row = lax.broadcasted_iota(jnp.int32, (tc, LANES), 0)
        for chain in range(qr):
            at, one = pl.ds(chain * tc, tc), pl.ds(chain, 1)
            p_r = jnp.where(row >= 1, pltpu.roll(hr_ref[at, :], 1, 0), 0.0)
            p_i = jnp.where(row >= 1, pltpu.roll(hi_ref[at, :], 1, 0), 0.0)
            g_r, g_i = gr_ref[at, :], gi_ref[at, :]
            dar_ref[one, :] += jnp.sum(g_r * p_r + g_i * p_i, axis=0, keepdims=True)
            dai_ref[one, :] += jnp.sum(g_i * p_r - g_r * p_i, axis=0, keepdims=True)
        for r in range(rr):
            s = r // per
            ub, dyb = u_ref[r], dy_ref[r]
            grb, gib = _s5_get(gr_ref, r, rr, tc).astype(BF16), _s5_get(gi_ref, r, rr, tc).astype(BF16)
            du_ref[r] = _dot_nt(grb, br_ref[s]) + _dot_nt(gib, bi_ref[s])
            dbr_ref[s] += _dot_tn(ub, grb)
            dbi_ref[s] += _dot_tn(ub, gib)
            dcr_ref[s] += _dot_tn(_s5_get(hr_ref, r, rr, tc).astype(BF16), dyb)
            dci_ref[s] -= _dot_tn(_s5_get(hi_ref, r, rr, tc).astype(BF16), dyb)

    full = lambda a: pl.BlockSpec(a.shape, lambda i: (0,) * a.ndim)
    back = lambda i: nt - 1 - i
    tok = pl.BlockSpec((rr, tc, S5_HALF), lambda i: (0, back(i), 0))
    h_spec = pl.BlockSpec((None, qr * tc, LANES), lambda i: (back(i), 0, 0))
    f = lambda a: jax.ShapeDtypeStruct(a.shape, F32)
    a_re, a_im = _s5_powers(_s5_tiles(a_re), _s5_tiles(a_im), True)
    da = jax.ShapeDtypeStruct((qr, LANES), F32)
    du, da_re, da_im, db_re, db_im, dc_re, dc_im = pl.pallas_call(
        body, name='s5_bwd', grid=(nt,),
        out_shape=(jax.ShapeDtypeStruct(u.shape, F32), da, da, f(b_re), f(b_im), f(c_re), f(c_im)),
        in_specs=[tok, tok, full(a_re), full(a_im), full(b_re), full(b_im), full(c_re), full(c_im), h_spec, h_spec],
        out_specs=(tok, full(da), full(da), full(b_re), full(b_im), full(c_re), full(c_im)),
        scratch_shapes=[pltpu.VMEM((qr * tc, LANES), F32), pltpu.VMEM((qr * tc, LANES), F32),
                        pltpu.VMEM((qr, LANES), F32), pltpu.VMEM((qr, LANES), F32)],
        compiler_params=_params('arbitrary'),
    )(u, dy, a_re, a_im, b_re, b_im, c_re, c_im, h_re, h_im)
    return du, _s5_untiles(da_re), _s5_untiles(da_im), db_re, db_im, dc_re, dc_im


@jax.custom_vjp
def s5_core(u, a_re, a_im, b_re, b_im, c_re, c_im):
    return _s5_fwd_call(u.astype(BF16), a_re, a_im, b_re.astype(BF16), b_im.astype(BF16), c_re.astype(BF16),
                        c_im.astype(BF16))[0]


def _s5_core_fwd(u, a_re, a_im, b_re, b_im, c_re, c_im):
    args = (u.astype(BF16), a_re, a_im, b_re.astype(BF16), b_im.astype(BF16), c_re.astype(BF16), c_im.astype(BF16))
    y, h_re, h_im = _s5_fwd_call(*args)
    return y, args + (h_re, h_im)


def _s5_core_bwd(res, g):
    return _s5_bwd_call(*res, g.astype(BF16))


s5_core.defvjp(_s5_core_fwd, _s5_core_bwd)


def _shard_view(ref, axis, index, width):
    return ref.at[(slice(None),) * axis + (pl.ds(pl.multiple_of(index * width, width), width),)]


def _exchange_many(xs, cuts, gather, name):
    n = len(xs)
    if gather:
        shards = [x.shape for x in xs]
    else:
        shards = [x.shape[1:] if cut is None else x.shape[:cut] + (x.shape[cut] // N_DEV,) + x.shape[cut + 1:]
                  for x, cut in zip(xs, cuts)]

    def full_shape(shard, cut):
        return shard[:cut] + (N_DEV * shard[cut],) + shard[cut + 1:]

    out_shapes = [jax.ShapeDtypeStruct((N_DEV,) + tuple(s) if (cut is None or not gather) else full_shape(tuple(s), cut), x.dtype)
                  for x, s, cut in zip(xs, shards, cuts)]

    def body(*refs):
        x_refs, out_refs = refs[:n], refs[n:2 * n]
        send_sems, recv_sems, local_sems = refs[2 * n:]
        ix, iy, ic = lax.axis_index('x'), lax.axis_index('y'), lax.axis_index('c')
        me = 4 * ix + 2 * iy + ic

        def flipped(k):
            px = 1 - ix if k & 4 else ix
            py = 1 - iy if k & 2 else iy
            pc = 1 - ic if k & 1 else ic
            return (px, py, pc), 4 * px + 2 * py + pc

        def block(ref, cut, shard, who):
            return ref.at[who] if cut is None else _shard_view(ref, cut, who, shard[cut])

        def ends(i, sender, receiver):
            if gather:
                return x_refs[i], block(out_refs[i], cuts[i], shards[i], sender)
            return block(x_refs[i], cuts[i], shards[i], receiver), out_refs[i].at[sender]

        def copy(i, k, sender, receiver):
            src, dst = ends(i, sender, receiver)
            return pltpu.make_async_remote_copy(src_ref=src, dst_ref=dst, send_sem=send_sems.at[i * (N_DEV - 1) + k - 1],
                                                 recv_sem=recv_sems.at[i * (N_DEV - 1) + k - 1], device_id=flipped(k)[0],
                                                 device_id_type=MESH)

        own = [pltpu.make_async_copy(*ends(i, me, me), local_sems.at[i]) for i in range(n)]
        for cp in own:
            cp.start()
        if gather:
            chips = (2, 4, 6)
            sent = [copy(i, k, me, flipped(k)[1]) for k in (1,) + chips for i in range(n)]
            for cp in sent:
                cp.start()
            for k in chips:
                for i in range(n):
                    copy(i, k, flipped(k)[1], me).wait_recv()
                    src, dst = ends(i, flipped(k)[1], me)
                    sent.append(pltpu.make_async_remote_copy(
                        src_ref=dst, dst_ref=dst, send_sem=send_sems.at[i * (N_DEV - 1) + k], recv_sem=recv_sems.at[i * (N_DEV - 1) + k],
                        device_id=flipped(1)[0], device_id_type=MESH))
                    sent[-1].start()
            for k in (1, 3, 5, 7):
                for i in range(n):
                    src, dst = ends(i, flipped(k)[1], me)
                    pltpu.make_async_remote_copy(
                        src_ref=dst, dst_ref=dst, send_sem=send_sems.at[i * (N_DEV - 1) + k - 1], recv_sem=recv_sems.at[i * (N_DEV - 1) + k - 1],
                        device_id=flipped(1)[0], device_id_type=MESH).wait_recv()
            for cp in sent:
                cp.wait_send()
            for cp in own:
                cp.wait()
            return
        sent = [copy(i, k, me, flipped(k)[1]) for k in range(1, N_DEV) for i in range(n)]
        for cp in sent:
            cp.start()
        for k in range(1, N_DEV):
            for i in range(n):
                copy(i, k, flipped(k)[1], me).wait_recv()
        for cp in sent:
            cp.wait_send()
        for cp in own:
            cp.wait()

    hbm = pl.BlockSpec(memory_space=pltpu.HBM)
    pairs = n * (N_DEV - 1)
    return pl.pallas_call(
        body, name=name, out_shape=out_shapes, in_specs=[hbm] * n, out_specs=[hbm] * n,
        scratch_shapes=[pltpu.SemaphoreType.DMA((pairs,)), pltpu.SemaphoreType.DMA((pairs,)), pltpu.SemaphoreType.DMA((n,))],
    )(*xs)


def _adamw_landed(landed, w, m, v):
    shape = w.shape
    slots = landed.shape[0]
    w, m, v = (_as_rows(a) for a in (w, m, v))
    rows, cols = w.shape
    landed = landed.reshape(slots, rows, cols)
    tr = _tile(rows, (256, 128, 64, 32, 16))

    def body(l_ref, w_ref, m_ref, v_ref, g_ref, d_ref, nm_ref, nv_ref):
        g = l_ref[0].astype(F32)
        for d in range(1, slots):
            g = g + l_ref[d].astype(F32)
        g_ref[...] = g
        d_ref[...], nm_ref[...], nv_ref[...] = _adamw_math(w_ref[...], g, m_ref[...], v_ref[...])

    spec = pl.BlockSpec((tr, cols), lambda i: (i, 0))
    out = pl.pallas_call(
        body, name='adamw_landed', grid=(rows // tr,), out_shape=(jax.ShapeDtypeStruct(w.shape, F32),) * 4,
        in_specs=[pl.BlockSpec((slots, tr, cols), lambda i: (0, i, 0))] + [spec] * 3, out_specs=(spec,) * 4,
        compiler_params=_params('parallel'),
    )(landed, w, m, v)
    return tuple(o.reshape(shape) for o in out)


def _all_reduce_small(x):
    g = _exchange_many([x], [None], True, 'gather_small_grads')[0]

    def body(g_ref, o_ref):
        acc = g_ref[0]
        for d in range(1, N_DEV):
            acc = acc + g_ref[d]
        o_ref[...] = acc

    return pl.pallas_call(body, name='sum_small', out_shape=jax.ShapeDtypeStruct(x.shape, F32))(g)


def _adamw_math(w, g, m, v):
    m = ADAM_B1 * m + (1.0 - ADAM_B1) * g
    v = ADAM_B2 * v + (1.0 - ADAM_B2) * (g * g)
    m_hat = m / (1.0 - ADAM_B1 ** ADAM_STEP)
    v_hat = v / (1.0 - ADAM_B2 ** ADAM_STEP)
    return -ADAM_LR * (m_hat / (jnp.sqrt(v_hat) + ADAM_EPS) + ADAM_WD * w), m, v


def _as_rows(a):
    return a.reshape(1, -1) if a.ndim < 2 else a.reshape(-1, a.shape[-1])


def _as_lanes(a):
    return a.reshape(-1, LANES) if a.size % LANES == 0 else a.reshape(1, -1)


def _adamw_big(w, g, m, v):
    shape = w.shape
    w, g, m, v = (_as_rows(a) for a in (w, g, m, v))
    rows, cols = w.shape
    tr = _tile(rows, (512, 256, 128, 64, 32, 16, 8))

    def body(w_ref, g_ref, m_ref, v_ref, d_ref, nm_ref, nv_ref):
        d_ref[...], nm_ref[...], nv_ref[...] = _adamw_math(w_ref[...], g_ref[...], m_ref[...], v_ref[...])

    spec = pl.BlockSpec((tr, cols), lambda i: (i, 0))
    out = pl.pallas_call(
        body, name='adamw', grid=(rows // tr,), out_shape=(jax.ShapeDtypeStruct(w.shape, F32),) * 3,
        in_specs=[spec] * 4, out_specs=(spec,) * 3, compiler_params=_params('parallel'),
    )(w, g, m, v)
    return tuple(o.reshape(shape) for o in out)


def _adamw_small(ws, gs, ms, vs):
    n = len(ws)
    shapes = [w.shape for w in ws]
    flat = [_as_lanes(a) for group in (ws, gs, ms, vs) for a in group]

    def body(*refs):
        ins, outs = refs[:4 * n], refs[4 * n:]
        for i in range(n):
            d, m, v = _adamw_math(ins[i][...], ins[n + i][...], ins[2 * n + i][...], ins[3 * n + i][...])
            outs[i][...], outs[n + i][...], outs[2 * n + i][...] = d, m, v

    out = pl.pallas_call(
        body, name='adamw_small', out_shape=tuple(jax.ShapeDtypeStruct(flat[i].shape, F32) for _ in range(3) for i in range(n)),
    )(*flat)
    return [tuple(out[j * n + i].reshape(shapes[i]) for j in range(3)) for i in range(n)]


def rms_norm(x, g):
    return x * lax.rsqrt(jnp.mean(jnp.square(x), axis=-1, keepdims=True) + EPS) * g


def modulate(x, g, shift, scale):
    return rms_norm(x, g) * (1 + scale) + shift


def rope_tables(n_tokens, rot_dim):
    t = jnp.arange(n_tokens)
    rows = (t // GRID_W).astype(F32)
    cols = (t % GRID_W).astype(F32)
    axis_dim = rot_dim // 2
    freqs = ROPE_BASE ** (-jnp.arange(0, axis_dim, 2, dtype=F32) / axis_dim)
    ang_r, ang_c = rows[:, None] * freqs, cols[:, None] * freqs
    ang = jnp.concatenate([ang_r, ang_r, ang_c, ang_c], axis=-1)
    return jnp.cos(ang), jnp.sin(ang)


def rope(x, cos, sin):
    x1, x2, x3, x4 = jnp.split(x, 4, axis=-1)
    rot = jnp.concatenate([-x2, x1, -x4, x3], axis=-1)
    return x * cos[:, None, :] + rot * sin[:, None, :]


def heads_first(t):
    return jnp.swapaxes(t, 1, 2)


def tokens_matmul(t, w):
    b, n, k = t.shape
    return linear(t.reshape(b * n, k), w).reshape(b, n, w.shape[1])


def s5_discretize(lam_re, lam_im, log_dt, b_re, b_im):
    dt = jnp.exp(log_dt)[:, None]
    mag = jnp.exp(lam_re * dt)
    a_re = mag * jnp.cos(lam_im * dt)
    a_im = mag * jnp.sin(lam_im * dt)
    den = jnp.square(lam_re) + jnp.square(lam_im)
    f_re = ((a_re - 1.0) * lam_re + a_im * lam_im) / den
    f_im = (a_im * lam_re - (a_re - 1.0) * lam_im) / den
    bb_re = f_re[..., None] * b_re - f_im[..., None] * b_im
    bb_im = f_re[..., None] * b_im + f_im[..., None] * b_re
    return a_re, a_im, bb_re, bb_im


def s5_mixer(u_lat, u_ctx, p, j, need_ctx):
    b, n, _ = u_lat.shape
    c = u_ctx.shape[1]
    half_groups = SSM_GROUPS // 2
    eye = jnp.eye(half_groups, dtype=F32)
    a_res, a_ims, b_res, b_ims, c_res, c_ims, seqs = [], [], [], [], [], [], []
    for d in range(2):
        a_re, a_im, bb_re, bb_im = s5_discretize(p['ssm_lam_re'][j, d], p['ssm_lam_im'][j, d], p['ssm_log_dt'][j, d],
                                                 p['ssm_b_re'][j, d], p['ssm_b_im'][j, d])
        for half in range(2):
            grp = slice(half * half_groups, (half + 1) * half_groups)
            a_res.append(a_re[grp].reshape(S5_LANES))
            a_ims.append(a_im[grp].reshape(S5_LANES))
            b_res.append(jnp.einsum('gsp,gh->gphs', bb_re[grp], eye).reshape(S5_HALF, S5_LANES))
            b_ims.append(jnp.einsum('gsp,gh->gphs', bb_im[grp], eye).reshape(S5_HALF, S5_LANES))
            c_res.append(jnp.einsum('gps,gh->gshp', p['ssm_c_re'][j, d][grp], eye).reshape(S5_LANES, S5_HALF))
            c_ims.append(jnp.einsum('gps,gh->gshp', p['ssm_c_im'][j, d][grp], eye).reshape(S5_LANES, S5_HALF))
        flip = (lambda t: t[:, ::-1]) if d == 1 else (lambda t: t)
        seq = jnp.concatenate([flip(u_ctx), flip(u_lat)], axis=1)
        seqs.append(jnp.transpose(seq.reshape(b, c + n, 2, S5_HALF), (2, 0, 1, 3)))
    u = jnp.stack(seqs).reshape(4 * b, c + n, S5_HALF)
    rep = lambda parts: jnp.repeat(jnp.stack(parts), b, axis=0)
    y = s5_core(u, rep(a_res), rep(a_ims), jnp.stack(b_res), jnp.stack(b_ims), jnp.stack(c_res), jnp.stack(c_ims))
    y = jnp.transpose(y.reshape(2, 2, b, c + n, S5_HALF), (0, 2, 3, 1, 4)).reshape(2, b, c + n, SSM_WIDTH)
    d_skip = p['ssm_d'][j]
    y_lat = d_skip * u_lat + y[0, :, c:] + y[1, :, c:][:, ::-1]
    wg, bg = p['ssm_w_glu'][j], p['ssm_b_glu'][j]

    def glu(t):
        t = jax.nn.gelu(t)
        return t * jax.nn.sigmoid(tokens_matmul(t, wg) + bg)

    if not need_ctx:
        return glu(y_lat), None
    y_ctx = d_skip * u_ctx + y[0, :, :c] + y[1, :, :c][:, ::-1]
    return glu(y_lat), glu(y_ctx)


def even_mixer(a_lat, a_ctx, p, j, need_ctx):
    b, n, _ = a_lat.shape
    c = a_ctx.shape[1]
    cos, sin = rope_tables(n, HEAD_DIM)
    proj = tokens_matmul(jnp.concatenate([a_ctx, a_lat], axis=1), p['e_w_in'][j])
    q, k, v, u = jnp.split(proj, [GQA_Q_W, GQA_Q_W + GQA_KV_W, GQA_Q_W + 2 * GQA_KV_W], axis=-1)
    q = rms_norm(q.reshape(b, c + n, GQA_Q_HEADS, HEAD_DIM), p['e_g_q'][j])
    k = rms_norm(k.reshape(b, c + n, GQA_KV_HEADS, HEAD_DIM), p['e_g_k'][j])
    v = v.reshape(b, c + n, GQA_KV_HEADS, HEAD_DIM)
    q_l = rope(q[:, c:], cos, sin)
    k = jnp.concatenate([k[:, :c], rope(k[:, c:], cos, sin)], axis=1)
    scale = HEAD_DIM ** -0.5
    kh, vh = heads_first(k), heads_first(v)
    att_l = heads_first(attention(heads_first(q_l), kh, vh, scale)).reshape(b, n, GQA_Q_W)
    ssm_l, ssm_c = s5_mixer(u[:, c:], u[:, :c], p, j, need_ctx)
    mix_l = jnp.concatenate([att_l, ssm_l], axis=-1)
    if not need_ctx:
        return tokens_matmul(mix_l, p['e_w_out'][j]), None
    att_c = heads_first(attention(heads_first(q[:, :c]), kh[:, :, :c], vh[:, :, :c], scale)).reshape(b, c, GQA_Q_W)
    mix = jnp.concatenate([jnp.concatenate([att_c, ssm_c], axis=-1), mix_l], axis=1)
    out = tokens_matmul(mix, p['e_w_out'][j])
    return out[:, c:], out[:, :c]


def odd_mixer(a_lat, a_ctx, p, j, need_ctx):
    b, n, _ = a_lat.shape
    c = a_ctx.shape[1]
    t = c + n
    cos, sin = rope_tables(n, MLA_ROPE)
    proj = tokens_matmul(jnp.concatenate([a_ctx, a_lat], axis=1), p['o_w_in'][j])
    c1, c2, c3 = MLA_Q_RANK, MLA_Q_RANK + MLA_KV_RANK, MLA_Q_RANK + MLA_KV_RANK + MLA_ROPE
    cq, ckv, kr = proj[..., :c1], proj[..., c1:c2], proj[..., c2:c3]
    nq, nk, nv = jnp.split(proj[..., ODD_NA_AT:], 3, axis=-1)
    q = tokens_matmul(rms_norm(cq, p['mla_g_cq'][j]), p['mla_w_uq'][j]).reshape(b, t, MLA_HEADS, MLA_QK)
    kv = tokens_matmul(rms_norm(ckv, p['mla_g_ckv'][j]), p['mla_w_ukv'][j]).reshape(b, t, MLA_HEADS, MLA_NOPE + MLA_V)
    k = jnp.concatenate([kv[..., :MLA_NOPE], jnp.broadcast_to(kr[:, :, None, :], (b, t, MLA_HEADS, MLA_ROPE))], axis=-1)
    q, k, mv = rms_norm(q, p['mla_g_q'][j]), rms_norm(k, p['mla_g_k'][j]), kv[..., MLA_NOPE:]

    def rope_tail(x):
        tail = jnp.concatenate([x[:, :c, :, MLA_NOPE:], rope(x[:, c:, :, MLA_NOPE:], cos, sin)], axis=1)
        return jnp.concatenate([x[..., :MLA_NOPE], tail], axis=-1)

    q, k = rope_tail(q), rope_tail(k)
    qh, kh, vh = heads_first(q), heads_first(k), heads_first(mv)
    mla_scale = MLA_QK ** -0.5
    mla_l = heads_first(attention(qh[:, :, c:], kh, vh, mla_scale)).reshape(b, n, MLA_HEADS * MLA_V)
    nq = heads_first(rms_norm(nq.reshape(b, t, NA_HEADS, HEAD_DIM), p['na_g_q'][j]))
    nk = heads_first(rms_norm(nk.reshape(b, t, NA_HEADS, HEAD_DIM), p['na_g_k'][j]))
    nv = heads_first(nv.reshape(b, t, NA_HEADS, HEAD_DIM))
    na_scale = HEAD_DIM ** -0.5
    na_l = na_attention(nq[:, :, c:], nk[:, :, c:], nv[:, :, c:], nk[:, :, :c], nv[:, :, :c], na_bias_table(p['na_rpb'][j]),
                        na_scale)
    na_l = heads_first(na_l).reshape(b, n, NA_W)
    mix_l = jnp.concatenate([mla_l, na_l], axis=-1)
    if not need_ctx:
        return tokens_matmul(mix_l, p['o_w_out'][j]), None
    mla_c = heads_first(attention(qh[:, :, :c], kh[:, :, :c], vh[:, :, :c], mla_scale)).reshape(b, c, MLA_HEADS * MLA_V)
    na_c = heads_first(attention(nq[:, :, :c], nk[:, :, :c], nv[:, :, :c], na_scale)).reshape(b, c, NA_W)
    mix = jnp.concatenate([jnp.concatenate([mla_c, na_c], axis=-1), mix_l], axis=1)
    out = tokens_matmul(mix, p['o_w_out'][j])
    return out[:, c:], out[:, :c]


def mlp(h, w1, w2):
    b, n, k = h.shape
    return mlp_rows(h.reshape(b * n, k), w1, w2).reshape(b, n, w2.shape[1])


def local_loss(x, p, m_lat, m_ctx, ctx, target):
    depth = m_lat.shape[0]
    c = ctx.shape[1]
    xc = ctx
    for i in range(depth):
        need_ctx = i < depth - 1
        j = i // 2
        ml = [m_lat[i, :, s][:, None, :] for s in range(N_MOD)]
        mc = [m_ctx[i, s][None, None, :] for s in range(N_MOD)]
        a_lat = modulate(x, p['g_norm1'][i], ml[0], ml[1])
        a_ctx = modulate(xc, p['g_norm1'][i], mc[0], mc[1])
        mixer = even_mixer if i % 2 == 0 else odd_mixer
        o_lat, o_ctx = mixer(a_lat, a_ctx, p, j, need_ctx)
        x = x + ml[2] * o_lat
        h_lat = modulate(x, p['g_norm2'][i], ml[3], ml[4])
        if need_ctx:
            xc = xc + mc[2] * o_ctx
            h_ctx = modulate(xc, p['g_norm2'][i], mc[3], mc[4])
            ff = mlp(jnp.concatenate([h_ctx, h_lat], axis=1), p['w_ff1'][i], p['w_ff2'][i])
            x = x + ml[5] * ff[:, c:]
            xc = xc + mc[5] * ff[:, :c]
        else:
            x = x + ml[5] * mlp(h_lat, p['w_ff1'][i], p['w_ff2'][i])
    return 0.5 * jnp.sum(jnp.mean(jnp.square(x - target), axis=-1))


def _packed_rows(size, layout):
    width, group = layout
    return -(-size // (width * group)) * group


def _pack_rows(flat, layout):
    width = layout[0]
    rows = _packed_rows(flat.shape[-1], layout)
    flat = jnp.pad(flat, [(0, 0)] * (flat.ndim - 1) + [(0, rows * width - flat.shape[-1])])
    return flat.reshape(flat.shape[:-1] + (rows, width))


def _unpack_rows(rows, shape):
    lead = rows.shape[:-2]
    return rows.reshape(lead + (-1,))[..., :math.prod(shape)].reshape(lead + tuple(shape))


def _unpack_all(packed, shapes, layout):
    out, at = [], 0
    for shape in shapes:
        rows = _packed_rows(math.prod(shape), layout)
        out.append(_unpack_rows(packed[..., at:at + rows, :], shape))
        at += rows
    return out


def _join_shards(g, axis):
    g = jnp.moveaxis(g, 0, axis)
    return g.reshape(g.shape[:axis] + (N_DEV * g.shape[axis + 1],) + g.shape[axis + 2:])


def _split_shards(full, axis):
    s = full.shape
    return jnp.moveaxis(full.reshape(s[:axis] + (N_DEV, s[axis] // N_DEV) + s[axis + 1:]), axis, 0)


def _gather_packed(parts, dtype, layout, name):
    packed = jnp.concatenate([_pack_rows(a.astype(dtype).reshape(-1), layout) for a in parts], axis=0)
    return _unpack_all(_exchange_many([packed], [None], True, name)[0], [a.shape for a in parts], layout)


def kernel(x, c, ctx, c_ctx, w_mod, b_mod, g_norm1, g_norm2, w_ff1, w_ff2, e_w_in, e_w_out, e_g_q, e_g_k, ssm_lam_re, ssm_lam_im, ssm_log_dt, ssm_b_re, ssm_b_im, ssm_c_re, ssm_c_im, ssm_d, ssm_w_glu, ssm_b_glu, o_w_in, o_w_out, mla_g_cq, mla_g_ckv, mla_w_uq, mla_w_ukv, mla_g_q, mla_g_k, na_g_q, na_g_k, na_rpb, loss_target, m_c_ctx, m_w_mod, m_b_mod, m_g_norm1, m_g_norm2, m_w_ff1, m_w_ff2, m_e_w_in, m_e_w_out, m_e_g_q, m_e_g_k, m_ssm_lam_re, m_ssm_lam_im, m_ssm_log_dt, m_ssm_b_re, m_ssm_b_im, m_ssm_c_re, m_ssm_c_im, m_ssm_d, m_ssm_w_glu, m_ssm_b_glu, m_o_w_in, m_o_w_out, m_mla_g_cq, m_mla_g_ckv, m_mla_w_uq, m_mla_w_ukv, m_mla_g_q, m_mla_g_k, m_na_g_q, m_na_g_k, m_na_rpb, v_c_ctx, v_w_mod, v_b_mod, v_g_norm1, v_g_norm2, v_w_ff1, v_w_ff2, v_e_w_in, v_e_w_out, v_e_g_q, v_e_g_k, v_ssm_lam_re, v_ssm_lam_im, v_ssm_log_dt, v_ssm_b_re, v_ssm_b_im, v_ssm_c_re, v_ssm_c_im, v_ssm_d, v_ssm_w_glu, v_ssm_b_glu, v_o_w_in, v_o_w_out, v_mla_g_cq, v_mla_g_ckv, v_mla_w_uq, v_mla_w_ukv, v_mla_g_q, v_mla_g_k, v_na_g_q, v_na_g_k, v_na_rpb):
    given = dict(locals())
    x, c, ctx, target = given['x'], given['c'], given['ctx'], given['loss_target']
    b_loc, _, d_model = x.shape
    depth = given['w_mod'].shape[0]
    ix, iy, ic = lax.axis_index('x'), lax.axis_index('y'), lax.axis_index('c')
    me = 4 * ix + 2 * iy + ic
    n_batch = N_DEV * b_loc
    mod_w = given['w_mod'].shape[2]

    c_rows = jnp.concatenate([c, jnp.zeros((8 - b_loc, d_model), F32)], axis=0)
    small = _gather_packed([c_rows] + [given[n] for n in SHARDED_SMALL], F32, PACK_SMALL,'gather_small')
    c_all = small[0][:, :b_loc].reshape(n_batch, d_model)
    full = {n: _join_shards(g, SHARDED_SMALL[n]) for n, g in zip(SHARDED_SMALL, small[1:])}
    cuts = {n: (a if given[n].shape[a] % (16 if a == 1 else LANES) == 0 else None) for n, a in BIG.items()}
    big = _exchange_many([given[n].astype(BF16) for n in BIG], [cuts[n] for n in BIG], True, 'gather_weights')
    for n, g in zip(BIG, big):
        g = g if cuts[n] is not None else _join_shards(g, BIG[n])
        full[n] = [g[i] for i in range(g.shape[0])]
    c3 = MLA_Q_RANK + MLA_KV_RANK + MLA_ROPE
    full['o_w_in'] = [jnp.concatenate([w[:, :c3], jnp.zeros((w.shape[0], ODD_NA_AT - c3), BF16), w[:, c3:]], axis=-1)
                      for w in full['o_w_in']]
    for n in REPLICATED:
        full[n] = given[n]

    rows17 = 16 * (-(-(n_batch + 1) // 16))
    cond = jnp.concatenate([jax.nn.silu(c_all), jax.nn.silu(given['c_ctx'])[None],
                            jnp.zeros((rows17 - n_batch - 1, d_model), F32)], axis=0)
    mod_mine = jnp.stack([_matmul(cond, given['w_mod'][i], 'nn', F32) for i in range(depth)])
    b_mine = lax.dynamic_slice_in_dim(given['b_mod'], me * mod_w, mod_w, axis=1)
    mod_mine = mod_mine + b_mine[:, None, :]
    mod_all = _gather_packed([mod_mine], F32, PACK_SMALL,'gather_mod')[0]
    mod_all = jnp.moveaxis(mod_all, 0, 2).reshape(depth, rows17, N_MOD, d_model)
    m_lat = lax.dynamic_slice_in_dim(mod_all, me * b_loc, b_loc, axis=1)
    m_ctx = mod_all[:, n_batch]

    diff = {n: full[n] for n in list(BIG) + list(SHARDED_SMALL) + REPLICATED}
    loss, (g_x, g_p, g_ml, g_mc) = jax.value_and_grad(local_loss, argnums=(0, 1, 2, 3))(x, diff, m_lat, m_ctx, ctx, target)
    loss = lax.psum(loss, ('x', 'y', 'c'))
    g_p['o_w_in'] = [jnp.concatenate([g[:, :c3], g[:, ODD_NA_AT:]], axis=-1) for g in g_p['o_w_in']]

    g_rows = jnp.concatenate([g_ml.reshape(depth, b_loc, N_MOD * d_model), g_mc.reshape(depth, 1, N_MOD * d_model),
                              jnp.zeros((depth, 8 - b_loc - 1, N_MOD * d_model), F32)], axis=1)
    g_mod_all = _gather_packed([g_rows], F32, PACK_SMALL,'gather_mod_grads')[0]
    g_lat_all = jnp.moveaxis(g_mod_all[:, :, :b_loc], 0, 1).reshape(depth, n_batch, N_MOD * d_model)
    g_ctx_all = g_mod_all[0, :, b_loc]
    for dev in range(1, N_DEV):
        g_ctx_all = g_ctx_all + g_mod_all[dev, :, b_loc]
    g_mod17 = jnp.concatenate([g_lat_all, g_ctx_all[:, None], jnp.zeros((depth, rows17 - n_batch - 1, N_MOD * d_model), F32)],
                              axis=1)
    grad_b_mod = jnp.sum(g_mod17, axis=1)
    g_mod_mine = lax.dynamic_slice_in_dim(g_mod17, me * mod_w, mod_w, axis=2)
    grad_w_mod = jnp.stack([_matmul(cond, g_mod_mine[i], 'tn', F32) for i in range(depth)])
    d_cond = _matmul(g_mod_mine[0], given['w_mod'][0], 'nt', F32)
    for i in range(1, depth):
        d_cond = d_cond + _matmul(g_mod_mine[i], given['w_mod'][i], 'nt', F32)
    d_cond_ctx = d_cond[n_batch]

    small_names = REPLICATED + list(SHARDED_SMALL)
    parts = [d_cond_ctx] + [g_p[n] for n in small_names]
    packed = jnp.concatenate([_pack_rows(a.reshape(-1), PACK_SMALL) for a in parts], axis=0)
    summed = _unpack_all(_all_reduce_small(packed), [a.shape for a in parts], PACK_SMALL)
    grads = dict(zip(['c_ctx'] + small_names, summed))
    c_ctx = given['c_ctx']
    sig = jax.nn.sigmoid(c_ctx)
    grads['c_ctx'] = grads['c_ctx'] * (sig * (1 + c_ctx * (1 - sig)))
    for n, axis in SHARDED_SMALL.items():
        width = given[n].shape[axis]
        grads[n] = lax.dynamic_slice_in_dim(grads[n], me * width, width, axis=axis)
    grads['w_mod'], grads['b_mod'] = grad_w_mod, grad_b_mod

    stacked = [jnp.stack(g_p[n]) for n in BIG]
    stacked = [g if cuts[n] is not None else _split_shards(g, BIG[n]) for n, g in zip(BIG, stacked)]
    landed = _exchange_many(stacked, [cuts[n] for n in BIG], False, 'scatter_weight_grads')

    upd = {}
    for n, slots in zip(BIG, landed):
        grads[n], *upd[n] = _adamw_landed(slots, given[n], given['m_' + n], given['v_' + n])
    upd['w_mod'] = _adamw_big(given['w_mod'], grads['w_mod'], given['m_w_mod'], given['v_w_mod'])
    rest = [n for n in WEIGHTS if n not in upd]
    out = _adamw_small([given[n] for n in rest], [grads[n] for n in rest], [given['m_' + n] for n in rest],
                       [given['v_' + n] for n in rest])
    upd.update(dict(zip(rest, out)))
    return (loss, g_x, *[grads[n] for n in WEIGHTS], *[upd[n][0] for n in WEIGHTS], *[upd[n][1] for n in WEIGHTS],
            *[upd[n][2] for n in WEIGHTS])
```

```python
import functools
import math

import jax
import jax.numpy as jnp
from jax import lax
from jax.experimental import pallas as pl
from jax.experimental.pallas import tpu as pltpu

F32, BF16 = jnp.float32, jnp.bfloat16
MESH = pl.DeviceIdType.MESH
N_DEV = 8
VMEM_LIMIT_BYTES = 56 * 1024 * 1024
MM_TILE_BYTES = 6 * 1024 * 1024
LANES = 128
PACK_SMALL = (128, 8)

GRID_W = 64
HEAD_DIM = 64
ROPE_BASE = 10000.0
EPS = 1e-6
N_MOD = 6
GQA_Q_HEADS, GQA_KV_HEADS = 12, 4
GQA_Q_W, GQA_KV_W = GQA_Q_HEADS * HEAD_DIM, GQA_KV_HEADS * HEAD_DIM
SSM_WIDTH, SSM_GROUP, SSM_GROUPS, SSM_STATE = 256, 16, 16, 64
MLA_HEADS, MLA_Q_RANK, MLA_KV_RANK, MLA_NOPE, MLA_ROPE, MLA_V = 8, 512, 256, 64, 32, 64
MLA_QK = MLA_NOPE + MLA_ROPE
NA_HEADS, NA_WIN_R, NA_WIN_C = 8, 8, 16
NA_W = NA_HEADS * HEAD_DIM
ODD_IN_W = MLA_Q_RANK + MLA_KV_RANK + MLA_ROPE + 3 * NA_W
ODD_NA_AT = 1024
ODD_IN_PAD = ODD_NA_AT + 3 * NA_W
NEG = -1e30

ADAM_LR, ADAM_B1, ADAM_B2, ADAM_EPS, ADAM_WD, ADAM_STEP = 0.001, 0.9, 0.999, 1e-08, 0.01, 10

FWD_PARAMS = ['x', 'c', 'ctx', 'c_ctx', 'w_mod', 'b_mod', 'g_norm1', 'g_norm2', 'w_ff1', 'w_ff2', 'e_w_in', 'e_w_out',
              'e_g_q', 'e_g_k', 'ssm_lam_re', 'ssm_lam_im', 'ssm_log_dt', 'ssm_b_re', 'ssm_b_im', 'ssm_c_re', 'ssm_c_im',
              'ssm_d', 'ssm_w_glu', 'ssm_b_glu', 'o_w_in', 'o_w_out', 'mla_g_cq', 'mla_g_ckv', 'mla_w_uq', 'mla_w_ukv',
              'mla_g_q', 'mla_g_k', 'na_g_q', 'na_g_k', 'na_rpb']
WEIGHTS = FWD_PARAMS[3:]
BIG = {'w_ff1': 2, 'w_ff2': 1, 'e_w_in': 2, 'e_w_out': 1, 'o_w_in': 2, 'o_w_out': 1, 'mla_w_uq': 2, 'mla_w_ukv': 2,
       'ssm_w_glu': 1}
SHARDED_SMALL = {'mla_g_cq': 1, 'mla_g_ckv': 1}
REPLICATED = [n for n in WEIGHTS if n not in BIG and n not in SHARDED_SMALL and n not in ('w_mod', 'c_ctx', 'b_mod')]


def _tile(dim, prefs):
    for p in prefs:
        if dim >= p and dim % p == 0:
            return p
    return dim


def _params(*sem):
    return pltpu.CompilerParams(dimension_semantics=sem, vmem_limit_bytes=VMEM_LIMIT_BYTES)


def _dot_nt(a, b):
    return lax.dot_general(a, b, (((1,), (1,)), ((), ())), preferred_element_type=F32)


def _dot_tn(a, b):
    return lax.dot_general(a, b, (((0,), (0,)), ((), ())), preferred_element_type=F32)


def _dot(a, b):
    return jnp.dot(a, b, preferred_element_type=F32)


def _matmul(a, b, kind, out_dtype, finish=None, extra=None, n_out=1):
    if kind == 'nn':
        (m, kd), n = a.shape, b.shape[1]
    elif kind == 'nt':
        (m, kd), n = a.shape, b.shape[0]
    else:
        (kd, m), n = a.shape, b.shape[1]
    if kind == 'tn':
        tm = m if m <= 1024 else _tile(m, (1024, 768, 512, 256, 128))
        tn = _tile(n, (1024, 768, 512, 256, 128))
    else:
        tn = n if kd * n * 2 <= MM_TILE_BYTES else _tile(n, (1024, 768, 512, 256, 128))
        tm = _tile(m, [t for t in (1536, 1024, 768, 512, 256, 128) if t * tn * 4 <= MM_TILE_BYTES])
    whole = kind != 'tn' and tn == n and kd * n * 2 <= MM_TILE_BYTES
    tk = kd if whole else _tile(kd, [t for t in (2048, 1536, 1024, 512, 256, 128) if t * max(tm, tn) * 4 <= MM_TILE_BYTES])
    nk = kd // tk
    dn = {'nn': (((1,), (0,)), ((), ())), 'nt': (((1,), (1,)), ((), ())), 'tn': (((0,), (0,)), ((), ()))}[kind]

    n_in = 2 if extra is None else 3

    def body(*refs):
        a_ref, b_ref = refs[:2]
        o_refs = refs[n_in:n_in + n_out]

        def store(total):
            outs = (total,) if finish is None else finish(total, refs[2][...] if extra is not None else None)
            for o_ref, val in zip(o_refs, outs):
                o_ref[...] = val.astype(o_ref.dtype)

        part = lax.dot_general(a_ref[...].astype(BF16), b_ref[...].astype(BF16), dn, preferred_element_type=F32)
        if nk == 1:
            store(part)
            return
        acc_ref, k = refs[n_in + n_out], pl.program_id(2)

        @pl.when(k == 0)
        def _():
            acc_ref[...] = part

        @pl.when((k > 0) & (k < nk - 1))
        def _():
            acc_ref[...] += part

        @pl.when(k == nk - 1)
        def _():
            store(acc_ref[...] + part)

    a_spec = pl.BlockSpec((tk, tm), lambda i, j, k: (k, i)) if kind == 'tn' else pl.BlockSpec((tm, tk), lambda i, j, k: (i, k))
    b_spec = pl.BlockSpec((tn, tk), lambda i, j, k: (j, k)) if kind == 'nt' else pl.BlockSpec((tk, tn), lambda i, j, k: (k, j))
    o_spec = pl.BlockSpec((tm, tn), lambda i, j, k: (i, j))
    out = pl.pallas_call(
        body, name='mm_' + kind, grid=(m // tm, n // tn, nk),
        out_shape=[jax.ShapeDtypeStruct((m, n), out_dtype)] * n_out,
        in_specs=[a_spec, b_spec] + ([o_spec] if extra is not None else []), out_specs=[o_spec] * n_out,
        scratch_shapes=[pltpu.VMEM((tm, tn), F32)] if nk > 1 else [],
        compiler_params=_params('parallel', 'parallel', 'arbitrary'),
    )(*((a, b) if extra is None else (a, b, extra)))
    return out[0] if n_out == 1 else out


@jax.custom_vjp
def linear(a, w):
    return _matmul(a, w, 'nn', F32)


def _linear_fwd(a, w):
    ab = a.astype(BF16)
    return _matmul(ab, w, 'nn', F32), (ab, w)


def _linear_bwd(res, g):
    ab, w = res
    gb = g.astype(BF16)
    return _matmul(gb, w, 'nt', F32), _matmul(ab, gb, 'tn', w.dtype)


linear.defvjp(_linear_fwd, _linear_bwd)


def _relu2(z, _):
    r = jnp.maximum(z, 0.0)
    return r, r * r


def _relu2_grad(d_act, r):
    return (d_act * (2.0 * r.astype(F32)),)


@jax.custom_vjp
def mlp_rows(h, w1, w2):
    return _mlp_rows_fwd(h, w1, w2)[0]


def _mlp_rows_fwd(h, w1, w2):
    hb = h.astype(BF16)
    r, act = _matmul(hb, w1, 'nn', BF16, finish=_relu2, n_out=2)
    return _matmul(act, w2, 'nn', F32), (hb, w1, w2, r, act)


def _mlp_rows_bwd(res, g):
    hb, w1, w2, r, act = res
    gb = g.astype(BF16)
    dz = _matmul(gb, w2, 'nt', BF16, finish=_relu2_grad, extra=r)
    return _matmul(dz, w1, 'nt', F32), _matmul(hb, dz, 'tn', w1.dtype), _matmul(act, gb, 'tn', w2.dtype)


mlp_rows.defvjp(_mlp_rows_fwd, _mlp_rows_bwd)


LOG2E = math.log2(math.e)


def _softmax_rows(t):
    m = jnp.max(t, axis=-1, keepdims=True)
    e = jnp.exp2(t - m)
    return e * (1.0 / jnp.sum(e, axis=-1, keepdims=True))


ATTN_SPLIT = 2


def _attn_specs(q, kt, vt, bq):
    _, h, nq, dq = q.shape
    _, hk, dv, nk = vt.shape
    g = h // hk
    q_spec = pl.BlockSpec((None, None, bq, dq), lambda b, j, gi, i: (b, j * g + gi, i, 0))
    k_spec = pl.BlockSpec((None, None, nk, dq), lambda b, j, gi, i: (b, j, 0, 0))
    v_spec = pl.BlockSpec((None, None, nk, dv), lambda b, j, gi, i: (b, j, 0, 0))
    o_spec = pl.BlockSpec((None, None, bq, dv), lambda b, j, gi, i: (b, j * g + gi, i, 0))
    t = dict(q=pl.BlockSpec((None, None, dq, bq), lambda b, j, gi, i: (b, j * g + gi, 0, i)),
             o=pl.BlockSpec((None, None, dv, bq), lambda b, j, gi, i: (b, j * g + gi, 0, i)),
             k=pl.BlockSpec((None, None, dq, nk), lambda b, j, gi, i: (b, j, 0, 0)),
             v=pl.BlockSpec((None, None, dv, nk), lambda b, j, gi, i: (b, j, 0, 0)))
    return (q.shape[0], hk, g, nq // bq), q_spec, k_spec, v_spec, o_spec, t


def _attn_blocks(nq):
    bq = _tile(nq, (512, 256, 128))
    return bq, [pl.ds(s * (bq // ATTN_SPLIT), bq // ATTN_SPLIT) for s in range(ATTN_SPLIT)]


def _attn_fwd_call(q, kt, vt, scale):
    b, h, nq, _ = q.shape
    dv = vt.shape[2]
    bq, subs = _attn_blocks(nq)
    grid, q_spec, _, _, _, t = _attn_specs(q, kt, vt, bq)

    def body(q_ref, kt_ref, vt_ref, ot_ref):
        ktb, vtb = kt_ref[...], vt_ref[...]
        for rows in subs:
            p = _softmax_rows(_dot(q_ref[rows, :], ktb) * (scale * LOG2E))
            ot_ref[:, rows] = _dot_nt(vtb, p.astype(BF16))

    return pl.pallas_call(
        body, name='attn_fwd', grid=grid, out_shape=jax.ShapeDtypeStruct((b, h, dv, nq), F32),
        in_specs=[q_spec, t['k'], t['v']], out_specs=t['o'],
        compiler_params=_params('parallel', 'parallel', 'arbitrary', 'arbitrary'),
    )(q, kt, vt)


def _attn_bwd_call(q, kt, vt, do, scale):
    b, h, nq, dq = q.shape
    _, hk, dv_width, nk = vt.shape
    bq, subs = _attn_blocks(nq)
    grid, q_spec, k_spec, v_spec, o_spec, t = _attn_specs(q, kt, vt, bq)

    def body(q_ref, kt_ref, vt_ref, do_ref, dqt_ref, dk_ref, dv_ref):
        @pl.when((pl.program_id(2) == 0) & (pl.program_id(3) == 0))
        def _():
            dk_ref[...] = jnp.zeros_like(dk_ref)
            dv_ref[...] = jnp.zeros_like(dv_ref)

        ktb, vtb = kt_ref[...], vt_ref[...]
        dk, dv = [], []
        for rows in subs:
            qb, dob = q_ref[rows, :], do_ref[rows, :]
            p = _softmax_rows(_dot(qb, ktb) * (scale * LOG2E))
            dp = _dot(dob, vtb)
            ds = p * (dp - jnp.sum(p * dp, axis=-1, keepdims=True))
            dsb = (ds * scale).astype(BF16)
            dqt_ref[:, rows] = _dot_nt(ktb, dsb)
            dk.append(_dot_tn(dsb, qb))
            dv.append(_dot_tn(p.astype(BF16), dob))
        dk_ref[...] += sum(dk[1:], dk[0])
        dv_ref[...] += sum(dv[1:], dv[0])

    return pl.pallas_call(
        body, name='attn_bwd', grid=grid,
        out_shape=(jax.ShapeDtypeStruct((b, h, dq, nq), F32), jax.ShapeDtypeStruct((b, hk, nk, dq), F32),
                   jax.ShapeDtypeStruct((b, hk, nk, dv_width), F32)),
        in_specs=[q_spec, t['k'], t['v'], o_spec], out_specs=(t['q'], k_spec, v_spec),
        compiler_params=_params('parallel', 'parallel', 'arbitrary', 'arbitrary'),
    )(q, kt, vt, do)


@functools.partial(jax.custom_vjp, nondiff_argnums=(3,))
def attention(q, k, v, scale):
    return _attention_fwd(q, k, v, scale)[0]


def _attention_fwd(q, k, v, scale):
    qb, kt, vt = q.astype(BF16), jnp.swapaxes(k.astype(BF16), 2, 3), jnp.swapaxes(v.astype(BF16), 2, 3)
    return jnp.swapaxes(_attn_fwd_call(qb, kt, vt, scale), 2, 3), (qb, kt, vt)


def _attention_bwd(scale, res, g):
    dqt, dk, dv = _attn_bwd_call(*res, g.astype(BF16), scale)
    return jnp.swapaxes(dqt, 2, 3), dk, dv


attention.defvjp(_attention_fwd, _attention_bwd)


def _na_window(r, rows):
    start = jnp.clip(r - NA_WIN_R // 2, 0, rows - NA_WIN_R)
    return start, r - start


NA_TOGETHER = 16


def _na_group(r0, rows):
    out = []
    for g in range(NA_TOGETHER):
        start, off = _na_window(r0 + g, rows)
        out.append((pl.ds(pl.multiple_of(start * GRID_W, GRID_W), NA_WIN_R * GRID_W), off))
    return out


def _na_rows(x, g):
    return x[g * GRID_W:(g + 1) * GRID_W]


def _na_scores(qs, kws, kc, biases, scale):
    n = len(kws)
    s1 = jnp.stack([_dot_nt(_na_rows(qs, g), kws[g]) * scale + biases[g] for g in range(n)])
    s2 = (_dot_nt(qs, kc) * scale).reshape(n, GRID_W, kc.shape[0])
    m = jnp.maximum(jnp.max(s1, axis=-1, keepdims=True), jnp.max(s2, axis=-1, keepdims=True))
    e1, e2 = jnp.exp(s1 - m), jnp.exp(s2 - m)
    inv = 1.0 / (jnp.sum(e1, axis=-1, keepdims=True) + jnp.sum(e2, axis=-1, keepdims=True))
    return e1 * inv, e2 * inv


def _na_specs(q, kc):
    _, _, n, d = q.shape
    c = kc.shape[2]
    win = NA_WIN_R * GRID_W
    tok = pl.BlockSpec((None, None, n, d), lambda b, h: (b, h, 0, 0))
    ctx = pl.BlockSpec((None, None, c, d), lambda b, h: (b, h, 0, 0))
    bias = pl.BlockSpec((None, NA_WIN_R, GRID_W, win), lambda b, h: (h, 0, 0, 0))
    dbias = pl.BlockSpec((None, None, NA_WIN_R, GRID_W, win), lambda b, h: (b, h, 0, 0, 0))
    return tok, ctx, bias, dbias


def _na_fwd_call(q, k, v, kc, vc, bias, scale):
    b, h, n, d = q.shape
    rows, span = n // GRID_W, NA_TOGETHER * GRID_W
    tok, ctx, bias_spec, _ = _na_specs(q, kc)

    def body(q_ref, k_ref, v_ref, kc_ref, vc_ref, b_ref, o_ref):
        def step(i, carry):
            at = pl.ds(pl.multiple_of(i * span, span), span)
            wins = _na_group(i * NA_TOGETHER, rows)
            p1, p2 = _na_scores(q_ref[at, :], [k_ref[w, :] for w, _ in wins], kc_ref[...], [b_ref[off] for _, off in wins],
                                scale)
            p1, p2 = p1.astype(BF16), p2.astype(BF16)
            local = jnp.concatenate([_dot(p1[g], v_ref[w, :]) for g, (w, _) in enumerate(wins)], axis=0)
            o_ref[at, :] = local + _dot(p2.reshape(span, p2.shape[2]), vc_ref[...])
            return carry

        lax.fori_loop(0, rows // NA_TOGETHER, step, 0)

    return pl.pallas_call(
        body, name='na_fwd', grid=(b, h), out_shape=jax.ShapeDtypeStruct(q.shape, F32),
        in_specs=[tok, tok, tok, ctx, ctx, bias_spec], out_specs=tok,
        compiler_params=_params('parallel', 'parallel'),
    )(q, k, v, kc, vc, bias)


def _na_bwd_call(q, k, v, kc, vc, bias, do, scale):
    b, h, n, d = q.shape
    rows, span = n // GRID_W, NA_TOGETHER * GRID_W
    tok, ctx, bias_spec, dbias_spec = _na_specs(q, kc)

    def body(q_ref, k_ref, v_ref, kc_ref, vc_ref, b_ref, do_ref, dq_ref, dk_ref, dv_ref, dkc_ref, dvc_ref, db_ref):
        for ref in (dk_ref, dv_ref, dkc_ref, dvc_ref, db_ref):
            ref[...] = jnp.zeros_like(ref)

        def step(i, carry):
            at = pl.ds(pl.multiple_of(i * span, span), span)
            wins = _na_group(i * NA_TOGETHER, rows)
            qs, dob, kcb, vcb = q_ref[at, :], do_ref[at, :], kc_ref[...], vc_ref[...]
            kws, vws = [k_ref[w, :] for w, _ in wins], [v_ref[w, :] for w, _ in wins]
            p1, p2 = _na_scores(qs, kws, kcb, [b_ref[off] for _, off in wins], scale)
            dp1 = jnp.stack([_dot_nt(_na_rows(dob, g), vws[g]) for g in range(NA_TOGETHER)])
            dp2 = _dot_nt(dob, vcb).reshape(p2.shape)
            delta = jnp.sum(p1 * dp1, axis=-1, keepdims=True) + jnp.sum(p2 * dp2, axis=-1, keepdims=True)
            ds1, ds2 = p1 * (dp1 - delta), p2 * (dp2 - delta)
            ds1b, p1b = (ds1 * scale).astype(BF16), p1.astype(BF16)
            ds2b = (ds2 * scale).astype(BF16).reshape(span, p2.shape[2])
            p2b = p2.astype(BF16).reshape(span, p2.shape[2])
            dq_ref[at, :] = jnp.concatenate([_dot(ds1b[g], kws[g]) for g in range(NA_TOGETHER)], axis=0) + _dot(ds2b, kcb)
            for g, (w, off) in enumerate(wins):
                db_ref[off] += ds1[g]
                dk_ref[w, :] += _dot_tn(ds1b[g], _na_rows(qs, g))
                dv_ref[w, :] += _dot_tn(p1b[g], _na_rows(dob, g))
            dkc_ref[...] += _dot_tn(ds2b, qs)
            dvc_ref[...] += _dot_tn(p2b, dob)
            return carry

        lax.fori_loop(0, rows // NA_TOGETHER, step, 0)

    f = lambda a: jax.ShapeDtypeStruct(a.shape, F32)
    return pl.pallas_call(
        body, name='na_bwd', grid=(b, h),
        out_shape=(f(q), f(k), f(v), f(kc), f(vc), jax.ShapeDtypeStruct((b,) + bias.shape, F32)),
        in_specs=[tok, tok, tok, ctx, ctx, bias_spec, tok], out_specs=(tok, tok, tok, ctx, ctx, dbias_spec),
        compiler_params=_params('parallel', 'parallel'),
    )(q, k, v, kc, vc, bias, do)


@functools.partial(jax.custom_vjp, nondiff_argnums=(6,))
def na_attention(q, k, v, kc, vc, bias, scale):
    return _na_fwd_call(q.astype(BF16), k.astype(BF16), v.astype(BF16), kc.astype(BF16), vc.astype(BF16), bias, scale)


def _na_attention_fwd(q, k, v, kc, vc, bias, scale):
    res = (q.astype(BF16), k.astype(BF16), v.astype(BF16), kc.astype(BF16), vc.astype(BF16), bias)
    return _na_fwd_call(*res, scale), res


def _na_attention_bwd(scale, res, g):
    dq, dk, dv, dkc, dvc, db = _na_bwd_call(*res, g.astype(BF16), scale)
    return dq, dk, dv, dkc, dvc, jnp.sum(db, axis=0)


na_attention.defvjp(_na_attention_fwd, _na_attention_bwd)


def _na_table_index():
    qcol = jnp.arange(GRID_W)
    kcol = jnp.arange(GRID_W)
    cstart = jnp.clip(qcol - NA_WIN_C // 2, 0, GRID_W - NA_WIN_C)
    inside = (kcol[None, :] >= cstart[:, None]) & (kcol[None, :] < cstart[:, None] + NA_WIN_C)
    cidx = jnp.clip(kcol[None, :] - qcol[:, None] + (NA_WIN_C - 1), 0, 2 * NA_WIN_C - 2)
    ridx = jnp.arange(NA_WIN_R)[None, :] - jnp.arange(NA_WIN_R)[:, None] + (NA_WIN_R - 1)
    return inside, cidx, ridx


@jax.custom_vjp
def na_bias_table(rpb):
    inside, pick_c, pick_r = _na_table_picks()
    rows = jnp.einsum('hab,oja->hojb', rpb, pick_r, precision=lax.Precision.HIGHEST)
    t = jnp.einsum('hojb,qkb->hoqjk', rows, pick_c, precision=lax.Precision.HIGHEST)
    t = jnp.where(inside[None, None, :, None, :], t, NEG)
    return t.reshape(rpb.shape[0], NA_WIN_R, GRID_W, NA_WIN_R * GRID_W)


def _na_table_picks():
    inside, cidx, ridx = _na_table_index()
    pick_c = ((cidx[..., None] == jnp.arange(2 * NA_WIN_C - 1)) & inside[..., None]).astype(F32)
    pick_r = (ridx[..., None] == jnp.arange(2 * NA_WIN_R - 1)).astype(F32)
    return inside, pick_c, pick_r


def _na_bias_table_bwd(_, dt):
    _, pick_c, pick_r = _na_table_picks()
    d5 = dt.reshape(dt.shape[0], NA_WIN_R, GRID_W, NA_WIN_R, GRID_W)
    part = jnp.einsum('hoqjk,qkb->hojb', d5, pick_c, precision=lax.Precision.HIGHEST)
    return (jnp.einsum('hojb,oja->hab', part, pick_r, precision=lax.Precision.HIGHEST),)


na_bias_table.defvjp(lambda rpb: (na_bias_table(rpb), None), _na_bias_table_bwd)


S5_HALF = SSM_WIDTH // 2
S5_LANES = (SSM_GROUPS // 2) * SSM_STATE
S5_Q = S5_LANES // LANES


def _s5_tiles(a):
    r = a.shape[0]
    return jnp.transpose(a.reshape(r, S5_Q, LANES), (1, 0, 2)).reshape(S5_Q * r, LANES)


def _s5_untiles(a):
    r = a.shape[0] // S5_Q
    return jnp.transpose(a.reshape(S5_Q, r, LANES), (1, 0, 2)).reshape(r, S5_LANES)


def _s5_put(ref, r, rr, tc, val):
    for q in range(S5_Q):
        ref[pl.ds((q * rr + r) * tc, tc), :] = val[:, q * LANES:(q + 1) * LANES]


def _s5_get(ref, r, rr, tc):
    return jnp.concatenate([ref[pl.ds((q * rr + r) * tc, tc), :] for q in range(S5_Q)], axis=1)


S5_BLOCK = 8
S5_TOGETHER = 8


def _s5_powers(a_re, a_im, backward):
    a_im = -a_im if backward else a_im
    pr, pi = [a_re], [a_im]
    for _ in range(S5_BLOCK - 1):
        pr, pi = pr + [pr[-1] * a_re - pi[-1] * a_im], pi + [pr[-1] * a_im + pi[-1] * a_re]
    order = range(S5_BLOCK - 1, -1, -1) if backward else range(S5_BLOCK)

    def table(p):
        rows = [jnp.broadcast_to(p[s - 1][:, None, :], (p[0].shape[0], S5_BLOCK, LANES)) for s in (1, 2, 4)]
        return jnp.stack(rows + [jnp.stack([p[t] for t in order], axis=1)], axis=1)

    return table(pr), table(pi)


def _s5_scan_block(xr, xi, pr_ref, pi_ref, chain, carry, backward):
    row = lax.broadcasted_iota(jnp.int32, (S5_BLOCK, LANES), 0)
    for e, s in enumerate((1, 2, 4)):
        ar, ai = pr_ref[chain, e], pi_ref[chain, e]
        keep = (row < S5_BLOCK - s) if backward else (row >= s)
        shift = S5_BLOCK - s if backward else s
        sr = jnp.where(keep, pltpu.roll(xr, shift, 0), 0.0)
        si = jnp.where(keep, pltpu.roll(xi, shift, 0), 0.0)
        xr, xi = xr + (ar * sr - ai * si), xi + (ar * si + ai * sr)
    ar, ai = pr_ref[chain, 3], pi_ref[chain, 3]
    cr, ci = carry
    xr, xi = xr + (ar * cr - ai * ci), xi + (ar * ci + ai * cr)
    edge = slice(0, 1) if backward else slice(S5_BLOCK - 1, S5_BLOCK)
    return xr, xi, (xr[edge], xi[edge])


def _s5_scan_chunk(xr_ref, xi_ref, pr_ref, pi_ref, sr_ref, si_ref, chains, tc, backward):
    blocks = tc // S5_BLOCK
    for first in range(0, chains, S5_TOGETHER):
        group = range(first, min(first + S5_TOGETHER, chains))

        def block(k, carries, group=group):
            j = blocks - 1 - k if backward else k
            out = []
            for chain, carry in zip(group, carries):
                at = pl.ds(pl.multiple_of(chain * tc + j * S5_BLOCK, S5_BLOCK), S5_BLOCK)
                xr, xi, carry = _s5_scan_block(xr_ref[at, :], xi_ref[at, :], pr_ref, pi_ref, chain, carry, backward)
                xr_ref[at, :] = xr
                xi_ref[at, :] = xi
                out.append(carry)
            return tuple(out)

        start = tuple((sr_ref[pl.ds(chain, 1), :], si_ref[pl.ds(chain, 1), :]) for chain in group)
        for chain, (cr, ci) in zip(group, lax.fori_loop(0, blocks, block, start)):
            sr_ref[pl.ds(chain, 1), :] = cr
            si_ref[pl.ds(chain, 1), :] = ci


def _s5_fwd_call(u, a_re, a_im, b_re, b_im, c_re, c_im):
    rr, t_len, _ = u.shape
    sets = b_re.shape[0]
    per = rr // sets
    tc = _tile(t_len, (256, 128))
    nt = t_len // tc
    qr = S5_Q * rr

    def body(u_ref, ar_ref, ai_ref, br_ref, bi_ref, cr_ref, ci_ref, y_ref, hr_ref, hi_ref, sr_ref, si_ref):
        @pl.when(pl.program_id(0) == 0)
        def _():
            sr_ref[...] = jnp.zeros_like(sr_ref)
            si_ref[...] = jnp.zeros_like(si_ref)

        for r in range(rr):
            ub = u_ref[r]
            _s5_put(hr_ref, r, rr, tc, _dot(ub, br_ref[r // per]))
            _s5_put(hi_ref, r, rr, tc, _dot(ub, bi_ref[r // per]))
        _s5_scan_chunk(hr_ref, hi_ref, ar_ref, ai_ref, sr_ref, si_ref, qr, tc, False)
        for r in range(rr):
            y_ref[r] = (_dot(_s5_get(hr_ref, r, rr, tc).astype(BF16), cr_ref[r // per])
                        - _dot(_s5_get(hi_ref, r, rr, tc).astype(BF16), ci_ref[r // per]))

    full = lambda a: pl.BlockSpec(a.shape, lambda i: (0,) * a.ndim)
    h_spec = pl.BlockSpec((None, qr * tc, LANES), lambda i: (i, 0, 0))
    h_shape = jax.ShapeDtypeStruct((nt, qr * tc, LANES), F32)
    a_re, a_im = _s5_powers(_s5_tiles(a_re), _s5_tiles(a_im), False)
    return pl.pallas_call(
        body, name='s5_fwd', grid=(nt,),
        out_shape=(jax.ShapeDtypeStruct((rr, t_len, S5_HALF), F32), h_shape, h_shape),
        in_specs=[pl.BlockSpec((rr, tc, S5_HALF), lambda i: (0, i, 0)), full(a_re), full(a_im), full(b_re), full(b_im),
                  full(c_re), full(c_im)],
        out_specs=(pl.BlockSpec((rr, tc, S5_HALF), lambda i: (0, i, 0)), h_spec, h_spec),
        scratch_shapes=[pltpu.VMEM((qr, LANES), F32), pltpu.VMEM((qr, LANES), F32)],
        compiler_params=_params('arbitrary'),
    )(u, a_re, a_im, b_re, b_im, c_re, c_im)


def _s5_bwd_call(u, a_re, a_im, b_re, b_im, c_re, c_im, h_re, h_im, dy):
    rr, t_len, _ = u.shape
    sets = b_re.shape[0]
    per = rr // sets
    nt, rows, _ = h_re.shape
    qr = S5_Q * rr
    tc = rows // qr

    def body(u_ref, dy_ref, ar_ref, ai_ref, br_ref, bi_ref, cr_ref, ci_ref, hr_ref, hi_ref,
             du_ref, dar_ref, dai_ref, dbr_ref, dbi_ref, dcr_ref, dci_ref, gr_ref, gi_ref, sr_ref, si_ref):
        i = pl.program_id(0)

        @pl.when(i == 0)
        def _():
            for ref in (dar_ref, dai_ref, dbr_ref, dbi_ref, dcr_ref, dci_ref, sr_ref, si_ref):
                ref[...] = jnp.zeros_like(ref)

        for r in range(rr):
            dyb = dy_ref[r]
            _s5_put(gr_ref, r, rr, tc, _dot_nt(dyb, cr_ref[r // per]))
            _s5_put(gi_ref, r, rr, tc, -_dot_nt(dyb, ci_ref[r // per]))
        g_r, g_i = sr_ref[...], si_ref[...]
        last = pl.ds(tc - 1, qr, stride=tc)
        dar_ref[...] += g_r * hr_ref[last, :] + g_i * hi_ref[last, :]
        dai_ref[...] += g_i * hr_ref[last, :] - g_r * hi_ref[last, :]
        _s5_scan_chunk(gr_ref, gi_ref, ar_ref, ai_ref, sr_ref, si_ref, qr, tc, True)
        row = lax.broadcasted_iota(jnp.int32, (tc, LANES), 0)
        for chain in range(qr):
            at, one = pl.ds(chain * tc, tc), pl.ds(chain, 1)
            p_r = jnp.where(row >= 1, pltpu.roll(hr_ref[at, :], 1, 0), 0.0)
            p_i = jnp.where(row >= 1, pltpu.roll(hi_ref[at, :], 1, 0), 0.0)
            g_r, g_i = gr_ref[at, :], gi_ref[at, :]
            dar_ref[one, :] += jnp.sum(g_r * p_r + g_i * p_i, axis=0, keepdims=True)
            dai_ref[one, :] += jnp.sum(g_i * p_r - g_r * p_i, axis=0, keepdims=True)
        for r in range(rr):
            s = r // per
            ub, dyb = u_ref[r], dy_ref[r]
            grb, gib = _s5_get(gr_ref, r, rr, tc).astype(BF16), _s5_get(gi_ref, r, rr, tc).astype(BF16)
            du_ref[r] = _dot_nt(grb, br_ref[s]) + _dot_nt(gib, bi_ref[s])
            dbr_ref[s] += _dot_tn(ub, grb)
            dbi_ref[s] += _dot_tn(ub, gib)
            dcr_ref[s] += _dot_tn(_s5_get(hr_ref, r, rr, tc).astype(BF16), dyb)
            dci_ref[s] -= _dot_tn(_s5_get(hi_ref, r, rr, tc).astype(BF16), dyb)

    full = lambda a: pl.BlockSpec(a.shape, lambda i: (0,) * a.ndim)
    back = lambda i: nt - 1 - i
    tok = pl.BlockSpec((rr, tc, S5_HALF), lambda i: (0, back(i), 0))
    h_spec = pl.BlockSpec((None, qr * tc, LANES), lambda i: (back(i), 0, 0))
    f = lambda a: jax.ShapeDtypeStruct(a.shape, F32)
    a_re, a_im = _s5_powers(_s5_tiles(a_re), _s5_tiles(a_im), True)
    da = jax.ShapeDtypeStruct((qr, LANES), F32)
    du, da_re, da_im, db_re, db_im, dc_re, dc_im = pl.pallas_call(
        body, name='s5_bwd', grid=(nt,),
        out_shape=(jax.ShapeDtypeStruct(u.shape, F32), da, da, f(b_re), f(b_im), f(c_re), f(c_im)),
        in_specs=[tok, tok, full(a_re), full(a_im), full(b_re), full(b_im), full(c_re), full(c_im), h_spec, h_spec],
        out_specs=(tok, full(da), full(da), full(b_re), full(b_im), full(c_re), full(c_im)),
        scratch_shapes=[pltpu.VMEM((qr * tc, LANES), F32), pltpu.VMEM((qr * tc, LANES), F32),
                        pltpu.VMEM((qr, LANES), F32), pltpu.VMEM((qr, LANES), F32)],
        compiler_params=_params('arbitrary'),
    )(u, dy, a_re, a_im, b_re, b_im, c_re, c_im, h_re, h_im)
    return du, _s5_untiles(da_re), _s5_untiles(da_im), db_re, db_im, dc_re, dc_im


@jax.custom_vjp
def s5_core(u, a_re, a_im, b_re, b_im, c_re, c_im):
    return _s5_fwd_call(u.astype(BF16), a_re, a_im, b_re.astype(BF16), b_im.astype(BF16), c_re.astype(BF16),
                        c_im.astype(BF16))[0]


def _s5_core_fwd(u, a_re, a_im, b_re, b_im, c_re, c_im):
    args = (u.astype(BF16), a_re, a_im, b_re.astype(BF16), b_im.astype(BF16), c_re.astype(BF16), c_im.astype(BF16))
    y, h_re, h_im = _s5_fwd_call(*args)
    return y, args + (h_re, h_im)


def _s5_core_bwd(res, g):
    return _s5_bwd_call(*res, g.astype(BF16))


s5_core.defvjp(_s5_core_fwd, _s5_core_bwd)


def _shard_view(ref, axis, index, width):
    return ref.at[(slice(None),) * axis + (pl.ds(pl.multiple_of(index * width, width), width),)]


def _exchange_many(xs, cuts, gather, name):
    n = len(xs)
    if gather:
        shards = [x.shape for x in xs]
    else:
        shards = [x.shape[1:] if cut is None else x.shape[:cut] + (x.shape[cut] // N_DEV,) + x.shape[cut + 1:]
                  for x, cut in zip(xs, cuts)]

    def full_shape(shard, cut):
        return shard[:cut] + (N_DEV * shard[cut],) + shard[cut + 1:]

    out_shapes = [jax.ShapeDtypeStruct((N_DEV,) + tuple(s) if (cut is None or not gather) else full_shape(tuple(s), cut), x.dtype)
                  for x, s, cut in zip(xs, shards, cuts)]

    def body(*refs):
        x_refs, out_refs = refs[:n], refs[n:2 * n]
        send_sems, recv_sems, local_sems = refs[2 * n:]
        ix, iy, ic = lax.axis_index('x'), lax.axis_index('y'), lax.axis_index('c')
        me = 4 * ix + 2 * iy + ic

        def flipped(k):
            px = 1 - ix if k & 4 else ix
            py = 1 - iy if k & 2 else iy
            pc = 1 - ic if k & 1 else ic
            return (px, py, pc), 4 * px + 2 * py + pc

        def block(ref, cut, shard, who):
            return ref.at[who] if cut is None else _shard_view(ref, cut, who, shard[cut])

        def ends(i, sender, receiver):
            if gather:
                return x_refs[i], block(out_refs[i], cuts[i], shards[i], sender)
            return block(x_refs[i], cuts[i], shards[i], receiver), out_refs[i].at[sender]

        def copy(i, k, sender, receiver):
            src, dst = ends(i, sender, receiver)
            return pltpu.make_async_remote_copy(src_ref=src, dst_ref=dst, send_sem=send_sems.at[i * (N_DEV - 1) + k - 1],
                                                 recv_sem=recv_sems.at[i * (N_DEV - 1) + k - 1], device_id=flipped(k)[0],
                                                 device_id_type=MESH)

        own = [pltpu.make_async_copy(*ends(i, me, me), local_sems.at[i]) for i in range(n)]
        for cp in own:
            cp.start()
        if gather:
            chips = (2, 4, 6)
            sent = [copy(i, k, me, flipped(k)[1]) for k in (1,) + chips for i in range(n)]
            for cp in sent:
                cp.start()
            for k in chips:
                for i in range(n):
                    copy(i, k, flipped(k)[1], me).wait_recv()
                    src, dst = ends(i, flipped(k)[1], me)
                    sent.append(pltpu.make_async_remote_copy(
                        src_ref=dst, dst_ref=dst, send_sem=send_sems.at[i * (N_DEV - 1) + k], recv_sem=recv_sems.at[i * (N_DEV - 1) + k],
                        device_id=flipped(1)[0], device_id_type=MESH))
                    sent[-1].start()
            for k in (1, 3, 5, 7):
                for i in range(n):
                    src, dst = ends(i, flipped(k)[1], me)
                    pltpu.make_async_remote_copy(
                        src_ref=dst, dst_ref=dst, send_sem=send_sems.at[i * (N_DEV - 1) + k - 1], recv_sem=recv_sems.at[i * (N_DEV - 1) + k - 1],
                        device_id=flipped(1)[0], device_id_type=MESH).wait_recv()
            for cp in sent:
                cp.wait_send()
            for cp in own:
                cp.wait()
            return
        sent = [copy(i, k, me, flipped(k)[1]) for k in range(1, N_DEV) for i in range(n)]
        for cp in sent:
            cp.start()
        for k in range(1, N_DEV):
            for i in range(n):
                copy(i, k, flipped(k)[1], me).wait_recv()
        for cp in sent:
            cp.wait_send()
        for cp in own:
            cp.wait()

    hbm = pl.BlockSpec(memory_space=pltpu.HBM)
    pairs = n * (N_DEV - 1)
    return pl.pallas_call(
        body, name=name, out_shape=out_shapes, in_specs=[hbm] * n, out_specs=[hbm] * n,
        scratch_shapes=[pltpu.SemaphoreType.DMA((pairs,)), pltpu.SemaphoreType.DMA((pairs,)), pltpu.SemaphoreType.DMA((n,))],
    )(*xs)


def _adamw_landed(landed, w, m, v):
    shape = w.shape
    slots = landed.shape[0]
    w, m, v = (_as_rows(a) for a in (w, m, v))
    rows, cols = w.shape
    landed = landed.reshape(slots, rows, cols)
    tr = _tile(rows, (256, 128, 64, 32, 16))

    def body(l_ref, w_ref, m_ref, v_ref, g_ref, d_ref, nm_ref, nv_ref):
        g = l_ref[0].astype(F32)
        for d in range(1, slots):
            g = g + l_ref[d].astype(F32)
        g_ref[...] = g
        d_ref[...], nm_ref[...], nv_ref[...] = _adamw_math(w_ref[...], g, m_ref[...], v_ref[...])

    spec = pl.BlockSpec((tr, cols), lambda i: (i, 0))
    out = pl.pallas_call(
        body, name='adamw_landed', grid=(rows // tr,), out_shape=(jax.ShapeDtypeStruct(w.shape, F32),) * 4,
        in_specs=[pl.BlockSpec((slots, tr, cols), lambda i: (0, i, 0))] + [spec] * 3, out_specs=(spec,) * 4,
        compiler_params=_params('parallel'),
    )(landed, w, m, v)
    return tuple(o.reshape(shape) for o in out)


def _all_reduce_small(x):
    g = _exchange_many([x], [None], True, 'gather_small_grads')[0]

    def body(g_ref, o_ref):
        acc = g_ref[0]
        for d in range(1, N_DEV):
            acc = acc + g_ref[d]
        o_ref[...] = acc

    return pl.pallas_call(body, name='sum_small', out_shape=jax.ShapeDtypeStruct(x.shape, F32))(g)


def _adamw_math(w, g, m, v):
    m = ADAM_B1 * m + (1.0 - ADAM_B1) * g
    v = ADAM_B2 * v + (1.0 - ADAM_B2) * (g * g)
    m_hat = m / (1.0 - ADAM_B1 ** ADAM_STEP)
    v_hat = v / (1.0 - ADAM_B2 ** ADAM_STEP)
    return -ADAM_LR * (m_hat / (jnp.sqrt(v_hat) + ADAM_EPS) + ADAM_WD * w), m, v


def _as_rows(a):
    return a.reshape(1, -1) if a.ndim < 2 else a.reshape(-1, a.shape[-1])


def _as_lanes(a):
    return a.reshape(-1, LANES) if a.size % LANES == 0 else a.reshape(1, -1)


def _adamw_big(w, g, m, v):
    shape = w.shape
    w, g, m, v = (_as_rows(a) for a in (w, g, m, v))
    rows, cols = w.shape
    tr = _tile(rows, (512, 256, 128, 64, 32, 16, 8))

    def body(w_ref, g_ref, m_ref, v_ref, d_ref, nm_ref, nv_ref):
        d_ref[...], nm_ref[...], nv_ref[...] = _adamw_math(w_ref[...], g_ref[...], m_ref[...], v_ref[...])

    spec = pl.BlockSpec((tr, cols), lambda i: (i, 0))
    out = pl.pallas_call(
        body, name='adamw', grid=(rows // tr,), out_shape=(jax.ShapeDtypeStruct(w.shape, F32),) * 3,
        in_specs=[spec] * 4, out_specs=(spec,) * 3, compiler_params=_params('parallel'),
    )(w, g, m, v)
    return tuple(o.reshape(shape) for o in out)


def _adamw_small(ws, gs, ms, vs):
    n = len(ws)
    shapes = [w.shape for w in ws]
    flat = [_as_lanes(a) for group in (ws, gs, ms, vs) for a in group]

    def body(*refs):
        ins, outs = refs[:4 * n], refs[4 * n:]
        for i in range(n):
            d, m, v = _adamw_math(ins[i][...], ins[n + i][...], ins[2 * n + i][...], ins[3 * n + i][...])
            outs[i][...], outs[n + i][...], outs[2 * n + i][...] = d, m, v

    out = pl.pallas_call(
        body, name='adamw_small', out_shape=tuple(jax.ShapeDtypeStruct(flat[i].shape, F32) for _ in range(3) for i in range(n)),
    )(*flat)
    return [tuple(out[j * n + i].reshape(shapes[i]) for j in range(3)) for i in range(n)]


def rms_norm(x, g):
    return x * lax.rsqrt(jnp.mean(jnp.square(x), axis=-1, keepdims=True) + EPS) * g


def modulate(x, g, shift, scale):
    return rms_norm(x, g) * (1 + scale) + shift


def rope_tables(n_tokens, rot_dim):
    t = jnp.arange(n_tokens)
    rows = (t // GRID_W).astype(F32)
    cols = (t % GRID_W).astype(F32)
    axis_dim = rot_dim // 2
    freqs = ROPE_BASE ** (-jnp.arange(0, axis_dim, 2, dtype=F32) / axis_dim)
    ang_r, ang_c = rows[:, None] * freqs, cols[:, None] * freqs
    ang = jnp.concatenate([ang_r, ang_r, ang_c, ang_c], axis=-1)
    return jnp.cos(ang), jnp.sin(ang)


def rope(x, cos, sin):
    x1, x2, x3, x4 = jnp.split(x, 4, axis=-1)
    rot = jnp.concatenate([-x2, x1, -x4, x3], axis=-1)
    return x * cos[:, None, :] + rot * sin[:, None, :]


def heads_first(t):
    return jnp.swapaxes(t, 1, 2)


def tokens_matmul(t, w):
    b, n, k = t.shape
    return linear(t.reshape(b * n, k), w).reshape(b, n, w.shape[1])


def s5_discretize(lam_re, lam_im, log_dt, b_re, b_im):
    dt = jnp.exp(log_dt)[:, None]
    mag = jnp.exp(lam_re * dt)
    a_re = mag * jnp.cos(lam_im * dt)
    a_im = mag * jnp.sin(lam_im * dt)
    den = jnp.square(lam_re) + jnp.square(lam_im)
    f_re = ((a_re - 1.0) * lam_re + a_im * lam_im) / den
    f_im = (a_im * lam_re - (a_re - 1.0) * lam_im) / den
    bb_re = f_re[..., None] * b_re - f_im[..., None] * b_im
    bb_im = f_re[..., None] * b_im + f_im[..., None] * b_re
    return a_re, a_im, bb_re, bb_im


def s5_mixer(u_lat, u_ctx, p, j, need_ctx):
    b, n, _ = u_lat.shape
    c = u_ctx.shape[1]
    half_groups = SSM_GROUPS // 2
    eye = jnp.eye(half_groups, dtype=F32)
    a_res, a_ims, b_res, b_ims, c_res, c_ims, seqs = [], [], [], [], [], [], []
    for d in range(2):
        a_re, a_im, bb_re, bb_im = s5_discretize(p['ssm_lam_re'][j, d], p['ssm_lam_im'][j, d], p['ssm_log_dt'][j, d],
                                                 p['ssm_b_re'][j, d], p['ssm_b_im'][j, d])
        for half in range(2):
            grp = slice(half * half_groups, (half + 1) * half_groups)
            a_res.append(a_re[grp].reshape(S5_LANES))
            a_ims.append(a_im[grp].reshape(S5_LANES))
            b_res.append(jnp.einsum('gsp,gh->gphs', bb_re[grp], eye).reshape(S5_HALF, S5_LANES))
            b_ims.append(jnp.einsum('gsp,gh->gphs', bb_im[grp], eye).reshape(S5_HALF, S5_LANES))
            c_res.append(jnp.einsum('gps,gh->gshp', p['ssm_c_re'][j, d][grp], eye).reshape(S5_LANES, S5_HALF))
            c_ims.append(jnp.einsum('gps,gh->gshp', p['ssm_c_im'][j, d][grp], eye).reshape(S5_LANES, S5_HALF))
        flip = (lambda t: t[:, ::-1]) if d == 1 else (lambda t: t)
        seq = jnp.concatenate([flip(u_ctx), flip(u_lat)], axis=1)
        seqs.append(jnp.transpose(seq.reshape(b, c + n, 2, S5_HALF), (2, 0, 1, 3)))
    u = jnp.stack(seqs).reshape(4 * b, c + n, S5_HALF)
    rep = lambda parts: jnp.repeat(jnp.stack(parts), b, axis=0)
    y = s5_core(u, rep(a_res), rep(a_ims), jnp.stack(b_res), jnp.stack(b_ims), jnp.stack(c_res), jnp.stack(c_ims))
    y = jnp.transpose(y.reshape(2, 2, b, c + n, S5_HALF), (0, 2, 3, 1, 4)).reshape(2, b, c + n, SSM_WIDTH)
    d_skip = p['ssm_d'][j]
    y_lat = d_skip * u_lat + y[0, :, c:] + y[1, :, c:][:, ::-1]
    wg, bg = p['ssm_w_glu'][j], p['ssm_b_glu'][j]

    def glu(t):
        t = jax.nn.gelu(t)
        return t * jax.nn.sigmoid(tokens_matmul(t, wg) + bg)

    if not need_ctx:
        return glu(y_lat), None
    y_ctx = d_skip * u_ctx + y[0, :, :c] + y[1, :, :c][:, ::-1]
    return glu(y_lat), glu(y_ctx)


def even_mixer(a_lat, a_ctx, p, j, need_ctx):
    b, n, _ = a_lat.shape
    c = a_ctx.shape[1]
    cos, sin = rope_tables(n, HEAD_DIM)
    proj = tokens_matmul(jnp.concatenate([a_ctx, a_lat], axis=1), p['e_w_in'][j])
    q, k, v, u = jnp.split(proj, [GQA_Q_W, GQA_Q_W + GQA_KV_W, GQA_Q_W + 2 * GQA_KV_W], axis=-1)
    q = rms_norm(q.reshape(b, c + n, GQA_Q_HEADS, HEAD_DIM), p['e_g_q'][j])
    k = rms_norm(k.reshape(b, c + n, GQA_KV_HEADS, HEAD_DIM), p['e_g_k'][j])
    v = v.reshape(b, c + n, GQA_KV_HEADS, HEAD_DIM)
    q_l = rope(q[:, c:], cos, sin)
    k = jnp.concatenate([k[:, :c], rope(k[:, c:], cos, sin)], axis=1)
    scale = HEAD_DIM ** -0.5
    kh, vh = heads_first(k), heads_first(v)
    att_l = heads_first(attention(heads_first(q_l), kh, vh, scale)).reshape(b, n, GQA_Q_W)
    ssm_l, ssm_c = s5_mixer(u[:, c:], u[:, :c], p, j, need_ctx)
    mix_l = jnp.concatenate([att_l, ssm_l], axis=-1)
    if not need_ctx:
        return tokens_matmul(mix_l, p['e_w_out'][j]), None
    att_c = heads_first(attention(heads_first(q[:, :c]), kh[:, :, :c], vh[:, :, :c], scale)).reshape(b, c, GQA_Q_W)
    mix = jnp.concatenate([jnp.concatenate([att_c, ssm_c], axis=-1), mix_l], axis=1)
    out = tokens_matmul(mix, p['e_w_out'][j])
    return out[:, c:], out[:, :c]


def odd_mixer(a_lat, a_ctx, p, j, need_ctx):
    b, n, _ = a_lat.shape
    c = a_ctx.shape[1]
    t = c + n
    cos, sin = rope_tables(n, MLA_ROPE)
    proj = tokens_matmul(jnp.concatenate([a_ctx, a_lat], axis=1), p['o_w_in'][j])
    c1, c2, c3 = MLA_Q_RANK, MLA_Q_RANK + MLA_KV_RANK, MLA_Q_RANK + MLA_KV_RANK + MLA_ROPE
    cq, ckv, kr = proj[..., :c1], proj[..., c1:c2], proj[..., c2:c3]
    nq, nk, nv = jnp.split(proj[..., ODD_NA_AT:], 3, axis=-1)
    q = tokens_matmul(rms_norm(cq, p['mla_g_cq'][j]), p['mla_w_uq'][j]).reshape(b, t, MLA_HEADS, MLA_QK)
    kv = tokens_matmul(rms_norm(ckv, p['mla_g_ckv'][j]), p['mla_w_ukv'][j]).reshape(b, t, MLA_HEADS, MLA_NOPE + MLA_V)
    k = jnp.concatenate([kv[..., :MLA_NOPE], jnp.broadcast_to(kr[:, :, None, :], (b, t, MLA_HEADS, MLA_ROPE))], axis=-1)
    q, k, mv = rms_norm(q, p['mla_g_q'][j]), rms_norm(k, p['mla_g_k'][j]), kv[..., MLA_NOPE:]

    def rope_tail(x):
        tail = jnp.concatenate([x[:, :c, :, MLA_NOPE:], rope(x[:, c:, :, MLA_NOPE:], cos, sin)], axis=1)
        return jnp.concatenate([x[..., :MLA_NOPE], tail], axis=-1)

    q, k = rope_tail(q), rope_tail(k)
    qh, kh, vh = heads_first(q), heads_first(k), heads_first(mv)
    mla_scale = MLA_QK ** -0.5
    mla_l = heads_first(attention(qh[:, :, c:], kh, vh, mla_scale)).reshape(b, n, MLA_HEADS * MLA_V)
    nq = heads_first(rms_norm(nq.reshape(b, t, NA_HEADS, HEAD_DIM), p['na_g_q'][j]))
    nk = heads_first(rms_norm(nk.reshape(b, t, NA_HEADS, HEAD_DIM), p['na_g_k'][j]))
    nv = heads_first(nv.reshape(b, t, NA_HEADS, HEAD_DIM))
    na_scale = HEAD_DIM ** -0.5
    na_l = na_attention(nq[:, :, c:], nk[:, :, c:], nv[:, :, c:], nk[:, :, :c], nv[:, :, :c], na_bias_table(p['na_rpb'][j]),
                        na_scale)
    na_l = heads_first(na_l).reshape(b, n, NA_W)
    mix_l = jnp.concatenate([mla_l, na_l], axis=-1)
    if not need_ctx:
        return tokens_matmul(mix_l, p['o_w_out'][j]), None
    mla_c = heads_first(attention(qh[:, :, :c], kh[:, :, :c], vh[:, :, :c], mla_scale)).reshape(b, c, MLA_HEADS * MLA_V)
    na_c = heads_first(attention(nq[:, :, :c], nk[:, :, :c], nv[:, :, :c], na_scale)).reshape(b, c, NA_W)
    mix = jnp.concatenate([jnp.concatenate([mla_c, na_c], axis=-1), mix_l], axis=1)
    out = tokens_matmul(mix, p['o_w_out'][j])
    return out[:, c:], out[:, :c]


def mlp(h, w1, w2):
    b, n, k = h.shape
    return mlp_rows(h.reshape(b * n, k), w1, w2).reshape(b, n, w2.shape[1])


def local_loss(x, p, m_lat, m_ctx, ctx, target):
    depth = m_lat.shape[0]
    c = ctx.shape[1]
    xc = ctx
    for i in range(depth):
        need_ctx = i < depth - 1
        j = i // 2
        ml = [m_lat[i, :, s][:, None, :] for s in range(N_MOD)]
        mc = [m_ctx[i, s][None, None, :] for s in range(N_MOD)]
        a_lat = modulate(x, p['g_norm1'][i], ml[0], ml[1])
        a_ctx = modulate(xc, p['g_norm1'][i], mc[0], mc[1])
        mixer = even_mixer if i % 2 == 0 else odd_mixer
        o_lat, o_ctx = mixer(a_lat, a_ctx, p, j, need_ctx)
        x = x + ml[2] * o_lat
        h_lat = modulate(x, p['g_norm2'][i], ml[3], ml[4])
        if need_ctx:
            xc = xc + mc[2] * o_ctx
            h_ctx = modulate(xc, p['g_norm2'][i], mc[3], mc[4])
            ff = mlp(jnp.concatenate([h_ctx, h_lat], axis=1), p['w_ff1'][i], p['w_ff2'][i])
            x = x + ml[5] * ff[:, c:]
            xc = xc + mc[5] * ff[:, :c]
        else:
            x = x + ml[5] * mlp(h_lat, p['w_ff1'][i], p['w_ff2'][i])
    return 0.5 * jnp.sum(jnp.mean(jnp.square(x - target), axis=-1))


def _packed_rows(size, layout):
    width, group = layout
    return -(-size // (width * group)) * group


def _pack_rows(flat, layout):
    width = layout[0]
    rows = _packed_rows(flat.shape[-1], layout)
    flat = jnp.pad(flat, [(0, 0)] * (flat.ndim - 1) + [(0, rows * width - flat.shape[-1])])
    return flat.reshape(flat.shape[:-1] + (rows, width))


def _unpack_rows(rows, shape):
    lead = rows.shape[:-2]
    return rows.reshape(lead + (-1,))[..., :math.prod(shape)].reshape(lead + tuple(shape))


def _unpack_all(packed, shapes, layout):
    out, at = [], 0
    for shape in shapes:
        rows = _packed_rows(math.prod(shape), layout)
        out.append(_unpack_rows(packed[..., at:at + rows, :], shape))
        at += rows
    return out


def _join_shards(g, axis):
    g = jnp.moveaxis(g, 0, axis)
    return g.reshape(g.shape[:axis] + (N_DEV * g.shape[axis + 1],) + g.shape[axis + 2:])


def _split_shards(full, axis):
    s = full.shape
    return jnp.moveaxis(full.reshape(s[:axis] + (N_DEV, s[axis] // N_DEV) + s[axis + 1:]), axis, 0)


def _gather_packed(parts, dtype, layout, name):
    packed = jnp.concatenate([_pack_rows(a.astype(dtype).reshape(-1), layout) for a in parts], axis=0)
    return _unpack_all(_exchange_many([packed], [None], True, name)[0], [a.shape for a in parts], layout)


def kernel(x, c, ctx, c_ctx, w_mod, b_mod, g_norm1, g_norm2, w_ff1, w_ff2, e_w_in, e_w_out, e_g_q, e_g_k, ssm_lam_re, ssm_lam_im, ssm_log_dt, ssm_b_re, ssm_b_im, ssm_c_re, ssm_c_im, ssm_d, ssm_w_glu, ssm_b_glu, o_w_in, o_w_out, mla_g_cq, mla_g_ckv, mla_w_uq, mla_w_ukv, mla_g_q, mla_g_k, na_g_q, na_g_k, na_rpb, loss_target, m_c_ctx, m_w_mod, m_b_mod, m_g_norm1, m_g_norm2, m_w_ff1, m_w_ff2, m_e_w_in, m_e_w_out, m_e_g_q, m_e_g_k, m_ssm_lam_re, m_ssm_lam_im, m_ssm_log_dt, m_ssm_b_re, m_ssm_b_im, m_ssm_c_re, m_ssm_c_im, m_ssm_d, m_ssm_w_glu, m_ssm_b_glu, m_o_w_in, m_o_w_out, m_mla_g_cq, m_mla_g_ckv, m_mla_w_uq, m_mla_w_ukv, m_mla_g_q, m_mla_g_k, m_na_g_q, m_na_g_k, m_na_rpb, v_c_ctx, v_w_mod, v_b_mod, v_g_norm1, v_g_norm2, v_w_ff1, v_w_ff2, v_e_w_in, v_e_w_out, v_e_g_q, v_e_g_k, v_ssm_lam_re, v_ssm_lam_im, v_ssm_log_dt, v_ssm_b_re, v_ssm_b_im, v_ssm_c_re, v_ssm_c_im, v_ssm_d, v_ssm_w_glu, v_ssm_b_glu, v_o_w_in, v_o_w_out, v_mla_g_cq, v_mla_g_ckv, v_mla_w_uq, v_mla_w_ukv, v_mla_g_q, v_mla_g_k, v_na_g_q, v_na_g_k, v_na_rpb):
    given = dict(locals())
    x, c, ctx, target = given['x'], given['c'], given['ctx'], given['loss_target']
    b_loc, _, d_model = x.shape
    depth = given['w_mod'].shape[0]
    ix, iy, ic = lax.axis_index('x'), lax.axis_index('y'), lax.axis_index('c')
    me = 4 * ix + 2 * iy + ic
    n_batch = N_DEV * b_loc
    mod_w = given['w_mod'].shape[2]

    c_rows = jnp.concatenate([c, jnp.zeros((8 - b_loc, d_model), F32)], axis=0)
    small = _gather_packed([c_rows] + [given[n] for n in SHARDED_SMALL], F32, PACK_SMALL,'gather_small')
    c_all = small[0][:, :b_loc].reshape(n_batch, d_model)
    full = {n: _join_shards(g, SHARDED_SMALL[n]) for n, g in zip(SHARDED_SMALL, small[1:])}
    cuts = {n: (a if given[n].shape[a] % (16 if a == 1 else LANES) == 0 else None) for n, a in BIG.items()}
    big = _exchange_many([given[n].astype(BF16) for n in BIG], [cuts[n] for n in BIG], True, 'gather_weights')
    for n, g in zip(BIG, big):
        g = g if cuts[n] is not None else _join_shards(g, BIG[n])
        full[n] = [g[i] for i in range(g.shape[0])]
    c3 = MLA_Q_RANK + MLA_KV_RANK + MLA_ROPE
    full['o_w_in'] = [jnp.concatenate([w[:, :c3], jnp.zeros((w.shape[0], ODD_NA_AT - c3), BF16), w[:, c3:]], axis=-1)
                      for w in full['o_w_in']]
    for n in REPLICATED:
        full[n] = given[n]

    rows17 = 16 * (-(-(n_batch + 1) // 16))
    cond = jnp.concatenate([jax.nn.silu(c_all), jax.nn.silu(given['c_ctx'])[None],
                            jnp.zeros((rows17 - n_batch - 1, d_model), F32)], axis=0)
    mod_mine = jnp.stack([_matmul(cond, given['w_mod'][i], 'nn', F32) for i in range(depth)])
    b_mine = lax.dynamic_slice_in_dim(given['b_mod'], me * mod_w, mod_w, axis=1)
    mod_mine = mod_mine + b_mine[:, None, :]
    mod_all = _gather_packed([mod_mine], F32, PACK_SMALL,'gather_mod')[0]
    mod_all = jnp.moveaxis(mod_all, 0, 2).reshape(depth, rows17, N_MOD, d_model)
    m_lat = lax.dynamic_slice_in_dim(mod_all, me * b_loc, b_loc, axis=1)
    m_ctx = mod_all[:, n_batch]

    diff = {n: full[n] for n in list(BIG) + list(SHARDED_SMALL) + REPLICATED}
    loss, (g_x, g_p, g_ml, g_mc) = jax.value_and_grad(local_loss, argnums=(0, 1, 2, 3))(x, diff, m_lat, m_ctx, ctx, target)
    loss = lax.psum(loss, ('x', 'y', 'c'))
    g_p['o_w_in'] = [jnp.concatenate([g[:, :c3], g[:, ODD_NA_AT:]], axis=-1) for g in g_p['o_w_in']]

    g_rows = jnp.concatenate([g_ml.reshape(depth, b_loc, N_MOD * d_model), g_mc.reshape(depth, 1, N_MOD * d_model),
                              jnp.zeros((depth, 8 - b_loc - 1, N_MOD * d_model), F32)], axis=1)
    g_mod_all = _gather_packed([g_rows], F32, PACK_SMALL,'gather_mod_grads')[0]
    g_lat_all = jnp.moveaxis(g_mod_all[:, :, :b_loc], 0, 1).reshape(depth, n_batch, N_MOD * d_model)
    g_ctx_all = g_mod_all[0, :, b_loc]
    for dev in range(1, N_DEV):
        g_ctx_all = g_ctx_all + g_mod_all[dev, :, b_loc]
    g_mod17 = jnp.concatenate([g_lat_all, g_ctx_all[:, None], jnp.zeros((depth, rows17 - n_batch - 1, N_MOD * d_model), F32)],
                              axis=1)
    grad_b_mod = jnp.sum(g_mod17, axis=1)
    g_mod_mine = lax.dynamic_slice_in_dim(g_mod17, me * mod_w, mod_w, axis=2)
    grad_w_mod = jnp.stack([_matmul(cond, g_mod_mine[i], 'tn', F32) for i in range(depth)])
    d_cond = _matmul(g_mod_mine[0], given['w_mod'][0], 'nt', F32)
    for i in range(1, depth):
        d_cond = d_cond + _matmul(g_mod_mine[i], given['w_mod'][i], 'nt', F32)
    d_cond_ctx = d_cond[n_batch]

    small_names = REPLICATED + list(SHARDED_SMALL)
    parts = [d_cond_ctx] + [g_p[n] for n in small_names]
    packed = jnp.concatenate([_pack_rows(a.reshape(-1), PACK_SMALL) for a in parts], axis=0)
    summed = _unpack_all(_all_reduce_small(packed), [a.shape for a in parts], PACK_SMALL)
    grads = dict(zip(['c_ctx'] + small_names, summed))
    c_ctx = given['c_ctx']
    sig = jax.nn.sigmoid(c_ctx)
    grads['c_ctx'] = grads['c_ctx'] * (sig * (1 + c_ctx * (1 - sig)))
    for n, axis in SHARDED_SMALL.items():
        width = given[n].shape[axis]
        grads[n] = lax.dynamic_slice_in_dim(grads[n], me * width, width, axis=axis)
    grads['w_mod'], grads['b_mod'] = grad_w_mod, grad_b_mod

    stacked = [jnp.stack(g_p[n]) for n in BIG]
    stacked = [g if cuts[n] is not None else _split_shards(g, BIG[n]) for n, g in zip(BIG, stacked)]
    landed = _exchange_many(stacked, [cuts[n] for n in BIG], False, 'scatter_weight_grads')

    upd = {}
    for n, slots in zip(BIG, landed):
        grads[n], *upd[n] = _adamw_landed(slots, given[n], given['m_' + n], given['v_' + n])
    upd['w_mod'] = _adamw_big(given['w_mod'], grads['w_mod'], given['m_w_mod'], given['v_w_mod'])
    rest = [n for n in WEIGHTS if n not in upd]
    out = _adamw_small([given[n] for n in rest], [grads[n] for n in rest], [given['m_' + n] for n in rest],
                       [given['v_' + n] for n in rest])
    upd.update(dict(zip(rest, out)))
    return (loss, g_x, *[grads[n] for n in WEIGHTS], *[upd[n][0] for n in WEIGHTS], *[upd[n][1] for n in WEIGHTS],
            *[upd[n][2] for n in WEIGHTS])
```
